```python
import jax, jax.numpy as jnp
from jax import lax
import numpy as np

D_MODEL = 1024
BATCH = 8
SEQ = 4096
DEPTH = 2

CHUNK = 128
D_SGU = D_MODEL
SGU_GROUPS = 8
SGU_GC = D_SGU // SGU_GROUPS
D_POOL = D_MODEL
POOL_WINDOWS = (2, 4, 8, 16)
POOL_GROUPS = len(POOL_WINDOWS)
POOL_GC = D_POOL // POOL_GROUPS
D_CONV = D_MODEL
CONV_WIDTH = 31
N_BRANCH = 3
D_FF = -(-8 * D_MODEL // (3 * 256)) * 256
EPS = 1e-6

OFF_U = 0
OFF_V = OFF_U + D_SGU
OFF_POOL = OFF_V + D_SGU
OFF_GLU = OFF_POOL + D_POOL
OFF_GATE = OFF_GLU + 2 * D_CONV
D_IN = OFF_GATE + N_BRANCH * D_MODEL

kernel_name = "hybrid_sgu_pool_conv_adaln_block"


def rmsnorm(x, g):
    xf = x.astype(jnp.float32)
    y = xf * lax.rsqrt(jnp.mean(xf * xf, axis=-1, keepdims=True) + EPS)
    return (y * g.astype(jnp.float32)).astype(x.dtype)


def layernorm(x, g, b):
    xf = x.astype(jnp.float32)
    mu = jnp.mean(xf, axis=-1, keepdims=True)
    var = jnp.mean(jnp.square(xf - mu), axis=-1, keepdims=True)
    y = (xf - mu) * lax.rsqrt(var + EPS) * g.astype(jnp.float32) + b.astype(jnp.float32)
    return y.astype(x.dtype)


def spatial_gating(u, v, ln_g, ln_b, w_s, b_s):
    B, S, _ = v.shape
    v = layernorm(v, ln_g, ln_b)
    vc = v.reshape(B, S // CHUNK, CHUNK, SGU_GROUPS, SGU_GC)
    mask = jnp.tril(jnp.ones((CHUNK, CHUNK), dtype=w_s.dtype))
    w = w_s * mask
    mixed = jnp.einsum('gts,bnsgc->bntgc', w, vc) + jnp.transpose(b_s)[None, None, :, :, None]
    return u * mixed.reshape(B, S, D_SGU)


def multiscale_pool(p, w_pool, pool_scale):
    B, S, _ = p.shape
    pg = p.reshape(B, S, POOL_GROUPS, POOL_GC).astype(jnp.float32)
    cs = jnp.cumsum(pg, axis=1)
    t = jnp.arange(1, S + 1, dtype=jnp.float32)
    outs = []
    for i, win in enumerate(POOL_WINDOWS):
        c_i = cs[:, :, i]
        lag = jnp.pad(c_i, ((0, 0), (win, 0), (0, 0)))[:, :S]
        cnt = jnp.minimum(t, float(win))[None, :, None]
        outs.append((c_i - lag) / cnt - pg[:, :, i])
    pooled = jnp.stack(outs, axis=2).astype(p.dtype)
    y = jnp.einsum('bsgc,gcd->bsgd', pooled, w_pool).reshape(B, S, D_POOL)
    return y * pool_scale


def conformer_conv(a, a_gate, conv_w, conv_b, ln_g, ln_b):
    z = a * jax.nn.sigmoid(a_gate)
    z = lax.conv_general_dilated(
        z, conv_w[:, None, :], window_strides=(1,), padding=[(CONV_WIDTH - 1, 0)],
        dimension_numbers=('NWC', 'WIO', 'NWC'), feature_group_count=D_CONV) + conv_b
    z = layernorm(z, ln_g, ln_b)
    return jax.nn.silu(z)


def _fwd_setup_inputs(seed: int = 0) -> dict:
    key = jax.random.key(seed)
    ks = jax.random.split(key, 32)
    L, D = DEPTH, D_MODEL

    def nrm(k, shape, scale):
        return jax.random.normal(k, shape, dtype=jnp.float32) * scale

    return {
        "x": nrm(ks[0], (BATCH, SEQ, D), 1.0),
        "c": nrm(ks[1], (BATCH, D), 1.0),
        "w_ada": nrm(ks[2], (L, D, 6 * D), 0.5 * D ** -0.5),
        "b_ada": nrm(ks[3], (L, 6 * D), 0.01),
        "g_mix": 1.0 + nrm(ks[4], (L, D), 0.1),
        "w_in": nrm(ks[5], (L, D, D_IN), D ** -0.5),
        "b_in": nrm(ks[6], (L, D_IN), 0.01),
        "sgu_ln_g": 1.0 + nrm(ks[7], (L, D_SGU), 0.1),
        "sgu_ln_b": nrm(ks[8], (L, D_SGU), 0.01),
        "sgu_w_s": nrm(ks[9], (L, SGU_GROUPS, CHUNK, CHUNK), 0.5 * CHUNK ** -0.5),
        "sgu_b_s": 1.0 + nrm(ks[10], (L, SGU_GROUPS, CHUNK), 0.1),
        "w_pa": nrm(ks[11], (L, D_SGU, D), D_SGU ** -0.5),
        "pool_w": nrm(ks[12], (L, POOL_GROUPS, POOL_GC, POOL_GC), POOL_GC ** -0.5),
        "pool_scale": 1.0 + nrm(ks[13], (L, D_POOL), 0.1),
        "w_pb": nrm(ks[14], (L, D_POOL, D), D_POOL ** -0.5),
        "conv_w": nrm(ks[15], (L, CONV_WIDTH, D_CONV), CONV_WIDTH ** -0.5),
        "conv_b": nrm(ks[16], (L, D_CONV), 0.01),
        "conv_ln_g": 1.0 + nrm(ks[17], (L, D_CONV), 0.1),
        "conv_ln_b": nrm(ks[18], (L, D_CONV), 0.01),
        "w_pc": nrm(ks[19], (L, D_CONV, D), D_CONV ** -0.5),
        "w_out": nrm(ks[20], (L, D, D), D ** -0.5),
        "g_ffn": 1.0 + nrm(ks[21], (L, D), 0.1),
        "w_ffn_in": nrm(ks[22], (L, D, 2 * D_FF), D ** -0.5),
        "w_ffn_out": nrm(ks[23], (L, D_FF, D), D_FF ** -0.5),
        "g_final": 1.0 + nrm(ks[24], (D,), 0.1),
    }


def _fwd_reference(x, c, w_ada, b_ada, g_mix, w_in, b_in, sgu_ln_g, sgu_ln_b, sgu_w_s, sgu_b_s, w_pa,
              pool_w, pool_scale, w_pb, conv_w, conv_b, conv_ln_g, conv_ln_b, w_pc, w_out,
              g_ffn, w_ffn_in, w_ffn_out, g_final):
    B, S, D = x.shape
    c_act = jax.nn.silu(c)
    for l in range(DEPTH):
        ada = (c_act @ w_ada[l] + b_ada[l])[:, None, :]
        sh_m, sc_m, gt_m, sh_f, sc_f, gt_f = jnp.split(ada, 6, axis=-1)

        h = rmsnorm(x, g_mix[l]) * (1.0 + sc_m) + sh_m
        z = h @ w_in[l] + b_in[l]
        u = jax.nn.gelu(z[..., OFF_U:OFF_V], approximate=False)
        v = jax.nn.gelu(z[..., OFF_V:OFF_POOL], approximate=False)
        p = z[..., OFF_POOL:OFF_GLU]
        a = z[..., OFF_GLU:OFF_GLU + D_CONV]
        a_gate = z[..., OFF_GLU + D_CONV:OFF_GATE]
        gates = jax.nn.sigmoid(z[..., OFF_GATE:]).reshape(B, S, N_BRANCH, D)

        y_a = spatial_gating(u, v, sgu_ln_g[l], sgu_ln_b[l], sgu_w_s[l], sgu_b_s[l]) @ w_pa[l]
        y_b = multiscale_pool(p, pool_w[l], pool_scale[l]) @ w_pb[l]
        y_c = conformer_conv(a, a_gate, conv_w[l], conv_b[l], conv_ln_g[l], conv_ln_b[l]) @ w_pc[l]
        merged = gates[:, :, 0] * y_a + gates[:, :, 1] * y_b + gates[:, :, 2] * y_c
        x = x + gt_m * (merged @ w_out[l])

        h = rmsnorm(x, g_ffn[l]) * (1.0 + sc_f) + sh_f
        gu = h @ w_ffn_in[l]
        g_part, u_part = jnp.split(gu, 2, axis=-1)
        x = x + gt_f * ((jax.nn.silu(g_part) * u_part) @ w_ffn_out[l])
    return rmsnorm(x, g_final)


import jax as _jax
import jax.numpy as _jnp

TWIN_FORMAT = 'train_step'
FWD_PARAMS = ['x', 'c', 'w_ada', 'b_ada', 'g_mix', 'w_in', 'b_in', 'sgu_ln_g', 'sgu_ln_b', 'sgu_w_s', 'sgu_b_s', 'w_pa', 'pool_w', 'pool_scale', 'w_pb', 'conv_w', 'conv_b', 'conv_ln_g', 'conv_ln_b', 'w_pc', 'w_out', 'g_ffn', 'w_ffn_in', 'w_ffn_out', 'g_final']
TWIN_WEIGHTS = ['w_ada', 'b_ada', 'g_mix', 'w_in', 'b_in', 'sgu_ln_g', 'sgu_ln_b', 'sgu_w_s', 'sgu_b_s', 'w_pa', 'pool_w', 'pool_scale', 'w_pb', 'conv_w', 'conv_b', 'conv_ln_g', 'conv_ln_b', 'w_pc', 'w_out', 'g_ffn', 'w_ffn_in', 'w_ffn_out', 'g_final']
TWIN_DIFF_INPUT = 'x'
TWIN_INPUTS = ['x', 'c', 'w_ada', 'b_ada', 'g_mix', 'w_in', 'b_in', 'sgu_ln_g', 'sgu_ln_b', 'sgu_w_s', 'sgu_b_s', 'w_pa', 'pool_w', 'pool_scale', 'w_pb', 'conv_w', 'conv_b', 'conv_ln_g', 'conv_ln_b', 'w_pc', 'w_out', 'g_ffn', 'w_ffn_in', 'w_ffn_out', 'g_final', 'loss_target', 'm_w_ada', 'm_b_ada', 'm_g_mix', 'm_w_in', 'm_b_in', 'm_sgu_ln_g', 'm_sgu_ln_b', 'm_sgu_w_s', 'm_sgu_b_s', 'm_w_pa', 'm_pool_w', 'm_pool_scale', 'm_w_pb', 'm_conv_w', 'm_conv_b', 'm_conv_ln_g', 'm_conv_ln_b', 'm_w_pc', 'm_w_out', 'm_g_ffn', 'm_w_ffn_in', 'm_w_ffn_out', 'm_g_final', 'v_w_ada', 'v_b_ada', 'v_g_mix', 'v_w_in', 'v_b_in', 'v_sgu_ln_g', 'v_sgu_ln_b', 'v_sgu_w_s', 'v_sgu_b_s', 'v_w_pa', 'v_pool_w', 'v_pool_scale', 'v_w_pb', 'v_conv_w', 'v_conv_b', 'v_conv_ln_g', 'v_conv_ln_b', 'v_w_pc', 'v_w_out', 'v_g_ffn', 'v_w_ffn_in', 'v_w_ffn_out', 'v_g_final']
TWIN_OUTPUTS = ['loss', 'grad_x', 'grad_w_ada', 'grad_b_ada', 'grad_g_mix', 'grad_w_in', 'grad_b_in', 'grad_sgu_ln_g', 'grad_sgu_ln_b', 'grad_sgu_w_s', 'grad_sgu_b_s', 'grad_w_pa', 'grad_pool_w', 'grad_pool_scale', 'grad_w_pb', 'grad_conv_w', 'grad_conv_b', 'grad_conv_ln_g', 'grad_conv_ln_b', 'grad_w_pc', 'grad_w_out', 'grad_g_ffn', 'grad_w_ffn_in', 'grad_w_ffn_out', 'grad_g_final', 'delta_w_ada', 'delta_b_ada', 'delta_g_mix', 'delta_w_in', 'delta_b_in', 'delta_sgu_ln_g', 'delta_sgu_ln_b', 'delta_sgu_w_s', 'delta_sgu_b_s', 'delta_w_pa', 'delta_pool_w', 'delta_pool_scale', 'delta_w_pb', 'delta_conv_w', 'delta_conv_b', 'delta_conv_ln_g', 'delta_conv_ln_b', 'delta_w_pc', 'delta_w_out', 'delta_g_ffn', 'delta_w_ffn_in', 'delta_w_ffn_out', 'delta_g_final', 'new_m_w_ada', 'new_m_b_ada', 'new_m_g_mix', 'new_m_w_in', 'new_m_b_in', 'new_m_sgu_ln_g', 'new_m_sgu_ln_b', 'new_m_sgu_w_s', 'new_m_sgu_b_s', 'new_m_w_pa', 'new_m_pool_w', 'new_m_pool_scale', 'new_m_w_pb', 'new_m_conv_w', 'new_m_conv_b', 'new_m_conv_ln_g', 'new_m_conv_ln_b', 'new_m_w_pc', 'new_m_w_out', 'new_m_g_ffn', 'new_m_w_ffn_in', 'new_m_w_ffn_out', 'new_m_g_final', 'new_v_w_ada', 'new_v_b_ada', 'new_v_g_mix', 'new_v_w_in', 'new_v_b_in', 'new_v_sgu_ln_g', 'new_v_sgu_ln_b', 'new_v_sgu_w_s', 'new_v_sgu_b_s', 'new_v_w_pa', 'new_v_pool_w', 'new_v_pool_scale', 'new_v_w_pb', 'new_v_conv_w', 'new_v_conv_b', 'new_v_conv_ln_g', 'new_v_conv_ln_b', 'new_v_w_pc', 'new_v_w_out', 'new_v_g_ffn', 'new_v_w_ffn_in', 'new_v_w_ffn_out', 'new_v_g_final']
TWIN_LEAF_KINDS = {'loss': 'loss', 'grad_x': 'grad_x', 'grad_w_ada': 'grad_w', 'grad_b_ada': 'grad_w', 'grad_g_mix': 'grad_w', 'grad_w_in': 'grad_w', 'grad_b_in': 'grad_w', 'grad_sgu_ln_g': 'grad_w', 'grad_sgu_ln_b': 'grad_w', 'grad_sgu_w_s': 'grad_w', 'grad_sgu_b_s': 'grad_w', 'grad_w_pa': 'grad_w', 'grad_pool_w': 'grad_w', 'grad_pool_scale': 'grad_w', 'grad_w_pb': 'grad_w', 'grad_conv_w': 'grad_w', 'grad_conv_b': 'grad_w', 'grad_conv_ln_g': 'grad_w', 'grad_conv_ln_b': 'grad_w', 'grad_w_pc': 'grad_w', 'grad_w_out': 'grad_w', 'grad_g_ffn': 'grad_w', 'grad_w_ffn_in': 'grad_w', 'grad_w_ffn_out': 'grad_w', 'grad_g_final': 'grad_w', 'delta_w_ada': 'delta_w', 'delta_b_ada': 'delta_w', 'delta_g_mix': 'delta_w', 'delta_w_in': 'delta_w', 'delta_b_in': 'delta_w', 'delta_sgu_ln_g': 'delta_w', 'delta_sgu_ln_b': 'delta_w', 'delta_sgu_w_s': 'delta_w', 'delta_sgu_b_s': 'delta_w', 'delta_w_pa': 'delta_w', 'delta_pool_w': 'delta_w', 'delta_pool_scale': 'delta_w', 'delta_w_pb': 'delta_w', 'delta_conv_w': 'delta_w', 'delta_conv_b': 'delta_w', 'delta_conv_ln_g': 'delta_w', 'delta_conv_ln_b': 'delta_w', 'delta_w_pc': 'delta_w', 'delta_w_out': 'delta_w', 'delta_g_ffn': 'delta_w', 'delta_w_ffn_in': 'delta_w', 'delta_w_ffn_out': 'delta_w', 'delta_g_final': 'delta_w', 'new_m_w_ada': 'new_m', 'new_m_b_ada': 'new_m', 'new_m_g_mix': 'new_m', 'new_m_w_in': 'new_m', 'new_m_b_in': 'new_m', 'new_m_sgu_ln_g': 'new_m', 'new_m_sgu_ln_b': 'new_m', 'new_m_sgu_w_s': 'new_m', 'new_m_sgu_b_s': 'new_m', 'new_m_w_pa': 'new_m', 'new_m_pool_w': 'new_m', 'new_m_pool_scale': 'new_m', 'new_m_w_pb': 'new_m', 'new_m_conv_w': 'new_m', 'new_m_conv_b': 'new_m', 'new_m_conv_ln_g': 'new_m', 'new_m_conv_ln_b': 'new_m', 'new_m_w_pc': 'new_m', 'new_m_w_out': 'new_m', 'new_m_g_ffn': 'new_m', 'new_m_w_ffn_in': 'new_m', 'new_m_w_ffn_out': 'new_m', 'new_m_g_final': 'new_m', 'new_v_w_ada': 'new_v', 'new_v_b_ada': 'new_v', 'new_v_g_mix': 'new_v', 'new_v_w_in': 'new_v', 'new_v_b_in': 'new_v', 'new_v_sgu_ln_g': 'new_v', 'new_v_sgu_ln_b': 'new_v', 'new_v_sgu_w_s': 'new_v', 'new_v_sgu_b_s': 'new_v', 'new_v_w_pa': 'new_v', 'new_v_pool_w': 'new_v', 'new_v_pool_scale': 'new_v', 'new_v_w_pb': 'new_v', 'new_v_conv_w': 'new_v', 'new_v_conv_b': 'new_v', 'new_v_conv_ln_g': 'new_v', 'new_v_conv_ln_b': 'new_v', 'new_v_w_pc': 'new_v', 'new_v_w_out': 'new_v', 'new_v_g_ffn': 'new_v', 'new_v_w_ffn_in': 'new_v', 'new_v_w_ffn_out': 'new_v', 'new_v_g_final': 'new_v'}


def _forward(args):
    return _fwd_reference(*[args[k] for k in FWD_PARAMS])


def _output_shape():
    out = _jax.eval_shape(lambda: _forward(_fwd_setup_inputs(0)))
    return out.shape, out.dtype

N_MICROBATCH = 1
ADAM_LR = 0.001
ADAM_B1 = 0.9
ADAM_B2 = 0.999
ADAM_EPS = 1e-08
ADAM_WD = 0.01
ADAM_STEP = 10
PER_EXAMPLE_BATCH_AXIS = {'x': 0, 'c': 0, 'loss_target': 0}
SHARED_INPUTS = []
_WEIGHT_DTYPES = {'w_ada': _jnp.float32, 'b_ada': _jnp.float32, 'g_mix': _jnp.float32, 'w_in': _jnp.float32, 'b_in': _jnp.float32, 'sgu_ln_g': _jnp.float32, 'sgu_ln_b': _jnp.float32, 'sgu_w_s': _jnp.float32, 'sgu_b_s': _jnp.float32, 'w_pa': _jnp.float32, 'pool_w': _jnp.float32, 'pool_scale': _jnp.float32, 'w_pb': _jnp.float32, 'conv_w': _jnp.float32, 'conv_b': _jnp.float32, 'conv_ln_g': _jnp.float32, 'conv_ln_b': _jnp.float32, 'w_pc': _jnp.float32, 'w_out': _jnp.float32, 'g_ffn': _jnp.float32, 'w_ffn_in': _jnp.float32, 'w_ffn_out': _jnp.float32, 'g_final': _jnp.float32}
MOMENT_SCALE = {'w_ada': 1.197465e-01, 'b_ada': 2.248900e-01, 'g_mix': 4.608488e-02, 'w_in': 1.705527e-02, 'b_in': 1.638320e-02, 'sgu_ln_g': 7.886217e-03, 'sgu_ln_b': 7.788882e-03, 'sgu_w_s': 1.572307e-02, 'sgu_b_s': 2.384074e-02, 'w_pa': 2.857838e-02, 'pool_w': 2.931979e-02, 'pool_scale': 2.903968e-02, 'w_pb': 2.945227e-02, 'conv_w': 2.022503e-02, 'conv_b': 4.353049e-02, 'conv_ln_g': 2.836172e-02, 'conv_ln_b': 2.934853e-02, 'w_pc': 2.097068e-02, 'w_out': 4.628970e-02, 'g_ffn': 5.313411e-02, 'w_ffn_in': 2.376979e-02, 'w_ffn_out': 3.909614e-02, 'g_final': 3.222295e+01}


def _to_microbatches(a, axis):
    t = _jnp.moveaxis(a, axis, 0)
    t = t.reshape((N_MICROBATCH, t.shape[0] // N_MICROBATCH) + t.shape[1:])
    return _jnp.moveaxis(t, 1, axis + 1)


def setup_inputs(seed: int = 0) -> dict:
    inp = _fwd_setup_inputs(seed)
    key = _jax.random.fold_in(_jax.random.key(seed), 7919)
    shape, _ = _output_shape()
    out = dict(inp)
    out["loss_target"] = _jax.random.normal(_jax.random.fold_in(key, 0), shape, _jnp.float32)
    for i, name in enumerate(TWIN_WEIGHTS):
        w = inp[name].astype(_jnp.float32)
        if MOMENT_SCALE is None:
            s = _jnp.sqrt(_jnp.mean(_jnp.square(w)) + 1e-30)
        else:
            s = MOMENT_SCALE[name]
        km, kv = _jax.random.split(_jax.random.fold_in(key, i + 1))
        out[name] = w
        out["m_" + name] = s * _jax.random.normal(km, w.shape, _jnp.float32)
        out["v_" + name] = (s * s) * _jax.random.uniform(kv, w.shape, _jnp.float32, 0.5, 1.5)
    if N_MICROBATCH > 1:
        for name, axis in PER_EXAMPLE_BATCH_AXIS.items():
            out[name] = _to_microbatches(out[name], axis)
    return {'x': out['x'], 'c': out['c'], 'w_ada': out['w_ada'], 'b_ada': out['b_ada'], 'g_mix': out['g_mix'], 'w_in': out['w_in'], 'b_in': out['b_in'], 'sgu_ln_g': out['sgu_ln_g'], 'sgu_ln_b': out['sgu_ln_b'], 'sgu_w_s': out['sgu_w_s'], 'sgu_b_s': out['sgu_b_s'], 'w_pa': out['w_pa'], 'pool_w': out['pool_w'], 'pool_scale': out['pool_scale'], 'w_pb': out['w_pb'], 'conv_w': out['conv_w'], 'conv_b': out['conv_b'], 'conv_ln_g': out['conv_ln_g'], 'conv_ln_b': out['conv_ln_b'], 'w_pc': out['w_pc'], 'w_out': out['w_out'], 'g_ffn': out['g_ffn'], 'w_ffn_in': out['w_ffn_in'], 'w_ffn_out': out['w_ffn_out'], 'g_final': out['g_final'], 'loss_target': out['loss_target'], 'm_w_ada': out['m_w_ada'], 'm_b_ada': out['m_b_ada'], 'm_g_mix': out['m_g_mix'], 'm_w_in': out['m_w_in'], 'm_b_in': out['m_b_in'], 'm_sgu_ln_g': out['m_sgu_ln_g'], 'm_sgu_ln_b': out['m_sgu_ln_b'], 'm_sgu_w_s': out['m_sgu_w_s'], 'm_sgu_b_s': out['m_sgu_b_s'], 'm_w_pa': out['m_w_pa'], 'm_pool_w': out['m_pool_w'], 'm_pool_scale': out['m_pool_scale'], 'm_w_pb': out['m_w_pb'], 'm_conv_w': out['m_conv_w'], 'm_conv_b': out['m_conv_b'], 'm_conv_ln_g': out['m_conv_ln_g'], 'm_conv_ln_b': out['m_conv_ln_b'], 'm_w_pc': out['m_w_pc'], 'm_w_out': out['m_w_out'], 'm_g_ffn': out['m_g_ffn'], 'm_w_ffn_in': out['m_w_ffn_in'], 'm_w_ffn_out': out['m_w_ffn_out'], 'm_g_final': out['m_g_final'], 'v_w_ada': out['v_w_ada'], 'v_b_ada': out['v_b_ada'], 'v_g_mix': out['v_g_mix'], 'v_w_in': out['v_w_in'], 'v_b_in': out['v_b_in'], 'v_sgu_ln_g': out['v_sgu_ln_g'], 'v_sgu_ln_b': out['v_sgu_ln_b'], 'v_sgu_w_s': out['v_sgu_w_s'], 'v_sgu_b_s': out['v_sgu_b_s'], 'v_w_pa': out['v_w_pa'], 'v_pool_w': out['v_pool_w'], 'v_pool_scale': out['v_pool_scale'], 'v_w_pb': out['v_w_pb'], 'v_conv_w': out['v_conv_w'], 'v_conv_b': out['v_conv_b'], 'v_conv_ln_g': out['v_conv_ln_g'], 'v_conv_ln_b': out['v_conv_ln_b'], 'v_w_pc': out['v_w_pc'], 'v_w_out': out['v_w_out'], 'v_g_ffn': out['v_g_ffn'], 'v_w_ffn_in': out['v_w_ffn_in'], 'v_w_ffn_out': out['v_w_ffn_out'], 'v_g_final': out['v_g_final']}


def _loss(weights, diff, rest, loss_target):
    with _jax.named_scope("forward"):
        args = {**rest, TWIN_DIFF_INPUT: diff, **{k: w.astype(_WEIGHT_DTYPES[k]) for k, w in weights.items()}}
        y = _forward(args)
    with _jax.named_scope("loss_head"):
        err = _jnp.square(y.astype(_jnp.float32) - loss_target)
        return 0.5 * _jnp.sum(_jnp.mean(err, axis=-1)) if err.ndim else 0.5 * err


def _adamw(w, g, m, v):
    m = ADAM_B1 * m + (1.0 - ADAM_B1) * g
    v = ADAM_B2 * v + (1.0 - ADAM_B2) * _jnp.square(g)
    m_hat = m / (1.0 - ADAM_B1 ** ADAM_STEP)
    v_hat = v / (1.0 - ADAM_B2 ** ADAM_STEP)
    delta = -ADAM_LR * (m_hat / (_jnp.sqrt(v_hat) + ADAM_EPS) + ADAM_WD * w)
    return delta, m, v


def reference(x, c, w_ada, b_ada, g_mix, w_in, b_in, sgu_ln_g, sgu_ln_b, sgu_w_s, sgu_b_s, w_pa, pool_w, pool_scale, w_pb, conv_w, conv_b, conv_ln_g, conv_ln_b, w_pc, w_out, g_ffn, w_ffn_in, w_ffn_out, g_final, loss_target, m_w_ada, m_b_ada, m_g_mix, m_w_in, m_b_in, m_sgu_ln_g, m_sgu_ln_b, m_sgu_w_s, m_sgu_b_s, m_w_pa, m_pool_w, m_pool_scale, m_w_pb, m_conv_w, m_conv_b, m_conv_ln_g, m_conv_ln_b, m_w_pc, m_w_out, m_g_ffn, m_w_ffn_in, m_w_ffn_out, m_g_final, v_w_ada, v_b_ada, v_g_mix, v_w_in, v_b_in, v_sgu_ln_g, v_sgu_ln_b, v_sgu_w_s, v_sgu_b_s, v_w_pa, v_pool_w, v_pool_scale, v_w_pb, v_conv_w, v_conv_b, v_conv_ln_g, v_conv_ln_b, v_w_pc, v_w_out, v_g_ffn, v_w_ffn_in, v_w_ffn_out, v_g_final):
    given = dict(x=x, c=c, w_ada=w_ada, b_ada=b_ada, g_mix=g_mix, w_in=w_in, b_in=b_in, sgu_ln_g=sgu_ln_g, sgu_ln_b=sgu_ln_b, sgu_w_s=sgu_w_s, sgu_b_s=sgu_b_s, w_pa=w_pa, pool_w=pool_w, pool_scale=pool_scale, w_pb=w_pb, conv_w=conv_w, conv_b=conv_b, conv_ln_g=conv_ln_g, conv_ln_b=conv_ln_b, w_pc=w_pc, w_out=w_out, g_ffn=g_ffn, w_ffn_in=w_ffn_in, w_ffn_out=w_ffn_out, g_final=g_final, loss_target=loss_target, m_w_ada=m_w_ada, m_b_ada=m_b_ada, m_g_mix=m_g_mix, m_w_in=m_w_in, m_b_in=m_b_in, m_sgu_ln_g=m_sgu_ln_g, m_sgu_ln_b=m_sgu_ln_b, m_sgu_w_s=m_sgu_w_s, m_sgu_b_s=m_sgu_b_s, m_w_pa=m_w_pa, m_pool_w=m_pool_w, m_pool_scale=m_pool_scale, m_w_pb=m_w_pb, m_conv_w=m_conv_w, m_conv_b=m_conv_b, m_conv_ln_g=m_conv_ln_g, m_conv_ln_b=m_conv_ln_b, m_w_pc=m_w_pc, m_w_out=m_w_out, m_g_ffn=m_g_ffn, m_w_ffn_in=m_w_ffn_in, m_w_ffn_out=m_w_ffn_out, m_g_final=m_g_final, v_w_ada=v_w_ada, v_b_ada=v_b_ada, v_g_mix=v_g_mix, v_w_in=v_w_in, v_b_in=v_b_in, v_sgu_ln_g=v_sgu_ln_g, v_sgu_ln_b=v_sgu_ln_b, v_sgu_w_s=v_sgu_w_s, v_sgu_b_s=v_sgu_b_s, v_w_pa=v_w_pa, v_pool_w=v_pool_w, v_pool_scale=v_pool_scale, v_w_pb=v_w_pb, v_conv_w=v_conv_w, v_conv_b=v_conv_b, v_conv_ln_g=v_conv_ln_g, v_conv_ln_b=v_conv_ln_b, v_w_pc=v_w_pc, v_w_out=v_w_out, v_g_ffn=v_g_ffn, v_w_ffn_in=v_w_ffn_in, v_w_ffn_out=v_w_ffn_out, v_g_final=v_g_final)
    weights = {n: given[n] for n in TWIN_WEIGHTS}
    shared = {n: given[n] for n in SHARED_INPUTS}
    per_example = {n: given[n] for n in ['x', 'c']}
    grad_fn = _jax.value_and_grad(_loss, argnums=(0, 1))

    def one_microbatch(ex, loss_target):
        ex = dict(ex)
        diff = ex.pop(TWIN_DIFF_INPUT)
        return grad_fn(weights, diff, {**shared, **ex}, loss_target)

    if N_MICROBATCH == 1:
        loss, (grad_w, grad_x) = one_microbatch(per_example, given["loss_target"])
    else:
        def body(carry, xs):
            loss_sum, grad_sum = carry
            l_k, (gw_k, gx_k) = one_microbatch(xs[0], xs[1])
            with _jax.named_scope("update"):
                return (loss_sum + l_k, _jax.tree.map(_jnp.add, grad_sum, gw_k)), gx_k

        init = (_jnp.zeros((), _jnp.float32), _jax.tree.map(_jnp.zeros_like, weights))
        (loss, grad_w), grad_x = _jax.lax.scan(body, init, (per_example, given["loss_target"]))
    with _jax.named_scope("update"):
        delta_w, new_m, new_v = {}, {}, {}
        for n in TWIN_WEIGHTS:
            delta_w[n], new_m[n], new_v[n] = _adamw(weights[n], grad_w[n], given["m_" + n], given["v_" + n])
    return (loss, grad_x, *[grad_w[n] for n in TWIN_WEIGHTS], *[delta_w[n] for n in TWIN_WEIGHTS],
            *[new_m[n] for n in TWIN_WEIGHTS], *[new_v[n] for n in TWIN_WEIGHTS])
```

```python
import functools
import math

import jax
import jax.numpy as jnp
from jax import lax
from jax.experimental import pallas as pl
from jax.experimental.pallas import tpu as pltpu

F32 = jnp.float32
BF16 = jnp.bfloat16
MESH = pl.DeviceIdType.MESH
AXES = ("x", "y", "c")
N_DEV = 8

D = 1024
DEPTH = 2
EPS = 1e-6
CHUNK = 128
SGU_G = 8
POOL_WINDOWS = (2, 4, 8, 16)
POOL_GC = 256
CONV_K = 31
HALO = 32
D_FF = 2816
FF_BLK = D_FF // 4
ADA_BLK = 6 * D // N_DEV

ADAM_LR = 0.001
ADAM_B1 = 0.9
ADAM_B2 = 0.999
ADAM_EPS = 1e-08
ADAM_WD = 0.01
ADAM_STEP = 10

VMEM_LIMIT_V7X = 56 * 1024 * 1024
INV_SQRT2 = 1.0 / math.sqrt(2.0)
INV_SQRT_2PI = 1.0 / math.sqrt(2.0 * math.pi)


def _params(*sem):
    return pltpu.CompilerParams(dimension_semantics=sem if sem else None, vmem_limit_bytes=VMEM_LIMIT_V7X)


def _tile(n, pref):
    if n <= pref:
        return n
    for t in range(pref - pref % 8, 0, -8):
        if n % t == 0:
            return t
    raise ValueError((n, pref))


def _sds(shape, dtype):
    return jax.ShapeDtypeStruct(shape, dtype)


def _sigmoid(x):
    return 1.0 / (1.0 + jnp.exp(-x))


def _gelu(x):
    return 0.5 * x * (1.0 + lax.erf(x * INV_SQRT2))


def _gelu_grad(x):
    return 0.5 * (1.0 + lax.erf(x * INV_SQRT2)) + x * (INV_SQRT_2PI * jnp.exp(-0.5 * x * x))


def _ln_stats(v):
    mu = jnp.mean(v, axis=-1, keepdims=True)
    vc = v - mu
    rstd = lax.rsqrt(jnp.mean(vc * vc, axis=-1, keepdims=True) + EPS)
    return vc * rstd, rstd


def _ln_bwd(dvhat, vhat, rstd):
    return rstd * (dvhat - jnp.mean(dvhat, axis=-1, keepdims=True)
                   - vhat * jnp.mean(dvhat * vhat, axis=-1, keepdims=True))


def _colsum(v):
    return jnp.sum(v, axis=0, keepdims=True)


def _dot(a, b):
    return jnp.dot(a, b, preferred_element_type=F32)


def _dot_nt(a, b):
    return lax.dot_general(a, b, (((1,), (1,)), ((), ())), preferred_element_type=F32)


def _dot_tn(a, b):
    return lax.dot_general(a, b, (((0,), (0,)), ((), ())), preferred_element_type=F32)


def _tril_mask():
    r = lax.broadcasted_iota(jnp.int32, (CHUNK, CHUNK), 0)
    c = lax.broadcasted_iota(jnp.int32, (CHUNK, CHUNK), 1)
    return (r >= c).astype(F32)


def _mesh_pos():
    return tuple(lax.axis_index(a) for a in AXES)


def _all_gather(arrs, name):
    n = len(arrs)

    def body(*refs):
        ins, outs = refs[:n], refs[n:2 * n]
        send_sems, recv_sems, local_sems = refs[2 * n:]
        x, y, c = _mesh_pos()
        me, sibling = (x, y, c), (x, y, 1 - c)
        chips = [(1 - x, y), (x, 1 - y), (1 - x, 1 - y)]

        def slot(a, p):
            return outs[a].at[4 * p[0] + 2 * p[1] + p[2]]

        def copy(a, k, block, to, src=None):
            dst = slot(a, block)
            return pltpu.make_async_remote_copy(
                src_ref=dst if src is None else src, dst_ref=dst, send_sem=send_sems.at[a, k],
                recv_sem=recv_sems.at[a, k], device_id=to, device_id_type=MESH)

        mine = [pltpu.make_async_copy(ins[a], slot(a, me), local_sems.at[a]) for a in range(n)]
        for m in mine:
            m.start()
        first = []
        for a in range(n):
            first.append(copy(a, 0, me, sibling, src=ins[a]))
            first += [copy(a, 1 + j, me, (*chip, c), src=ins[a]) for j, chip in enumerate(chips)]
        for cp in first:
            cp.start()
        passed = []
        for j, chip in enumerate(chips):
            for a in range(n):
                copy(a, 1 + j, (*chip, c), me).wait_recv()
                fwd = copy(a, 4 + j, (*chip, c), sibling)
                fwd.start()
                passed.append(fwd)
        for a in range(n):
            copy(a, 0, sibling, me).wait_recv()
            for j, chip in enumerate(chips):
                copy(a, 4 + j, (*chip, 1 - c), me).wait_recv()
        for cp in first + passed:
            cp.wait_send()
        for m in mine:
            m.wait()

    any_spec = pl.BlockSpec(memory_space=pl.ANY)
    return pl.pallas_call(
        body, name=name,
        out_shape=[_sds((N_DEV,) + a.shape, a.dtype) for a in arrs],
        in_specs=[any_spec] * n, out_specs=[any_spec] * n,
        scratch_shapes=[pltpu.SemaphoreType.DMA((n, 7)), pltpu.SemaphoreType.DMA((n, 7)),
                        pltpu.SemaphoreType.DMA((n,))],
    )(*arrs)


def _exchange_sibling(arrs, name):
    n = len(arrs)

    def body(*refs):
        ins, outs = refs[:n], refs[n:2 * n]
        send_sems, recv_sems = refs[2 * n:]
        x, y, c = _mesh_pos()
        copies = []
        for a in range(n):
            for q in range(4):
                copies.append(pltpu.make_async_remote_copy(
                    src_ref=ins[a].at[2 * q + (1 - c)], dst_ref=outs[a].at[q], send_sem=send_sems.at[a, q],
                    recv_sem=recv_sems.at[a, q], device_id=(x, y, 1 - c), device_id_type=MESH))
        for cp in copies:
            cp.start()
        for cp in copies:
            cp.wait()

    any_spec = pl.BlockSpec(memory_space=pl.ANY)
    return pl.pallas_call(
        body, name=name,
        out_shape=[_sds((4,) + a.shape[1:], a.dtype) for a in arrs],
        in_specs=[any_spec] * n, out_specs=[any_spec] * n,
        scratch_shapes=[pltpu.SemaphoreType.DMA((n, 4)), pltpu.SemaphoreType.DMA((n, 4))],
    )(*arrs)


def _exchange_chips(arrs, name):
    n = len(arrs)

    def body(*refs):
        ins, outs = refs[:n], refs[n:2 * n]
        send_sems, recv_sems, local_sems = refs[2 * n:]
        x, y, c = _mesh_pos()
        q_me = 2 * x + y
        chips = [(1 - x, y), (x, 1 - y), (1 - x, 1 - y)]
        own = [pltpu.make_async_copy(ins[a].at[q_me], outs[a].at[q_me], local_sems.at[a]) for a in range(n)]
        for cp in own:
            cp.start()
        copies = []
        for a in range(n):
            for j, chip in enumerate(chips):
                copies.append(pltpu.make_async_remote_copy(
                    src_ref=ins[a].at[2 * chip[0] + chip[1]], dst_ref=outs[a].at[q_me],
                    send_sem=send_sems.at[a, j], recv_sem=recv_sems.at[a, j],
                    device_id=(*chip, c), device_id_type=MESH))
        for cp in copies:
            cp.start()
        for cp in copies:
            cp.wait()
        for cp in own:
            cp.wait()

    any_spec = pl.BlockSpec(memory_space=pl.ANY)
    return pl.pallas_call(
        body, name=name,
        out_shape=[_sds(a.shape, a.dtype) for a in arrs],
        in_specs=[any_spec] * n, out_specs=[any_spec] * n,
        scratch_shapes=[pltpu.SemaphoreType.DMA((n, 3)), pltpu.SemaphoreType.DMA((n, 3)),
                        pltpu.SemaphoreType.DMA((n,))],
    )(*arrs)


def _sibling_sum(arr, land, core, name):
    _, rows, cols = arr.shape
    tr = _tile(rows, 512)
    arr4 = arr.reshape(4, 2, rows, cols)

    def body(c_ref, a_ref, l_ref, o_ref):
        o_ref[...] = a_ref[...] + l_ref[...]

    grid_spec = pltpu.PrefetchScalarGridSpec(
        num_scalar_prefetch=1, grid=(4, rows // tr),
        in_specs=[pl.BlockSpec((None, None, tr, cols), lambda q, r, c_ref: (q, c_ref[0], r, 0)),
                  pl.BlockSpec((None, tr, cols), lambda q, r, c_ref: (q, r, 0))],
        out_specs=pl.BlockSpec((None, tr, cols), lambda q, r, c_ref: (q, r, 0)))
    return pl.pallas_call(body, name=name, grid_spec=grid_spec, out_shape=_sds((4, rows, cols), F32),
                          compiler_params=_params("arbitrary", "arbitrary"))(core, arr4, land)


def _reduce_scatter(arrs, core, tag):
    land = _exchange_sibling(arrs, name=f"rs_sibling_{tag}")
    part = [_sibling_sum(a, l, core, name=f"rs_sum_{tag}_{k}") for k, (a, l) in enumerate(zip(arrs, land))]
    return _exchange_chips(part, name=f"rs_chips_{tag}")


def _adam(gparts, w, m, v, name):
    p, rows, cols = gparts.shape
    tr = _tile(rows, 256)
    c1 = 1.0 - ADAM_B1 ** ADAM_STEP
    c2 = 1.0 - ADAM_B2 ** ADAM_STEP

    def body(g_ref, w_ref, m_ref, v_ref, go_ref, d_ref, mo_ref, vo_ref):
        g = g_ref[0]
        for k in range(1, p):
            g = g + g_ref[k]
        m_new = ADAM_B1 * m_ref[...] + (1.0 - ADAM_B1) * g
        v_new = ADAM_B2 * v_ref[...] + (1.0 - ADAM_B2) * (g * g)
        m_hat = m_new / c1
        v_hat = v_new / c2
        go_ref[...] = g
        d_ref[...] = -ADAM_LR * (m_hat / (jnp.sqrt(v_hat) + ADAM_EPS) + ADAM_WD * w_ref[...])
        mo_ref[...] = m_new
        vo_ref[...] = v_new

    blk = pl.BlockSpec((tr, cols), lambda r: (r, 0))
    return pl.pallas_call(
        body, name=name, grid=(rows // tr,),
        in_specs=[pl.BlockSpec((p, tr, cols), lambda r: (0, r, 0)), blk, blk, blk],
        out_specs=[blk] * 4, out_shape=[_sds((rows, cols), F32)] * 4,
        compiler_params=_params("arbitrary"))(gparts, w, m, v)


def _adam_nd(gparts, w, m, v, name):
    shape = w.shape
    cols = shape[-1]
    rows = w.size // cols
    out = _adam(gparts.reshape(gparts.shape[0], rows, cols), w.reshape(rows, cols), m.reshape(rows, cols),
                v.reshape(rows, cols), name)
    return [o.reshape(shape) for o in out]


def _norm_mod(x, g, sc, sh, name):
    t = x.shape[0]
    tm = _tile(t, 512)

    def body(x_ref, g_ref, sc_ref, sh_ref, h_ref):
        xv = x_ref[...]
        r = lax.rsqrt(jnp.mean(xv * xv, axis=-1, keepdims=True) + EPS)
        h_ref[...] = (xv * r * g_ref[...] * (1.0 + sc_ref[...]) + sh_ref[...]).astype(BF16)

    row = pl.BlockSpec((1, D), lambda i: (0, 0))
    blk = pl.BlockSpec((tm, D), lambda i: (i, 0))
    return pl.pallas_call(body, name=name, grid=(t // tm,), in_specs=[blk, row, row, row], out_specs=blk,
                          out_shape=_sds((t, D), BF16), compiler_params=_params("arbitrary"))(x, g, sc, sh)


def _mm_cols(a, b8, bias8, name):
    t, k = a.shape
    j, _, n = b8.shape
    tm = _tile(t, 1024)

    def body(a_ref, b_ref, bias_ref, o_ref):
        o_ref[...] = _dot(a_ref[...], b_ref[...]) + bias_ref[...]

    return pl.pallas_call(
        body, name=name, grid=(j, t // tm),
        in_specs=[pl.BlockSpec((tm, k), lambda jj, i: (i, 0)),
                  pl.BlockSpec((None, k, n), lambda jj, i: (jj, 0, 0)),
                  pl.BlockSpec((None, 1, n), lambda jj, i: (jj, 0, 0))],
        out_specs=pl.BlockSpec((None, tm, n), lambda jj, i: (jj, i, 0)),
        out_shape=_sds((j, t, n), F32), compiler_params=_params("arbitrary", "arbitrary"))(a, b8, bias8)


def _halo_before(tm, col):
    return pl.BlockSpec((None, HALO, D), lambda i: (col, jnp.maximum(i * (tm // HALO) - 1, 0), 0))


def _pool_forward(p_ext, t0, rows):
    t = t0 + lax.broadcasted_iota(jnp.int32, (rows, 1), 0)
    out = []
    for gi, win in enumerate(POOL_WINDOWS):
        e = p_ext[:, gi * POOL_GC:(gi + 1) * POOL_GC]
        s, sh = e, 1
        while sh < win:
            s = s + pltpu.roll(s, sh, 0)
            sh *= 2
        cnt = jnp.minimum(t + 1, win).astype(F32)
        out.append(s[HALO:] / cnt - e[HALO:])
    return out


def _branches_fwd(z8, ln_g, ln_b, w_s, bs_b, pool_w, pool_scale, conv_w, conv_b, cln_g, cln_b, name):
    t = z8.shape[1]
    tm = _tile(t, 256)
    n_ext = tm + HALO

    def body(zu_ref, zv_ref, p_ref, a_ref, ag_ref, ph_ref, ah_ref, agh_ref, lng_ref, lnb_ref, ws_ref, bsb_ref,
             wp_ref, ps_ref, cw_ref, cb_ref, clg_ref, clb_ref, sa_ref, sb_ref, sc_ref, cv_ref):
        i = pl.program_id(0)
        has_past = (i > 0).astype(F32)
        u = _gelu(zu_ref[...])
        vhat, _ = _ln_stats(_gelu(zv_ref[...]))
        vb = (vhat * lng_ref[...] + lnb_ref[...]).astype(BF16)
        mask = _tril_mask()
        for g in range(SGU_G):
            cols = slice(g * CHUNK, (g + 1) * CHUNK)
            wm = (ws_ref[g] * mask).astype(BF16)
            for n in range(tm // CHUNK):
                rows = slice(n * CHUNK, (n + 1) * CHUNK)
                mixed = _dot(wm, vb[rows, cols]) + bsb_ref[g]
                sa_ref[rows, cols] = (u[rows, cols] * mixed).astype(BF16)
        p_ext = jnp.concatenate([ph_ref[...] * has_past, p_ref[...]], axis=0)
        pooled = _pool_forward(p_ext, i * tm, tm)
        for gi in range(len(POOL_WINDOWS)):
            cols = slice(gi * POOL_GC, (gi + 1) * POOL_GC)
            y = _dot(pooled[gi].astype(BF16), wp_ref[gi].astype(BF16))
            sb_ref[:, cols] = (y * ps_ref[:, cols]).astype(BF16)
        a_ext = jnp.concatenate([ah_ref[...] * has_past, a_ref[...]], axis=0)
        ag_ext = jnp.concatenate([agh_ref[...], ag_ref[...]], axis=0)
        zc = a_ext * _sigmoid(ag_ext)

        def tap(k, acc):
            return acc + cw_ref[pl.ds(k, 1), :] * pltpu.roll(zc, CONV_K - 1 - k, 0)

        cv = lax.fori_loop(0, CONV_K, tap, jnp.zeros((n_ext, D), F32))[HALO:] + cb_ref[...]
        cv_ref[...] = cv
        chat, _ = _ln_stats(cv)
        cl = chat * clg_ref[...] + clb_ref[...]
        sc_ref[...] = (cl * _sigmoid(cl)).astype(BF16)

    def col(j):
        return pl.BlockSpec((None, tm, D), lambda i: (j, i, 0))

    row = pl.BlockSpec((1, D), lambda i: (0, 0))
    full3 = lambda s: pl.BlockSpec(s, lambda i: (0, 0, 0))
    blk = pl.BlockSpec((tm, D), lambda i: (i, 0))
    return pl.pallas_call(
        body, name=name, grid=(t // tm,),
        in_specs=[col(0), col(1), col(2), col(3), col(4), _halo_before(tm, 2), _halo_before(tm, 3),
                  _halo_before(tm, 4), row, row, full3((SGU_G, CHUNK, CHUNK)), full3((SGU_G, CHUNK, CHUNK)),
                  full3((4, POOL_GC, POOL_GC)), row, pl.BlockSpec((HALO, D), lambda i: (0, 0)), row, row, row],
        out_specs=[blk, blk, blk, blk],
        out_shape=[_sds((t, D), BF16)] * 3 + [_sds((t, D), F32)],
        compiler_params=_params("arbitrary"),
    )(z8, z8, z8, z8, z8, z8, z8, z8, ln_g, ln_b, w_s, bs_b, pool_w, pool_scale, conv_w, conv_b, cln_g, cln_b)


def _proj_merge(sa, sb, sc, w_pa, w_pb, w_pc, z8, name):
    t = sa.shape[0]
    tm = _tile(t, 512)

    def body(sa_ref, sb_ref, sc_ref, wa_ref, wb_ref, wc_ref, g0_ref, g1_ref, g2_ref, ya_ref, yb_ref, yc_ref, m_ref):
        merged = jnp.zeros((tm, D), F32)
        for s_ref, w_ref, g_ref, y_ref in ((sa_ref, wa_ref, g0_ref, ya_ref), (sb_ref, wb_ref, g1_ref, yb_ref),
                                           (sc_ref, wc_ref, g2_ref, yc_ref)):
            y = _dot(s_ref[...], w_ref[...])
            y_ref[...] = y.astype(BF16)
            merged = merged + _sigmoid(g_ref[...]) * y
        m_ref[...] = merged.astype(BF16)

    blk = pl.BlockSpec((tm, D), lambda i: (i, 0))
    wspec = pl.BlockSpec((D, D), lambda i: (0, 0))
    gate = lambda j: pl.BlockSpec((None, tm, D), lambda i: (j, i, 0))
    return pl.pallas_call(
        body, name=name, grid=(t // tm,),
        in_specs=[blk, blk, blk, wspec, wspec, wspec, gate(5), gate(6), gate(7)],
        out_specs=[blk] * 4, out_shape=[_sds((t, D), BF16)] * 4,
        compiler_params=_params("arbitrary"))(sa, sb, sc, w_pa, w_pb, w_pc, z8, z8, z8)


def _out_proj(merged, w_out, x, gt, name):
    t = x.shape[0]
    tm = _tile(t, 512)

    def body(m_ref, w_ref, x_ref, gt_ref, om_ref, x1_ref):
        om = _dot(m_ref[...], w_ref[...])
        om_ref[...] = om
        x1_ref[...] = x_ref[...] + gt_ref[...] * om

    blk = pl.BlockSpec((tm, D), lambda i: (i, 0))
    return pl.pallas_call(
        body, name=name, grid=(t // tm,),
        in_specs=[blk, pl.BlockSpec((D, D), lambda i: (0, 0)), blk, pl.BlockSpec((1, D), lambda i: (0, 0))],
        out_specs=[blk, blk], out_shape=[_sds((t, D), F32)] * 2,
        compiler_params=_params("arbitrary"))(merged, w_out, x, gt)


def _ffn_in(h2, wfi, name):
    t = h2.shape[0]
    tm = _tile(t, 512)

    def body(h_ref, w_ref, gu_ref, f_ref):
        hv = h_ref[...]
        gp = _dot(hv, w_ref[0])
        up = _dot(hv, w_ref[1])
        gu_ref[0] = gp
        gu_ref[1] = up
        f_ref[...] = (gp * _sigmoid(gp) * up).astype(BF16)

    return pl.pallas_call(
        body, name=name, grid=(4, t // tm),
        in_specs=[pl.BlockSpec((tm, D), lambda j, i: (i, 0)),
                  pl.BlockSpec((2, None, D, FF_BLK), lambda j, i: (0, j, 0, 0))],
        out_specs=[pl.BlockSpec((2, None, tm, FF_BLK), lambda j, i: (0, j, i, 0)),
                   pl.BlockSpec((None, tm, FF_BLK), lambda j, i: (j, i, 0))],
        out_shape=[_sds((2, 4, t, FF_BLK), F32), _sds((4, t, FF_BLK), BF16)],
        compiler_params=_params("arbitrary", "arbitrary"))(h2, wfi)


def _ffn_out(f4, wfo4, x1, gt, name):
    t = x1.shape[0]
    tm = _tile(t, 512)

    def body(f_ref, w_ref, x_ref, gt_ref, o_ref, x2_ref):
        j = pl.program_id(1)

        @pl.when(j == 0)
        def _():
            o_ref[...] = jnp.zeros_like(o_ref)

        o_ref[...] += _dot(f_ref[...], w_ref[...])

        @pl.when(j == 3)
        def _():
            x2_ref[...] = x_ref[...] + gt_ref[...] * o_ref[...]

    blk = pl.BlockSpec((tm, D), lambda i, j: (i, 0))
    return pl.pallas_call(
        body, name=name, grid=(t // tm, 4),
        in_specs=[pl.BlockSpec((None, tm, FF_BLK), lambda i, j: (j, i, 0)),
                  pl.BlockSpec((None, FF_BLK, D), lambda i, j: (j, 0, 0)), blk,
                  pl.BlockSpec((1, D), lambda i, j: (0, 0))],
        out_specs=[blk, blk], out_shape=[_sds((t, D), F32)] * 2,
        compiler_params=_params("arbitrary", "arbitrary"))(f4, wfo4, x1, gt)


def _final_loss(x, g, target, name):
    t = x.shape[0]
    tm = _tile(t, 512)

    def body(x_ref, g_ref, t_ref, loss_ref, dx_ref, dg_ref):
        @pl.when(pl.program_id(0) == 0)
        def _():
            loss_ref[...] = jnp.zeros_like(loss_ref)
            dg_ref[...] = jnp.zeros_like(dg_ref)

        xv = x_ref[...]
        r = lax.rsqrt(jnp.mean(xv * xv, axis=-1, keepdims=True) + EPS)
        xn = xv * r
        diff = xn * g_ref[...] - t_ref[...]
        loss_ref[...] += 0.5 * jnp.sum(jnp.mean(diff * diff, axis=-1, keepdims=True))
        dy = diff * (1.0 / D)
        dg_ref[...] += _colsum(dy * xn)
        dxn = dy * g_ref[...]
        dx_ref[...] = r * (dxn - xn * jnp.mean(dxn * xn, axis=-1, keepdims=True))

    blk = pl.BlockSpec((tm, D), lambda i: (i, 0))
    row = pl.BlockSpec((1, D), lambda i: (0, 0))
    return pl.pallas_call(
        body, name=name, grid=(t // tm,), in_specs=[blk, row, blk],
        out_specs=[pl.BlockSpec((8, 128), lambda i: (0, 0)), blk, row],
        out_shape=[_sds((8, 128), F32), _sds((t, D), F32), _sds((1, D), F32)],
        compiler_params=_params("arbitrary"))(x, g, target)


def _gate_bwd(dx, o, gt, name):
    t = dx.shape[0]
    tm = _tile(t, 512)

    def body(dx_ref, o_ref, gt_ref, do_ref, dgt_ref):
        @pl.when(pl.program_id(0) == 0)
        def _():
            dgt_ref[...] = jnp.zeros_like(dgt_ref)

        dxv = dx_ref[...]
        do_ref[...] = (dxv * gt_ref[...]).astype(BF16)
        dgt_ref[...] += _colsum(dxv * o_ref[...])

    blk = pl.BlockSpec((tm, D), lambda i: (i, 0))
    row = pl.BlockSpec((1, D), lambda i: (0, 0))
    return pl.pallas_call(body, name=name, grid=(t // tm,), in_specs=[blk, blk, row], out_specs=[blk, row],
                          out_shape=[_sds((t, D), BF16), _sds((1, D), F32)],
                          compiler_params=_params("arbitrary"))(dx, o, gt)


def _norm_mod_bwd(dh, x, dres, g, sc, name):
    t = x.shape[0]
    tm = _tile(t, 512)

    def body(dh_ref, x_ref, dr_ref, g_ref, sc_ref, dx_ref, st_ref):
        @pl.when(pl.program_id(0) == 0)
        def _():
            st_ref[...] = jnp.zeros_like(st_ref)

        xv, dhv = x_ref[...], dh_ref[...]
        r = lax.rsqrt(jnp.mean(xv * xv, axis=-1, keepdims=True) + EPS)
        xn = xv * r
        gv, mod = g_ref[...], 1.0 + sc_ref[...]
        st_ref[0:1, :] += _colsum(dhv)
        st_ref[1:2, :] += _colsum(dhv * xn * gv)
        st_ref[2:3, :] += _colsum(dhv * xn * mod)
        dxn = dhv * gv * mod
        dx_ref[...] = dr_ref[...] + r * (dxn - xn * jnp.mean(dxn * xn, axis=-1, keepdims=True))

    blk = pl.BlockSpec((tm, D), lambda i: (i, 0))
    row = pl.BlockSpec((1, D), lambda i: (0, 0))
    return pl.pallas_call(body, name=name, grid=(t // tm,), in_specs=[blk, blk, blk, row, row],
                          out_specs=[blk, pl.BlockSpec((3, D), lambda i: (0, 0))],
                          out_shape=[_sds((t, D), F32), _sds((3, D), F32)],
                          compiler_params=_params("arbitrary"))(dh, x, dres, g, sc)


def _ffn_bwd_act(do, wfo4, gu, name):
    t = do.shape[0]
    tm = _tile(t, 512)

    def body(do_ref, w_ref, gu_ref, dgu_ref):
        df = _dot_nt(do_ref[...], w_ref[...])
        gp, up = gu_ref[0], gu_ref[1]
        sg = _sigmoid(gp)
        dgu_ref[0] = (df * up * (sg * (1.0 + gp * (1.0 - sg)))).astype(BF16)
        dgu_ref[1] = (df * (gp * sg)).astype(BF16)

    gu_spec = pl.BlockSpec((2, None, tm, FF_BLK), lambda j, i: (0, j, i, 0))
    return pl.pallas_call(
        body, name=name, grid=(4, t // tm),
        in_specs=[pl.BlockSpec((tm, D), lambda j, i: (i, 0)),
                  pl.BlockSpec((None, FF_BLK, D), lambda j, i: (j, 0, 0)), gu_spec],
        out_specs=gu_spec, out_shape=_sds((2, 4, t, FF_BLK), BF16),
        compiler_params=_params("arbitrary", "arbitrary"))(do, wfo4, gu)


def _mm_nt_sum(a8, b8, name):
    j, t, k = a8.shape
    n = b8.shape[1]
    tm = _tile(t, 512)

    def body(a_ref, b_ref, o_ref):
        @pl.when(pl.program_id(1) == 0)
        def _():
            o_ref[...] = jnp.zeros_like(o_ref)

        o_ref[...] += _dot_nt(a_ref[...], b_ref[...])

    return pl.pallas_call(
        body, name=name, grid=(t // tm, j),
        in_specs=[pl.BlockSpec((None, tm, k), lambda i, jj: (jj, i, 0)),
                  pl.BlockSpec((None, n, k), lambda i, jj: (jj, 0, 0))],
        out_specs=pl.BlockSpec((tm, n), lambda i, jj: (i, 0)), out_shape=_sds((t, n), F32),
        compiler_params=_params("arbitrary", "arbitrary"))(a8, b8)


def _mm_tn(a8, b8, name):
    ja, t, m = a8.shape
    jb, _, n = b8.shape
    j = max(ja, jb)
    tk = _tile(t, 1024)

    def body(a_ref, b_ref, o_ref):
        @pl.when(pl.program_id(1) == 0)
        def _():
            o_ref[...] = jnp.zeros_like(o_ref)

        o_ref[...] += _dot_tn(a_ref[...], b_ref[...])

    return pl.pallas_call(
        body, name=name, grid=(j, t // tk),
        in_specs=[pl.BlockSpec((None, tk, m), (lambda jj, kk: (jj, kk, 0)) if ja > 1 else (lambda jj, kk: (0, kk, 0))),
                  pl.BlockSpec((None, tk, n), (lambda jj, kk: (jj, kk, 0)) if jb > 1 else (lambda jj, kk: (0, kk, 0)))],
        out_specs=pl.BlockSpec((None, m, n), lambda jj, kk: (jj, 0, 0)), out_shape=_sds((j, m, n), F32),
        compiler_params=_params("arbitrary", "arbitrary"))(a8, b8)


def _merge_bwd(dom, w_out, z8, ya, yb, yc, name):
    t = dom.shape[0]
    tm = _tile(t, 512)

    def body(dom_ref, w_ref, g0_ref, g1_ref, g2_ref, ya_ref, yb_ref, yc_ref, dya_ref, dyb_ref, dyc_ref, dzg_ref,
             db_ref):
        @pl.when(pl.program_id(0) == 0)
        def _():
            db_ref[...] = jnp.zeros_like(db_ref)

        dm = _dot_nt(dom_ref[...], w_ref[...])
        for k, (g_ref, y_ref, dy_ref) in enumerate(((g0_ref, ya_ref, dya_ref), (g1_ref, yb_ref, dyb_ref),
                                                    (g2_ref, yc_ref, dyc_ref))):
            sg = _sigmoid(g_ref[...])
            dy_ref[...] = (dm * sg).astype(BF16)
            dzg = dm * y_ref[...].astype(F32) * (sg * (1.0 - sg))
            dzg_ref[k] = dzg.astype(BF16)
            db_ref[k:k + 1, :] += _colsum(dzg)

    blk = pl.BlockSpec((tm, D), lambda i: (i, 0))
    gate = lambda j: pl.BlockSpec((None, tm, D), lambda i: (j, i, 0))
    return pl.pallas_call(
        body, name=name, grid=(t // tm,),
        in_specs=[blk, pl.BlockSpec((D, D), lambda i: (0, 0)), gate(5), gate(6), gate(7), blk, blk, blk],
        out_specs=[blk, blk, blk, pl.BlockSpec((3, tm, D), lambda i: (0, i, 0)), pl.BlockSpec((3, D), lambda i: (0, 0))],
        out_shape=[_sds((t, D), BF16)] * 3 + [_sds((3, t, D), BF16), _sds((3, D), F32)],
        compiler_params=_params("arbitrary"))(dom, w_out, z8, z8, z8, ya, yb, yc)


def _branches_bwd(z8, cv, dsa, dsb, dsc, dzg, ln_g, ln_b, w_s, bs_b, pool_w, pool_scale, conv_w, cln_g, cln_b, name):
    t = z8.shape[1]
    tm = _tile(t, 256)
    n_ext = tm + HALO
    n_tiles = t // tm

    def body(zu_ref, zv_ref, p_ref, a_ref, ag_ref, ph_ref, ah_ref, agh_ref, cv_ref, cvf_ref, dsa_ref, dsb_ref,
             dsbf_ref, dsc_ref, dscf_ref, dzg_ref, lng_ref, lnb_ref, ws_ref, bsb_ref, wp_ref, ps_ref, cw_ref,
             clg_ref, clb_ref, dz_ref, dbin_ref, rows_ref, dws_ref, dbs_ref, dwp_ref, dcw_ref, mixed_scr, dvln_scr):
        i = pl.program_id(0)

        @pl.when(i == 0)
        def _():
            for ref in (dbin_ref, rows_ref, dws_ref, dbs_ref, dwp_ref, dcw_ref):
                ref[...] = jnp.zeros_like(ref)

        has_past = (i > 0).astype(F32)
        has_next = (i < n_tiles - 1).astype(F32)

        def emit(j, val):
            dz_ref[j] = val.astype(BF16)
            dbin_ref[j:j + 1, :] += _colsum(val)

        zu, zv = zu_ref[...], zv_ref[...]
        u = _gelu(zu)
        vhat, v_rstd = _ln_stats(_gelu(zv))
        vb = (vhat * lng_ref[...] + lnb_ref[...]).astype(BF16)
        dsa = dsa_ref[...]
        dmixed = dsa * u
        dmb = dmixed.astype(BF16)
        mask = _tril_mask()
        lane = lax.broadcasted_iota(jnp.int32, (CHUNK, CHUNK), 1)
        for g in range(SGU_G):
            cols = slice(g * CHUNK, (g + 1) * CHUNK)
            wm = (ws_ref[g] * mask).astype(BF16)
            dws = jnp.zeros((CHUNK, CHUNK), F32)
            dbs = jnp.zeros((CHUNK, 1), F32)
            for n in range(tm // CHUNK):
                rows = slice(n * CHUNK, (n + 1) * CHUNK)
                mixed_scr[rows, cols] = _dot(wm, vb[rows, cols]) + bsb_ref[g]
                dvln_scr[rows, cols] = _dot_tn(wm, dmb[rows, cols])
                dws = dws + _dot_nt(dmb[rows, cols], vb[rows, cols])
                dbs = dbs + jnp.sum(dmixed[rows, cols], axis=1, keepdims=True)
            dws_ref[g] += dws
            dbs_ref[...] += jnp.where(lane == g, dbs, 0.0)
        emit(0, dsa * mixed_scr[...] * _gelu_grad(zu))
        dvln = dvln_scr[...]
        rows_ref[0:1, :] += _colsum(dvln * vhat)
        rows_ref[1:2, :] += _colsum(dvln)
        emit(1, _ln_bwd(dvln * lng_ref[...], vhat, v_rstd) * _gelu_grad(zv))

        p_ext = jnp.concatenate([ph_ref[...] * has_past, p_ref[...]], axis=0)
        pooled = _pool_forward(p_ext, i * tm, tm)
        dsb = dsb_ref[...]
        dpl_ext = jnp.concatenate([dsb, dsbf_ref[...] * has_next], axis=0) * ps_ref[...]
        t_ext = i * tm + lax.broadcasted_iota(jnp.int32, (n_ext, 1), 0)
        dp_parts = []
        for gi, win in enumerate(POOL_WINDOWS):
            cols = slice(gi * POOL_GC, (gi + 1) * POOL_GC)
            pooled_b = pooled[gi].astype(BF16)
            wpb = wp_ref[gi].astype(BF16)
            rows_ref[2:3, cols] += _colsum(dsb[:, cols] * _dot(pooled_b, wpb))
            dplb = dpl_ext[:, cols].astype(BF16)
            dwp_ref[gi] += _dot_tn(pooled_b, dplb[:tm])
            dpooled = _dot_nt(dplb, wpb)
            s, sh = dpooled / jnp.minimum(t_ext + 1, win).astype(F32), 1
            while sh < win:
                s = s + pltpu.roll(s, n_ext - sh, 0)
                sh *= 2
            dp_parts.append(s[:tm] - dpooled[:tm])
        emit(2, jnp.concatenate(dp_parts, axis=1))

        cv_ext = jnp.concatenate([cv_ref[...], cvf_ref[...]], axis=0)
        chat, c_rstd = _ln_stats(cv_ext)
        cl = chat * clg_ref[...] + clb_ref[...]
        sg = _sigmoid(cl)
        dcl = jnp.concatenate([dsc_ref[...], dscf_ref[...]], axis=0) * (sg * (1.0 + cl * (1.0 - sg)))
        rows_ref[4:5, :] += _colsum((dcl * chat)[:tm])
        rows_ref[5:6, :] += _colsum(dcl[:tm])
        in_seq = jnp.concatenate([jnp.ones((tm, 1), F32), jnp.zeros((HALO, 1), F32) + has_next], axis=0)
        dcv = jnp.where(in_seq > 0.0, _ln_bwd(dcl * clg_ref[...], chat, c_rstd), 0.0)
        dcv_t = dcv[:tm]
        rows_ref[3:4, :] += _colsum(dcv_t)
        a_t, ag_t = a_ref[...], ag_ref[...]
        sga_t = _sigmoid(ag_t)
        zc = jnp.concatenate([ah_ref[...] * has_past * _sigmoid(agh_ref[...]), a_t * sga_t], axis=0)

        def tap(k, dzc):
            shift = CONV_K - 1 - k
            dcw_ref[pl.ds(k, 1), :] += _colsum(dcv_t * pltpu.roll(zc, shift, 0)[HALO:])
            return dzc + cw_ref[pl.ds(k, 1), :] * pltpu.roll(dcv, (n_ext - shift) % n_ext, 0)[:tm]

        dzc = lax.fori_loop(0, CONV_K, tap, jnp.zeros((tm, D), F32))
        emit(3, dzc * sga_t)
        emit(4, dzc * a_t * (sga_t * (1.0 - sga_t)))
        for k in range(3):
            dz_ref[5 + k] = dzg_ref[k]

    def col(j):
        return pl.BlockSpec((None, tm, D), lambda i: (j, i, 0))

    blk = pl.BlockSpec((tm, D), lambda i: (i, 0))
    after = pl.BlockSpec((HALO, D), lambda i: (jnp.minimum((i + 1) * (tm // HALO), t // HALO - 1), 0))
    row = pl.BlockSpec((1, D), lambda i: (0, 0))
    full2 = lambda s: pl.BlockSpec(s, lambda i: (0, 0))
    full3 = lambda s: pl.BlockSpec(s, lambda i: (0, 0, 0))
    return pl.pallas_call(
        body, name=name, grid=(n_tiles,),
        in_specs=[col(0), col(1), col(2), col(3), col(4), _halo_before(tm, 2), _halo_before(tm, 3),
                  _halo_before(tm, 4), blk, after, blk, blk, after, blk, after,
                  pl.BlockSpec((3, tm, D), lambda i: (0, i, 0)), row, row, full3((SGU_G, CHUNK, CHUNK)),
                  full3((SGU_G, CHUNK, CHUNK)), full3((4, POOL_GC, POOL_GC)), row, full2((HALO, D)), row, row],
        out_specs=[pl.BlockSpec((8, tm, D), lambda i: (0, i, 0)), full2((8, D)), full2((8, D)),
                   full3((SGU_G, CHUNK, CHUNK)), full2((CHUNK, CHUNK)), full3((4, POOL_GC, POOL_GC)),
                   full2((HALO, D))],
        out_shape=[_sds((8, t, D), BF16), _sds((8, D), F32), _sds((8, D), F32), _sds((SGU_G, CHUNK, CHUNK), F32),
                   _sds((CHUNK, CHUNK), F32), _sds((4, POOL_GC, POOL_GC), F32), _sds((HALO, D), F32)],
        scratch_shapes=[pltpu.VMEM((tm, D), F32), pltpu.VMEM((tm, D), F32)],
        compiler_params=_params("arbitrary"),
    )(z8, z8, z8, z8, z8, z8, z8, z8, cv, cv, dsa, dsb, dsb, dsc, dsc, dzg, ln_g, ln_b, w_s, bs_b, pool_w,
      pool_scale, conv_w, cln_g, cln_b)


def _ada_fwd(c_all, w_ada, b_loc, name):
    def body(c_ref, w_ref, b_ref, o_ref):
        cv = c_ref[...]
        ca = (cv * _sigmoid(cv)).astype(BF16)
        for l in range(DEPTH):
            o_ref[l] = _dot(ca, w_ref[l].astype(BF16)) + b_ref[l]

    return pl.pallas_call(body, name=name, out_shape=_sds((DEPTH, N_DEV, ADA_BLK), F32),
                          compiler_params=_params())(c_all, w_ada, b_loc)


def _ada_bwd(c_all_t, d_loc, name):
    def body(c_ref, d_ref, o_ref):
        cv = c_ref[...]
        ca = cv * _sigmoid(cv)
        for l in range(DEPTH):
            acc = jnp.zeros((D, ADA_BLK), F32)
            for j in range(N_DEV):
                acc = acc + ca[:, j:j + 1] * d_ref[l, j:j + 1, :]
            o_ref[l] = acc

    return pl.pallas_call(body, name=name, out_shape=_sds((DEPTH, D, ADA_BLK), F32),
                          compiler_params=_params())(c_all_t, d_loc)


def _sum8(g8, name):
    _, rows, cols = g8.shape
    tr = _tile(rows, 88)

    def body(g_ref, o_ref):
        acc = g_ref[0]
        for k in range(1, N_DEV):
            acc = acc + g_ref[k]
        o_ref[...] = acc

    return pl.pallas_call(body, name=name, grid=(rows // tr,),
                          in_specs=[pl.BlockSpec((N_DEV, tr, cols), lambda r: (0, r, 0))],
                          out_specs=pl.BlockSpec((tr, cols), lambda r: (r, 0)), out_shape=_sds((rows, cols), F32),
                          compiler_params=_params("arbitrary"))(g8)


REPLICATED = ("b_ada", "g_mix", "b_in", "sgu_ln_g", "sgu_ln_b", "sgu_w_s", "sgu_b_s", "pool_scale", "conv_b",
              "conv_ln_g", "conv_ln_b", "g_ffn", "g_final")
WEIGHT_ORDER = ("w_ada", "b_ada", "g_mix", "w_in", "b_in", "sgu_ln_g", "sgu_ln_b", "sgu_w_s", "sgu_b_s", "w_pa",
                "pool_w", "pool_scale", "w_pb", "conv_w", "conv_b", "conv_ln_g", "conv_ln_b", "w_pc", "w_out",
                "g_ffn", "w_ffn_in", "w_ffn_out", "g_final")


def _rows(a):
    return a.reshape(-1, D)


def kernel(x, c, w_ada, b_ada, g_mix, w_in, b_in, sgu_ln_g, sgu_ln_b, sgu_w_s, sgu_b_s, w_pa, pool_w, pool_scale, w_pb, conv_w, conv_b, conv_ln_g, conv_ln_b, w_pc, w_out, g_ffn, w_ffn_in, w_ffn_out, g_final, loss_target, m_w_ada, m_b_ada, m_g_mix, m_w_in, m_b_in, m_sgu_ln_g, m_sgu_ln_b, m_sgu_w_s, m_sgu_b_s, m_w_pa, m_pool_w, m_pool_scale, m_w_pb, m_conv_w, m_conv_b, m_conv_ln_g, m_conv_ln_b, m_w_pc, m_w_out, m_g_ffn, m_w_ffn_in, m_w_ffn_out, m_g_final, v_w_ada, v_b_ada, v_g_mix, v_w_in, v_b_in, v_sgu_ln_g, v_sgu_ln_b, v_sgu_w_s, v_sgu_b_s, v_w_pa, v_pool_w, v_pool_scale, v_w_pb, v_conv_w, v_conv_b, v_conv_ln_g, v_conv_ln_b, v_w_pc, v_w_out, v_g_ffn, v_w_ffn_in, v_w_ffn_out, v_g_final):
    weights = dict(w_ada=w_ada, b_ada=b_ada, g_mix=g_mix, w_in=w_in, b_in=b_in, sgu_ln_g=sgu_ln_g, sgu_ln_b=sgu_ln_b,
                   sgu_w_s=sgu_w_s, sgu_b_s=sgu_b_s, w_pa=w_pa, pool_w=pool_w, pool_scale=pool_scale, w_pb=w_pb,
                   conv_w=conv_w, conv_b=conv_b, conv_ln_g=conv_ln_g, conv_ln_b=conv_ln_b, w_pc=w_pc, w_out=w_out,
                   g_ffn=g_ffn, w_ffn_in=w_ffn_in, w_ffn_out=w_ffn_out, g_final=g_final)
    mom1 = dict(w_ada=m_w_ada, b_ada=m_b_ada, g_mix=m_g_mix, w_in=m_w_in, b_in=m_b_in, sgu_ln_g=m_sgu_ln_g,
                sgu_ln_b=m_sgu_ln_b, sgu_w_s=m_sgu_w_s, sgu_b_s=m_sgu_b_s, w_pa=m_w_pa, pool_w=m_pool_w,
                pool_scale=m_pool_scale, w_pb=m_w_pb, conv_w=m_conv_w, conv_b=m_conv_b, conv_ln_g=m_conv_ln_g,
                conv_ln_b=m_conv_ln_b, w_pc=m_w_pc, w_out=m_w_out, g_ffn=m_g_ffn, w_ffn_in=m_w_ffn_in,
                w_ffn_out=m_w_ffn_out, g_final=m_g_final)
    mom2 = dict(w_ada=v_w_ada, b_ada=v_b_ada, g_mix=v_g_mix, w_in=v_w_in, b_in=v_b_in, sgu_ln_g=v_sgu_ln_g,
                sgu_ln_b=v_sgu_ln_b, sgu_w_s=v_sgu_w_s, sgu_b_s=v_sgu_b_s, w_pa=v_w_pa, pool_w=v_pool_w,
                pool_scale=v_pool_scale, w_pb=v_w_pb, conv_w=v_conv_w, conv_b=v_conv_b, conv_ln_g=v_conv_ln_g,
                conv_ln_b=v_conv_ln_b, w_pc=v_w_pc, w_out=v_w_out, g_ffn=v_g_ffn, w_ffn_in=v_w_ffn_in,
                w_ffn_out=v_w_ffn_out, g_final=v_g_final)

    t = x.shape[1]
    xs = x.reshape(t, D)
    target = loss_target.reshape(t, D)
    me = 4 * lax.axis_index("x") + 2 * lax.axis_index("y") + lax.axis_index("c")
    core = lax.axis_index("c").astype(jnp.int32).reshape(1)

    big = ("w_in", "w_pa", "w_pb", "w_pc", "w_out", "w_ffn_in", "w_ffn_out")
    shards = [weights[n][l].astype(BF16) for l in range(DEPTH) for n in big]
    gathered = _all_gather(shards + [c, pool_w, conv_w], name="gather_weights")
    full = [dict(zip(big, gathered[l * len(big):(l + 1) * len(big)])) for l in range(DEPTH)]
    c_all, pool_all, conv_all = gathered[DEPTH * len(big):]
    c_all = c_all.reshape(N_DEV, D)
    pool_full = jnp.transpose(pool_all, (1, 2, 0, 3, 4)).reshape(DEPTH, 4, POOL_GC, POOL_GC)
    conv_full = jnp.transpose(conv_all, (1, 2, 0, 3)).reshape(DEPTH, CONV_K, D)
    conv_full = jnp.pad(conv_full, ((0, 0), (0, HALO - CONV_K), (0, 0)))

    b_loc = lax.dynamic_slice_in_dim(b_ada, me * ADA_BLK, ADA_BLK, axis=1).reshape(DEPTH, 1, ADA_BLK)
    ada_part = _ada_fwd(c_all, w_ada, b_loc, name="ada_fwd")
    (ada_all,) = _all_gather([ada_part], name="gather_ada")
    ada = lax.dynamic_index_in_dim(ada_all, me, axis=2, keepdims=False)
    ada = jnp.transpose(ada, (1, 0, 2)).reshape(DEPTH, 6, 1, D)

    bs_b = jnp.broadcast_to(sgu_b_s[..., None], (DEPTH, SGU_G, CHUNK, CHUNK))

    saved = []
    xl = xs
    for l in range(DEPTH):
        w = full[l]
        sh_m, sc_m, gt_m, sh_f, sc_f, gt_f = (ada[l, k] for k in range(6))
        row = lambda a: a[l].reshape(1, D)
        h = _norm_mod(xl, row(g_mix), sc_m, sh_m, name=f"norm_mix_{l}")
        z8 = _mm_cols(h, w["w_in"], b_in[l].reshape(8, 1, D), name=f"in_proj_{l}")
        sa, sb, sc, cv = _branches_fwd(z8, row(sgu_ln_g), row(sgu_ln_b), sgu_w_s[l], bs_b[l], pool_full[l],
                                       row(pool_scale), conv_full[l], row(conv_b), row(conv_ln_g), row(conv_ln_b),
                                       name=f"branches_{l}")
        wpa, wpb, wpc, wout = (w[n].reshape(D, D) for n in ("w_pa", "w_pb", "w_pc", "w_out"))
        ya, yb, yc, merged = _proj_merge(sa, sb, sc, wpa, wpb, wpc, z8, name=f"proj_merge_{l}")
        om, x1 = _out_proj(merged, wout, xl, gt_m, name=f"out_proj_{l}")
        h2 = _norm_mod(x1, row(g_ffn), sc_f, sh_f, name=f"norm_ffn_{l}")
        wfi = w["w_ffn_in"].reshape(2, 4, D, FF_BLK)
        wfo4 = w["w_ffn_out"].reshape(4, FF_BLK, D)
        gu, f4 = _ffn_in(h2, wfi, name=f"ffn_in_{l}")
        o, x2 = _ffn_out(f4, wfo4, x1, gt_f, name=f"ffn_out_{l}")
        saved.append(dict(x=xl, h=h, z8=z8, sa=sa, sb=sb, sc=sc, cv=cv, ya=ya, yb=yb, yc=yc, merged=merged, om=om,
                          x1=x1, h2=h2, gu=gu, f4=f4, o=o, wpa=wpa, wpb=wpb, wpc=wpc, wout=wout, wfi=wfi, wfo4=wfo4))
        xl = x2

    loss_tile, dx, dg_final = _final_loss(xl, g_final.reshape(1, D), target, name="final_loss")
    loss = lax.psum(loss_tile[0, 0], AXES)

    big_grads = [None] * DEPTH
    small = [None] * DEPTH
    d_ada = [None] * DEPTH
    for l in reversed(range(DEPTH)):
        s, w = saved[l], full[l]
        sh_m, sc_m, gt_m, sh_f, sc_f, gt_f = (ada[l, k] for k in range(6))
        row = lambda a: a[l].reshape(1, D)
        do, dgt_f = _gate_bwd(dx, s["o"], gt_f, name=f"gate_bwd_ffn_{l}")
        dgu = _ffn_bwd_act(do, s["wfo4"], s["gu"], name=f"ffn_bwd_act_{l}")
        d_wfo = _mm_tn(s["f4"], do[None], name=f"dw_ffn_out_{l}")
        dgu8 = dgu.reshape(8, t, FF_BLK)
        dh2 = _mm_nt_sum(dgu8, w["w_ffn_in"], name=f"dh_ffn_{l}")
        d_wfi = _mm_tn(s["h2"][None], dgu8, name=f"dw_ffn_in_{l}")
        dx1, st_f = _norm_mod_bwd(dh2, s["x1"], dx, row(g_ffn), sc_f, name=f"norm_ffn_bwd_{l}")
        dom, dgt_m = _gate_bwd(dx1, s["om"], gt_m, name=f"gate_bwd_mix_{l}")
        dya, dyb, dyc, dzg, db_gate = _merge_bwd(dom, s["wout"], s["z8"], s["ya"], s["yb"], s["yc"],
                                                 name=f"merge_bwd_{l}")
        d_wout = _mm_tn(s["merged"][None], dom[None], name=f"dw_out_{l}")
        d_wpa = _mm_tn(s["sa"][None], dya[None], name=f"dw_pa_{l}")
        d_wpb = _mm_tn(s["sb"][None], dyb[None], name=f"dw_pb_{l}")
        d_wpc = _mm_tn(s["sc"][None], dyc[None], name=f"dw_pc_{l}")
        dsa = _mm_nt_sum(dya[None], s["wpa"][None], name=f"ds_a_{l}")
        dsb = _mm_nt_sum(dyb[None], s["wpb"][None], name=f"ds_b_{l}")
        dsc = _mm_nt_sum(dyc[None], s["wpc"][None], name=f"ds_c_{l}")
        dz8, db_in5, rows6, dws, dbs, dwp, dcw = _branches_bwd(
            s["z8"], s["cv"], dsa, dsb, dsc, dzg, row(sgu_ln_g), row(sgu_ln_b), sgu_w_s[l], bs_b[l], pool_full[l],
            row(pool_scale), conv_full[l], row(conv_ln_g), row(conv_ln_b), name=f"branches_bwd_{l}")
        dh = _mm_nt_sum(dz8, w["w_in"], name=f"dh_in_{l}")
        d_win = _mm_tn(s["h"][None], dz8, name=f"dw_in_{l}")
        dx, st_m = _norm_mod_bwd(dh, s["x"], dx1, row(g_mix), sc_m, name=f"norm_mix_bwd_{l}")

        big_grads[l] = [d_win, d_wpa.reshape(8, D // 8, D), d_wpb.reshape(8, D // 8, D), d_wpc.reshape(8, D // 8, D),
                        d_wout.reshape(8, D // 8, D), d_wfi, d_wfo.reshape(8, D_FF // 8, D)]
        d_ada[l] = jnp.concatenate([st_m[0:1], st_m[1:2], dgt_m, st_f[0:1], st_f[1:2], dgt_f], axis=0)
        tril = jnp.tril(jnp.ones((CHUNK, CHUNK), F32))
        small[l] = dict(
            g_mix=st_m[2:3], b_in=jnp.concatenate([db_in5[0:5], db_gate], axis=0), sgu_ln_g=rows6[0:1],
            sgu_ln_b=rows6[1:2], sgu_w_s=_rows(dws * tril), sgu_b_s=_rows(jnp.transpose(dbs[:, :SGU_G])),
            pool_scale=rows6[2:3], conv_b=rows6[3:4], conv_ln_g=rows6[4:5], conv_ln_b=rows6[5:6], g_ffn=st_f[2:3],
            pool_w=_rows(dwp), conv_w=dcw)

    rs = [_reduce_scatter(big_grads[l], core, tag=str(l)) for l in range(DEPTH)]

    per_layer = ("g_mix", "b_in", "sgu_ln_g", "sgu_ln_b", "sgu_w_s", "sgu_b_s", "pool_scale", "conv_b", "conv_ln_g",
                 "conv_ln_b", "g_ffn", "pool_w", "conv_w")
    segs = [("d_ada", jnp.concatenate(d_ada, axis=0))]
    segs += [(n, jnp.concatenate([small[l][n] for l in range(DEPTH)], axis=0)) for n in per_layer]
    segs += [("g_final", dg_final)]
    n_rows = sum(a.shape[0] for _, a in segs)
    pad = (-n_rows) % 88
    buf = jnp.concatenate([a for _, a in segs] + [jnp.zeros((pad, D), F32)], axis=0)
    (buf_all,) = _all_gather([buf], name="gather_small_grads")
    red = _sum8(buf_all, name="sum_small_grads")
    offs, o = {}, 0
    for n, a in segs:
        offs[n] = (o, a.shape[0])
        o += a.shape[0]
    seg = lambda arr, n: lax.slice_in_dim(arr, offs[n][0], offs[n][0] + offs[n][1], axis=0)

    grads = {n: seg(red, n).reshape(weights[n].shape) for n in REPLICATED if n != "b_ada"}
    grads["b_ada"] = seg(red, "d_ada").reshape(DEPTH, 6 * D)
    pool_g = seg(red, "pool_w").reshape(DEPTH, 4, POOL_GC, POOL_GC)
    grads["pool_w"] = lax.dynamic_slice_in_dim(pool_g, me * (POOL_GC // 8), POOL_GC // 8, axis=2)
    conv_g = seg(red, "conv_w").reshape(DEPTH, HALO, D)[:, :CONV_K]
    grads["conv_w"] = lax.dynamic_slice_in_dim(conv_g, me * (D // 8), D // 8, axis=2)

    d_ada_all = lax.slice_in_dim(buf_all, offs["d_ada"][0], offs["d_ada"][0] + offs["d_ada"][1], axis=1)
    d_ada_all = d_ada_all.reshape(N_DEV, DEPTH, 6 * D)
    d_loc = jnp.transpose(lax.dynamic_slice_in_dim(d_ada_all, me * ADA_BLK, ADA_BLK, axis=2), (1, 0, 2))
    grads["w_ada"] = _ada_bwd(jnp.transpose(c_all), d_loc, name="ada_bwd")

    out = {}
    for n in REPLICATED + ("pool_w", "conv_w", "w_ada"):
        g = grads[n]
        out[n] = _adam_nd(g.reshape((1,) + g.shape), weights[n], mom1[n], mom2[n], name=f"adam_{n}")
    for k, n in enumerate(big):
        per = [_adam(rs[l][k], weights[n][l].reshape(rs[l][k].shape[1:]), mom1[n][l].reshape(rs[l][k].shape[1:]),
                     mom2[n][l].reshape(rs[l][k].shape[1:]), name=f"adam_{n}_{l}") for l in range(DEPTH)]
        out[n] = [jnp.stack([per[l][i] for l in range(DEPTH)], axis=0).reshape(weights[n].shape) for i in range(4)]

    grad_x = dx.reshape(1, t, D)
    return (loss, grad_x, *[out[n][0] for n in WEIGHT_ORDER], *[out[n][1] for n in WEIGHT_ORDER],
            *[out[n][2] for n in WEIGHT_ORDER], *[out[n][3] for n in WEIGHT_ORDER])
```

```python
import functools
import math

import jax
import jax.numpy as jnp
from jax import lax
from jax.experimental import pallas as pl
from jax.experimental.pallas import tpu as pltpu

F32 = jnp.float32
BF16 = jnp.bfloat16
MESH = pl.DeviceIdType.MESH
AXES = ("x", "y", "c")
N_DEV = 8

D = 1024
DEPTH = 2
EPS = 1e-6
CHUNK = 128
SGU_G = 8
POOL_WINDOWS = (2, 4, 8, 16)
POOL_GC = 256
CONV_K = 31
HALO = 32
SUBLANE = 8
LANE = 128
CONV_STRIP = 128
D_FF = 2816
FF_BLK = D_FF // 4
ADA_BLK = 6 * D // N_DEV

ADAM_LR = 0.001
ADAM_B1 = 0.9
ADAM_B2 = 0.999
ADAM_EPS = 1e-08
ADAM_WD = 0.01
ADAM_STEP = 10

VMEM_LIMIT_V7X = 56 * 1024 * 1024
INV_SQRT2 = 1.0 / math.sqrt(2.0)
INV_SQRT_2PI = 1.0 / math.sqrt(2.0 * math.pi)


def _params(*sem):
    return pltpu.CompilerParams(dimension_semantics=sem if sem else None, vmem_limit_bytes=VMEM_LIMIT_V7X)


def _tile(n, pref):
    if n <= pref:
        return n
    for t in range(pref - pref % 8, 0, -8):
        if n % t == 0:
            return t
    raise ValueError((n, pref))


def _sds(shape, dtype):
    return jax.ShapeDtypeStruct(shape, dtype)


def _sigmoid(x):
    return 1.0 / (1.0 + jnp.exp(-x))


def _gelu(x):
    return 0.5 * x * (1.0 + lax.erf(x * INV_SQRT2))


def _gelu_grad(x):
    return 0.5 * (1.0 + lax.erf(x * INV_SQRT2)) + x * (INV_SQRT_2PI * jnp.exp(-0.5 * x * x))


def _ln_stats(v):
    mu = jnp.mean(v, axis=-1, keepdims=True)
    vc = v - mu
    rstd = lax.rsqrt(jnp.mean(vc * vc, axis=-1, keepdims=True) + EPS)
    return vc * rstd, rstd


def _ln_bwd(dvhat, vhat, rstd):
    return rstd * (dvhat - jnp.mean(dvhat, axis=-1, keepdims=True)
                   - vhat * jnp.mean(dvhat * vhat, axis=-1, keepdims=True))


def _colsum(v):
    return jnp.sum(v, axis=0, keepdims=True)


def _dot(a, b):
    return jnp.dot(a, b, preferred_element_type=F32)


def _dot_nt(a, b):
    return lax.dot_general(a, b, (((1,), (1,)), ((), ())), preferred_element_type=F32)


def _dot_tn(a, b):
    return lax.dot_general(a, b, (((0,), (0,)), ((), ())), preferred_element_type=F32)


def _tril_mask():
    r = lax.broadcasted_iota(jnp.int32, (CHUNK, CHUNK), 0)
    c = lax.broadcasted_iota(jnp.int32, (CHUNK, CHUNK), 1)
    return (r >= c).astype(F32)


def _mesh_pos():
    return tuple(lax.axis_index(a) for a in AXES)


def _all_gather(arrs, name):
    n = len(arrs)

    def body(*refs):
        ins, outs = refs[:n], refs[n:2 * n]
        send_sems, recv_sems, local_sems = refs[2 * n:]
        x, y, c = _mesh_pos()
        me, sibling = (x, y, c), (x, y, 1 - c)
        chips = [(1 - x, y), (x, 1 - y), (1 - x, 1 - y)]

        def slot(a, p):
            return outs[a].at[4 * p[0] + 2 * p[1] + p[2]]

        def copy(a, k, block, to, src=None):
            dst = slot(a, block)
            return pltpu.make_async_remote_copy(
                src_ref=dst if src is None else src, dst_ref=dst, send_sem=send_sems.at[a, k],
                recv_sem=recv_sems.at[a, k], device_id=to, device_id_type=MESH)

        mine = [pltpu.make_async_copy(ins[a], slot(a, me), local_sems.at[a]) for a in range(n)]
        for m in mine:
            m.start()
        first = []
        for a in range(n):
            first.append(copy(a, 0, me, sibling, src=ins[a]))
            first += [copy(a, 1 + j, me, (*chip, c), src=ins[a]) for j, chip in enumerate(chips)]
        for cp in first:
            cp.start()
        passed = []
        for j, chip in enumerate(chips):
            for a in range(n):
                copy(a, 1 + j, (*chip, c), me).wait_recv()
                fwd = copy(a, 4 + j, (*chip, c), sibling)
                fwd.start()
                passed.append(fwd)
        for a in range(n):
            copy(a, 0, sibling, me).wait_recv()
            for j, chip in enumerate(chips):
                copy(a, 4 + j, (*chip, 1 - c), me).wait_recv()
        for cp in first + passed:
            cp.wait_send()
        for m in mine:
            m.wait()

    any_spec = pl.BlockSpec(memory_space=pl.ANY)
    return pl.pallas_call(
        body, name=name,
        out_shape=[_sds((N_DEV,) + a.shape, a.dtype) for a in arrs],
        in_specs=[any_spec] * n, out_specs=[any_spec] * n,
        scratch_shapes=[pltpu.SemaphoreType.DMA((n, 7)), pltpu.SemaphoreType.DMA((n, 7)),
                        pltpu.SemaphoreType.DMA((n,))],
    )(*arrs)


def _exchange_sibling(arrs, name):
    n = len(arrs)

    def body(*refs):
        ins, outs = refs[:n], refs[n:2 * n]
        send_sems, recv_sems = refs[2 * n:]
        x, y, c = _mesh_pos()
        copies = []
        for a in range(n):
            for q in range(4):
                copies.append(pltpu.make_async_remote_copy(
                    src_ref=ins[a].at[2 * q + (1 - c)], dst_ref=outs[a].at[q], send_sem=send_sems.at[a, q],
                    recv_sem=recv_sems.at[a, q], device_id=(x, y, 1 - c), device_id_type=MESH))
        for cp in copies:
            cp.start()
        for cp in copies:
            cp.wait()

    any_spec = pl.BlockSpec(memory_space=pl.ANY)
    return pl.pallas_call(
        body, name=name,
        out_shape=[_sds((4,) + a.shape[1:], a.dtype) for a in arrs],
        in_specs=[any_spec] * n, out_specs=[any_spec] * n,
        scratch_shapes=[pltpu.SemaphoreType.DMA((n, 4)), pltpu.SemaphoreType.DMA((n, 4))],
    )(*arrs)


def _exchange_chips(arrs, name):
    n = len(arrs)

    def body(*refs):
        ins, outs = refs[:n], refs[n:2 * n]
        send_sems, recv_sems, local_sems = refs[2 * n:]
        x, y, c = _mesh_pos()
        q_me = 2 * x + y
        chips = [(1 - x, y), (x, 1 - y), (1 - x, 1 - y)]
        own = [pltpu.make_async_copy(ins[a].at[q_me], outs[a].at[q_me], local_sems.at[a]) for a in range(n)]
        for cp in own:
            cp.start()
        copies = []
        for a in range(n):
            for j, chip in enumerate(chips):
                copies.append(pltpu.make_async_remote_copy(
                    src_ref=ins[a].at[2 * chip[0] + chip[1]], dst_ref=outs[a].at[q_me],
                    send_sem=send_sems.at[a, j], recv_sem=recv_sems.at[a, j],
                    device_id=(*chip, c), device_id_type=MESH))
        for cp in copies:
            cp.start()
        for cp in copies:
            cp.wait()
        for cp in own:
            cp.wait()

    any_spec = pl.BlockSpec(memory_space=pl.ANY)
    return pl.pallas_call(
        body, name=name,
        out_shape=[_sds(a.shape, a.dtype) for a in arrs],
        in_specs=[any_spec] * n, out_specs=[any_spec] * n,
        scratch_shapes=[pltpu.SemaphoreType.DMA((n, 3)), pltpu.SemaphoreType.DMA((n, 3)),
                        pltpu.SemaphoreType.DMA((n,))],
    )(*arrs)


def _sibling_sum(arr, land, core, name):
    _, rows, cols = arr.shape
    tr = _tile(rows, 512)
    arr4 = arr.reshape(4, 2, rows, cols)

    def body(c_ref, a_ref, l_ref, o_ref):
        o_ref[...] = (a_ref[...] + l_ref[...]).astype(BF16)

    grid_spec = pltpu.PrefetchScalarGridSpec(
        num_scalar_prefetch=1, grid=(4, rows // tr),
        in_specs=[pl.BlockSpec((None, None, tr, cols), lambda q, r, c_ref: (q, c_ref[0], r, 0)),
                  pl.BlockSpec((None, tr, cols), lambda q, r, c_ref: (q, r, 0))],
        out_specs=pl.BlockSpec((None, tr, cols), lambda q, r, c_ref: (q, r, 0)))
    return pl.pallas_call(body, name=name, grid_spec=grid_spec, out_shape=_sds((4, rows, cols), BF16),
                          compiler_params=_params("arbitrary", "arbitrary"))(core, arr4, land)


def _reduce_scatter(arrs, core, tag):
    land = _exchange_sibling(arrs, name=f"rs_sibling_{tag}")
    part = [_sibling_sum(a, l, core, name=f"rs_sum_{tag}_{k}") for k, (a, l) in enumerate(zip(arrs, land))]
    return _exchange_chips(part, name=f"rs_chips_{tag}")


def _adam(gparts, w, m, v, name):
    n_l = len(gparts)
    p, rows, cols = gparts[0].shape
    tr = _tile(rows, 256)
    n_r = rows // tr
    c1 = 1.0 - ADAM_B1 ** ADAM_STEP
    c2 = 1.0 - ADAM_B2 ** ADAM_STEP

    def body(*refs):
        g_refs = refs[:n_l]
        w_ref, m_ref, v_ref, go_ref, d_ref, mo_ref, vo_ref = refs[n_l:]
        layer = pl.program_id(0)
        g = jnp.zeros((tr, cols), F32)
        for li, g_ref in enumerate(g_refs):
            gl = g_ref[0].astype(F32)
            for k in range(1, p):
                gl = gl + g_ref[k].astype(F32)
            g = gl if n_l == 1 else jnp.where(layer == li, gl, g)
        m_new = ADAM_B1 * m_ref[...] + (1.0 - ADAM_B1) * g
        v_new = ADAM_B2 * v_ref[...] + (1.0 - ADAM_B2) * (g * g)
        m_hat = m_new / c1
        v_hat = v_new / c2
        go_ref[...] = g
        d_ref[...] = -ADAM_LR * (m_hat / (jnp.sqrt(v_hat) + ADAM_EPS) + ADAM_WD * w_ref[...])
        mo_ref[...] = m_new
        vo_ref[...] = v_new

    def g_spec(li):
        def index(l, r):
            return (0, jnp.where(l == li, r, jnp.where(l < li, 0, n_r - 1)), 0)
        return pl.BlockSpec((p, tr, cols), index)

    blk = pl.BlockSpec((None, tr, cols), lambda l, r: (l, r, 0))
    return pl.pallas_call(
        body, name=name, grid=(n_l, n_r),
        in_specs=[g_spec(li) for li in range(n_l)] + [blk, blk, blk],
        out_specs=[blk] * 4, out_shape=[_sds((n_l, rows, cols), F32)] * 4,
        compiler_params=_params("arbitrary", "arbitrary"))(*gparts, w, m, v)


def _adam_nd(grad, w, m, v, name):
    shape = w.shape
    cols = shape[-1]
    rows = w.size // cols
    as_rows = lambda a: a.reshape(1, rows, cols)
    out = _adam([as_rows(grad)], as_rows(w), as_rows(m), as_rows(v), name)
    return [o.reshape(shape) for o in out]


def _norm_mod(x, g, sc, sh, name):
    t = x.shape[0]
    tm = _tile(t, 512)

    def body(x_ref, g_ref, sc_ref, sh_ref, h_ref):
        xv = x_ref[...]
        r = lax.rsqrt(jnp.mean(xv * xv, axis=-1, keepdims=True) + EPS)
        h_ref[...] = (xv * r * g_ref[...] * (1.0 + sc_ref[...]) + sh_ref[...]).astype(BF16)

    row = pl.BlockSpec((1, D), lambda i: (0, 0))
    blk = pl.BlockSpec((tm, D), lambda i: (i, 0))
    return pl.pallas_call(body, name=name, grid=(t // tm,), in_specs=[blk, row, row, row], out_specs=blk,
                          out_shape=_sds((t, D), BF16), compiler_params=_params("arbitrary"))(x, g, sc, sh)


def _mm_cols(a, b8, bias8, name):
    t, k = a.shape
    j, _, n = b8.shape
    tm = _tile(t, 1024)

    def body(a_ref, b_ref, bias_ref, o_ref):
        o_ref[...] = _dot(a_ref[...], b_ref[...]) + bias_ref[...]

    return pl.pallas_call(
        body, name=name, grid=(j, t // tm),
        in_specs=[pl.BlockSpec((tm, k), lambda jj, i: (i, 0)),
                  pl.BlockSpec((None, k, n), lambda jj, i: (jj, 0, 0)),
                  pl.BlockSpec((None, 1, n), lambda jj, i: (jj, 0, 0))],
        out_specs=pl.BlockSpec((None, tm, n), lambda jj, i: (jj, i, 0)),
        out_shape=_sds((j, t, n), F32), compiler_params=_params("arbitrary", "arbitrary"))(a, b8, bias8)


def _halo_before(tm, col):
    return pl.BlockSpec((None, HALO, D), lambda i: (col, jnp.maximum(i * (tm // HALO) - 1, 0), 0))


def _pool_forward(p_ext, t0, rows):
    t = t0 + lax.broadcasted_iota(jnp.int32, (rows, 1), 0)
    out = []
    for gi, win in enumerate(POOL_WINDOWS):
        e = p_ext[:, gi * POOL_GC:(gi + 1) * POOL_GC]
        s, sh = e, 1
        while sh < win:
            s = s + pltpu.roll(s, sh, 0)
            sh *= 2
        cnt = jnp.minimum(t + 1, win).astype(F32)
        out.append(s[HALO:] / cnt - e[HALO:])
    return out


def _fill_shift_bank(bank_ref, ext, causal):
    n = ext.shape[0]
    bank_ref[0] = ext
    for b in range(1, SUBLANE):
        bank_ref[b] = pltpu.roll(ext, b if causal else n - b, 0)


def _branches_fwd(z8, ln_g, ln_b, w_s, bs_b, pool_w, pool_scale, conv_w, conv_b, cln_g, cln_b, name):
    t = z8.shape[1]
    tm = _tile(t, 256)
    n_ext = tm + HALO

    def body(zu_ref, zv_ref, p_ref, a_ref, ag_ref, ph_ref, ah_ref, agh_ref, lng_ref, lnb_ref, ws_ref, bsb_ref,
             wp_ref, ps_ref, cw_ref, cb_ref, clg_ref, clb_ref, sa_ref, sb_ref, sc_ref, cv_ref, bank_ref):
        i = pl.program_id(0)
        has_past = (i > 0).astype(F32)
        u = _gelu(zu_ref[...])
        vhat, _ = _ln_stats(_gelu(zv_ref[...]))
        vb = (vhat * lng_ref[...] + lnb_ref[...]).astype(BF16)
        mask = _tril_mask()
        for g in range(SGU_G):
            cols = slice(g * CHUNK, (g + 1) * CHUNK)
            wm = (ws_ref[g] * mask).astype(BF16)
            for n in range(tm // CHUNK):
                rows = slice(n * CHUNK, (n + 1) * CHUNK)
                mixed = _dot(wm, vb[rows, cols]) + bsb_ref[g]
                sa_ref[rows, cols] = (u[rows, cols] * mixed).astype(BF16)
        p_ext = jnp.concatenate([ph_ref[...] * has_past, p_ref[...]], axis=0)
        pooled = _pool_forward(p_ext, i * tm, tm)
        for gi in range(len(POOL_WINDOWS)):
            cols = slice(gi * POOL_GC, (gi + 1) * POOL_GC)
            y = _dot(pooled[gi].astype(BF16), wp_ref[gi].astype(BF16))
            sb_ref[:, cols] = (y * ps_ref[:, cols]).astype(BF16)
        for cb in range(D // LANE):
            cols = slice(cb * LANE, (cb + 1) * LANE)
            zc = jnp.concatenate([ah_ref[:, cols] * has_past * _sigmoid(agh_ref[:, cols]),
                                  a_ref[:, cols] * _sigmoid(ag_ref[:, cols])], axis=0)
            _fill_shift_bank(bank_ref, zc, causal=True)
            for r0 in range(0, tm, CONV_STRIP):
                acc = jnp.zeros((CONV_STRIP, LANE), F32) + cb_ref[:, cols]
                for k in range(CONV_K):
                    hi, lo = divmod(CONV_K - 1 - k, SUBLANE)
                    acc = acc + cw_ref[k:k + 1, cols] * bank_ref[lo, pl.ds(HALO - SUBLANE * hi + r0, CONV_STRIP), :]
                cv_ref[r0:r0 + CONV_STRIP, cols] = acc
        cv = cv_ref[...]
        chat, _ = _ln_stats(cv)
        cl = chat * clg_ref[...] + clb_ref[...]
        sc_ref[...] = (cl * _sigmoid(cl)).astype(BF16)

    def col(j):
        return pl.BlockSpec((None, tm, D), lambda i: (j, i, 0))

    row = pl.BlockSpec((1, D), lambda i: (0, 0))
    full3 = lambda s: pl.BlockSpec(s, lambda i: (0, 0, 0))
    blk = pl.BlockSpec((tm, D), lambda i: (i, 0))
    return pl.pallas_call(
        body, name=name, grid=(t // tm,),
        in_specs=[col(0), col(1), col(2), col(3), col(4), _halo_before(tm, 2), _halo_before(tm, 3),
                  _halo_before(tm, 4), row, row, full3((SGU_G, CHUNK, CHUNK)), full3((SGU_G, CHUNK, CHUNK)),
                  full3((4, POOL_GC, POOL_GC)), row, pl.BlockSpec((HALO, D), lambda i: (0, 0)), row, row, row],
        out_specs=[blk, blk, blk, blk],
        out_shape=[_sds((t, D), BF16)] * 3 + [_sds((t, D), F32)],
        scratch_shapes=[pltpu.VMEM((SUBLANE, n_ext, LANE), F32)],
        compiler_params=_params("arbitrary"),
    )(z8, z8, z8, z8, z8, z8, z8, z8, ln_g, ln_b, w_s, bs_b, pool_w, pool_scale, conv_w, conv_b, cln_g, cln_b)


def _proj_merge(sa, sb, sc, w_pa, w_pb, w_pc, z8, name):
    t = sa.shape[0]
    tm = _tile(t, 512)

    def body(sa_ref, sb_ref, sc_ref, wa_ref, wb_ref, wc_ref, g0_ref, g1_ref, g2_ref, ya_ref, yb_ref, yc_ref, m_ref):
        merged = jnp.zeros((tm, D), F32)
        for s_ref, w_ref, g_ref, y_ref in ((sa_ref, wa_ref, g0_ref, ya_ref), (sb_ref, wb_ref, g1_ref, yb_ref),
                                           (sc_ref, wc_ref, g2_ref, yc_ref)):
            y = _dot(s_ref[...], w_ref[...])
            y_ref[...] = y.astype(BF16)
            merged = merged + _sigmoid(g_ref[...]) * y
        m_ref[...] = merged.astype(BF16)

    blk = pl.BlockSpec((tm, D), lambda i: (i, 0))
    wspec = pl.BlockSpec((D, D), lambda i: (0, 0))
    gate = lambda j: pl.BlockSpec((None, tm, D), lambda i: (j, i, 0))
    return pl.pallas_call(
        body, name=name, grid=(t // tm,),
        in_specs=[blk, blk, blk, wspec, wspec, wspec, gate(5), gate(6), gate(7)],
        out_specs=[blk] * 4, out_shape=[_sds((t, D), BF16)] * 4,
        compiler_params=_params("arbitrary"))(sa, sb, sc, w_pa, w_pb, w_pc, z8, z8, z8)


def _out_proj(merged, w_out, x, gt, name):
    t = x.shape[0]
    tm = _tile(t, 512)

    def body(m_ref, w_ref, x_ref, gt_ref, om_ref, x1_ref):
        om = _dot(m_ref[...], w_ref[...])
        om_ref[...] = om
        x1_ref[...] = x_ref[...] + gt_ref[...] * om

    blk = pl.BlockSpec((tm, D), lambda i: (i, 0))
    return pl.pallas_call(
        body, name=name, grid=(t // tm,),
        in_specs=[blk, pl.BlockSpec((D, D), lambda i: (0, 0)), blk, pl.BlockSpec((1, D), lambda i: (0, 0))],
        out_specs=[blk, blk], out_shape=[_sds((t, D), F32)] * 2,
        compiler_params=_params("arbitrary"))(merged, w_out, x, gt)


def _ffn_in(h2, wfi, name):
    t = h2.shape[0]
    tm = _tile(t, 512)

    def body(h_ref, w_ref, gu_ref, f_ref):
        hv = h_ref[...]
        gp = _dot(hv, w_ref[0])
        up = _dot(hv, w_ref[1])
        gu_ref[0] = gp
        gu_ref[1] = up
        f_ref[...] = (gp * _sigmoid(gp) * up).astype(BF16)

    return pl.pallas_call(
        body, name=name, grid=(4, t // tm),
        in_specs=[pl.BlockSpec((tm, D), lambda j, i: (i, 0)),
                  pl.BlockSpec((2, None, D, FF_BLK), lambda j, i: (0, j, 0, 0))],
        out_specs=[pl.BlockSpec((2, None, tm, FF_BLK), lambda j, i: (0, j, i, 0)),
                   pl.BlockSpec((None, tm, FF_BLK), lambda j, i: (j, i, 0))],
        out_shape=[_sds((2, 4, t, FF_BLK), F32), _sds((4, t, FF_BLK), BF16)],
        compiler_params=_params("arbitrary", "arbitrary"))(h2, wfi)


def _ffn_out(f4, wfo4, x1, gt, name):
    t = x1.shape[0]
    tm = _tile(t, 512)

    def body(f_ref, w_ref, x_ref, gt_ref, o_ref, x2_ref):
        j = pl.program_id(1)

        @pl.when(j == 0)
        def _():
            o_ref[...] = jnp.zeros_like(o_ref)

        o_ref[...] += _dot(f_ref[...], w_ref[...])

        @pl.when(j == 3)
        def _():
            x2_ref[...] = x_ref[...] + gt_ref[...] * o_ref[...]

    blk = pl.BlockSpec((tm, D), lambda i, j: (i, 0))
    return pl.pallas_call(
        body, name=name, grid=(t // tm, 4),
        in_specs=[pl.BlockSpec((None, tm, FF_BLK), lambda i, j: (j, i, 0)),
                  pl.BlockSpec((None, FF_BLK, D), lambda i, j: (j, 0, 0)), blk,
                  pl.BlockSpec((1, D), lambda i, j: (0, 0))],
        out_specs=[blk, blk], out_shape=[_sds((t, D), F32)] * 2,
        compiler_params=_params("arbitrary", "arbitrary"))(f4, wfo4, x1, gt)


def _final_loss(x, g, target, name):
    t = x.shape[0]
    tm = _tile(t, 512)

    def body(x_ref, g_ref, t_ref, loss_ref, dx_ref, dg_ref):
        @pl.when(pl.program_id(0) == 0)
        def _():
            loss_ref[...] = jnp.zeros_like(loss_ref)
            dg_ref[...] = jnp.zeros_like(dg_ref)

        xv = x_ref[...]
        r = lax.rsqrt(jnp.mean(xv * xv, axis=-1, keepdims=True) + EPS)
        xn = xv * r
        diff = xn * g_ref[...] - t_ref[...]
        loss_ref[...] += 0.5 * jnp.sum(jnp.mean(diff * diff, axis=-1, keepdims=True))
        dy = diff * (1.0 / D)
        dg_ref[...] += _colsum(dy * xn)
        dxn = dy * g_ref[...]
        dx_ref[...] = r * (dxn - xn * jnp.mean(dxn * xn, axis=-1, keepdims=True))

    blk = pl.BlockSpec((tm, D), lambda i: (i, 0))
    row = pl.BlockSpec((1, D), lambda i: (0, 0))
    return pl.pallas_call(
        body, name=name, grid=(t // tm,), in_specs=[blk, row, blk],
        out_specs=[pl.BlockSpec((8, 128), lambda i: (0, 0)), blk, row],
        out_shape=[_sds((8, 128), F32), _sds((t, D), F32), _sds((1, D), F32)],
        compiler_params=_params("arbitrary"))(x, g, target)


def _gate_bwd(dx, o, gt, name):
    t = dx.shape[0]
    tm = _tile(t, 512)

    def body(dx_ref, o_ref, gt_ref, do_ref, dgt_ref):
        @pl.when(pl.program_id(0) == 0)
        def _():
            dgt_ref[...] = jnp.zeros_like(dgt_ref)

        dxv = dx_ref[...]
        do_ref[...] = (dxv * gt_ref[...]).astype(BF16)
        dgt_ref[...] += _colsum(dxv * o_ref[...])

    blk = pl.BlockSpec((tm, D), lambda i: (i, 0))
    row = pl.BlockSpec((1, D), lambda i: (0, 0))
    return pl.pallas_call(body, name=name, grid=(t // tm,), in_specs=[blk, blk, row], out_specs=[blk, row],
                          out_shape=[_sds((t, D), BF16), _sds((1, D), F32)],
                          compiler_params=_params("arbitrary"))(dx, o, gt)


def _norm_mod_bwd(dh, x, dres, g, sc, name):
    t = x.shape[0]
    tm = _tile(t, 512)

    def body(dh_ref, x_ref, dr_ref, g_ref, sc_ref, dx_ref, st_ref):
        @pl.when(pl.program_id(0) == 0)
        def _():
            st_ref[...] = jnp.zeros_like(st_ref)

        xv, dhv = x_ref[...], dh_ref[...]
        r = lax.rsqrt(jnp.mean(xv * xv, axis=-1, keepdims=True) + EPS)
        xn = xv * r
        gv, mod = g_ref[...], 1.0 + sc_ref[...]
        st_ref[0:1, :] += _colsum(dhv)
        st_ref[1:2, :] += _colsum(dhv * xn * gv)
        st_ref[2:3, :] += _colsum(dhv * xn * mod)
        dxn = dhv * gv * mod
        dx_ref[...] = dr_ref[...] + r * (dxn - xn * jnp.mean(dxn * xn, axis=-1, keepdims=True))

    blk = pl.BlockSpec((tm, D), lambda i: (i, 0))
    row = pl.BlockSpec((1, D), lambda i: (0, 0))
    return pl.pallas_call(body, name=name, grid=(t // tm,), in_specs=[blk, blk, blk, row, row],
                          out_specs=[blk, pl.BlockSpec((3, D), lambda i: (0, 0))],
                          out_shape=[_sds((t, D), F32), _sds((3, D), F32)],
                          compiler_params=_params("arbitrary"))(dh, x, dres, g, sc)


def _ffn_bwd_act(do, wfo4, gu, name):
    t = do.shape[0]
    tm = _tile(t, 512)

    def body(do_ref, w_ref, gu_ref, dgu_ref):
        df = _dot_nt(do_ref[...], w_ref[...])
        gp, up = gu_ref[0], gu_ref[1]
        sg = _sigmoid(gp)
        dgu_ref[0] = (df * up * (sg * (1.0 + gp * (1.0 - sg)))).astype(BF16)
        dgu_ref[1] = (df * (gp * sg)).astype(BF16)

    gu_spec = pl.BlockSpec((2, None, tm, FF_BLK), lambda j, i: (0, j, i, 0))
    return pl.pallas_call(
        body, name=name, grid=(4, t // tm),
        in_specs=[pl.BlockSpec((tm, D), lambda j, i: (i, 0)),
                  pl.BlockSpec((None, FF_BLK, D), lambda j, i: (j, 0, 0)), gu_spec],
        out_specs=gu_spec, out_shape=_sds((2, 4, t, FF_BLK), BF16),
        compiler_params=_params("arbitrary", "arbitrary"))(do, wfo4, gu)


def _mm_nt_sum(a8, b8, name):
    j, t, k = a8.shape
    n = b8.shape[1]
    tm = _tile(t, 512)

    def body(a_ref, b_ref, o_ref):
        @pl.when(pl.program_id(1) == 0)
        def _():
            o_ref[...] = jnp.zeros_like(o_ref)

        o_ref[...] += _dot_nt(a_ref[...], b_ref[...])

    return pl.pallas_call(
        body, name=name, grid=(t // tm, j),
        in_specs=[pl.BlockSpec((None, tm, k), lambda i, jj: (jj, i, 0)),
                  pl.BlockSpec((None, n, k), lambda i, jj: (jj, 0, 0))],
        out_specs=pl.BlockSpec((tm, n), lambda i, jj: (i, 0)), out_shape=_sds((t, n), F32),
        compiler_params=_params("arbitrary", "arbitrary"))(a8, b8)


def _mm_tn(a8, b8, name):
    ja, t, m = a8.shape
    jb, _, n = b8.shape
    j = max(ja, jb)
    tk = _tile(t, 1024)

    def body(a_ref, b_ref, o_ref):
        @pl.when(pl.program_id(1) == 0)
        def _():
            o_ref[...] = jnp.zeros_like(o_ref)

        o_ref[...] += _dot_tn(a_ref[...], b_ref[...])

    return pl.pallas_call(
        body, name=name, grid=(j, t // tk),
        in_specs=[pl.BlockSpec((None, tk, m), (lambda jj, kk: (jj, kk, 0)) if ja > 1 else (lambda jj, kk: (0, kk, 0))),
                  pl.BlockSpec((None, tk, n), (lambda jj, kk: (jj, kk, 0)) if jb > 1 else (lambda jj, kk: (0, kk, 0)))],
        out_specs=pl.BlockSpec((None, m, n), lambda jj, kk: (jj, 0, 0)), out_shape=_sds((j, m, n), F32),
        compiler_params=_params("arbitrary", "arbitrary"))(a8, b8)


def _merge_bwd(dom, w_out, z8, ya, yb, yc, name):
    t = dom.shape[0]
    tm = _tile(t, 512)

    def body(dom_ref, w_ref, g0_ref, g1_ref, g2_ref, ya_ref, yb_ref, yc_ref, dya_ref, dyb_ref, dyc_ref, dzg_ref,
             db_ref):
        @pl.when(pl.program_id(0) == 0)
        def _():
            db_ref[...] = jnp.zeros_like(db_ref)

        dm = _dot_nt(dom_ref[...], w_ref[...])
        for k, (g_ref, y_ref, dy_ref) in enumerate(((g0_ref, ya_ref, dya_ref), (g1_ref, yb_ref, dyb_ref),
                                                    (g2_ref, yc_ref, dyc_ref))):
            sg = _sigmoid(g_ref[...])
            dy_ref[...] = (dm * sg).astype(BF16)
            dzg = dm * y_ref[...].astype(F32) * (sg * (1.0 - sg))
            dzg_ref[k] = dzg.astype(BF16)
            db_ref[k:k + 1, :] += _colsum(dzg)

    blk = pl.BlockSpec((tm, D), lambda i: (i, 0))
    gate = lambda j: pl.BlockSpec((None, tm, D), lambda i: (j, i, 0))
    return pl.pallas_call(
        body, name=name, grid=(t // tm,),
        in_specs=[blk, pl.BlockSpec((D, D), lambda i: (0, 0)), gate(5), gate(6), gate(7), blk, blk, blk],
        out_specs=[blk, blk, blk, pl.BlockSpec((3, tm, D), lambda i: (0, i, 0)), pl.BlockSpec((3, D), lambda i: (0, 0))],
        out_shape=[_sds((t, D), BF16)] * 3 + [_sds((3, t, D), BF16), _sds((3, D), F32)],
        compiler_params=_params("arbitrary"))(dom, w_out, z8, z8, z8, ya, yb, yc)


def _branches_bwd(z8, cv, dsa, dsb, dsc, dzg, ln_g, ln_b, w_s, bs_b, pool_w, pool_scale, conv_w, cln_g, cln_b, name):
    t = z8.shape[1]
    tm = _tile(t, 128)
    n_ext = tm + HALO
    n_tiles = t // tm

    def body(zu_ref, zv_ref, p_ref, a_ref, ag_ref, ph_ref, ah_ref, agh_ref, cv_ref, cvf_ref, dsa_ref, dsb_ref,
             dsbf_ref, dsc_ref, dscf_ref, dzg_ref, lng_ref, lnb_ref, ws_ref, bsb_ref, wp_ref, ps_ref, cw_ref,
             clg_ref, clb_ref, dz_ref, dbin_ref, rows_ref, dws_ref, dbs_ref, dwp_ref, dcw_ref, mixed_scr, dvln_scr,
             dcv_scr, zbank_ref, dbank_ref, dcw8_scr):
        i = pl.program_id(0)

        @pl.when(i == 0)
        def _():
            for ref in (dbin_ref, rows_ref, dws_ref, dbs_ref, dwp_ref, dcw8_scr):
                ref[...] = jnp.zeros_like(ref)

        has_past = (i > 0).astype(F32)
        has_next = (i < n_tiles - 1).astype(F32)

        def emit(j, val):
            dz_ref[j] = val.astype(BF16)
            dbin_ref[j:j + 1, :] += _colsum(val)

        zu, zv = zu_ref[...], zv_ref[...]
        u = _gelu(zu)
        vhat, v_rstd = _ln_stats(_gelu(zv))
        vb = (vhat * lng_ref[...] + lnb_ref[...]).astype(BF16)
        dsa = dsa_ref[...]
        dmixed = dsa * u
        dmb = dmixed.astype(BF16)
        mask = _tril_mask()
        lane = lax.broadcasted_iota(jnp.int32, (CHUNK, CHUNK), 1)
        for g in range(SGU_G):
            cols = slice(g * CHUNK, (g + 1) * CHUNK)
            wm = (ws_ref[g] * mask).astype(BF16)
            dws = jnp.zeros((CHUNK, CHUNK), F32)
            dbs = jnp.zeros((CHUNK, 1), F32)
            for n in range(tm // CHUNK):
                rows = slice(n * CHUNK, (n + 1) * CHUNK)
                mixed_scr[rows, cols] = _dot(wm, vb[rows, cols]) + bsb_ref[g]
                dvln_scr[rows, cols] = _dot_tn(wm, dmb[rows, cols])
                dws = dws + _dot_nt(dmb[rows, cols], vb[rows, cols])
                dbs = dbs + jnp.sum(dmixed[rows, cols], axis=1, keepdims=True)
            dws_ref[g] += dws
            dbs_ref[...] += jnp.where(lane == g, dbs, 0.0)
        emit(0, dsa * mixed_scr[...] * _gelu_grad(zu))
        dvln = dvln_scr[...]
        rows_ref[0:1, :] += _colsum(dvln * vhat)
        rows_ref[1:2, :] += _colsum(dvln)
        emit(1, _ln_bwd(dvln * lng_ref[...], vhat, v_rstd) * _gelu_grad(zv))

        p_ext = jnp.concatenate([ph_ref[...] * has_past, p_ref[...]], axis=0)
        pooled = _pool_forward(p_ext, i * tm, tm)
        dsb = dsb_ref[...]
        dpl_ext = jnp.concatenate([dsb, dsbf_ref[...] * has_next], axis=0) * ps_ref[...]
        t_ext = i * tm + lax.broadcasted_iota(jnp.int32, (n_ext, 1), 0)
        dp_parts = []
        for gi, win in enumerate(POOL_WINDOWS):
            cols = slice(gi * POOL_GC, (gi + 1) * POOL_GC)
            pooled_b = pooled[gi].astype(BF16)
            wpb = wp_ref[gi].astype(BF16)
            rows_ref[2:3, cols] += _colsum(dsb[:, cols] * _dot(pooled_b, wpb))
            dplb = dpl_ext[:, cols].astype(BF16)
            dwp_ref[gi] += _dot_tn(pooled_b, dplb[:tm])
            dpooled = _dot_nt(dplb, wpb)
            s, sh = dpooled / jnp.minimum(t_ext + 1, win).astype(F32), 1
            while sh < win:
                s = s + pltpu.roll(s, n_ext - sh, 0)
                sh *= 2
            dp_parts.append(s[:tm] - dpooled[:tm])
        emit(2, jnp.concatenate(dp_parts, axis=1))

        cv_ext = jnp.concatenate([cv_ref[...], cvf_ref[...]], axis=0)
        chat, c_rstd = _ln_stats(cv_ext)
        cl = chat * clg_ref[...] + clb_ref[...]
        sg = _sigmoid(cl)
        dcl = jnp.concatenate([dsc_ref[...], dscf_ref[...]], axis=0) * (sg * (1.0 + cl * (1.0 - sg)))
        rows_ref[4:5, :] += _colsum((dcl * chat)[:tm])
        rows_ref[5:6, :] += _colsum(dcl[:tm])
        in_seq = jnp.concatenate([jnp.ones((tm, 1), F32), jnp.zeros((HALO, 1), F32) + has_next], axis=0)
        dcv = jnp.where(in_seq > 0.0, _ln_bwd(dcl * clg_ref[...], chat, c_rstd), 0.0)
        rows_ref[3:4, :] += _colsum(dcv[:tm])
        dcv_scr[...] = dcv
        for cb in range(D // LANE):
            cols = slice(cb * LANE, (cb + 1) * LANE)
            zc = jnp.concatenate([ah_ref[:, cols] * has_past * _sigmoid(agh_ref[:, cols]),
                                  a_ref[:, cols] * _sigmoid(ag_ref[:, cols])], axis=0)
            _fill_shift_bank(zbank_ref, zc, causal=True)
            _fill_shift_bank(dbank_ref, dcv_scr[:, cols], causal=False)
            for r0 in range(0, tm, CONV_STRIP):
                rows = slice(r0, r0 + CONV_STRIP)
                dcv_s = dcv_scr[rows, cols]
                dzc = jnp.zeros((CONV_STRIP, LANE), F32)
                for k in range(CONV_K):
                    hi, lo = divmod(CONV_K - 1 - k, SUBLANE)
                    z_win = zbank_ref[lo, pl.ds(HALO - SUBLANE * hi + r0, CONV_STRIP), :]
                    dcw8_scr[k, :, cols] += jnp.sum((dcv_s * z_win).reshape(CONV_STRIP // SUBLANE, SUBLANE, LANE), axis=0)
                    dzc = dzc + cw_ref[k:k + 1, cols] * dbank_ref[lo, pl.ds(SUBLANE * hi + r0, CONV_STRIP), :]
                a_s = a_ref[rows, cols]
                sga = _sigmoid(ag_ref[rows, cols])
                dza = dzc * sga
                dzag = dzc * a_s * (sga * (1.0 - sga))
                dz_ref[3, rows, cols] = dza.astype(BF16)
                dz_ref[4, rows, cols] = dzag.astype(BF16)
                dbin_ref[3:4, cols] += _colsum(dza)
                dbin_ref[4:5, cols] += _colsum(dzag)
        for k in range(3):
            dz_ref[5 + k] = dzg_ref[k]

        @pl.when(i == n_tiles - 1)
        def _():
            dcw_ref[...] = jnp.sum(dcw8_scr[...], axis=1)

    def col(j):
        return pl.BlockSpec((None, tm, D), lambda i: (j, i, 0))

    blk = pl.BlockSpec((tm, D), lambda i: (i, 0))
    after = pl.BlockSpec((HALO, D), lambda i: (jnp.minimum((i + 1) * (tm // HALO), t // HALO - 1), 0))
    row = pl.BlockSpec((1, D), lambda i: (0, 0))
    full2 = lambda s: pl.BlockSpec(s, lambda i: (0, 0))
    full3 = lambda s: pl.BlockSpec(s, lambda i: (0, 0, 0))
    return pl.pallas_call(
        body, name=name, grid=(n_tiles,),
        in_specs=[col(0), col(1), col(2), col(3), col(4), _halo_before(tm, 2), _halo_before(tm, 3),
                  _halo_before(tm, 4), blk, after, blk, blk, after, blk, after,
                  pl.BlockSpec((3, tm, D), lambda i: (0, i, 0)), row, row, full3((SGU_G, CHUNK, CHUNK)),
                  full3((SGU_G, CHUNK, CHUNK)), full3((4, POOL_GC, POOL_GC)), row, full2((HALO, D)), row, row],
        out_specs=[pl.BlockSpec((8, tm, D), lambda i: (0, i, 0)), full2((8, D)), full2((8, D)),
                   full3((SGU_G, CHUNK, CHUNK)), full2((CHUNK, CHUNK)), full3((4, POOL_GC, POOL_GC)),
                   full2((HALO, D))],
        out_shape=[_sds((8, t, D), BF16), _sds((8, D), F32), _sds((8, D), F32), _sds((SGU_G, CHUNK, CHUNK), F32),
                   _sds((CHUNK, CHUNK), F32), _sds((4, POOL_GC, POOL_GC), F32), _sds((HALO, D), F32)],
        scratch_shapes=[pltpu.VMEM((tm, D), F32), pltpu.VMEM((tm, D), F32), pltpu.VMEM((n_ext, D), F32),
                        pltpu.VMEM((SUBLANE, n_ext, LANE), F32), pltpu.VMEM((SUBLANE, n_ext, LANE), F32),
                        pltpu.VMEM((HALO, SUBLANE, D), F32)],
        compiler_params=_params("arbitrary"),
    )(z8, z8, z8, z8, z8, z8, z8, z8, cv, cv, dsa, dsb, dsb, dsc, dsc, dzg, ln_g, ln_b, w_s, bs_b, pool_w,
      pool_scale, conv_w, cln_g, cln_b)


def _ada_fwd(c_all, w_ada, b_loc, name):
    def body(c_ref, w_ref, b_ref, o_ref):
        cv = c_ref[...]
        ca = (cv * _sigmoid(cv)).astype(BF16)
        for l in range(DEPTH):
            o_ref[l] = _dot(ca, w_ref[l].astype(BF16)) + b_ref[l]

    return pl.pallas_call(body, name=name, out_shape=_sds((DEPTH, N_DEV, ADA_BLK), F32),
                          compiler_params=_params())(c_all, w_ada, b_loc)


def _ada_bwd(c_all_t, d_loc, name):
    def body(c_ref, d_ref, o_ref):
        cv = c_ref[...]
        ca = cv * _sigmoid(cv)
        for l in range(DEPTH):
            acc = jnp.zeros((D, ADA_BLK), F32)
            for j in range(N_DEV):
                acc = acc + ca[:, j:j + 1] * d_ref[l, j:j + 1, :]
            o_ref[l] = acc

    return pl.pallas_call(body, name=name, out_shape=_sds((DEPTH, D, ADA_BLK), F32),
                          compiler_params=_params())(c_all_t, d_loc)


def _sum8(g8, name):
    _, rows, cols = g8.shape
    tr = _tile(rows, 88)

    def body(g_ref, o_ref):
        acc = g_ref[0]
        for k in range(1, N_DEV):
            acc = acc + g_ref[k]
        o_ref[...] = acc

    return pl.pallas_call(body, name=name, grid=(rows // tr,),
                          in_specs=[pl.BlockSpec((N_DEV, tr, cols), lambda r: (0, r, 0))],
                          out_specs=pl.BlockSpec((tr, cols), lambda r: (r, 0)), out_shape=_sds((rows, cols), F32),
                          compiler_params=_params("arbitrary"))(g8)


REPLICATED = ("b_ada", "g_mix", "b_in", "sgu_ln_g", "sgu_ln_b", "sgu_w_s", "sgu_b_s", "pool_scale", "conv_b",
              "conv_ln_g", "conv_ln_b", "g_ffn", "g_final")
WEIGHT_ORDER = ("w_ada", "b_ada", "g_mix", "w_in", "b_in", "sgu_ln_g", "sgu_ln_b", "sgu_w_s", "sgu_b_s", "w_pa",
                "pool_w", "pool_scale", "w_pb", "conv_w", "conv_b", "conv_ln_g", "conv_ln_b", "w_pc", "w_out",
                "g_ffn", "w_ffn_in", "w_ffn_out", "g_final")


def _rows(a):
    return a.reshape(-1, D)


def kernel(x, c, w_ada, b_ada, g_mix, w_in, b_in, sgu_ln_g, sgu_ln_b, sgu_w_s, sgu_b_s, w_pa, pool_w, pool_scale, w_pb, conv_w, conv_b, conv_ln_g, conv_ln_b, w_pc, w_out, g_ffn, w_ffn_in, w_ffn_out, g_final, loss_target, m_w_ada, m_b_ada, m_g_mix, m_w_in, m_b_in, m_sgu_ln_g, m_sgu_ln_b, m_sgu_w_s, m_sgu_b_s, m_w_pa, m_pool_w, m_pool_scale, m_w_pb, m_conv_w, m_conv_b, m_conv_ln_g, m_conv_ln_b, m_w_pc, m_w_out, m_g_ffn, m_w_ffn_in, m_w_ffn_out, m_g_final, v_w_ada, v_b_ada, v_g_mix, v_w_in, v_b_in, v_sgu_ln_g, v_sgu_ln_b, v_sgu_w_s, v_sgu_b_s, v_w_pa, v_pool_w, v_pool_scale, v_w_pb, v_conv_w, v_conv_b, v_conv_ln_g, v_conv_ln_b, v_w_pc, v_w_out, v_g_ffn, v_w_ffn_in, v_w_ffn_out, v_g_final):
    weights = dict(w_ada=w_ada, b_ada=b_ada, g_mix=g_mix, w_in=w_in, b_in=b_in, sgu_ln_g=sgu_ln_g, sgu_ln_b=sgu_ln_b,
                   sgu_w_s=sgu_w_s, sgu_b_s=sgu_b_s, w_pa=w_pa, pool_w=pool_w, pool_scale=pool_scale, w_pb=w_pb,
                   conv_w=conv_w, conv_b=conv_b, conv_ln_g=conv_ln_g, conv_ln_b=conv_ln_b, w_pc=w_pc, w_out=w_out,
                   g_ffn=g_ffn, w_ffn_in=w_ffn_in, w_ffn_out=w_ffn_out, g_final=g_final)
    mom1 = dict(w_ada=m_w_ada, b_ada=m_b_ada, g_mix=m_g_mix, w_in=m_w_in, b_in=m_b_in, sgu_ln_g=m_sgu_ln_g,
                sgu_ln_b=m_sgu_ln_b, sgu_w_s=m_sgu_w_s, sgu_b_s=m_sgu_b_s, w_pa=m_w_pa, pool_w=m_pool_w,
                pool_scale=m_pool_scale, w_pb=m_w_pb, conv_w=m_conv_w, conv_b=m_conv_b, conv_ln_g=m_conv_ln_g,
                conv_ln_b=m_conv_ln_b, w_pc=m_w_pc, w_out=m_w_out, g_ffn=m_g_ffn, w_ffn_in=m_w_ffn_in,
                w_ffn_out=m_w_ffn_out, g_final=m_g_final)
    mom2 = dict(w_ada=v_w_ada, b_ada=v_b_ada, g_mix=v_g_mix, w_in=v_w_in, b_in=v_b_in, sgu_ln_g=v_sgu_ln_g,
                sgu_ln_b=v_sgu_ln_b, sgu_w_s=v_sgu_w_s, sgu_b_s=v_sgu_b_s, w_pa=v_w_pa, pool_w=v_pool_w,
                pool_scale=v_pool_scale, w_pb=v_w_pb, conv_w=v_conv_w, conv_b=v_conv_b, conv_ln_g=v_conv_ln_g,
                conv_ln_b=v_conv_ln_b, w_pc=v_w_pc, w_out=v_w_out, g_ffn=v_g_ffn, w_ffn_in=v_w_ffn_in,
                w_ffn_out=v_w_ffn_out, g_final=v_g_final)

    t = x.shape[1]
    xs = x.reshape(t, D)
    target = loss_target.reshape(t, D)
    me = 4 * lax.axis_index("x") + 2 * lax.axis_index("y") + lax.axis_index("c")
    core = lax.axis_index("c").astype(jnp.int32).reshape(1)

    big = ("w_in", "w_pa", "w_pb", "w_pc", "w_out", "w_ffn_in", "w_ffn_out")
    shards = [weights[n][l].astype(BF16) for l in range(DEPTH) for n in big]
    gathered = _all_gather(shards + [c, pool_w, conv_w], name="gather_weights")
    full = [dict(zip(big, gathered[l * len(big):(l + 1) * len(big)])) for l in range(DEPTH)]
    c_all, pool_all, conv_all = gathered[DEPTH * len(big):]
    c_all = c_all.reshape(N_DEV, D)
    pool_full = jnp.transpose(pool_all, (1, 2, 0, 3, 4)).reshape(DEPTH, 4, POOL_GC, POOL_GC)
    conv_full = jnp.transpose(conv_all, (1, 2, 0, 3)).reshape(DEPTH, CONV_K, D)
    conv_full = jnp.pad(conv_full, ((0, 0), (0, HALO - CONV_K), (0, 0)))

    b_loc = lax.dynamic_slice_in_dim(b_ada, me * ADA_BLK, ADA_BLK, axis=1).reshape(DEPTH, 1, ADA_BLK)
    ada_part = _ada_fwd(c_all, w_ada, b_loc, name="ada_fwd")
    (ada_all,) = _all_gather([ada_part], name="gather_ada")
    ada = lax.dynamic_index_in_dim(ada_all, me, axis=2, keepdims=False)
    ada = jnp.transpose(ada, (1, 0, 2)).reshape(DEPTH, 6, 1, D)

    bs_b = jnp.broadcast_to(sgu_b_s[..., None], (DEPTH, SGU_G, CHUNK, CHUNK))

    saved = []
    xl = xs
    for l in range(DEPTH):
        w = full[l]
        sh_m, sc_m, gt_m, sh_f, sc_f, gt_f = (ada[l, k] for k in range(6))
        row = lambda a: a[l].reshape(1, D)
        h = _norm_mod(xl, row(g_mix), sc_m, sh_m, name=f"norm_mix_{l}")
        z8 = _mm_cols(h, w["w_in"], b_in[l].reshape(8, 1, D), name=f"in_proj_{l}")
        sa, sb, sc, cv = _branches_fwd(z8, row(sgu_ln_g), row(sgu_ln_b), sgu_w_s[l], bs_b[l], pool_full[l],
                                       row(pool_scale), conv_full[l], row(conv_b), row(conv_ln_g), row(conv_ln_b),
                                       name=f"branches_{l}")
        wpa, wpb, wpc, wout = (w[n].reshape(D, D) for n in ("w_pa", "w_pb", "w_pc", "w_out"))
        ya, yb, yc, merged = _proj_merge(sa, sb, sc, wpa, wpb, wpc, z8, name=f"proj_merge_{l}")
        om, x1 = _out_proj(merged, wout, xl, gt_m, name=f"out_proj_{l}")
        h2 = _norm_mod(x1, row(g_ffn), sc_f, sh_f, name=f"norm_ffn_{l}")
        wfi = w["w_ffn_in"].reshape(2, 4, D, FF_BLK)
        wfo4 = w["w_ffn_out"].reshape(4, FF_BLK, D)
        gu, f4 = _ffn_in(h2, wfi, name=f"ffn_in_{l}")
        o, x2 = _ffn_out(f4, wfo4, x1, gt_f, name=f"ffn_out_{l}")
        saved.append(dict(x=xl, h=h, z8=z8, sa=sa, sb=sb, sc=sc, cv=cv, ya=ya, yb=yb, yc=yc, merged=merged, om=om,
                          x1=x1, h2=h2, gu=gu, f4=f4, o=o, wpa=wpa, wpb=wpb, wpc=wpc, wout=wout, wfi=wfi, wfo4=wfo4))
        xl = x2

    loss_tile, dx, dg_final = _final_loss(xl, g_final.reshape(1, D), target, name="final_loss")
    loss = lax.psum(loss_tile[0, 0], AXES)

    big_grads = [None] * DEPTH
    small = [None] * DEPTH
    d_ada = [None] * DEPTH
    for l in reversed(range(DEPTH)):
        s, w = saved[l], full[l]
        sh_m, sc_m, gt_m, sh_f, sc_f, gt_f = (ada[l, k] for k in range(6))
        row = lambda a: a[l].reshape(1, D)
        do, dgt_f = _gate_bwd(dx, s["o"], gt_f, name=f"gate_bwd_ffn_{l}")
        dgu = _ffn_bwd_act(do, s["wfo4"], s["gu"], name=f"ffn_bwd_act_{l}")
        d_wfo = _mm_tn(s["f4"], do[None], name=f"dw_ffn_out_{l}")
        dgu8 = dgu.reshape(8, t, FF_BLK)
        dh2 = _mm_nt_sum(dgu8, w["w_ffn_in"], name=f"dh_ffn_{l}")
        d_wfi = _mm_tn(s["h2"][None], dgu8, name=f"dw_ffn_in_{l}")
        dx1, st_f = _norm_mod_bwd(dh2, s["x1"], dx, row(g_ffn), sc_f, name=f"norm_ffn_bwd_{l}")
        dom, dgt_m = _gate_bwd(dx1, s["om"], gt_m, name=f"gate_bwd_mix_{l}")
        dya, dyb, dyc, dzg, db_gate = _merge_bwd(dom, s["wout"], s["z8"], s["ya"], s["yb"], s["yc"],
                                                 name=f"merge_bwd_{l}")
        d_wout = _mm_tn(s["merged"][None], dom[None], name=f"dw_out_{l}")
        d_wpa = _mm_tn(s["sa"][None], dya[None], name=f"dw_pa_{l}")
        d_wpb = _mm_tn(s["sb"][None], dyb[None], name=f"dw_pb_{l}")
        d_wpc = _mm_tn(s["sc"][None], dyc[None], name=f"dw_pc_{l}")
        dsa = _mm_nt_sum(dya[None], s["wpa"][None], name=f"ds_a_{l}")
        dsb = _mm_nt_sum(dyb[None], s["wpb"][None], name=f"ds_b_{l}")
        dsc = _mm_nt_sum(dyc[None], s["wpc"][None], name=f"ds_c_{l}")
        dz8, db_in5, rows6, dws, dbs, dwp, dcw = _branches_bwd(
            s["z8"], s["cv"], dsa, dsb, dsc, dzg, row(sgu_ln_g), row(sgu_ln_b), sgu_w_s[l], bs_b[l], pool_full[l],
            row(pool_scale), conv_full[l], row(conv_ln_g), row(conv_ln_b), name=f"branches_bwd_{l}")
        dh = _mm_nt_sum(dz8, w["w_in"], name=f"dh_in_{l}")
        d_win = _mm_tn(s["h"][None], dz8, name=f"dw_in_{l}")
        dx, st_m = _norm_mod_bwd(dh, s["x"], dx1, row(g_mix), sc_m, name=f"norm_mix_bwd_{l}")

        big_grads[l] = [d_win, d_wpa.reshape(8, D // 8, D), d_wpb.reshape(8, D // 8, D), d_wpc.reshape(8, D // 8, D),
                        d_wout.reshape(8, D // 8, D), d_wfi, d_wfo.reshape(8, D_FF // 8, D)]
        d_ada[l] = jnp.concatenate([st_m[0:1], st_m[1:2], dgt_m, st_f[0:1], st_f[1:2], dgt_f], axis=0)
        tril = jnp.tril(jnp.ones((CHUNK, CHUNK), F32))
        small[l] = dict(
            g_mix=st_m[2:3], b_in=jnp.concatenate([db_in5[0:5], db_gate], axis=0), sgu_ln_g=rows6[0:1],
            sgu_ln_b=rows6[1:2], sgu_w_s=_rows(dws * tril), sgu_b_s=_rows(jnp.transpose(dbs[:, :SGU_G])),
            pool_scale=rows6[2:3], conv_b=rows6[3:4], conv_ln_g=rows6[4:5], conv_ln_b=rows6[5:6], g_ffn=st_f[2:3],
            pool_w=_rows(dwp), conv_w=dcw)

    rs = [_reduce_scatter(big_grads[l], core, tag=str(l)) for l in range(DEPTH)]

    per_layer = ("g_mix", "b_in", "sgu_ln_g", "sgu_ln_b", "sgu_w_s", "sgu_b_s", "pool_scale", "conv_b", "conv_ln_g",
                 "conv_ln_b", "g_ffn", "pool_w", "conv_w")
    segs = [("d_ada", jnp.concatenate(d_ada, axis=0))]
    segs += [(n, jnp.concatenate([small[l][n] for l in range(DEPTH)], axis=0)) for n in per_layer]
    segs += [("g_final", dg_final)]
    n_rows = sum(a.shape[0] for _, a in segs)
    pad = (-n_rows) % 88
    buf = jnp.concatenate([a for _, a in segs] + [jnp.zeros((pad, D), F32)], axis=0)
    (buf_all,) = _all_gather([buf], name="gather_small_grads")
    red = _sum8(buf_all, name="sum_small_grads")
    offs, o = {}, 0
    for n, a in segs:
        offs[n] = (o, a.shape[0])
        o += a.shape[0]
    seg = lambda arr, n: lax.slice_in_dim(arr, offs[n][0], offs[n][0] + offs[n][1], axis=0)

    grads = {n: seg(red, n).reshape(weights[n].shape) for n in REPLICATED if n != "b_ada"}
    grads["b_ada"] = seg(red, "d_ada").reshape(DEPTH, 6 * D)
    pool_g = seg(red, "pool_w").reshape(DEPTH, 4, POOL_GC, POOL_GC)
    grads["pool_w"] = lax.dynamic_slice_in_dim(pool_g, me * (POOL_GC // 8), POOL_GC // 8, axis=2)
    conv_g = seg(red, "conv_w").reshape(DEPTH, HALO, D)[:, :CONV_K]
    grads["conv_w"] = lax.dynamic_slice_in_dim(conv_g, me * (D // 8), D // 8, axis=2)

    d_ada_all = lax.slice_in_dim(buf_all, offs["d_ada"][0], offs["d_ada"][0] + offs["d_ada"][1], axis=1)
    d_ada_all = d_ada_all.reshape(N_DEV, DEPTH, 6 * D)
    d_loc = jnp.transpose(lax.dynamic_slice_in_dim(d_ada_all, me * ADA_BLK, ADA_BLK, axis=2), (1, 0, 2))
    grads["w_ada"] = _ada_bwd(jnp.transpose(c_all), d_loc, name="ada_bwd")

    out = {}
    for n in REPLICATED + ("pool_w", "conv_w", "w_ada"):
        out[n] = _adam_nd(grads[n], weights[n], mom1[n], mom2[n], name=f"adam_{n}")
    for k, n in enumerate(big):
        shape = (DEPTH,) + rs[0][k].shape[1:]
        res = _adam([rs[l][k] for l in range(DEPTH)], weights[n].reshape(shape), mom1[n].reshape(shape),
                    mom2[n].reshape(shape), name=f"adam_{n}")
        out[n] = [r.reshape(weights[n].shape) for r in res]

    grad_x = dx.reshape(1, t, D)
    return (loss, grad_x, *[out[n][0] for n in WEIGHT_ORDER], *[out[n][1] for n in WEIGHT_ORDER],
            *[out[n][2] for n in WEIGHT_ORDER], *[out[n][3] for n in WEIGHT_ORDER])
```

```python
import math

import jax
import jax.numpy as jnp
from jax import lax
from jax.experimental import pallas as pl
from jax.experimental.pallas import tpu as pltpu

F32 = jnp.float32
BF16 = jnp.bfloat16
MESH = pl.DeviceIdType.MESH
AXES = ("x", "y", "c")
N_DEV = 8

D = 1024
DEPTH = 2
EPS = 1e-6
CHUNK = 128
SGU_G = 8
POOL_WINDOWS = (2, 4, 8, 16)
POOL_GC = 256
CONV_K = 31
HALO = 32
SUBLANE = 8
LANE = 128
CONV_STRIP = 128
D_FF = 2816
FF_BLK = D_FF // 4
ADA_BLK = 6 * D // N_DEV

ADAM_LR = 0.001
ADAM_B1 = 0.9
ADAM_B2 = 0.999
ADAM_EPS = 1e-08
ADAM_WD = 0.01
ADAM_STEP = 10

VMEM_LIMIT_V7X = 56 * 1024 * 1024
INV_SQRT2 = 1.0 / math.sqrt(2.0)
INV_SQRT_2PI = 1.0 / math.sqrt(2.0 * math.pi)


def _params(*sem):
    return pltpu.CompilerParams(dimension_semantics=sem if sem else None, vmem_limit_bytes=VMEM_LIMIT_V7X)


def _tile(n, pref):
    if n <= pref:
        return n
    for t in range(pref - pref % 8, 0, -8):
        if n % t == 0:
            return t
    raise ValueError((n, pref))


def _sds(shape, dtype):
    return jax.ShapeDtypeStruct(shape, dtype)


def _sigmoid(x):
    return 1.0 / (1.0 + jnp.exp(-x))


def _gelu(x):
    return 0.5 * x * (1.0 + lax.erf(x * INV_SQRT2))


def _gelu_grad(x):
    return 0.5 * (1.0 + lax.erf(x * INV_SQRT2)) + x * (INV_SQRT_2PI * jnp.exp(-0.5 * x * x))


def _ln_stats(v):
    mu = jnp.mean(v, axis=-1, keepdims=True)
    vc = v - mu
    rstd = lax.rsqrt(jnp.mean(vc * vc, axis=-1, keepdims=True) + EPS)
    return vc * rstd, rstd


def _ln_bwd(dvhat, vhat, rstd):
    return rstd * (dvhat - jnp.mean(dvhat, axis=-1, keepdims=True)
                   - vhat * jnp.mean(dvhat * vhat, axis=-1, keepdims=True))


def _colsum(v):
    return jnp.sum(v, axis=0, keepdims=True)


def _dot(a, b):
    return jnp.dot(a, b, preferred_element_type=F32)


def _dot_nt(a, b):
    return lax.dot_general(a, b, (((1,), (1,)), ((), ())), preferred_element_type=F32)


def _dot_tn(a, b):
    return lax.dot_general(a, b, (((0,), (0,)), ((), ())), preferred_element_type=F32)


def _tril_mask():
    r = lax.broadcasted_iota(jnp.int32, (CHUNK, CHUNK), 0)
    c = lax.broadcasted_iota(jnp.int32, (CHUNK, CHUNK), 1)
    return (r >= c).astype(F32)


def _mesh_pos():
    return tuple(lax.axis_index(a) for a in AXES)


class _Task:
    def __init__(self, arrays, out_shapes, scratch, start, finish):
        self.arrays, self.out_shapes, self.scratch, self.start, self.finish = arrays, out_shapes, scratch, start, finish


def _hosted(tasks, body, *, name, grid, in_specs, out_specs, out_shape, scratch_shapes=()):
    single = not isinstance(out_shape, (list, tuple))
    out_shape, out_specs = ([out_shape], [out_specs]) if single else (list(out_shape), list(out_specs))
    n_in, n_out, n_scr = len(in_specs), len(out_shape), len(scratch_shapes)
    sizes = [(len(t.arrays), len(t.out_shapes), len(t.scratch)) for t in tasks]
    t_in, t_out, t_scr = (sum(s[k] for s in sizes) for k in range(3))
    any_spec = pl.BlockSpec(memory_space=pl.ANY)

    def wrapped(*refs):
        refs = list(refs)
        ins, refs = refs[:n_in + t_in], refs[n_in + t_in:]
        outs, scr = refs[:n_out + t_out], refs[n_out + t_out:]

        def per_task(fn_name):
            i0, o0, s0 = n_in, n_out, n_scr
            for t, (ni, no, ns) in zip(tasks, sizes):
                getattr(t, fn_name)(ins[i0:i0 + ni], outs[o0:o0 + no], scr[s0:s0 + ns])
                i0, o0, s0 = i0 + ni, o0 + no, s0 + ns

        if tasks and grid:
            first, last = None, None
            for d, g in enumerate(grid):
                f, e = pl.program_id(d) == 0, pl.program_id(d) == g - 1
                first, last = (f, e) if first is None else (first & f, last & e)
            pl.when(first)(lambda: per_task("start"))
        elif tasks:
            per_task("start")
        body(*ins[:n_in], *outs[:n_out], *scr[:n_scr])
        if tasks and grid:
            pl.when(last)(lambda: per_task("finish"))
        elif tasks:
            per_task("finish")

    call = pl.pallas_call(
        wrapped, name=name, grid=grid,
        in_specs=list(in_specs) + [any_spec] * t_in, out_specs=out_specs + [any_spec] * t_out,
        out_shape=out_shape + [s for t in tasks for s in t.out_shapes],
        scratch_shapes=list(scratch_shapes) + [s for t in tasks for s in t.scratch],
        compiler_params=_params(*(("arbitrary",) * len(grid))))

    def run(*operands):
        res = list(call(*operands, *[a for t in tasks for a in t.arrays]))
        host, rest, per = res[:n_out], res[n_out:], []
        for _, no, _ in sizes:
            per.append(rest[:no])
            rest = rest[no:]
        return (host[0] if single else host), per

    return run


def _transfer(tasks, name):
    return _hosted(tasks, lambda: None, name=name, grid=(), in_specs=[], out_specs=[], out_shape=[])()[1]


def _gather_task(arrs):
    n = len(arrs)

    def plan(ins, outs, sems):
        send_sems, recv_sems, local_sems = sems
        x, y, c = _mesh_pos()
        me, sibling = (x, y, c), (x, y, 1 - c)
        chips = [(1 - x, y), (x, 1 - y), (1 - x, 1 - y)]

        def slot(a, p):
            return outs[a].at[4 * p[0] + 2 * p[1] + p[2]]

        def copy(a, k, block, to, src=None):
            dst = slot(a, block)
            return pltpu.make_async_remote_copy(
                src_ref=dst if src is None else src, dst_ref=dst, send_sem=send_sems.at[a, k],
                recv_sem=recv_sems.at[a, k], device_id=to, device_id_type=MESH)

        mine = [pltpu.make_async_copy(ins[a], slot(a, me), local_sems.at[a]) for a in range(n)]
        first = []
        for a in range(n):
            first.append(copy(a, 0, me, sibling, src=ins[a]))
            first += [copy(a, 1 + j, me, (*chip, c), src=ins[a]) for j, chip in enumerate(chips)]
        return c, me, sibling, chips, copy, mine, first

    def start(ins, outs, sems):
        *_, mine, first = plan(ins, outs, sems)
        for cp in mine + first:
            cp.start()

    def finish(ins, outs, sems):
        c, me, sibling, chips, copy, mine, first = plan(ins, outs, sems)
        passed = []
        for j, chip in enumerate(chips):
            for a in range(n):
                copy(a, 1 + j, (*chip, c), me).wait_recv()
                fwd = copy(a, 4 + j, (*chip, c), sibling)
                fwd.start()
                passed.append(fwd)
        for a in range(n):
            copy(a, 0, sibling, me).wait_recv()
            for j, chip in enumerate(chips):
                copy(a, 4 + j, (*chip, 1 - c), me).wait_recv()
        for cp in first + passed:
            cp.wait_send()
        for m in mine:
            m.wait()

    return _Task(list(arrs), [_sds((N_DEV,) + a.shape, a.dtype) for a in arrs],
                 [pltpu.SemaphoreType.DMA((n, 7)), pltpu.SemaphoreType.DMA((n, 7)), pltpu.SemaphoreType.DMA((n,))],
                 start, finish)


def _sibling_task(arrs):
    n = len(arrs)

    def copies(ins, outs, sems):
        send_sems, recv_sems = sems
        x, y, c = _mesh_pos()
        return [pltpu.make_async_remote_copy(
            src_ref=ins[a].at[2 * q + (1 - c)], dst_ref=outs[a].at[q], send_sem=send_sems.at[a, q],
            recv_sem=recv_sems.at[a, q], device_id=(x, y, 1 - c), device_id_type=MESH)
            for a in range(n) for q in range(4)]

    def start(ins, outs, sems):
        for cp in copies(ins, outs, sems):
            cp.start()

    def finish(ins, outs, sems):
        for cp in copies(ins, outs, sems):
            cp.wait()

    return _Task(list(arrs), [_sds((4,) + a.shape[1:], a.dtype) for a in arrs],
                 [pltpu.SemaphoreType.DMA((n, 4)), pltpu.SemaphoreType.DMA((n, 4))], start, finish)


def _chips_task(arrs):
    n = len(arrs)

    def copies(ins, outs, sems):
        send_sems, recv_sems, local_sems = sems
        x, y, c = _mesh_pos()
        q_me = 2 * x + y
        chips = [(1 - x, y), (x, 1 - y), (1 - x, 1 - y)]
        own = [pltpu.make_async_copy(ins[a].at[q_me], outs[a].at[q_me], local_sems.at[a]) for a in range(n)]
        remote = [pltpu.make_async_remote_copy(
            src_ref=ins[a].at[2 * chip[0] + chip[1]], dst_ref=outs[a].at[q_me], send_sem=send_sems.at[a, j],
            recv_sem=recv_sems.at[a, j], device_id=(*chip, c), device_id_type=MESH)
            for a in range(n) for j, chip in enumerate(chips)]
        return own + remote

    def start(ins, outs, sems):
        for cp in copies(ins, outs, sems):
            cp.start()

    def finish(ins, outs, sems):
        for cp in copies(ins, outs, sems):
            cp.wait()

    return _Task(list(arrs), [_sds(a.shape, a.dtype) for a in arrs],
                 [pltpu.SemaphoreType.DMA((n, 3)), pltpu.SemaphoreType.DMA((n, 3)), pltpu.SemaphoreType.DMA((n,))],
                 start, finish)


def _sibling_sum(arr, land, core, name):
    _, rows, cols = arr.shape
    tr = _tile(rows, 512)
    arr4 = arr.reshape(4, 2, rows, cols)

    def body(c_ref, a_ref, l_ref, o_ref):
        o_ref[...] = (a_ref[...] + l_ref[...]).astype(BF16)

    grid_spec = pltpu.PrefetchScalarGridSpec(
        num_scalar_prefetch=1, grid=(4, rows // tr),
        in_specs=[pl.BlockSpec((None, None, tr, cols), lambda q, r, c_ref: (q, c_ref[0], r, 0)),
                  pl.BlockSpec((None, tr, cols), lambda q, r, c_ref: (q, r, 0))],
        out_specs=pl.BlockSpec((None, tr, cols), lambda q, r, c_ref: (q, r, 0)))
    return pl.pallas_call(body, name=name, grid_spec=grid_spec, out_shape=_sds((4, rows, cols), BF16),
                          compiler_params=_params("arbitrary", "arbitrary"))(core, arr4, land)


def _sibling_sums(arrs, land, core, tag):
    return [_sibling_sum(a, l, core, name=f"rs_sum_{tag}_{k}") for k, (a, l) in enumerate(zip(arrs, land))]


def _adam(gparts, w, m, v, name):
    n_l = len(gparts)
    p, rows, cols = gparts[0].shape
    tr = _tile(rows, 256)
    n_r = rows // tr
    c1 = 1.0 - ADAM_B1 ** ADAM_STEP
    c2 = 1.0 - ADAM_B2 ** ADAM_STEP

    def body(*refs):
        g_refs = refs[:n_l]
        w_ref, m_ref, v_ref, go_ref, d_ref, mo_ref, vo_ref = refs[n_l:]
        layer = pl.program_id(0)
        g = jnp.zeros((tr, cols), F32)
        for li, g_ref in enumerate(g_refs):
            gl = g_ref[0].astype(F32)
            for k in range(1, p):
                gl = gl + g_ref[k].astype(F32)
            g = gl if n_l == 1 else jnp.where(layer == li, gl, g)
        m_new = ADAM_B1 * m_ref[...] + (1.0 - ADAM_B1) * g
        v_new = ADAM_B2 * v_ref[...] + (1.0 - ADAM_B2) * (g * g)
        m_hat = m_new / c1
        v_hat = v_new / c2
        go_ref[...] = g
        d_ref[...] = -ADAM_LR * (m_hat / (jnp.sqrt(v_hat) + ADAM_EPS) + ADAM_WD * w_ref[...])
        mo_ref[...] = m_new
        vo_ref[...] = v_new

    def g_spec(li):
        def index(l, r):
            return (0, jnp.where(l == li, r, jnp.where(l < li, 0, n_r - 1)), 0)
        return pl.BlockSpec((p, tr, cols), index)

    blk = pl.BlockSpec((None, tr, cols), lambda l, r: (l, r, 0))
    return pl.pallas_call(
        body, name=name, grid=(n_l, n_r),
        in_specs=[g_spec(li) for li in range(n_l)] + [blk, blk, blk],
        out_specs=[blk] * 4, out_shape=[_sds((n_l, rows, cols), F32)] * 4,
        compiler_params=_params("arbitrary", "arbitrary"))(*gparts, w, m, v)


def _adam_nd(grad, w, m, v, name):
    shape = w.shape
    cols = shape[-1]
    rows = w.size // cols
    as_rows = lambda a: a.reshape(1, rows, cols)
    out = _adam([as_rows(grad)], as_rows(w), as_rows(m), as_rows(v), name)
    return [o.reshape(shape) for o in out]


def _norm_mod(x, g, sc, sh, name):
    t = x.shape[0]
    tm = _tile(t, 512)

    def body(x_ref, g_ref, sc_ref, sh_ref, h_ref):
        xv = x_ref[...]
        r = lax.rsqrt(jnp.mean(xv * xv, axis=-1, keepdims=True) + EPS)
        h_ref[...] = (xv * r * g_ref[...] * (1.0 + sc_ref[...]) + sh_ref[...]).astype(BF16)

    row = pl.BlockSpec((1, D), lambda i: (0, 0))
    blk = pl.BlockSpec((tm, D), lambda i: (i, 0))
    return pl.pallas_call(body, name=name, grid=(t // tm,), in_specs=[blk, row, row, row], out_specs=blk,
                          out_shape=_sds((t, D), BF16), compiler_params=_params("arbitrary"))(x, g, sc, sh)


def _with_tasks(res_per, tasks):
    return res_per if tasks else res_per[0]


def _mm_cols(a, b8, bias8, name, tasks=()):
    t, k = a.shape
    j, _, n = b8.shape
    tm = _tile(t, 1024)

    def body(a_ref, b_ref, bias_ref, o_ref):
        o_ref[...] = _dot(a_ref[...], b_ref[...]) + bias_ref[...]

    return _with_tasks(_hosted(
        tasks, body, name=name, grid=(j, t // tm),
        in_specs=[pl.BlockSpec((tm, k), lambda jj, i: (i, 0)),
                  pl.BlockSpec((None, k, n), lambda jj, i: (jj, 0, 0)),
                  pl.BlockSpec((None, 1, n), lambda jj, i: (jj, 0, 0))],
        out_specs=pl.BlockSpec((None, tm, n), lambda jj, i: (jj, i, 0)),
        out_shape=_sds((j, t, n), F32))(a, b8, bias8), tasks)


def _halo_before(tm, col):
    return pl.BlockSpec((None, HALO, D), lambda i: (col, jnp.maximum(i * (tm // HALO) - 1, 0), 0))


def _pool_forward(p_ext, t0, rows):
    t = t0 + lax.broadcasted_iota(jnp.int32, (rows, 1), 0)
    out = []
    for gi, win in enumerate(POOL_WINDOWS):
        e = p_ext[:, gi * POOL_GC:(gi + 1) * POOL_GC]
        s, sh = e, 1
        while sh < win:
            s = s + pltpu.roll(s, sh, 0)
            sh *= 2
        cnt = jnp.minimum(t + 1, win).astype(F32)
        out.append(s[HALO:] / cnt - e[HALO:])
    return out


def _fill_shift_bank(bank_ref, ext, causal):
    n = ext.shape[0]
    bank_ref[0] = ext
    for b in range(1, SUBLANE):
        bank_ref[b] = pltpu.roll(ext, b if causal else n - b, 0)


def _branches_fwd(z8, ln_g, ln_b, w_s, bs_b, pool_w, pool_scale, conv_w, conv_b, cln_g, cln_b, name, tasks=()):
    t = z8.shape[1]
    tm = _tile(t, 256)
    n_ext = tm + HALO

    def body(zu_ref, zv_ref, p_ref, a_ref, ag_ref, ph_ref, ah_ref, agh_ref, lng_ref, lnb_ref, ws_ref, bsb_ref,
             wp_ref, ps_ref, cw_ref, cb_ref, clg_ref, clb_ref, sa_ref, sb_ref, sc_ref, cv_ref, bank_ref):
        i = pl.program_id(0)
        has_past = (i > 0).astype(F32)
        u = _gelu(zu_ref[...])
        vhat, _ = _ln_stats(_gelu(zv_ref[...]))
        vb = (vhat * lng_ref[...] + lnb_ref[...]).astype(BF16)
        mask = _tril_mask()
        for g in range(SGU_G):
            cols = slice(g * CHUNK, (g + 1) * CHUNK)
            wm = (ws_ref[g] * mask).astype(BF16)
            for n in range(tm // CHUNK):
                rows = slice(n * CHUNK, (n + 1) * CHUNK)
                mixed = _dot(wm, vb[rows, cols]) + bsb_ref[g]
                sa_ref[rows, cols] = (u[rows, cols] * mixed).astype(BF16)
        p_ext = jnp.concatenate([ph_ref[...] * has_past, p_ref[...]], axis=0)
        pooled = _pool_forward(p_ext, i * tm, tm)
        for gi in range(len(POOL_WINDOWS)):
            cols = slice(gi * POOL_GC, (gi + 1) * POOL_GC)
            y = _dot(pooled[gi].astype(BF16), wp_ref[gi].astype(BF16))
            sb_ref[:, cols] = (y * ps_ref[:, cols]).astype(BF16)
        for cb in range(D // LANE):
            cols = slice(cb * LANE, (cb + 1) * LANE)
            zc = jnp.concatenate([ah_ref[:, cols] * has_past * _sigmoid(agh_ref[:, cols]),
                                  a_ref[:, cols] * _sigmoid(ag_ref[:, cols])], axis=0)
            _fill_shift_bank(bank_ref, zc, causal=True)
            for r0 in range(0, tm, CONV_STRIP):
                acc = jnp.zeros((CONV_STRIP, LANE), F32) + cb_ref[:, cols]
                for k in range(CONV_K):
                    hi, lo = divmod(CONV_K - 1 - k, SUBLANE)
                    acc = acc + cw_ref[k:k + 1, cols] * bank_ref[lo, pl.ds(HALO - SUBLANE * hi + r0, CONV_STRIP), :]
                cv_ref[r0:r0 + CONV_STRIP, cols] = acc
        cv = cv_ref[...]
        chat, _ = _ln_stats(cv)
        cl = chat * clg_ref[...] + clb_ref[...]
        sc_ref[...] = (cl * _sigmoid(cl)).astype(BF16)

    def col(j):
        return pl.BlockSpec((None, tm, D), lambda i: (j, i, 0))

    row = pl.BlockSpec((1, D), lambda i: (0, 0))
    full3 = lambda s: pl.BlockSpec(s, lambda i: (0, 0, 0))
    blk = pl.BlockSpec((tm, D), lambda i: (i, 0))
    return _with_tasks(_hosted(
        tasks, body, name=name, grid=(t // tm,),
        in_specs=[col(0), col(1), col(2), col(3), col(4), _halo_before(tm, 2), _halo_before(tm, 3),
                  _halo_before(tm, 4), row, row, full3((SGU_G, CHUNK, CHUNK)), full3((SGU_G, CHUNK, CHUNK)),
                  full3((4, POOL_GC, POOL_GC)), row, pl.BlockSpec((HALO, D), lambda i: (0, 0)), row, row, row],
        out_specs=[blk, blk, blk, blk],
        out_shape=[_sds((t, D), BF16)] * 3 + [_sds((t, D), F32)],
        scratch_shapes=[pltpu.VMEM((SUBLANE, n_ext, LANE), F32)],
    )(z8, z8, z8, z8, z8, z8, z8, z8, ln_g, ln_b, w_s, bs_b, pool_w, pool_scale, conv_w, conv_b, cln_g, cln_b), tasks)


def _proj_merge(sa, sb, sc, w_pa, w_pb, w_pc, z8, name, tasks=()):
    t = sa.shape[0]
    tm = _tile(t, 512)

    def body(sa_ref, sb_ref, sc_ref, wa_ref, wb_ref, wc_ref, g0_ref, g1_ref, g2_ref, ya_ref, yb_ref, yc_ref, m_ref):
        merged = jnp.zeros((tm, D), F32)
        for s_ref, w_ref, g_ref, y_ref in ((sa_ref, wa_ref, g0_ref, ya_ref), (sb_ref, wb_ref, g1_ref, yb_ref),
                                           (sc_ref, wc_ref, g2_ref, yc_ref)):
            y = _dot(s_ref[...], w_ref[...])
            y_ref[...] = y.astype(BF16)
            merged = merged + _sigmoid(g_ref[...]) * y
        m_ref[...] = merged.astype(BF16)

    blk = pl.BlockSpec((tm, D), lambda i: (i, 0))
    wspec = pl.BlockSpec((D, D), lambda i: (0, 0))
    gate = lambda j: pl.BlockSpec((None, tm, D), lambda i: (j, i, 0))
    return _with_tasks(_hosted(
        tasks, body, name=name, grid=(t // tm,),
        in_specs=[blk, blk, blk, wspec, wspec, wspec, gate(5), gate(6), gate(7)],
        out_specs=[blk] * 4, out_shape=[_sds((t, D), BF16)] * 4)(sa, sb, sc, w_pa, w_pb, w_pc, z8, z8, z8), tasks)


def _out_proj(merged, w_out, x, gt, name):
    t = x.shape[0]
    tm = _tile(t, 512)

    def body(m_ref, w_ref, x_ref, gt_ref, om_ref, x1_ref):
        om = _dot(m_ref[...], w_ref[...])
        om_ref[...] = om
        x1_ref[...] = x_ref[...] + gt_ref[...] * om

    blk = pl.BlockSpec((tm, D), lambda i: (i, 0))
    return pl.pallas_call(
        body, name=name, grid=(t // tm,),
        in_specs=[blk, pl.BlockSpec((D, D), lambda i: (0, 0)), blk, pl.BlockSpec((1, D), lambda i: (0, 0))],
        out_specs=[blk, blk], out_shape=[_sds((t, D), F32)] * 2,
        compiler_params=_params("arbitrary"))(merged, w_out, x, gt)


def _ffn_in(h2, wfi, name, tasks=()):
    t = h2.shape[0]
    tm = _tile(t, 512)

    def body(h_ref, w_ref, gu_ref, f_ref):
        hv = h_ref[...]
        gp = _dot(hv, w_ref[0])
        up = _dot(hv, w_ref[1])
        gu_ref[0] = gp
        gu_ref[1] = up
        f_ref[...] = (gp * _sigmoid(gp) * up).astype(BF16)

    return _with_tasks(_hosted(
        tasks, body, name=name, grid=(4, t // tm),
        in_specs=[pl.BlockSpec((tm, D), lambda j, i: (i, 0)),
                  pl.BlockSpec((2, None, D, FF_BLK), lambda j, i: (0, j, 0, 0))],
        out_specs=[pl.BlockSpec((2, None, tm, FF_BLK), lambda j, i: (0, j, i, 0)),
                   pl.BlockSpec((None, tm, FF_BLK), lambda j, i: (j, i, 0))],
        out_shape=[_sds((2, 4, t, FF_BLK), F32), _sds((4, t, FF_BLK), BF16)])(h2, wfi), tasks)


def _ffn_out(f4, wfo4, x1, gt, name, tasks=()):
    t = x1.shape[0]
    tm = _tile(t, 512)

    def body(f_ref, w_ref, x_ref, gt_ref, o_ref, x2_ref):
        j = pl.program_id(1)

        @pl.when(j == 0)
        def _():
            o_ref[...] = jnp.zeros_like(o_ref)

        o_ref[...] += _dot(f_ref[...], w_ref[...])

        @pl.when(j == 3)
        def _():
            x2_ref[...] = x_ref[...] + gt_ref[...] * o_ref[...]

    blk = pl.BlockSpec((tm, D), lambda i, j: (i, 0))
    return _with_tasks(_hosted(
        tasks, body, name=name, grid=(t // tm, 4),
        in_specs=[pl.BlockSpec((None, tm, FF_BLK), lambda i, j: (j, i, 0)),
                  pl.BlockSpec((None, FF_BLK, D), lambda i, j: (j, 0, 0)), blk,
                  pl.BlockSpec((1, D), lambda i, j: (0, 0))],
        out_specs=[blk, blk], out_shape=[_sds((t, D), F32)] * 2)(f4, wfo4, x1, gt), tasks)


def _final_loss(x, g, target, name):
    t = x.shape[0]
    tm = _tile(t, 512)

    def body(x_ref, g_ref, t_ref, loss_ref, dx_ref, dg_ref):
        @pl.when(pl.program_id(0) == 0)
        def _():
            loss_ref[...] = jnp.zeros_like(loss_ref)
            dg_ref[...] = jnp.zeros_like(dg_ref)

        xv = x_ref[...]
        r = lax.rsqrt(jnp.mean(xv * xv, axis=-1, keepdims=True) + EPS)
        xn = xv * r
        diff = xn * g_ref[...] - t_ref[...]
        loss_ref[...] += 0.5 * jnp.sum(jnp.mean(diff * diff, axis=-1, keepdims=True))
        dy = diff * (1.0 / D)
        dg_ref[...] += _colsum(dy * xn)
        dxn = dy * g_ref[...]
        dx_ref[...] = r * (dxn - xn * jnp.mean(dxn * xn, axis=-1, keepdims=True))

    blk = pl.BlockSpec((tm, D), lambda i: (i, 0))
    row = pl.BlockSpec((1, D), lambda i: (0, 0))
    return pl.pallas_call(
        body, name=name, grid=(t // tm,), in_specs=[blk, row, blk],
        out_specs=[pl.BlockSpec((8, 128), lambda i: (0, 0)), blk, row],
        out_shape=[_sds((8, 128), F32), _sds((t, D), F32), _sds((1, D), F32)],
        compiler_params=_params("arbitrary"))(x, g, target)


def _gate_bwd(dx, o, gt, name):
    t = dx.shape[0]
    tm = _tile(t, 512)

    def body(dx_ref, o_ref, gt_ref, do_ref, dgt_ref):
        @pl.when(pl.program_id(0) == 0)
        def _():
            dgt_ref[...] = jnp.zeros_like(dgt_ref)

        dxv = dx_ref[...]
        do_ref[...] = (dxv * gt_ref[...]).astype(BF16)
        dgt_ref[...] += _colsum(dxv * o_ref[...])

    blk = pl.BlockSpec((tm, D), lambda i: (i, 0))
    row = pl.BlockSpec((1, D), lambda i: (0, 0))
    return pl.pallas_call(body, name=name, grid=(t // tm,), in_specs=[blk, blk, row], out_specs=[blk, row],
                          out_shape=[_sds((t, D), BF16), _sds((1, D), F32)],
                          compiler_params=_params("arbitrary"))(dx, o, gt)


def _norm_mod_bwd(dh, x, dres, g, sc, name, tasks=()):
    t = x.shape[0]
    tm = _tile(t, 512)

    def body(dh_ref, x_ref, dr_ref, g_ref, sc_ref, dx_ref, st_ref):
        @pl.when(pl.program_id(0) == 0)
        def _():
            st_ref[...] = jnp.zeros_like(st_ref)

        xv, dhv = x_ref[...], dh_ref[...]
        r = lax.rsqrt(jnp.mean(xv * xv, axis=-1, keepdims=True) + EPS)
        xn = xv * r
        gv, mod = g_ref[...], 1.0 + sc_ref[...]
        st_ref[0:1, :] += _colsum(dhv)
        st_ref[1:2, :] += _colsum(dhv * xn * gv)
        st_ref[2:3, :] += _colsum(dhv * xn * mod)
        dxn = dhv * gv * mod
        dx_ref[...] = dr_ref[...] + r * (dxn - xn * jnp.mean(dxn * xn, axis=-1, keepdims=True))

    blk = pl.BlockSpec((tm, D), lambda i: (i, 0))
    row = pl.BlockSpec((1, D), lambda i: (0, 0))
    return _with_tasks(_hosted(tasks, body, name=name, grid=(t // tm,), in_specs=[blk, blk, blk, row, row],
                               out_specs=[blk, pl.BlockSpec((3, D), lambda i: (0, 0))],
                               out_shape=[_sds((t, D), F32), _sds((3, D), F32)])(dh, x, dres, g, sc), tasks)


def _ffn_bwd_act(do, wfo4, gu, name, tasks=()):
    t = do.shape[0]
    tm = _tile(t, 512)

    def body(do_ref, w_ref, gu_ref, dgu_ref):
        df = _dot_nt(do_ref[...], w_ref[...])
        gp, up = gu_ref[0], gu_ref[1]
        sg = _sigmoid(gp)
        dgu_ref[0] = (df * up * (sg * (1.0 + gp * (1.0 - sg)))).astype(BF16)
        dgu_ref[1] = (df * (gp * sg)).astype(BF16)

    gu_spec = pl.BlockSpec((2, None, tm, FF_BLK), lambda j, i: (0, j, i, 0))
    return _with_tasks(_hosted(
        tasks, body, name=name, grid=(4, t // tm),
        in_specs=[pl.BlockSpec((tm, D), lambda j, i: (i, 0)),
                  pl.BlockSpec((None, FF_BLK, D), lambda j, i: (j, 0, 0)), gu_spec],
        out_specs=gu_spec, out_shape=_sds((2, 4, t, FF_BLK), BF16))(do, wfo4, gu), tasks)


def _mm_nt_sum(a8, b8, name, tasks=()):
    j, t, k = a8.shape
    n = b8.shape[1]
    tm = _tile(t, 512)

    def body(a_ref, b_ref, o_ref):
        @pl.when(pl.program_id(1) == 0)
        def _():
            o_ref[...] = jnp.zeros_like(o_ref)

        o_ref[...] += _dot_nt(a_ref[...], b_ref[...])

    return _with_tasks(_hosted(
        tasks, body, name=name, grid=(t // tm, j),
        in_specs=[pl.BlockSpec((None, tm, k), lambda i, jj: (jj, i, 0)),
                  pl.BlockSpec((None, n, k), lambda i, jj: (jj, 0, 0))],
        out_specs=pl.BlockSpec((tm, n), lambda i, jj: (i, 0)), out_shape=_sds((t, n), F32))(a8, b8), tasks)


def _mm_tn(a8, b8, name):
    ja, t, m = a8.shape
    jb, _, n = b8.shape
    j = max(ja, jb)
    tk = _tile(t, 1024)

    def body(a_ref, b_ref, o_ref):
        @pl.when(pl.program_id(1) == 0)
        def _():
            o_ref[...] = jnp.zeros_like(o_ref)

        o_ref[...] += _dot_tn(a_ref[...], b_ref[...])

    return pl.pallas_call(
        body, name=name, grid=(j, t // tk),
        in_specs=[pl.BlockSpec((None, tk, m), (lambda jj, kk: (jj, kk, 0)) if ja > 1 else (lambda jj, kk: (0, kk, 0))),
                  pl.BlockSpec((None, tk, n), (lambda jj, kk: (jj, kk, 0)) if jb > 1 else (lambda jj, kk: (0, kk, 0)))],
        out_specs=pl.BlockSpec((None, m, n), lambda jj, kk: (jj, 0, 0)), out_shape=_sds((j, m, n), F32),
        compiler_params=_params("arbitrary", "arbitrary"))(a8, b8)


def _merge_bwd(dom, w_out, z8, ya, yb, yc, name, tasks=()):
    t = dom.shape[0]
    tm = _tile(t, 512)

    def body(dom_ref, w_ref, g0_ref, g1_ref, g2_ref, ya_ref, yb_ref, yc_ref, dya_ref, dyb_ref, dyc_ref, dzg_ref,
             db_ref):
        @pl.when(pl.program_id(0) == 0)
        def _():
            db_ref[...] = jnp.zeros_like(db_ref)

        dm = _dot_nt(dom_ref[...], w_ref[...])
        for k, (g_ref, y_ref, dy_ref) in enumerate(((g0_ref, ya_ref, dya_ref), (g1_ref, yb_ref, dyb_ref),
                                                    (g2_ref, yc_ref, dyc_ref))):
            sg = _sigmoid(g_ref[...])
            dy_ref[...] = (dm * sg).astype(BF16)
            dzg = dm * y_ref[...].astype(F32) * (sg * (1.0 - sg))
            dzg_ref[k] = dzg.astype(BF16)
            db_ref[k:k + 1, :] += _colsum(dzg)

    blk = pl.BlockSpec((tm, D), lambda i: (i, 0))
    gate = lambda j: pl.BlockSpec((None, tm, D), lambda i: (j, i, 0))
    return _with_tasks(_hosted(
        tasks, body, name=name, grid=(t // tm,),
        in_specs=[blk, pl.BlockSpec((D, D), lambda i: (0, 0)), gate(5), gate(6), gate(7), blk, blk, blk],
        out_specs=[blk, blk, blk, pl.BlockSpec((3, tm, D), lambda i: (0, i, 0)), pl.BlockSpec((3, D), lambda i: (0, 0))],
        out_shape=[_sds((t, D), BF16)] * 3 + [_sds((3, t, D), BF16), _sds((3, D), F32)],
    )(dom, w_out, z8, z8, z8, ya, yb, yc), tasks)


def _branches_bwd(z8, cv, dsa, dsb, dsc, dzg, ln_g, ln_b, w_s, bs_b, pool_w, pool_scale, conv_w, cln_g, cln_b, name,
                  tasks=()):
    t = z8.shape[1]
    tm = _tile(t, 128)
    n_ext = tm + HALO
    n_tiles = t // tm

    def body(zu_ref, zv_ref, p_ref, a_ref, ag_ref, ph_ref, ah_ref, agh_ref, cv_ref, cvf_ref, dsa_ref, dsb_ref,
             dsbf_ref, dsc_ref, dscf_ref, dzg_ref, lng_ref, lnb_ref, ws_ref, bsb_ref, wp_ref, ps_ref, cw_ref,
             clg_ref, clb_ref, dz_ref, dbin_ref, rows_ref, dws_ref, dbs_ref, dwp_ref, dcw_ref, mixed_scr, dvln_scr,
             dcv_scr, zbank_ref, dbank_ref, dcw8_scr):
        i = pl.program_id(0)

        @pl.when(i == 0)
        def _():
            for ref in (dbin_ref, rows_ref, dws_ref, dbs_ref, dwp_ref, dcw8_scr):
                ref[...] = jnp.zeros_like(ref)

        has_past = (i > 0).astype(F32)
        has_next = (i < n_tiles - 1).astype(F32)

        def emit(j, val):
            dz_ref[j] = val.astype(BF16)
            dbin_ref[j:j + 1, :] += _colsum(val)

        zu, zv = zu_ref[...], zv_ref[...]
        u = _gelu(zu)
        vhat, v_rstd = _ln_stats(_gelu(zv))
        vb = (vhat * lng_ref[...] + lnb_ref[...]).astype(BF16)
        dsa = dsa_ref[...]
        dmixed = dsa * u
        dmb = dmixed.astype(BF16)
        mask = _tril_mask()
        lane = lax.broadcasted_iota(jnp.int32, (CHUNK, CHUNK), 1)
        for g in range(SGU_G):
            cols = slice(g * CHUNK, (g + 1) * CHUNK)
            wm = (ws_ref[g] * mask).astype(BF16)
            dws = jnp.zeros((CHUNK, CHUNK), F32)
            dbs = jnp.zeros((CHUNK, 1), F32)
            for n in range(tm // CHUNK):
                rows = slice(n * CHUNK, (n + 1) * CHUNK)
                mixed_scr[rows, cols] = _dot(wm, vb[rows, cols]) + bsb_ref[g]
                dvln_scr[rows, cols] = _dot_tn(wm, dmb[rows, cols])
                dws = dws + _dot_nt(dmb[rows, cols], vb[rows, cols])
                dbs = dbs + jnp.sum(dmixed[rows, cols], axis=1, keepdims=True)
            dws_ref[g] += dws
            dbs_ref[...] += jnp.where(lane == g, dbs, 0.0)
        emit(0, dsa * mixed_scr[...] * _gelu_grad(zu))
        dvln = dvln_scr[...]
        rows_ref[0:1, :] += _colsum(dvln * vhat)
        rows_ref[1:2, :] += _colsum(dvln)
        emit(1, _ln_bwd(dvln * lng_ref[...], vhat, v_rstd) * _gelu_grad(zv))

        p_ext = jnp.concatenate([ph_ref[...] * has_past, p_ref[...]], axis=0)
        pooled = _pool_forward(p_ext, i * tm, tm)
        dsb = dsb_ref[...]
        dpl_ext = jnp.concatenate([dsb, dsbf_ref[...] * has_next], axis=0) * ps_ref[...]
        t_ext = i * tm + lax.broadcasted_iota(jnp.int32, (n_ext, 1), 0)
        dp_parts = []
        for gi, win in enumerate(POOL_WINDOWS):
            cols = slice(gi * POOL_GC, (gi + 1) * POOL_GC)
            pooled_b = pooled[gi].astype(BF16)
            wpb = wp_ref[gi].astype(BF16)
            rows_ref[2:3, cols] += _colsum(dsb[:, cols] * _dot(pooled_b, wpb))
            dplb = dpl_ext[:, cols].astype(BF16)
            dwp_ref[gi] += _dot_tn(pooled_b, dplb[:tm])
            dpooled = _dot_nt(dplb, wpb)
            s, sh = dpooled / jnp.minimum(t_ext + 1, win).astype(F32), 1
            while sh < win:
                s = s + pltpu.roll(s, n_ext - sh, 0)
                sh *= 2
            dp_parts.append(s[:tm] - dpooled[:tm])
        emit(2, jnp.concatenate(dp_parts, axis=1))

        cv_ext = jnp.concatenate([cv_ref[...], cvf_ref[...]], axis=0)
        chat, c_rstd = _ln_stats(cv_ext)
        cl = chat * clg_ref[...] + clb_ref[...]
        sg = _sigmoid(cl)
        dcl = jnp.concatenate([dsc_ref[...], dscf_ref[...]], axis=0) * (sg * (1.0 + cl * (1.0 - sg)))
        rows_ref[4:5, :] += _colsum((dcl * chat)[:tm])
        rows_ref[5:6, :] += _colsum(dcl[:tm])
        in_seq = jnp.concatenate([jnp.ones((tm, 1), F32), jnp.zeros((HALO, 1), F32) + has_next], axis=0)
        dcv = jnp.where(in_seq > 0.0, _ln_bwd(dcl * clg_ref[...], chat, c_rstd), 0.0)
        rows_ref[3:4, :] += _colsum(dcv[:tm])
        dcv_scr[...] = dcv
        for cb in range(D // LANE):
            cols = slice(cb * LANE, (cb + 1) * LANE)
            zc = jnp.concatenate([ah_ref[:, cols] * has_past * _sigmoid(agh_ref[:, cols]),
                                  a_ref[:, cols] * _sigmoid(ag_ref[:, cols])], axis=0)
            _fill_shift_bank(zbank_ref, zc, causal=True)
            _fill_shift_bank(dbank_ref, dcv_scr[:, cols], causal=False)
            for r0 in range(0, tm, CONV_STRIP):
                rows = slice(r0, r0 + CONV_STRIP)
                dcv_s = dcv_scr[rows, cols]
                dzc = jnp.zeros((CONV_STRIP, LANE), F32)
                for k in range(CONV_K):
                    hi, lo = divmod(CONV_K - 1 - k, SUBLANE)
                    z_win = zbank_ref[lo, pl.ds(HALO - SUBLANE * hi + r0, CONV_STRIP), :]
                    dcw8_scr[k, :, cols] += jnp.sum((dcv_s * z_win).reshape(CONV_STRIP // SUBLANE, SUBLANE, LANE), axis=0)
                    dzc = dzc + cw_ref[k:k + 1, cols] * dbank_ref[lo, pl.ds(SUBLANE * hi + r0, CONV_STRIP), :]
                a_s = a_ref[rows, cols]
                sga = _sigmoid(ag_ref[rows, cols])
                dza = dzc * sga
                dzag = dzc * a_s * (sga * (1.0 - sga))
                dz_ref[3, rows, cols] = dza.astype(BF16)
                dz_ref[4, rows, cols] = dzag.astype(BF16)
                dbin_ref[3:4, cols] += _colsum(dza)
                dbin_ref[4:5, cols] += _colsum(dzag)
        for k in range(3):
            dz_ref[5 + k] = dzg_ref[k]

        @pl.when(i == n_tiles - 1)
        def _():
            dcw_ref[...] = jnp.sum(dcw8_scr[...], axis=1)

    def col(j):
        return pl.BlockSpec((None, tm, D), lambda i: (j, i, 0))

    blk = pl.BlockSpec((tm, D), lambda i: (i, 0))
    after = pl.BlockSpec((HALO, D), lambda i: (jnp.minimum((i + 1) * (tm // HALO), t // HALO - 1), 0))
    row = pl.BlockSpec((1, D), lambda i: (0, 0))
    full2 = lambda s: pl.BlockSpec(s, lambda i: (0, 0))
    full3 = lambda s: pl.BlockSpec(s, lambda i: (0, 0, 0))
    return _with_tasks(_hosted(
        tasks, body, name=name, grid=(n_tiles,),
        in_specs=[col(0), col(1), col(2), col(3), col(4), _halo_before(tm, 2), _halo_before(tm, 3),
                  _halo_before(tm, 4), blk, after, blk, blk, after, blk, after,
                  pl.BlockSpec((3, tm, D), lambda i: (0, i, 0)), row, row, full3((SGU_G, CHUNK, CHUNK)),
                  full3((SGU_G, CHUNK, CHUNK)), full3((4, POOL_GC, POOL_GC)), row, full2((HALO, D)), row, row],
        out_specs=[pl.BlockSpec((8, tm, D), lambda i: (0, i, 0)), full2((8, D)), full2((8, D)),
                   full3((SGU_G, CHUNK, CHUNK)), full2((CHUNK, CHUNK)), full3((4, POOL_GC, POOL_GC)),
                   full2((HALO, D))],
        out_shape=[_sds((8, t, D), BF16), _sds((8, D), F32), _sds((8, D), F32), _sds((SGU_G, CHUNK, CHUNK), F32),
                   _sds((CHUNK, CHUNK), F32), _sds((4, POOL_GC, POOL_GC), F32), _sds((HALO, D), F32)],
        scratch_shapes=[pltpu.VMEM((tm, D), F32), pltpu.VMEM((tm, D), F32), pltpu.VMEM((n_ext, D), F32),
                        pltpu.VMEM((SUBLANE, n_ext, LANE), F32), pltpu.VMEM((SUBLANE, n_ext, LANE), F32),
                        pltpu.VMEM((HALO, SUBLANE, D), F32)],
    )(z8, z8, z8, z8, z8, z8, z8, z8, cv, cv, dsa, dsb, dsb, dsc, dsc, dzg, ln_g, ln_b, w_s, bs_b, pool_w,
      pool_scale, conv_w, cln_g, cln_b), tasks)


def _ada_fwd(c_all, w_ada, b_loc, name):
    def body(c_ref, w_ref, b_ref, o_ref):
        cv = c_ref[...]
        ca = (cv * _sigmoid(cv)).astype(BF16)
        for l in range(DEPTH):
            o_ref[l] = _dot(ca, w_ref[l].astype(BF16)) + b_ref[l]

    return pl.pallas_call(body, name=name, out_shape=_sds((DEPTH, N_DEV, ADA_BLK), F32),
                          compiler_params=_params())(c_all, w_ada, b_loc)


def _ada_bwd(c_all_t, d_loc, name):
    def body(c_ref, d_ref, o_ref):
        cv = c_ref[...]
        ca = cv * _sigmoid(cv)
        for l in range(DEPTH):
            acc = jnp.zeros((D, ADA_BLK), F32)
            for j in range(N_DEV):
                acc = acc + ca[:, j:j + 1] * d_ref[l, j:j + 1, :]
            o_ref[l] = acc

    return pl.pallas_call(body, name=name, out_shape=_sds((DEPTH, D, ADA_BLK), F32),
                          compiler_params=_params())(c_all_t, d_loc)


def _sum8(g8, name):
    _, rows, cols = g8.shape
    tr = _tile(rows, 256)

    def body(g_ref, o_ref):
        acc = g_ref[0]
        for k in range(1, N_DEV):
            acc = acc + g_ref[k]
        o_ref[...] = acc

    return pl.pallas_call(body, name=name, grid=(rows // tr,),
                          in_specs=[pl.BlockSpec((N_DEV, tr, cols), lambda r: (0, r, 0))],
                          out_specs=pl.BlockSpec((tr, cols), lambda r: (r, 0)), out_shape=_sds((rows, cols), F32),
                          compiler_params=_params("arbitrary"))(g8)


PROJ = ("w_pa", "w_pb", "w_pc", "w_out")
FWD_GATHERS = {
    (0, "in_proj"): tuple((n, 0) for n in PROJ),
    (0, "branches"): (("w_ffn_in", 0), ("w_ffn_out", 0)),
    (0, "proj_merge"): tuple((n, 1) for n in PROJ),
    (0, "ffn_in"): (("w_in", 1),),
    (0, "ffn_out"): (("w_ffn_out", 1),),
    (1, "in_proj"): (("w_ffn_in", 1),),
}
GRAD_GROUP = {"w_in": ("in", 0), "pool_w": ("in", 1), "w_pa": ("proj", 0), "w_pb": ("proj", 1), "w_pc": ("proj", 2),
              "w_out": ("proj", 3), "w_ffn_in": ("ffn", 0), "w_ffn_out": ("ffn", 1)}


def _run(factory, *args, tasks=(), **kw):
    out = factory(*args, tasks=tasks, **kw)
    return out if tasks else (out, [])


REPLICATED = ("b_ada", "g_mix", "b_in", "sgu_ln_g", "sgu_ln_b", "sgu_w_s", "sgu_b_s", "pool_scale", "conv_b",
              "conv_ln_g", "conv_ln_b", "g_ffn", "g_final")
WEIGHT_ORDER = ("w_ada", "b_ada", "g_mix", "w_in", "b_in", "sgu_ln_g", "sgu_ln_b", "sgu_w_s", "sgu_b_s", "w_pa",
                "pool_w", "pool_scale", "w_pb", "conv_w", "conv_b", "conv_ln_g", "conv_ln_b", "w_pc", "w_out",
                "g_ffn", "w_ffn_in", "w_ffn_out", "g_final")


def _rows(a):
    return a.reshape(-1, D)


def kernel(x, c, w_ada, b_ada, g_mix, w_in, b_in, sgu_ln_g, sgu_ln_b, sgu_w_s, sgu_b_s, w_pa, pool_w, pool_scale, w_pb, conv_w, conv_b, conv_ln_g, conv_ln_b, w_pc, w_out, g_ffn, w_ffn_in, w_ffn_out, g_final, loss_target, m_w_ada, m_b_ada, m_g_mix, m_w_in, m_b_in, m_sgu_ln_g, m_sgu_ln_b, m_sgu_w_s, m_sgu_b_s, m_w_pa, m_pool_w, m_pool_scale, m_w_pb, m_conv_w, m_conv_b, m_conv_ln_g, m_conv_ln_b, m_w_pc, m_w_out, m_g_ffn, m_w_ffn_in, m_w_ffn_out, m_g_final, v_w_ada, v_b_ada, v_g_mix, v_w_in, v_b_in, v_sgu_ln_g, v_sgu_ln_b, v_sgu_w_s, v_sgu_b_s, v_w_pa, v_pool_w, v_pool_scale, v_w_pb, v_conv_w, v_conv_b, v_conv_ln_g, v_conv_ln_b, v_w_pc, v_w_out, v_g_ffn, v_w_ffn_in, v_w_ffn_out, v_g_final):
    weights = dict(w_ada=w_ada, b_ada=b_ada, g_mix=g_mix, w_in=w_in, b_in=b_in, sgu_ln_g=sgu_ln_g, sgu_ln_b=sgu_ln_b,
                   sgu_w_s=sgu_w_s, sgu_b_s=sgu_b_s, w_pa=w_pa, pool_w=pool_w, pool_scale=pool_scale, w_pb=w_pb,
                   conv_w=conv_w, conv_b=conv_b, conv_ln_g=conv_ln_g, conv_ln_b=conv_ln_b, w_pc=w_pc, w_out=w_out,
                   g_ffn=g_ffn, w_ffn_in=w_ffn_in, w_ffn_out=w_ffn_out, g_final=g_final)
    mom1 = dict(w_ada=m_w_ada, b_ada=m_b_ada, g_mix=m_g_mix, w_in=m_w_in, b_in=m_b_in, sgu_ln_g=m_sgu_ln_g,
                sgu_ln_b=m_sgu_ln_b, sgu_w_s=m_sgu_w_s, sgu_b_s=m_sgu_b_s, w_pa=m_w_pa, pool_w=m_pool_w,
                pool_scale=m_pool_scale, w_pb=m_w_pb, conv_w=m_conv_w, conv_b=m_conv_b, conv_ln_g=m_conv_ln_g,
                conv_ln_b=m_conv_ln_b, w_pc=m_w_pc, w_out=m_w_out, g_ffn=m_g_ffn, w_ffn_in=m_w_ffn_in,
                w_ffn_out=m_w_ffn_out, g_final=m_g_final)
    mom2 = dict(w_ada=v_w_ada, b_ada=v_b_ada, g_mix=v_g_mix, w_in=v_w_in, b_in=v_b_in, sgu_ln_g=v_sgu_ln_g,
                sgu_ln_b=v_sgu_ln_b, sgu_w_s=v_sgu_w_s, sgu_b_s=v_sgu_b_s, w_pa=v_w_pa, pool_w=v_pool_w,
                pool_scale=v_pool_scale, w_pb=v_w_pb, conv_w=v_conv_w, conv_b=v_conv_b, conv_ln_g=v_conv_ln_g,
                conv_ln_b=v_conv_ln_b, w_pc=v_w_pc, w_out=v_w_out, g_ffn=v_g_ffn, w_ffn_in=v_w_ffn_in,
                w_ffn_out=v_w_ffn_out, g_final=v_g_final)

    t = x.shape[1]
    xs = x.reshape(t, D)
    target = loss_target.reshape(t, D)
    me = 4 * lax.axis_index("x") + 2 * lax.axis_index("y") + lax.axis_index("c")
    core = lax.axis_index("c").astype(jnp.int32).reshape(1)

    bf = lambda n, l: weights[n][l].astype(BF16)
    (first,) = _transfer([_gather_task([bf("w_in", 0), c, pool_w, conv_w])], name="gather_first")
    w_in0, c_all, pool_all, conv_all = first
    full = [dict(w_in=w_in0)] + [dict() for _ in range(1, DEPTH)]

    def gather_at(l, stage):
        names = FWD_GATHERS.get((l, stage), ())
        return [_gather_task([bf(n, ll) for n, ll in names])] if names else []

    def landed(l, stage, per):
        for (n, ll), arr in zip(FWD_GATHERS.get((l, stage), ()), per[0] if per else ()):
            full[ll][n] = arr

    c_all = c_all.reshape(N_DEV, D)
    pool_full = jnp.transpose(pool_all, (1, 2, 0, 3, 4)).reshape(DEPTH, 4, POOL_GC, POOL_GC)
    conv_full = jnp.transpose(conv_all, (1, 2, 0, 3)).reshape(DEPTH, CONV_K, D)
    conv_full = jnp.pad(conv_full, ((0, 0), (0, HALO - CONV_K), (0, 0)))

    b_loc = lax.dynamic_slice_in_dim(b_ada, me * ADA_BLK, ADA_BLK, axis=1).reshape(DEPTH, 1, ADA_BLK)
    ada_part = _ada_fwd(c_all, w_ada, b_loc, name="ada_fwd")
    ((ada_all,),) = _transfer([_gather_task([ada_part])], name="gather_ada")
    ada = lax.dynamic_index_in_dim(ada_all, me, axis=2, keepdims=False)
    ada = jnp.transpose(ada, (1, 0, 2)).reshape(DEPTH, 6, 1, D)

    bs_b = jnp.broadcast_to(sgu_b_s[..., None], (DEPTH, SGU_G, CHUNK, CHUNK))

    saved = []
    xl = xs
    for l in range(DEPTH):
        w = full[l]
        sh_m, sc_m, gt_m, sh_f, sc_f, gt_f = (ada[l, k] for k in range(6))
        row = lambda a: a[l].reshape(1, D)
        h = _norm_mod(xl, row(g_mix), sc_m, sh_m, name=f"norm_mix_{l}")
        z8, per = _run(_mm_cols, h, w["w_in"], b_in[l].reshape(8, 1, D), name=f"in_proj_{l}",
                       tasks=gather_at(l, "in_proj"))
        landed(l, "in_proj", per)
        (sa, sb, sc, cv), per = _run(
            _branches_fwd, z8, row(sgu_ln_g), row(sgu_ln_b), sgu_w_s[l], bs_b[l], pool_full[l], row(pool_scale),
            conv_full[l], row(conv_b), row(conv_ln_g), row(conv_ln_b), name=f"branches_{l}",
            tasks=gather_at(l, "branches"))
        landed(l, "branches", per)
        wpa, wpb, wpc, wout = (w[n].reshape(D, D) for n in PROJ)
        (ya, yb, yc, merged), per = _run(_proj_merge, sa, sb, sc, wpa, wpb, wpc, z8, name=f"proj_merge_{l}",
                                         tasks=gather_at(l, "proj_merge"))
        landed(l, "proj_merge", per)
        om, x1 = _out_proj(merged, wout, xl, gt_m, name=f"out_proj_{l}")
        h2 = _norm_mod(x1, row(g_ffn), sc_f, sh_f, name=f"norm_ffn_{l}")
        wfi = w["w_ffn_in"].reshape(2, 4, D, FF_BLK)
        (gu, f4), per = _run(_ffn_in, h2, wfi, name=f"ffn_in_{l}", tasks=gather_at(l, "ffn_in"))
        landed(l, "ffn_in", per)
        wfo4 = w["w_ffn_out"].reshape(4, FF_BLK, D)
        (o, x2), per = _run(_ffn_out, f4, wfo4, x1, gt_f, name=f"ffn_out_{l}", tasks=gather_at(l, "ffn_out"))
        landed(l, "ffn_out", per)
        saved.append(dict(x=xl, h=h, z8=z8, sa=sa, sb=sb, sc=sc, cv=cv, ya=ya, yb=yb, yc=yc, merged=merged, om=om,
                          x1=x1, h2=h2, gu=gu, f4=f4, o=o, wpa=wpa, wpb=wpb, wpc=wpc, wout=wout, wfi=wfi, wfo4=wfo4))
        xl = x2

    loss_tile, dx, dg_final = _final_loss(xl, g_final.reshape(1, D), target, name="final_loss")
    loss = lax.psum(loss_tile[0, 0], AXES)

    chip_parts = [dict() for _ in range(DEPTH)]
    in_group = [None] * DEPTH
    small_buf = [None] * DEPTH
    small_all = [None] * DEPTH
    tril = jnp.tril(jnp.ones((CHUNK, CHUNK), F32))
    for l in reversed(range(DEPTH)):
        s, w = saved[l], full[l]
        above = l + 1 if l + 1 < DEPTH else None
        sh_m, sc_m, gt_m, sh_f, sc_f, gt_f = (ada[l, k] for k in range(6))
        row = lambda a: a[l].reshape(1, D)
        do, dgt_f = _gate_bwd(dx, s["o"], gt_f, name=f"gate_bwd_ffn_{l}")
        tasks = [] if above is None else [_sibling_task(in_group[above]), _gather_task([small_buf[above]])]
        dgu, per = _run(_ffn_bwd_act, do, s["wfo4"], s["gu"], name=f"ffn_bwd_act_{l}", tasks=tasks)
        if above is not None:
            in_sums = _sibling_sums(in_group[above], per[0], core, tag=f"in_{above}")
            (small_all[above],) = per[1]
        d_wfo = _mm_tn(s["f4"], do[None], name=f"dw_ffn_out_{l}")
        dgu8 = dgu.reshape(8, t, FF_BLK)
        dh2, per = _run(_mm_nt_sum, dgu8, w["w_ffn_in"], name=f"dh_ffn_{l}",
                        tasks=[] if above is None else [_chips_task(in_sums)])
        if above is not None:
            chip_parts[above]["in"] = per[0]
        d_wfi = _mm_tn(s["h2"][None], dgu8, name=f"dw_ffn_in_{l}")
        dx1, st_f = _norm_mod_bwd(dh2, s["x1"], dx, row(g_ffn), sc_f, name=f"norm_ffn_bwd_{l}")
        dom, dgt_m = _gate_bwd(dx1, s["om"], gt_m, name=f"gate_bwd_mix_{l}")
        ffn_group = [d_wfi, d_wfo.reshape(8, D_FF // 8, D)]
        (dya, dyb, dyc, dzg, db_gate), per = _run(_merge_bwd, dom, s["wout"], s["z8"], s["ya"], s["yb"], s["yc"],
                                                  name=f"merge_bwd_{l}", tasks=[_sibling_task(ffn_group)])
        ffn_sums = _sibling_sums(ffn_group, per[0], core, tag=f"ffn_{l}")
        d_wout = _mm_tn(s["merged"][None], dom[None], name=f"dw_out_{l}")
        d_wpa = _mm_tn(s["sa"][None], dya[None], name=f"dw_pa_{l}")
        d_wpb = _mm_tn(s["sb"][None], dyb[None], name=f"dw_pb_{l}")
        d_wpc = _mm_tn(s["sc"][None], dyc[None], name=f"dw_pc_{l}")
        dsa = _mm_nt_sum(dya[None], s["wpa"][None], name=f"ds_a_{l}")
        dsb = _mm_nt_sum(dyb[None], s["wpb"][None], name=f"ds_b_{l}")
        dsc = _mm_nt_sum(dyc[None], s["wpc"][None], name=f"ds_c_{l}")
        proj_group = [g.reshape(8, D // 8, D) for g in (d_wpa, d_wpb, d_wpc, d_wout)]
        (dz8, db_in5, rows6, dws, dbs, dwp, dcw), per = _run(
            _branches_bwd, s["z8"], s["cv"], dsa, dsb, dsc, dzg, row(sgu_ln_g), row(sgu_ln_b), sgu_w_s[l], bs_b[l],
            pool_full[l], row(pool_scale), conv_full[l], row(conv_ln_g), row(conv_ln_b), name=f"branches_bwd_{l}",
            tasks=[_chips_task(ffn_sums), _sibling_task(proj_group)])
        chip_parts[l]["ffn"] = per[0]
        proj_sums = _sibling_sums(proj_group, per[1], core, tag=f"proj_{l}")
        dh, per = _run(_mm_nt_sum, dz8, w["w_in"], name=f"dh_in_{l}", tasks=[_chips_task(proj_sums)])
        chip_parts[l]["proj"] = per[0]
        d_win = _mm_tn(s["h"][None], dz8, name=f"dw_in_{l}")
        d_pool = jnp.transpose(dwp.reshape(4, N_DEV, POOL_GC // N_DEV, POOL_GC), (1, 0, 2, 3))
        in_group[l] = [d_win, d_pool.reshape(N_DEV, 4 * POOL_GC // N_DEV, POOL_GC)]
        (dx, st_m), per = _run(_norm_mod_bwd, dh, s["x"], dx1, row(g_mix), sc_m, name=f"norm_mix_bwd_{l}",
                               tasks=[_sibling_task(in_group[l])] if l == 0 else [])
        segs = [st_m[0:1], st_m[1:2], dgt_m, st_f[0:1], st_f[1:2], dgt_f]
        segs += [st_m[2:3], db_in5[0:5], db_gate, rows6[0:2], _rows(dws * tril), _rows(jnp.transpose(dbs[:, :SGU_G])),
                 rows6[2:6], st_f[2:3], dcw, dg_final if l == 0 else jnp.zeros((1, D), F32)]
        small_buf[l] = jnp.concatenate(segs, axis=0)
    in_sums = _sibling_sums(in_group[0], per[0], core, tag="in_0")
    chip_parts[0]["in"], (small_all[0],) = _transfer([_chips_task(in_sums), _gather_task([small_buf[0]])],
                                                     name="exchange_last")

    red = [_sum8(small_all[l], name=f"sum_small_grads_{l}") for l in range(DEPTH)]
    layers = lambda lo, hi: jnp.stack([red[l][lo:hi] for l in range(DEPTH)], axis=0)
    small_rows = dict(b_ada=(0, 6), g_mix=(6, 7), b_in=(7, 15), sgu_ln_g=(15, 16), sgu_ln_b=(16, 17),
                      sgu_w_s=(17, 145), sgu_b_s=(145, 146), pool_scale=(146, 147), conv_b=(147, 148),
                      conv_ln_g=(148, 149), conv_ln_b=(149, 150), g_ffn=(150, 151))
    grads = {n: layers(lo, hi).reshape(weights[n].shape) for n, (lo, hi) in small_rows.items()}
    grads["g_final"] = red[0][183]
    conv_g = layers(151, 151 + CONV_K)
    grads["conv_w"] = lax.dynamic_slice_in_dim(conv_g, me * (D // N_DEV), D // N_DEV, axis=2)
    d_ada_all = jnp.stack([small_all[l][:, 0:6] for l in range(DEPTH)], axis=1).reshape(N_DEV, DEPTH, 6 * D)
    d_loc = jnp.transpose(lax.dynamic_slice_in_dim(d_ada_all, me * ADA_BLK, ADA_BLK, axis=2), (1, 0, 2))
    grads["w_ada"] = _ada_bwd(jnp.transpose(c_all), d_loc, name="ada_bwd")

    out = {}
    for n in REPLICATED + ("conv_w", "w_ada"):
        out[n] = _adam_nd(grads[n], weights[n], mom1[n], mom2[n], name=f"adam_{n}")
    for n, (group, k) in GRAD_GROUP.items():
        parts = [chip_parts[l][group][k] for l in range(DEPTH)]
        shape = (DEPTH,) + parts[0].shape[1:]
        res = _adam(parts, weights[n].reshape(shape), mom1[n].reshape(shape), mom2[n].reshape(shape),
                    name=f"adam_{n}")
        out[n] = [r.reshape(weights[n].shape) for r in res]

    grad_x = dx.reshape(1, t, D)
    return (loss, grad_x, *[out[n][0] for n in WEIGHT_ORDER], *[out[n][1] for n in WEIGHT_ORDER],
            *[out[n][2] for n in WEIGHT_ORDER], *[out[n][3] for n in WEIGHT_ORDER])
```

```python
import math

import jax
import jax.numpy as jnp
from jax import lax
from jax.experimental import pallas as pl
from jax.experimental.pallas import tpu as pltpu

F32 = jnp.float32
BF16 = jnp.bfloat16
MESH = pl.DeviceIdType.MESH
AXES = ("x", "y", "c")
N_DEV = 8

D = 1024
DEPTH = 2
EPS = 1e-6
CHUNK = 128
SGU_G = 8
POOL_WINDOWS = (2, 4, 8, 16)
POOL_GC = 256
CONV_K = 31
HALO = 32
SUBLANE = 8
LANE = 128
CONV_STRIP = 128
D_FF = 2816
FF_BLK = D_FF // 4
ADA_BLK = 6 * D // N_DEV

ADAM_LR = 0.001
ADAM_B1 = 0.9
ADAM_B2 = 0.999
ADAM_EPS = 1e-08
ADAM_WD = 0.01
ADAM_STEP = 10

VMEM_LIMIT_V7X = 56 * 1024 * 1024
INV_SQRT2 = 1.0 / math.sqrt(2.0)
INV_SQRT_2PI = 1.0 / math.sqrt(2.0 * math.pi)


def _params(*sem):
    return pltpu.CompilerParams(dimension_semantics=sem if sem else None, vmem_limit_bytes=VMEM_LIMIT_V7X)


def _tile(n, pref):
    if n <= pref:
        return n
    for t in range(pref - pref % 8, 0, -8):
        if n % t == 0:
            return t
    raise ValueError((n, pref))


def _sds(shape, dtype):
    return jax.ShapeDtypeStruct(shape, dtype)


def _sigmoid(x):
    return 1.0 / (1.0 + jnp.exp(-x))


def _gelu(x):
    return 0.5 * x * (1.0 + lax.erf(x * INV_SQRT2))


def _gelu_grad(x):
    return 0.5 * (1.0 + lax.erf(x * INV_SQRT2)) + x * (INV_SQRT_2PI * jnp.exp(-0.5 * x * x))


def _ln_stats(v):
    mu = jnp.mean(v, axis=-1, keepdims=True)
    vc = v - mu
    rstd = lax.rsqrt(jnp.mean(vc * vc, axis=-1, keepdims=True) + EPS)
    return vc * rstd, rstd


def _ln_bwd(dvhat, vhat, rstd):
    return rstd * (dvhat - jnp.mean(dvhat, axis=-1, keepdims=True)
                   - vhat * jnp.mean(dvhat * vhat, axis=-1, keepdims=True))


def _colsum(v):
    return jnp.sum(v, axis=0, keepdims=True)


def _dot(a, b):
    return jnp.dot(a, b, preferred_element_type=F32)


def _dot_nt(a, b):
    return lax.dot_general(a, b, (((1,), (1,)), ((), ())), preferred_element_type=F32)


def _dot_tn(a, b):
    return lax.dot_general(a, b, (((0,), (0,)), ((), ())), preferred_element_type=F32)


def _tril_mask():
    r = lax.broadcasted_iota(jnp.int32, (CHUNK, CHUNK), 0)
    c = lax.broadcasted_iota(jnp.int32, (CHUNK, CHUNK), 1)
    return (r >= c).astype(F32)


def _mesh_pos():
    return tuple(lax.axis_index(a) for a in AXES)


class _Task:
    def __init__(self, arrays, out_shapes, scratch, start, finish):
        self.arrays, self.out_shapes, self.scratch, self.start, self.finish = arrays, out_shapes, scratch, start, finish


def _hosted(tasks, body, *, name, grid, in_specs, out_specs, out_shape, scratch_shapes=()):
    single = not isinstance(out_shape, (list, tuple))
    out_shape, out_specs = ([out_shape], [out_specs]) if single else (list(out_shape), list(out_specs))
    n_in, n_out, n_scr = len(in_specs), len(out_shape), len(scratch_shapes)
    sizes = [(len(t.arrays), len(t.out_shapes), len(t.scratch)) for t in tasks]
    t_in, t_out, t_scr = (sum(s[k] for s in sizes) for k in range(3))
    any_spec = pl.BlockSpec(memory_space=pl.ANY)

    def wrapped(*refs):
        refs = list(refs)
        ins, refs = refs[:n_in + t_in], refs[n_in + t_in:]
        outs, scr = refs[:n_out + t_out], refs[n_out + t_out:]

        def per_task(fn_name):
            i0, o0, s0 = n_in, n_out, n_scr
            for t, (ni, no, ns) in zip(tasks, sizes):
                getattr(t, fn_name)(ins[i0:i0 + ni], outs[o0:o0 + no], scr[s0:s0 + ns])
                i0, o0, s0 = i0 + ni, o0 + no, s0 + ns

        if tasks and grid:
            first, last = None, None
            for d, g in enumerate(grid):
                f, e = pl.program_id(d) == 0, pl.program_id(d) == g - 1
                first, last = (f, e) if first is None else (first & f, last & e)
            pl.when(first)(lambda: per_task("start"))
        elif tasks:
            per_task("start")
        body(*ins[:n_in], *outs[:n_out], *scr[:n_scr])
        if tasks and grid:
            pl.when(last)(lambda: per_task("finish"))
        elif tasks:
            per_task("finish")

    call = pl.pallas_call(
        wrapped, name=name, grid=grid,
        in_specs=list(in_specs) + [any_spec] * t_in, out_specs=out_specs + [any_spec] * t_out,
        out_shape=out_shape + [s for t in tasks for s in t.out_shapes],
        scratch_shapes=list(scratch_shapes) + [s for t in tasks for s in t.scratch],
        compiler_params=_params(*(("arbitrary",) * len(grid))))

    def run(*operands):
        res = list(call(*operands, *[a for t in tasks for a in t.arrays]))
        host, rest, per = res[:n_out], res[n_out:], []
        for _, no, _ in sizes:
            per.append(rest[:no])
            rest = rest[no:]
        return (host[0] if single else host), per

    return run


def _transfer(tasks, name):
    return _hosted(tasks, lambda: None, name=name, grid=(), in_specs=[], out_specs=[], out_shape=[])()[1]


def _gather_task(arrs):
    n = len(arrs)

    def plan(ins, outs, sems):
        send_sems, recv_sems, local_sems = sems
        x, y, c = _mesh_pos()
        me, sibling = (x, y, c), (x, y, 1 - c)
        chips = [(1 - x, y), (x, 1 - y), (1 - x, 1 - y)]

        def slot(a, p):
            return outs[a].at[4 * p[0] + 2 * p[1] + p[2]]

        def copy(a, k, block, to, src=None):
            dst = slot(a, block)
            return pltpu.make_async_remote_copy(
                src_ref=dst if src is None else src, dst_ref=dst, send_sem=send_sems.at[a, k],
                recv_sem=recv_sems.at[a, k], device_id=to, device_id_type=MESH)

        mine = [pltpu.make_async_copy(ins[a], slot(a, me), local_sems.at[a]) for a in range(n)]
        first = []
        for a in range(n):
            first.append(copy(a, 0, me, sibling, src=ins[a]))
            first += [copy(a, 1 + j, me, (*chip, c), src=ins[a]) for j, chip in enumerate(chips)]
        return c, me, sibling, chips, copy, mine, first

    def start(ins, outs, sems):
        *_, mine, first = plan(ins, outs, sems)
        for cp in mine + first:
            cp.start()

    def finish(ins, outs, sems):
        c, me, sibling, chips, copy, mine, first = plan(ins, outs, sems)
        passed = []
        for j, chip in enumerate(chips):
            for a in range(n):
                copy(a, 1 + j, (*chip, c), me).wait_recv()
                fwd = copy(a, 4 + j, (*chip, c), sibling)
                fwd.start()
                passed.append(fwd)
        for a in range(n):
            copy(a, 0, sibling, me).wait_recv()
            for j, chip in enumerate(chips):
                copy(a, 4 + j, (*chip, 1 - c), me).wait_recv()
        for cp in first + passed:
            cp.wait_send()
        for m in mine:
            m.wait()

    return _Task(list(arrs), [_sds((N_DEV,) + a.shape, a.dtype) for a in arrs],
                 [pltpu.SemaphoreType.DMA((n, 7)), pltpu.SemaphoreType.DMA((n, 7)), pltpu.SemaphoreType.DMA((n,))],
                 start, finish)


def _sibling_task(arrs):
    n = len(arrs)

    def copies(ins, outs, sems):
        send_sems, recv_sems = sems
        x, y, c = _mesh_pos()
        return [pltpu.make_async_remote_copy(
            src_ref=ins[a].at[2 * q + (1 - c)], dst_ref=outs[a].at[q], send_sem=send_sems.at[a, q],
            recv_sem=recv_sems.at[a, q], device_id=(x, y, 1 - c), device_id_type=MESH)
            for a in range(n) for q in range(4)]

    def start(ins, outs, sems):
        for cp in copies(ins, outs, sems):
            cp.start()

    def finish(ins, outs, sems):
        for cp in copies(ins, outs, sems):
            cp.wait()

    return _Task(list(arrs), [_sds((4,) + a.shape[1:], a.dtype) for a in arrs],
                 [pltpu.SemaphoreType.DMA((n, 4)), pltpu.SemaphoreType.DMA((n, 4))], start, finish)


def _chips_task(arrs):
    n = len(arrs)

    def copies(ins, outs, sems):
        send_sems, recv_sems, local_sems = sems
        x, y, c = _mesh_pos()
        q_me = 2 * x + y
        chips = [(1 - x, y), (x, 1 - y), (1 - x, 1 - y)]
        own = [pltpu.make_async_copy(ins[a].at[q_me], outs[a].at[q_me], local_sems.at[a]) for a in range(n)]
        remote = [pltpu.make_async_remote_copy(
            src_ref=ins[a].at[2 * chip[0] + chip[1]], dst_ref=outs[a].at[q_me], send_sem=send_sems.at[a, j],
            recv_sem=recv_sems.at[a, j], device_id=(*chip, c), device_id_type=MESH)
            for a in range(n) for j, chip in enumerate(chips)]
        return own + remote

    def start(ins, outs, sems):
        for cp in copies(ins, outs, sems):
            cp.start()

    def finish(ins, outs, sems):
        for cp in copies(ins, outs, sems):
            cp.wait()

    return _Task(list(arrs), [_sds(a.shape, a.dtype) for a in arrs],
                 [pltpu.SemaphoreType.DMA((n, 3)), pltpu.SemaphoreType.DMA((n, 3)), pltpu.SemaphoreType.DMA((n,))],
                 start, finish)


def _sibling_sum(arr, land, core, name):
    _, rows, cols = arr.shape
    tr = _tile(rows, 512)
    arr4 = arr.reshape(4, 2, rows, cols)

    def body(c_ref, a_ref, l_ref, o_ref):
        o_ref[...] = (a_ref[...] + l_ref[...]).astype(BF16)

    grid_spec = pltpu.PrefetchScalarGridSpec(
        num_scalar_prefetch=1, grid=(4, rows // tr),
        in_specs=[pl.BlockSpec((None, None, tr, cols), lambda q, r, c_ref: (q, c_ref[0], r, 0)),
                  pl.BlockSpec((None, tr, cols), lambda q, r, c_ref: (q, r, 0))],
        out_specs=pl.BlockSpec((None, tr, cols), lambda q, r, c_ref: (q, r, 0)))
    return pl.pallas_call(body, name=name, grid_spec=grid_spec, out_shape=_sds((4, rows, cols), BF16),
                          compiler_params=_params("arbitrary", "arbitrary"))(core, arr4, land)


def _sibling_sums(arrs, land, core, tag):
    return [_sibling_sum(a, l, core, name=f"rs_sum_{tag}_{k}") for k, (a, l) in enumerate(zip(arrs, land))]


def _adam(gparts, w, m, v, name):
    n_l = len(gparts)
    p, rows, cols = gparts[0].shape
    tr = _tile(rows, 256)
    n_r = rows // tr
    c1 = 1.0 - ADAM_B1 ** ADAM_STEP
    c2 = 1.0 - ADAM_B2 ** ADAM_STEP

    def body(*refs):
        g_refs = refs[:n_l]
        w_ref, m_ref, v_ref, go_ref, d_ref, mo_ref, vo_ref = refs[n_l:]
        layer = pl.program_id(0)
        g = jnp.zeros((tr, cols), F32)
        for li, g_ref in enumerate(g_refs):
            gl = g_ref[0].astype(F32)
            for k in range(1, p):
                gl = gl + g_ref[k].astype(F32)
            g = gl if n_l == 1 else jnp.where(layer == li, gl, g)
        m_new = ADAM_B1 * m_ref[...] + (1.0 - ADAM_B1) * g
        v_new = ADAM_B2 * v_ref[...] + (1.0 - ADAM_B2) * (g * g)
        m_hat = m_new / c1
        v_hat = v_new / c2
        go_ref[...] = g
        d_ref[...] = -ADAM_LR * (m_hat / (jnp.sqrt(v_hat) + ADAM_EPS) + ADAM_WD * w_ref[...])
        mo_ref[...] = m_new
        vo_ref[...] = v_new

    def g_spec(li):
        def index(l, r):
            return (0, jnp.where(l == li, r, jnp.where(l < li, 0, n_r - 1)), 0)
        return pl.BlockSpec((p, tr, cols), index)

    blk = pl.BlockSpec((None, tr, cols), lambda l, r: (l, r, 0))
    return pl.pallas_call(
        body, name=name, grid=(n_l, n_r),
        in_specs=[g_spec(li) for li in range(n_l)] + [blk, blk, blk],
        out_specs=[blk] * 4, out_shape=[_sds((n_l, rows, cols), F32)] * 4,
        compiler_params=_params("arbitrary", "arbitrary"))(*gparts, w, m, v)


def _adam_nd(grad, w, m, v, name):
    shape = w.shape
    cols = shape[-1]
    rows = w.size // cols
    as_rows = lambda a: a.reshape(1, rows, cols)
    out = _adam([as_rows(grad)], as_rows(w), as_rows(m), as_rows(v), name)
    return [o.reshape(shape) for o in out]


def _norm_mod(x, g, sc, sh, name):
    t = x.shape[0]
    tm = _tile(t, 512)

    def body(x_ref, g_ref, sc_ref, sh_ref, h_ref):
        h_ref[...] = _modulated_norm(x_ref[...], g_ref[...], sc_ref[...], sh_ref[...])

    row = pl.BlockSpec((1, D), lambda i: (0, 0))
    blk = pl.BlockSpec((tm, D), lambda i: (i, 0))
    return pl.pallas_call(body, name=name, grid=(t // tm,), in_specs=[blk, row, row, row], out_specs=blk,
                          out_shape=_sds((t, D), BF16), compiler_params=_params("arbitrary"))(x, g, sc, sh)


def _with_tasks(res_per, tasks):
    return res_per if tasks else res_per[0]


def _mm_cols(a, b8, bias8, name, tasks=()):
    t, k = a.shape
    j, _, n = b8.shape
    tm = _tile(t, 1024)

    def body(a_ref, b_ref, bias_ref, o_ref):
        o_ref[...] = (_dot(a_ref[...], b_ref[...]) + bias_ref[...]).astype(BF16)

    return _with_tasks(_hosted(
        tasks, body, name=name, grid=(j, t // tm),
        in_specs=[pl.BlockSpec((tm, k), lambda jj, i: (i, 0)),
                  pl.BlockSpec((None, k, n), lambda jj, i: (jj, 0, 0)),
                  pl.BlockSpec((None, 1, n), lambda jj, i: (jj, 0, 0))],
        out_specs=pl.BlockSpec((None, tm, n), lambda jj, i: (jj, i, 0)),
        out_shape=_sds((j, t, n), BF16))(a, b8, bias8), tasks)


def _halo_before(tm, col):
    return pl.BlockSpec((None, HALO, D), lambda i: (col, jnp.maximum(i * (tm // HALO) - 1, 0), 0))


def _pool_forward(p_ext, t0, rows):
    t = t0 + lax.broadcasted_iota(jnp.int32, (rows, 1), 0)
    out = []
    for gi, win in enumerate(POOL_WINDOWS):
        e = p_ext[:, gi * POOL_GC:(gi + 1) * POOL_GC]
        s, sh = e, 1
        while sh < win:
            s = s + pltpu.roll(s, sh, 0)
            sh *= 2
        cnt = jnp.minimum(t + 1, win).astype(F32)
        out.append(s[HALO:] / cnt - e[HALO:])
    return out


def _fill_shift_bank(bank_ref, ext, causal):
    n = ext.shape[0]
    bank_ref[0] = ext
    for b in range(1, SUBLANE):
        bank_ref[b] = pltpu.roll(ext, b if causal else n - b, 0)


def _branches_fwd(z8, ln_g, ln_b, w_s, bs_b, pool_w, pool_scale, conv_w, conv_b, cln_g, cln_b, name, tasks=()):
    t = z8.shape[1]
    tm = _tile(t, 256)
    n_ext = tm + HALO

    def body(zu_ref, zv_ref, p_ref, a_ref, ag_ref, ph_ref, ah_ref, agh_ref, lng_ref, lnb_ref, ws_ref, bsb_ref,
             wp_ref, ps_ref, cw_ref, cb_ref, clg_ref, clb_ref, sa_ref, sb_ref, sc_ref, cv_ref, bank_ref):
        i = pl.program_id(0)
        has_past = (i > 0).astype(F32)
        u = _gelu(zu_ref[...].astype(F32))
        vhat, _ = _ln_stats(_gelu(zv_ref[...].astype(F32)))
        vb = (vhat * lng_ref[...] + lnb_ref[...]).astype(BF16)
        mask = _tril_mask()
        for g in range(SGU_G):
            cols = slice(g * CHUNK, (g + 1) * CHUNK)
            wm = (ws_ref[g] * mask).astype(BF16)
            for n in range(tm // CHUNK):
                rows = slice(n * CHUNK, (n + 1) * CHUNK)
                mixed = _dot(wm, vb[rows, cols]) + bsb_ref[g]
                sa_ref[rows, cols] = (u[rows, cols] * mixed).astype(BF16)
        p_ext = jnp.concatenate([ph_ref[...].astype(F32) * has_past, p_ref[...].astype(F32)], axis=0)
        pooled = _pool_forward(p_ext, i * tm, tm)
        for gi in range(len(POOL_WINDOWS)):
            cols = slice(gi * POOL_GC, (gi + 1) * POOL_GC)
            y = _dot(pooled[gi].astype(BF16), wp_ref[gi].astype(BF16))
            sb_ref[:, cols] = (y * ps_ref[:, cols]).astype(BF16)
        for cb in range(D // LANE):
            cols = slice(cb * LANE, (cb + 1) * LANE)
            zc = jnp.concatenate(
                [ah_ref[:, cols].astype(F32) * has_past * _sigmoid(agh_ref[:, cols].astype(F32)),
                 a_ref[:, cols].astype(F32) * _sigmoid(ag_ref[:, cols].astype(F32))], axis=0)
            _fill_shift_bank(bank_ref, zc, causal=True)
            for r0 in range(0, tm, CONV_STRIP):
                acc = jnp.zeros((CONV_STRIP, LANE), F32) + cb_ref[:, cols]
                for k in range(CONV_K):
                    hi, lo = divmod(CONV_K - 1 - k, SUBLANE)
                    acc = acc + cw_ref[k:k + 1, cols] * bank_ref[lo, pl.ds(HALO - SUBLANE * hi + r0, CONV_STRIP), :]
                cv_ref[r0:r0 + CONV_STRIP, cols] = acc
        cv = cv_ref[...]
        chat, _ = _ln_stats(cv)
        cl = chat * clg_ref[...] + clb_ref[...]
        sc_ref[...] = (cl * _sigmoid(cl)).astype(BF16)

    def col(j):
        return pl.BlockSpec((None, tm, D), lambda i: (j, i, 0))

    row = pl.BlockSpec((1, D), lambda i: (0, 0))
    full3 = lambda s: pl.BlockSpec(s, lambda i: (0, 0, 0))
    blk = pl.BlockSpec((tm, D), lambda i: (i, 0))
    return _with_tasks(_hosted(
        tasks, body, name=name, grid=(t // tm,),
        in_specs=[col(0), col(1), col(2), col(3), col(4), _halo_before(tm, 2), _halo_before(tm, 3),
                  _halo_before(tm, 4), row, row, full3((SGU_G, CHUNK, CHUNK)), full3((SGU_G, CHUNK, CHUNK)),
                  full3((4, POOL_GC, POOL_GC)), row, pl.BlockSpec((HALO, D), lambda i: (0, 0)), row, row, row],
        out_specs=[blk, blk, blk, blk],
        out_shape=[_sds((t, D), BF16)] * 3 + [_sds((t, D), F32)],
        scratch_shapes=[pltpu.VMEM((SUBLANE, n_ext, LANE), F32)],
    )(z8, z8, z8, z8, z8, z8, z8, z8, ln_g, ln_b, w_s, bs_b, pool_w, pool_scale, conv_w, conv_b, cln_g, cln_b), tasks)


def _proj_merge(sa, sb, sc, w_pa, w_pb, w_pc, z8, name, tasks=()):
    t = sa.shape[0]
    tm = _tile(t, 512)

    def body(sa_ref, sb_ref, sc_ref, wa_ref, wb_ref, wc_ref, g0_ref, g1_ref, g2_ref, ya_ref, yb_ref, yc_ref, m_ref):
        merged = jnp.zeros((tm, D), F32)
        for s_ref, w_ref, g_ref, y_ref in ((sa_ref, wa_ref, g0_ref, ya_ref), (sb_ref, wb_ref, g1_ref, yb_ref),
                                           (sc_ref, wc_ref, g2_ref, yc_ref)):
            y = _dot(s_ref[...], w_ref[...])
            y_ref[...] = y.astype(BF16)
            merged = merged + _sigmoid(g_ref[...].astype(F32)) * y
        m_ref[...] = merged.astype(BF16)

    blk = pl.BlockSpec((tm, D), lambda i: (i, 0))
    wspec = pl.BlockSpec((D, D), lambda i: (0, 0))
    gate = lambda j: pl.BlockSpec((None, tm, D), lambda i: (j, i, 0))
    return _with_tasks(_hosted(
        tasks, body, name=name, grid=(t // tm,),
        in_specs=[blk, blk, blk, wspec, wspec, wspec, gate(5), gate(6), gate(7)],
        out_specs=[blk] * 4, out_shape=[_sds((t, D), BF16)] * 4)(sa, sb, sc, w_pa, w_pb, w_pc, z8, z8, z8), tasks)


def _modulated_norm(xv, g, sc, sh):
    r = lax.rsqrt(jnp.mean(xv * xv, axis=-1, keepdims=True) + EPS)
    return (xv * r * g * (1.0 + sc) + sh).astype(BF16)


def _out_proj(merged, w_out, x, gt, g, sc, sh, name):
    t = x.shape[0]
    tm = _tile(t, 512)

    def body(m_ref, w_ref, x_ref, gt_ref, g_ref, sc_ref, sh_ref, om_ref, x1_ref, h2_ref):
        om = _dot(m_ref[...], w_ref[...])
        om_ref[...] = om
        x1 = x_ref[...] + gt_ref[...] * om
        x1_ref[...] = x1
        h2_ref[...] = _modulated_norm(x1, g_ref[...], sc_ref[...], sh_ref[...])

    blk = pl.BlockSpec((tm, D), lambda i: (i, 0))
    row = pl.BlockSpec((1, D), lambda i: (0, 0))
    return pl.pallas_call(
        body, name=name, grid=(t // tm,),
        in_specs=[blk, pl.BlockSpec((D, D), lambda i: (0, 0)), blk, row, row, row, row],
        out_specs=[blk, blk, blk], out_shape=[_sds((t, D), F32)] * 2 + [_sds((t, D), BF16)],
        compiler_params=_params("arbitrary"))(merged, w_out, x, gt, g, sc, sh)


def _ffn_in(h2, wfi, name, tasks=()):
    t = h2.shape[0]
    tm = _tile(t, 512)

    def body(h_ref, w_ref, gu_ref, f_ref):
        hv = h_ref[...]
        gp = _dot(hv, w_ref[0])
        up = _dot(hv, w_ref[1])
        gu_ref[0] = gp.astype(BF16)
        gu_ref[1] = up.astype(BF16)
        f_ref[...] = (gp * _sigmoid(gp) * up).astype(BF16)

    return _with_tasks(_hosted(
        tasks, body, name=name, grid=(4, t // tm),
        in_specs=[pl.BlockSpec((tm, D), lambda j, i: (i, 0)),
                  pl.BlockSpec((2, None, D, FF_BLK), lambda j, i: (0, j, 0, 0))],
        out_specs=[pl.BlockSpec((2, None, tm, FF_BLK), lambda j, i: (0, j, i, 0)),
                   pl.BlockSpec((None, tm, FF_BLK), lambda j, i: (j, i, 0))],
        out_shape=[_sds((2, 4, t, FF_BLK), BF16), _sds((4, t, FF_BLK), BF16)])(h2, wfi), tasks)


def _ffn_out(f4, wfo4, x1, gt, g, sc, sh, name, tasks=()):
    t = x1.shape[0]
    tm = _tile(t, 512)
    with_norm = g is not None

    def body(f_ref, w_ref, x_ref, gt_ref, *rest):
        o_ref, x2_ref = rest[-3:-1] if with_norm else rest[-2:]
        j = pl.program_id(1)

        @pl.when(j == 0)
        def _():
            o_ref[...] = jnp.zeros_like(o_ref)

        o_ref[...] += _dot(f_ref[...], w_ref[...])

        @pl.when(j == 3)
        def _():
            x2 = x_ref[...] + gt_ref[...] * o_ref[...]
            x2_ref[...] = x2
            if with_norm:
                g_ref, sc_ref, sh_ref = rest[:3]
                rest[-1][...] = _modulated_norm(x2, g_ref[...], sc_ref[...], sh_ref[...])

    blk = pl.BlockSpec((tm, D), lambda i, j: (i, 0))
    row = pl.BlockSpec((1, D), lambda i, j: (0, 0))
    norm_args = [g, sc, sh] if with_norm else []
    return _with_tasks(_hosted(
        tasks, body, name=name, grid=(t // tm, 4),
        in_specs=[pl.BlockSpec((None, tm, FF_BLK), lambda i, j: (j, i, 0)),
                  pl.BlockSpec((None, FF_BLK, D), lambda i, j: (j, 0, 0)), blk, row] + [row] * len(norm_args),
        out_specs=[blk, blk] + [blk] * with_norm,
        out_shape=[_sds((t, D), F32)] * 2 + [_sds((t, D), BF16)] * with_norm)(f4, wfo4, x1, gt, *norm_args), tasks)


def _gate_grads(dx, o_ref, gt_ref, do_ref, dgt_ref):
    do_ref[...] = (dx * gt_ref[...]).astype(BF16)
    dgt_ref[...] += _colsum(dx * o_ref[...])


def _final_loss(x, g, target, o, gt, name):
    t = x.shape[0]
    tm = _tile(t, 512)

    def body(x_ref, g_ref, t_ref, o_ref, gt_ref, loss_ref, dx_ref, dg_ref, do_ref, dgt_ref):
        @pl.when(pl.program_id(0) == 0)
        def _():
            for ref in (loss_ref, dg_ref, dgt_ref):
                ref[...] = jnp.zeros_like(ref)

        xv = x_ref[...]
        r = lax.rsqrt(jnp.mean(xv * xv, axis=-1, keepdims=True) + EPS)
        xn = xv * r
        diff = xn * g_ref[...] - t_ref[...]
        loss_ref[...] += 0.5 * jnp.sum(jnp.mean(diff * diff, axis=-1, keepdims=True))
        dy = diff * (1.0 / D)
        dg_ref[...] += _colsum(dy * xn)
        dxn = dy * g_ref[...]
        dx = r * (dxn - xn * jnp.mean(dxn * xn, axis=-1, keepdims=True))
        dx_ref[...] = dx
        _gate_grads(dx, o_ref, gt_ref, do_ref, dgt_ref)

    blk = pl.BlockSpec((tm, D), lambda i: (i, 0))
    row = pl.BlockSpec((1, D), lambda i: (0, 0))
    return pl.pallas_call(
        body, name=name, grid=(t // tm,), in_specs=[blk, row, blk, blk, row],
        out_specs=[pl.BlockSpec((8, 128), lambda i: (0, 0)), blk, row, blk, row],
        out_shape=[_sds((8, 128), F32), _sds((t, D), F32), _sds((1, D), F32), _sds((t, D), BF16), _sds((1, D), F32)],
        compiler_params=_params("arbitrary"))(x, g, target, o, gt)


def _norm_mod_bwd(dh, x, dres, g, sc, o, gt, name, tasks=()):
    t = x.shape[0]
    tm = _tile(t, 512)
    with_gate = o is not None

    def body(dh_ref, x_ref, dr_ref, g_ref, sc_ref, *rest):
        dx_ref, st_ref = rest[2:4] if with_gate else rest

        @pl.when(pl.program_id(0) == 0)
        def _():
            st_ref[...] = jnp.zeros_like(st_ref)
            if with_gate:
                rest[5][...] = jnp.zeros_like(rest[5])

        xv, dhv = x_ref[...], dh_ref[...]
        r = lax.rsqrt(jnp.mean(xv * xv, axis=-1, keepdims=True) + EPS)
        xn = xv * r
        gv, mod = g_ref[...], 1.0 + sc_ref[...]
        st_ref[0:1, :] += _colsum(dhv)
        st_ref[1:2, :] += _colsum(dhv * xn * gv)
        st_ref[2:3, :] += _colsum(dhv * xn * mod)
        dxn = dhv * gv * mod
        dx = dr_ref[...] + r * (dxn - xn * jnp.mean(dxn * xn, axis=-1, keepdims=True))
        dx_ref[...] = dx
        if with_gate:
            _gate_grads(dx, rest[0], rest[1], rest[4], rest[5])

    blk = pl.BlockSpec((tm, D), lambda i: (i, 0))
    row = pl.BlockSpec((1, D), lambda i: (0, 0))
    gate_args = [o, gt] if with_gate else []
    return _with_tasks(_hosted(
        tasks, body, name=name, grid=(t // tm,), in_specs=[blk, blk, blk, row, row] + [blk, row] * with_gate,
        out_specs=[blk, pl.BlockSpec((3, D), lambda i: (0, 0))] + [blk, row] * with_gate,
        out_shape=[_sds((t, D), F32), _sds((3, D), F32)] + [_sds((t, D), BF16), _sds((1, D), F32)] * with_gate,
    )(dh, x, dres, g, sc, *gate_args), tasks)


def _ffn_bwd_act(do, wfo4, gu, name, tasks=()):
    t = do.shape[0]
    tm = _tile(t, 512)

    def body(do_ref, w_ref, gu_ref, dgu_ref):
        df = _dot_nt(do_ref[...], w_ref[...])
        gp, up = gu_ref[0].astype(F32), gu_ref[1].astype(F32)
        sg = _sigmoid(gp)
        dgu_ref[0] = (df * up * (sg * (1.0 + gp * (1.0 - sg)))).astype(BF16)
        dgu_ref[1] = (df * (gp * sg)).astype(BF16)

    gu_spec = pl.BlockSpec((2, None, tm, FF_BLK), lambda j, i: (0, j, i, 0))
    return _with_tasks(_hosted(
        tasks, body, name=name, grid=(4, t // tm),
        in_specs=[pl.BlockSpec((tm, D), lambda j, i: (i, 0)),
                  pl.BlockSpec((None, FF_BLK, D), lambda j, i: (j, 0, 0)), gu_spec],
        out_specs=gu_spec, out_shape=_sds((2, 4, t, FF_BLK), BF16))(do, wfo4, gu), tasks)


def _mm_nt_sum(a8, b8, name, tasks=()):
    j, t, k = a8.shape
    n = b8.shape[1]
    tm = _tile(t, 1024 if j > 1 else 512)
    out_dtype = F32 if j > 1 else BF16

    def body(a_ref, b_ref, o_ref):
        if j == 1:
            o_ref[...] = _dot_nt(a_ref[...], b_ref[...]).astype(out_dtype)
            return

        @pl.when(pl.program_id(1) == 0)
        def _():
            o_ref[...] = jnp.zeros_like(o_ref)

        o_ref[...] += _dot_nt(a_ref[...], b_ref[...])

    return _with_tasks(_hosted(
        tasks, body, name=name, grid=(t // tm, j),
        in_specs=[pl.BlockSpec((None, tm, k), lambda i, jj: (jj, i, 0)),
                  pl.BlockSpec((None, n, k), lambda i, jj: (jj, 0, 0))],
        out_specs=pl.BlockSpec((tm, n), lambda i, jj: (i, 0)), out_shape=_sds((t, n), out_dtype))(a8, b8), tasks)


def _mm_tn(a8, b8, name):
    ja, t, m = a8.shape
    jb, _, n = b8.shape
    j = max(ja, jb)
    tk = _tile(t, 1024)

    def body(a_ref, b_ref, o_ref):
        @pl.when(pl.program_id(1) == 0)
        def _():
            o_ref[...] = jnp.zeros_like(o_ref)

        o_ref[...] += _dot_tn(a_ref[...], b_ref[...])

    return pl.pallas_call(
        body, name=name, grid=(j, t // tk),
        in_specs=[pl.BlockSpec((None, tk, m), (lambda jj, kk: (jj, kk, 0)) if ja > 1 else (lambda jj, kk: (0, kk, 0))),
                  pl.BlockSpec((None, tk, n), (lambda jj, kk: (jj, kk, 0)) if jb > 1 else (lambda jj, kk: (0, kk, 0)))],
        out_specs=pl.BlockSpec((None, m, n), lambda jj, kk: (jj, 0, 0)), out_shape=_sds((j, m, n), F32),
        compiler_params=_params("arbitrary", "arbitrary"))(a8, b8)


def _merge_bwd(dom, w_out, z8, ya, yb, yc, name, tasks=()):
    t = dom.shape[0]
    tm = _tile(t, 512)

    def body(dom_ref, w_ref, g0_ref, g1_ref, g2_ref, ya_ref, yb_ref, yc_ref, dya_ref, dyb_ref, dyc_ref, dzg_ref,
             db_ref):
        @pl.when(pl.program_id(0) == 0)
        def _():
            db_ref[...] = jnp.zeros_like(db_ref)

        dm = _dot_nt(dom_ref[...], w_ref[...])
        for k, (g_ref, y_ref, dy_ref) in enumerate(((g0_ref, ya_ref, dya_ref), (g1_ref, yb_ref, dyb_ref),
                                                    (g2_ref, yc_ref, dyc_ref))):
            sg = _sigmoid(g_ref[...].astype(F32))
            dy_ref[...] = (dm * sg).astype(BF16)
            dzg = dm * y_ref[...].astype(F32) * (sg * (1.0 - sg))
            dzg_ref[k] = dzg.astype(BF16)
            db_ref[k:k + 1, :] += _colsum(dzg)

    blk = pl.BlockSpec((tm, D), lambda i: (i, 0))
    gate = lambda j: pl.BlockSpec((None, tm, D), lambda i: (j, i, 0))
    return _with_tasks(_hosted(
        tasks, body, name=name, grid=(t // tm,),
        in_specs=[blk, pl.BlockSpec((D, D), lambda i: (0, 0)), gate(5), gate(6), gate(7), blk, blk, blk],
        out_specs=[blk, blk, blk, pl.BlockSpec((3, tm, D), lambda i: (0, i, 0)), pl.BlockSpec((3, D), lambda i: (0, 0))],
        out_shape=[_sds((t, D), BF16)] * 3 + [_sds((3, t, D), BF16), _sds((3, D), F32)],
    )(dom, w_out, z8, z8, z8, ya, yb, yc), tasks)


def _branches_bwd(z8, cv, dsa, dsb, dsc, dzg, ln_g, ln_b, w_s, bs_b, pool_w, pool_scale, conv_w, cln_g, cln_b, name,
                  tasks=()):
    t = z8.shape[1]
    tm = _tile(t, 128)
    n_ext = tm + HALO
    n_tiles = t // tm

    def body(zu_ref, zv_ref, p_ref, a_ref, ag_ref, ph_ref, ah_ref, agh_ref, cv_ref, cvf_ref, dsa_ref, dsb_ref,
             dsbf_ref, dsc_ref, dscf_ref, dzg_ref, lng_ref, lnb_ref, ws_ref, bsb_ref, wp_ref, ps_ref, cw_ref,
             clg_ref, clb_ref, dz_ref, dbin_ref, rows_ref, dws_ref, dbs_ref, dwp_ref, dcw_ref, mixed_scr, dvln_scr,
             dcv_scr, zbank_ref, dbank_ref, dcw8_scr):
        i = pl.program_id(0)

        @pl.when(i == 0)
        def _():
            for ref in (dbin_ref, rows_ref, dws_ref, dbs_ref, dwp_ref, dcw8_scr):
                ref[...] = jnp.zeros_like(ref)

        has_past = (i > 0).astype(F32)
        has_next = (i < n_tiles - 1).astype(F32)

        def emit(j, val):
            dz_ref[j] = val.astype(BF16)
            dbin_ref[j:j + 1, :] += _colsum(val)

        zu, zv = zu_ref[...].astype(F32), zv_ref[...].astype(F32)
        u = _gelu(zu)
        vhat, v_rstd = _ln_stats(_gelu(zv))
        vb = (vhat * lng_ref[...] + lnb_ref[...]).astype(BF16)
        dsa = dsa_ref[...].astype(F32)
        dmixed = dsa * u
        dmb = dmixed.astype(BF16)
        mask = _tril_mask()
        lane = lax.broadcasted_iota(jnp.int32, (CHUNK, CHUNK), 1)
        for g in range(SGU_G):
            cols = slice(g * CHUNK, (g + 1) * CHUNK)
            wm = (ws_ref[g] * mask).astype(BF16)
            dws = jnp.zeros((CHUNK, CHUNK), F32)
            dbs = jnp.zeros((CHUNK, 1), F32)
            for n in range(tm // CHUNK):
                rows = slice(n * CHUNK, (n + 1) * CHUNK)
                mixed_scr[rows, cols] = _dot(wm, vb[rows, cols]) + bsb_ref[g]
                dvln_scr[rows, cols] = _dot_tn(wm, dmb[rows, cols])
                dws = dws + _dot_nt(dmb[rows, cols], vb[rows, cols])
                dbs = dbs + jnp.sum(dmixed[rows, cols], axis=1, keepdims=True)
            dws_ref[g] += dws
            dbs_ref[...] += jnp.where(lane == g, dbs, 0.0)
        emit(0, dsa * mixed_scr[...] * _gelu_grad(zu))
        dvln = dvln_scr[...]
        rows_ref[0:1, :] += _colsum(dvln * vhat)
        rows_ref[1:2, :] += _colsum(dvln)
        emit(1, _ln_bwd(dvln * lng_ref[...], vhat, v_rstd) * _gelu_grad(zv))

        p_ext = jnp.concatenate([ph_ref[...].astype(F32) * has_past, p_ref[...].astype(F32)], axis=0)
        pooled = _pool_forward(p_ext, i * tm, tm)
        dsb = dsb_ref[...].astype(F32)
        dpl_ext = jnp.concatenate([dsb, dsbf_ref[...].astype(F32) * has_next], axis=0) * ps_ref[...]
        t_ext = i * tm + lax.broadcasted_iota(jnp.int32, (n_ext, 1), 0)
        dp_parts = []
        for gi, win in enumerate(POOL_WINDOWS):
            cols = slice(gi * POOL_GC, (gi + 1) * POOL_GC)
            pooled_b = pooled[gi].astype(BF16)
            wpb = wp_ref[gi].astype(BF16)
            rows_ref[2:3, cols] += _colsum(dsb[:, cols] * _dot(pooled_b, wpb))
            dplb = dpl_ext[:, cols].astype(BF16)
            dwp_ref[gi] += _dot_tn(pooled_b, dplb[:tm])
            dpooled = _dot_nt(dplb, wpb)
            s, sh = dpooled / jnp.minimum(t_ext + 1, win).astype(F32), 1
            while sh < win:
                s = s + pltpu.roll(s, n_ext - sh, 0)
                sh *= 2
            dp_parts.append(s[:tm] - dpooled[:tm])
        emit(2, jnp.concatenate(dp_parts, axis=1))

        cv_ext = jnp.concatenate([cv_ref[...], cvf_ref[...]], axis=0)
        chat, c_rstd = _ln_stats(cv_ext)
        cl = chat * clg_ref[...] + clb_ref[...]
        sg = _sigmoid(cl)
        dsc_ext = jnp.concatenate([dsc_ref[...].astype(F32), dscf_ref[...].astype(F32)], axis=0)
        dcl = dsc_ext * (sg * (1.0 + cl * (1.0 - sg)))
        rows_ref[4:5, :] += _colsum((dcl * chat)[:tm])
        rows_ref[5:6, :] += _colsum(dcl[:tm])
        in_seq = jnp.concatenate([jnp.ones((tm, 1), F32), jnp.zeros((HALO, 1), F32) + has_next], axis=0)
        dcv = jnp.where(in_seq > 0.0, _ln_bwd(dcl * clg_ref[...], chat, c_rstd), 0.0)
        rows_ref[3:4, :] += _colsum(dcv[:tm])
        dcv_scr[...] = dcv
        for cb in range(D // LANE):
            cols = slice(cb * LANE, (cb + 1) * LANE)
            zc = jnp.concatenate(
                [ah_ref[:, cols].astype(F32) * has_past * _sigmoid(agh_ref[:, cols].astype(F32)),
                 a_ref[:, cols].astype(F32) * _sigmoid(ag_ref[:, cols].astype(F32))], axis=0)
            _fill_shift_bank(zbank_ref, zc, causal=True)
            _fill_shift_bank(dbank_ref, dcv_scr[:, cols], causal=False)
            for r0 in range(0, tm, CONV_STRIP):
                rows = slice(r0, r0 + CONV_STRIP)
                dcv_s = dcv_scr[rows, cols]
                dzc = jnp.zeros((CONV_STRIP, LANE), F32)
                for k in range(CONV_K):
                    hi, lo = divmod(CONV_K - 1 - k, SUBLANE)
                    z_win = zbank_ref[lo, pl.ds(HALO - SUBLANE * hi + r0, CONV_STRIP), :]
                    dcw8_scr[k, :, cols] += jnp.sum((dcv_s * z_win).reshape(CONV_STRIP // SUBLANE, SUBLANE, LANE), axis=0)
                    dzc = dzc + cw_ref[k:k + 1, cols] * dbank_ref[lo, pl.ds(SUBLANE * hi + r0, CONV_STRIP), :]
                a_s = a_ref[rows, cols].astype(F32)
                sga = _sigmoid(ag_ref[rows, cols].astype(F32))
                dza = dzc * sga
                dzag = dzc * a_s * (sga * (1.0 - sga))
                dz_ref[3, rows, cols] = dza.astype(BF16)
                dz_ref[4, rows, cols] = dzag.astype(BF16)
                dbin_ref[3:4, cols] += _colsum(dza)
                dbin_ref[4:5, cols] += _colsum(dzag)
        for k in range(3):
            dz_ref[5 + k] = dzg_ref[k]

        @pl.when(i == n_tiles - 1)
        def _():
            dcw_ref[...] = jnp.sum(dcw8_scr[...], axis=1)

    def col(j):
        return pl.BlockSpec((None, tm, D), lambda i: (j, i, 0))

    blk = pl.BlockSpec((tm, D), lambda i: (i, 0))
    after = pl.BlockSpec((HALO, D), lambda i: (jnp.minimum((i + 1) * (tm // HALO), t // HALO - 1), 0))
    row = pl.BlockSpec((1, D), lambda i: (0, 0))
    full2 = lambda s: pl.BlockSpec(s, lambda i: (0, 0))
    full3 = lambda s: pl.BlockSpec(s, lambda i: (0, 0, 0))
    return _with_tasks(_hosted(
        tasks, body, name=name, grid=(n_tiles,),
        in_specs=[col(0), col(1), col(2), col(3), col(4), _halo_before(tm, 2), _halo_before(tm, 3),
                  _halo_before(tm, 4), blk, after, blk, blk, after, blk, after,
                  pl.BlockSpec((3, tm, D), lambda i: (0, i, 0)), row, row, full3((SGU_G, CHUNK, CHUNK)),
                  full3((SGU_G, CHUNK, CHUNK)), full3((4, POOL_GC, POOL_GC)), row, full2((HALO, D)), row, row],
        out_specs=[pl.BlockSpec((8, tm, D), lambda i: (0, i, 0)), full2((8, D)), full2((8, D)),
                   full3((SGU_G, CHUNK, CHUNK)), full2((CHUNK, CHUNK)), full3((4, POOL_GC, POOL_GC)),
                   full2((HALO, D))],
        out_shape=[_sds((8, t, D), BF16), _sds((8, D), F32), _sds((8, D), F32), _sds((SGU_G, CHUNK, CHUNK), F32),
                   _sds((CHUNK, CHUNK), F32), _sds((4, POOL_GC, POOL_GC), F32), _sds((HALO, D), F32)],
        scratch_shapes=[pltpu.VMEM((tm, D), F32), pltpu.VMEM((tm, D), F32), pltpu.VMEM((n_ext, D), F32),
                        pltpu.VMEM((SUBLANE, n_ext, LANE), F32), pltpu.VMEM((SUBLANE, n_ext, LANE), F32),
                        pltpu.VMEM((HALO, SUBLANE, D), F32)],
    )(z8, z8, z8, z8, z8, z8, z8, z8, cv, cv, dsa, dsb, dsb, dsc, dsc, dzg, ln_g, ln_b, w_s, bs_b, pool_w,
      pool_scale, conv_w, cln_g, cln_b), tasks)


def _ada_fwd(c_all, w_ada, b_loc, name):
    def body(c_ref, w_ref, b_ref, o_ref):
        cv = c_ref[...]
        ca = (cv * _sigmoid(cv)).astype(BF16)
        for l in range(DEPTH):
            o_ref[l] = _dot(ca, w_ref[l].astype(BF16)) + b_ref[l]

    return pl.pallas_call(body, name=name, out_shape=_sds((DEPTH, N_DEV, ADA_BLK), F32),
                          compiler_params=_params())(c_all, w_ada, b_loc)


def _ada_bwd(c_all_t, d_loc, name):
    def body(c_ref, d_ref, o_ref):
        cv = c_ref[...]
        ca = cv * _sigmoid(cv)
        for l in range(DEPTH):
            acc = jnp.zeros((D, ADA_BLK), F32)
            for j in range(N_DEV):
                acc = acc + ca[:, j:j + 1] * d_ref[l, j:j + 1, :]
            o_ref[l] = acc

    return pl.pallas_call(body, name=name, out_shape=_sds((DEPTH, D, ADA_BLK), F32),
                          compiler_params=_params())(c_all_t, d_loc)


def _sum8(g8, name):
    _, rows, cols = g8.shape
    tr = _tile(rows, 256)

    def body(g_ref, o_ref):
        acc = g_ref[0]
        for k in range(1, N_DEV):
            acc = acc + g_ref[k]
        o_ref[...] = acc

    return pl.pallas_call(body, name=name, grid=(rows // tr,),
                          in_specs=[pl.BlockSpec((N_DEV, tr, cols), lambda r: (0, r, 0))],
                          out_specs=pl.BlockSpec((tr, cols), lambda r: (r, 0)), out_shape=_sds((rows, cols), F32),
                          compiler_params=_params("arbitrary"))(g8)


PROJ = ("w_pa", "w_pb", "w_pc", "w_out")
FWD_GATHERS = {
    (0, "in_proj"): tuple((n, 0) for n in PROJ),
    (0, "branches"): (("w_ffn_in", 0), ("w_ffn_out", 0)),
    (0, "proj_merge"): tuple((n, 1) for n in PROJ),
    (0, "ffn_in"): (("w_in", 1),),
    (0, "ffn_out"): (("w_ffn_out", 1),),
    (1, "in_proj"): (("w_ffn_in", 1),),
}
GRAD_GROUP = {"w_in": ("in", 0), "pool_w": ("in", 1), "w_pa": ("proj", 0), "w_pb": ("proj", 1), "w_pc": ("proj", 2),
              "w_out": ("proj", 3), "w_ffn_in": ("ffn", 0), "w_ffn_out": ("ffn", 1)}


def _run(factory, *args, tasks=(), **kw):
    out = factory(*args, tasks=tasks, **kw)
    return out if tasks else (out, [])


REPLICATED = ("b_ada", "g_mix", "b_in", "sgu_ln_g", "sgu_ln_b", "sgu_w_s", "sgu_b_s", "pool_scale", "conv_b",
              "conv_ln_g", "conv_ln_b", "g_ffn", "g_final")
WEIGHT_ORDER = ("w_ada", "b_ada", "g_mix", "w_in", "b_in", "sgu_ln_g", "sgu_ln_b", "sgu_w_s", "sgu_b_s", "w_pa",
                "pool_w", "pool_scale", "w_pb", "conv_w", "conv_b", "conv_ln_g", "conv_ln_b", "w_pc", "w_out",
                "g_ffn", "w_ffn_in", "w_ffn_out", "g_final")


def _rows(a):
    return a.reshape(-1, D)


def kernel(x, c, w_ada, b_ada, g_mix, w_in, b_in, sgu_ln_g, sgu_ln_b, sgu_w_s, sgu_b_s, w_pa, pool_w, pool_scale, w_pb, conv_w, conv_b, conv_ln_g, conv_ln_b, w_pc, w_out, g_ffn, w_ffn_in, w_ffn_out, g_final, loss_target, m_w_ada, m_b_ada, m_g_mix, m_w_in, m_b_in, m_sgu_ln_g, m_sgu_ln_b, m_sgu_w_s, m_sgu_b_s, m_w_pa, m_pool_w, m_pool_scale, m_w_pb, m_conv_w, m_conv_b, m_conv_ln_g, m_conv_ln_b, m_w_pc, m_w_out, m_g_ffn, m_w_ffn_in, m_w_ffn_out, m_g_final, v_w_ada, v_b_ada, v_g_mix, v_w_in, v_b_in, v_sgu_ln_g, v_sgu_ln_b, v_sgu_w_s, v_sgu_b_s, v_w_pa, v_pool_w, v_pool_scale, v_w_pb, v_conv_w, v_conv_b, v_conv_ln_g, v_conv_ln_b, v_w_pc, v_w_out, v_g_ffn, v_w_ffn_in, v_w_ffn_out, v_g_final):
    weights = dict(w_ada=w_ada, b_ada=b_ada, g_mix=g_mix, w_in=w_in, b_in=b_in, sgu_ln_g=sgu_ln_g, sgu_ln_b=sgu_ln_b,
                   sgu_w_s=sgu_w_s, sgu_b_s=sgu_b_s, w_pa=w_pa, pool_w=pool_w, pool_scale=pool_scale, w_pb=w_pb,
                   conv_w=conv_w, conv_b=conv_b, conv_ln_g=conv_ln_g, conv_ln_b=conv_ln_b, w_pc=w_pc, w_out=w_out,
                   g_ffn=g_ffn, w_ffn_in=w_ffn_in, w_ffn_out=w_ffn_out, g_final=g_final)
    mom1 = dict(w_ada=m_w_ada, b_ada=m_b_ada, g_mix=m_g_mix, w_in=m_w_in, b_in=m_b_in, sgu_ln_g=m_sgu_ln_g,
                sgu_ln_b=m_sgu_ln_b, sgu_w_s=m_sgu_w_s, sgu_b_s=m_sgu_b_s, w_pa=m_w_pa, pool_w=m_pool_w,
                pool_scale=m_pool_scale, w_pb=m_w_pb, conv_w=m_conv_w, conv_b=m_conv_b, conv_ln_g=m_conv_ln_g,
                conv_ln_b=m_conv_ln_b, w_pc=m_w_pc, w_out=m_w_out, g_ffn=m_g_ffn, w_ffn_in=m_w_ffn_in,
                w_ffn_out=m_w_ffn_out, g_final=m_g_final)
    mom2 = dict(w_ada=v_w_ada, b_ada=v_b_ada, g_mix=v_g_mix, w_in=v_w_in, b_in=v_b_in, sgu_ln_g=v_sgu_ln_g,
                sgu_ln_b=v_sgu_ln_b, sgu_w_s=v_sgu_w_s, sgu_b_s=v_sgu_b_s, w_pa=v_w_pa, pool_w=v_pool_w,
                pool_scale=v_pool_scale, w_pb=v_w_pb, conv_w=v_conv_w, conv_b=v_conv_b, conv_ln_g=v_conv_ln_g,
                conv_ln_b=v_conv_ln_b, w_pc=v_w_pc, w_out=v_w_out, g_ffn=v_g_ffn, w_ffn_in=v_w_ffn_in,
                w_ffn_out=v_w_ffn_out, g_final=v_g_final)

    t = x.shape[1]
    xs = x.reshape(t, D)
    target = loss_target.reshape(t, D)
    me = 4 * lax.axis_index("x") + 2 * lax.axis_index("y") + lax.axis_index("c")
    core = lax.axis_index("c").astype(jnp.int32).reshape(1)

    bf = lambda n, l: weights[n][l].astype(BF16)
    (first,) = _transfer([_gather_task([bf("w_in", 0), c, pool_w, conv_w])], name="gather_first")
    w_in0, c_all, pool_all, conv_all = first
    full = [dict(w_in=w_in0)] + [dict() for _ in range(1, DEPTH)]

    def gather_at(l, stage):
        names = FWD_GATHERS.get((l, stage), ())
        return [_gather_task([bf(n, ll) for n, ll in names])] if names else []

    def landed(l, stage, per):
        for (n, ll), arr in zip(FWD_GATHERS.get((l, stage), ()), per[0] if per else ()):
            full[ll][n] = arr

    c_all = c_all.reshape(N_DEV, D)
    pool_full = jnp.transpose(pool_all, (1, 2, 0, 3, 4)).reshape(DEPTH, 4, POOL_GC, POOL_GC)
    conv_full = jnp.transpose(conv_all, (1, 2, 0, 3)).reshape(DEPTH, CONV_K, D)
    conv_full = jnp.pad(conv_full, ((0, 0), (0, HALO - CONV_K), (0, 0)))

    b_loc = lax.dynamic_slice_in_dim(b_ada, me * ADA_BLK, ADA_BLK, axis=1).reshape(DEPTH, 1, ADA_BLK)
    ada_part = _ada_fwd(c_all, w_ada, b_loc, name="ada_fwd")
    ((ada_all,),) = _transfer([_gather_task([ada_part])], name="gather_ada")
    ada = lax.dynamic_index_in_dim(ada_all, me, axis=2, keepdims=False)
    ada = jnp.transpose(ada, (1, 0, 2)).reshape(DEPTH, 6, 1, D)

    bs_b = jnp.broadcast_to(sgu_b_s[..., None], (DEPTH, SGU_G, CHUNK, CHUNK))

    saved = []
    xl = xs
    h = _norm_mod(xl, g_mix[0].reshape(1, D), ada[0, 1], ada[0, 0], name="norm_mix_0")
    for l in range(DEPTH):
        w = full[l]
        sh_m, sc_m, gt_m, sh_f, sc_f, gt_f = (ada[l, k] for k in range(6))
        row = lambda a: a[l].reshape(1, D)
        z8, per = _run(_mm_cols, h, w["w_in"], b_in[l].reshape(8, 1, D), name=f"in_proj_{l}",
                       tasks=gather_at(l, "in_proj"))
        landed(l, "in_proj", per)
        (sa, sb, sc, cv), per = _run(
            _branches_fwd, z8, row(sgu_ln_g), row(sgu_ln_b), sgu_w_s[l], bs_b[l], pool_full[l], row(pool_scale),
            conv_full[l], row(conv_b), row(conv_ln_g), row(conv_ln_b), name=f"branches_{l}",
            tasks=gather_at(l, "branches"))
        landed(l, "branches", per)
        wpa, wpb, wpc, wout = (w[n].reshape(D, D) for n in PROJ)
        (ya, yb, yc, merged), per = _run(_proj_merge, sa, sb, sc, wpa, wpb, wpc, z8, name=f"proj_merge_{l}",
                                         tasks=gather_at(l, "proj_merge"))
        landed(l, "proj_merge", per)
        om, x1, h2 = _out_proj(merged, wout, xl, gt_m, row(g_ffn), sc_f, sh_f, name=f"out_proj_{l}")
        wfi = w["w_ffn_in"].reshape(2, 4, D, FF_BLK)
        (gu, f4), per = _run(_ffn_in, h2, wfi, name=f"ffn_in_{l}", tasks=gather_at(l, "ffn_in"))
        landed(l, "ffn_in", per)
        wfo4 = w["w_ffn_out"].reshape(4, FF_BLK, D)
        nxt = (g_mix[l + 1].reshape(1, D), ada[l + 1, 1], ada[l + 1, 0]) if l + 1 < DEPTH else (None, None, None)
        res, per = _run(_ffn_out, f4, wfo4, x1, gt_f, *nxt, name=f"ffn_out_{l}", tasks=gather_at(l, "ffn_out"))
        landed(l, "ffn_out", per)
        o, x2 = res[0], res[1]
        saved.append(dict(x=xl, h=h, z8=z8, sa=sa, sb=sb, sc=sc, cv=cv, ya=ya, yb=yb, yc=yc, merged=merged, om=om,
                          x1=x1, h2=h2, gu=gu, f4=f4, o=o, wpa=wpa, wpb=wpb, wpc=wpc, wout=wout, wfi=wfi, wfo4=wfo4))
        xl = x2
        h = res[2] if l + 1 < DEPTH else None

    loss_tile, dx, dg_final, do, dgt_f = _final_loss(xl, g_final.reshape(1, D), target, saved[-1]["o"],
                                                     ada[DEPTH - 1, 5], name="final_loss")
    loss_row = jnp.broadcast_to(loss_tile[0:1, 0:1], (1, D))

    chip_parts = [dict() for _ in range(DEPTH)]
    small_buf = [None] * DEPTH
    small_all = [None] * DEPTH
    tril = jnp.tril(jnp.ones((CHUNK, CHUNK), F32))
    for l in reversed(range(DEPTH)):
        s, w = saved[l], full[l]
        above = l + 1 if l + 1 < DEPTH else None
        sh_m, sc_m, gt_m, sh_f, sc_f, gt_f = (ada[l, k] for k in range(6))
        row = lambda a: a[l].reshape(1, D)
        dgu, per = _run(_ffn_bwd_act, do, s["wfo4"], s["gu"], name=f"ffn_bwd_act_{l}",
                        tasks=[] if above is None else [_gather_task([small_buf[above]])])
        if above is not None:
            (small_all[above],) = per[0]
        d_wfo = _mm_tn(s["f4"], do[None], name=f"dw_ffn_out_{l}")
        dgu8 = dgu.reshape(8, t, FF_BLK)
        dh2, per = _run(_mm_nt_sum, dgu8, w["w_ffn_in"], name=f"dh_ffn_{l}",
                        tasks=[] if above is None else [_chips_task(in_sums)])
        if above is not None:
            chip_parts[above]["in"] = per[0]
        d_wfi = _mm_tn(s["h2"][None], dgu8, name=f"dw_ffn_in_{l}")
        dx1, st_f, dom, dgt_m = _norm_mod_bwd(dh2, s["x1"], dx, row(g_ffn), sc_f, s["om"], gt_m,
                                              name=f"norm_ffn_bwd_{l}")
        ffn_group = [d_wfi, d_wfo.reshape(8, D_FF // 8, D)]
        (dya, dyb, dyc, dzg, db_gate), per = _run(_merge_bwd, dom, s["wout"], s["z8"], s["ya"], s["yb"], s["yc"],
                                                  name=f"merge_bwd_{l}", tasks=[_sibling_task(ffn_group)])
        ffn_sums = _sibling_sums(ffn_group, per[0], core, tag=f"ffn_{l}")
        d_wout = _mm_tn(s["merged"][None], dom[None], name=f"dw_out_{l}")
        d_wpa = _mm_tn(s["sa"][None], dya[None], name=f"dw_pa_{l}")
        d_wpb = _mm_tn(s["sb"][None], dyb[None], name=f"dw_pb_{l}")
        d_wpc = _mm_tn(s["sc"][None], dyc[None], name=f"dw_pc_{l}")
        dsa = _mm_nt_sum(dya[None], s["wpa"][None], name=f"ds_a_{l}")
        dsb = _mm_nt_sum(dyb[None], s["wpb"][None], name=f"ds_b_{l}")
        dsc = _mm_nt_sum(dyc[None], s["wpc"][None], name=f"ds_c_{l}")
        proj_group = [g.reshape(8, D // 8, D) for g in (d_wpa, d_wpb, d_wpc, d_wout)]
        (dz8, db_in5, rows6, dws, dbs, dwp, dcw), per = _run(
            _branches_bwd, s["z8"], s["cv"], dsa, dsb, dsc, dzg, row(sgu_ln_g), row(sgu_ln_b), sgu_w_s[l], bs_b[l],
            pool_full[l], row(pool_scale), conv_full[l], row(conv_ln_g), row(conv_ln_b), name=f"branches_bwd_{l}",
            tasks=[_chips_task(ffn_sums), _sibling_task(proj_group)])
        chip_parts[l]["ffn"] = per[0]
        proj_sums = _sibling_sums(proj_group, per[1], core, tag=f"proj_{l}")
        d_win = _mm_tn(s["h"][None], dz8, name=f"dw_in_{l}")
        d_pool = jnp.transpose(dwp.reshape(4, N_DEV, POOL_GC // N_DEV, POOL_GC), (1, 0, 2, 3))
        in_group = [d_win, d_pool.reshape(N_DEV, 4 * POOL_GC // N_DEV, POOL_GC)]
        dh, per = _run(_mm_nt_sum, dz8, w["w_in"], name=f"dh_in_{l}",
                       tasks=[_chips_task(proj_sums), _sibling_task(in_group)])
        chip_parts[l]["proj"] = per[0]
        in_sums = _sibling_sums(in_group, per[1], core, tag=f"in_{l}")
        gate = (saved[l - 1]["o"], ada[l - 1, 5]) if l > 0 else (None, None)
        res, per = _run(_norm_mod_bwd, dh, s["x"], dx1, row(g_mix), sc_m, *gate, name=f"norm_mix_bwd_{l}",
                        tasks=[_chips_task(in_sums)] if l == 0 else [])
        dx, st_m = res[0], res[1]
        if l == 0:
            chip_parts[0]["in"] = per[0]
        segs = [st_m[0:1], st_m[1:2], dgt_m, st_f[0:1], st_f[1:2], dgt_f]
        segs += [st_m[2:3], db_in5[0:5], db_gate, rows6[0:2], _rows(dws * tril), _rows(jnp.transpose(dbs[:, :SGU_G])),
                 rows6[2:6], st_f[2:3], dcw, dg_final if l == 0 else loss_row]
        small_buf[l] = jnp.concatenate(segs, axis=0)
        if l > 0:
            do, dgt_f = res[2], res[3]
    small_all[0] = _transfer([_gather_task([small_buf[0]])], name="gather_last")[0][0]

    red = [_sum8(small_all[l], name=f"sum_small_grads_{l}") for l in range(DEPTH)]
    layers = lambda lo, hi: jnp.stack([red[l][lo:hi] for l in range(DEPTH)], axis=0)
    small_rows = dict(b_ada=(0, 6), g_mix=(6, 7), b_in=(7, 15), sgu_ln_g=(15, 16), sgu_ln_b=(16, 17),
                      sgu_w_s=(17, 145), sgu_b_s=(145, 146), pool_scale=(146, 147), conv_b=(147, 148),
                      conv_ln_g=(148, 149), conv_ln_b=(149, 150), g_ffn=(150, 151))
    grads = {n: layers(lo, hi).reshape(weights[n].shape) for n, (lo, hi) in small_rows.items()}
    grads["g_final"] = red[0][183]
    loss = red[DEPTH - 1][183, 0]
    conv_g = layers(151, 151 + CONV_K)
    grads["conv_w"] = lax.dynamic_slice_in_dim(conv_g, me * (D // N_DEV), D // N_DEV, axis=2)
    d_ada_all = jnp.stack([small_all[l][:, 0:6] for l in range(DEPTH)], axis=1).reshape(N_DEV, DEPTH, 6 * D)
    d_loc = jnp.transpose(lax.dynamic_slice_in_dim(d_ada_all, me * ADA_BLK, ADA_BLK, axis=2), (1, 0, 2))
    grads["w_ada"] = _ada_bwd(jnp.transpose(c_all), d_loc, name="ada_bwd")

    out = {}
    for n in REPLICATED + ("conv_w", "w_ada"):
        out[n] = _adam_nd(grads[n], weights[n], mom1[n], mom2[n], name=f"adam_{n}")
    for n, (group, k) in GRAD_GROUP.items():
        parts = [chip_parts[l][group][k] for l in range(DEPTH)]
        shape = (DEPTH,) + parts[0].shape[1:]
        res = _adam(parts, weights[n].reshape(shape), mom1[n].reshape(shape), mom2[n].reshape(shape),
                    name=f"adam_{n}")
        out[n] = [r.reshape(weights[n].shape) for r in res]

    grad_x = dx.reshape(1, t, D)
    return (loss, grad_x, *[out[n][0] for n in WEIGHT_ORDER], *[out[n][1] for n in WEIGHT_ORDER],
            *[out[n][2] for n in WEIGHT_ORDER], *[out[n][3] for n in WEIGHT_ORDER])
```

```python
import math

import jax
import jax.numpy as jnp
from jax import lax
from jax.experimental import pallas as pl
from jax.experimental.pallas import tpu as pltpu

F32 = jnp.float32
BF16 = jnp.bfloat16
MESH = pl.DeviceIdType.MESH
AXES = ("x", "y", "c")
N_DEV = 8

D = 1024
DEPTH = 2
EPS = 1e-6
CHUNK = 128
SGU_G = 8
POOL_WINDOWS = (2, 4, 8, 16)
POOL_GC = 256
CONV_K = 31
HALO = 32
SUBLANE = 8
LANE = 128
CONV_STRIP = 128
D_FF = 2816
FF_BLK = D_FF // 4
ADA_BLK = 6 * D // N_DEV

ADAM_LR = 0.001
ADAM_B1 = 0.9
ADAM_B2 = 0.999
ADAM_EPS = 1e-08
ADAM_WD = 0.01
ADAM_STEP = 10

VMEM_LIMIT_V7X = 56 * 1024 * 1024
INV_SQRT2 = 1.0 / math.sqrt(2.0)
INV_SQRT_2PI = 1.0 / math.sqrt(2.0 * math.pi)


def _params(*sem):
    return pltpu.CompilerParams(dimension_semantics=sem if sem else None, vmem_limit_bytes=VMEM_LIMIT_V7X)


def _tile(n, pref):
    if n <= pref:
        return n
    for t in range(pref - pref % 8, 0, -8):
        if n % t == 0:
            return t
    raise ValueError((n, pref))


def _sds(shape, dtype):
    return jax.ShapeDtypeStruct(shape, dtype)


def _sigmoid(x):
    return 1.0 / (1.0 + jnp.exp(-x))


def _gelu(x):
    return 0.5 * x * (1.0 + lax.erf(x * INV_SQRT2))


def _gelu_grad(x):
    return 0.5 * (1.0 + lax.erf(x * INV_SQRT2)) + x * (INV_SQRT_2PI * jnp.exp(-0.5 * x * x))


def _ln_stats(v):
    mu = jnp.mean(v, axis=-1, keepdims=True)
    vc = v - mu
    rstd = lax.rsqrt(jnp.mean(vc * vc, axis=-1, keepdims=True) + EPS)
    return vc * rstd, rstd


def _ln_bwd(dvhat, vhat, rstd):
    return rstd * (dvhat - jnp.mean(dvhat, axis=-1, keepdims=True)
                   - vhat * jnp.mean(dvhat * vhat, axis=-1, keepdims=True))


def _colsum(v):
    return jnp.sum(v, axis=0, keepdims=True)


def _dot(a, b):
    return jnp.dot(a, b, preferred_element_type=F32)


def _dot_nt(a, b):
    return lax.dot_general(a, b, (((1,), (1,)), ((), ())), preferred_element_type=F32)


def _dot_tn(a, b):
    return lax.dot_general(a, b, (((0,), (0,)), ((), ())), preferred_element_type=F32)


def _tril_mask():
    r = lax.broadcasted_iota(jnp.int32, (CHUNK, CHUNK), 0)
    c = lax.broadcasted_iota(jnp.int32, (CHUNK, CHUNK), 1)
    return (r >= c).astype(F32)


def _mesh_pos():
    return tuple(lax.axis_index(a) for a in AXES)


class _Task:
    def __init__(self, arrays, out_shapes, scratch, start, finish):
        self.arrays, self.out_shapes, self.scratch, self.start, self.finish = arrays, out_shapes, scratch, start, finish


def _hosted(tasks, body, *, name, grid, in_specs, out_specs, out_shape, scratch_shapes=()):
    single = not isinstance(out_shape, (list, tuple))
    out_shape, out_specs = ([out_shape], [out_specs]) if single else (list(out_shape), list(out_specs))
    n_in, n_out, n_scr = len(in_specs), len(out_shape), len(scratch_shapes)
    sizes = [(len(t.arrays), len(t.out_shapes), len(t.scratch)) for t in tasks]
    t_in, t_out, t_scr = (sum(s[k] for s in sizes) for k in range(3))
    any_spec = pl.BlockSpec(memory_space=pl.ANY)

    def wrapped(*refs):
        refs = list(refs)
        ins, refs = refs[:n_in + t_in], refs[n_in + t_in:]
        outs, scr = refs[:n_out + t_out], refs[n_out + t_out:]

        def per_task(fn_name):
            i0, o0, s0 = n_in, n_out, n_scr
            for t, (ni, no, ns) in zip(tasks, sizes):
                getattr(t, fn_name)(ins[i0:i0 + ni], outs[o0:o0 + no], scr[s0:s0 + ns])
                i0, o0, s0 = i0 + ni, o0 + no, s0 + ns

        if tasks and grid:
            first, last = None, None
            for d, g in enumerate(grid):
                f, e = pl.program_id(d) == 0, pl.program_id(d) == g - 1
                first, last = (f, e) if first is None else (first & f, last & e)
            pl.when(first)(lambda: per_task("start"))
        elif tasks:
            per_task("start")
        body(*ins[:n_in], *outs[:n_out], *scr[:n_scr])
        if tasks and grid:
            pl.when(last)(lambda: per_task("finish"))
        elif tasks:
            per_task("finish")

    call = pl.pallas_call(
        wrapped, name=name, grid=grid,
        in_specs=list(in_specs) + [any_spec] * t_in, out_specs=out_specs + [any_spec] * t_out,
        out_shape=out_shape + [s for t in tasks for s in t.out_shapes],
        scratch_shapes=list(scratch_shapes) + [s for t in tasks for s in t.scratch],
        compiler_params=_params(*(("arbitrary",) * len(grid))))

    def run(*operands):
        res = list(call(*operands, *[a for t in tasks for a in t.arrays]))
        host, rest, per = res[:n_out], res[n_out:], []
        for _, no, _ in sizes:
            per.append(rest[:no])
            rest = rest[no:]
        return (host[0] if single else host), per

    return run


def _transfer(tasks, name):
    return _hosted(tasks, lambda: None, name=name, grid=(), in_specs=[], out_specs=[], out_shape=[])()[1]


def _gather_task(arrs):
    n = len(arrs)

    def plan(ins, outs, sems):
        send_sems, recv_sems, local_sems = sems
        x, y, c = _mesh_pos()
        me, sibling = (x, y, c), (x, y, 1 - c)
        chips = [(1 - x, y), (x, 1 - y), (1 - x, 1 - y)]

        def slot(a, p):
            return outs[a].at[4 * p[0] + 2 * p[1] + p[2]]

        def copy(a, k, block, to, src=None):
            dst = slot(a, block)
            return pltpu.make_async_remote_copy(
                src_ref=dst if src is None else src, dst_ref=dst, send_sem=send_sems.at[a, k],
                recv_sem=recv_sems.at[a, k], device_id=to, device_id_type=MESH)

        mine = [pltpu.make_async_copy(ins[a], slot(a, me), local_sems.at[a]) for a in range(n)]
        first = []
        for a in range(n):
            first.append(copy(a, 0, me, sibling, src=ins[a]))
            first += [copy(a, 1 + j, me, (*chip, c), src=ins[a]) for j, chip in enumerate(chips)]
        return c, me, sibling, chips, copy, mine, first

    def start(ins, outs, sems):
        *_, mine, first = plan(ins, outs, sems)
        for cp in mine + first:
            cp.start()

    def finish(ins, outs, sems):
        c, me, sibling, chips, copy, mine, first = plan(ins, outs, sems)
        passed = []
        for j, chip in enumerate(chips):
            for a in range(n):
                copy(a, 1 + j, (*chip, c), me).wait_recv()
                fwd = copy(a, 4 + j, (*chip, c), sibling)
                fwd.start()
                passed.append(fwd)
        for a in range(n):
            copy(a, 0, sibling, me).wait_recv()
            for j, chip in enumerate(chips):
                copy(a, 4 + j, (*chip, 1 - c), me).wait_recv()
        for cp in first + passed:
            cp.wait_send()
        for m in mine:
            m.wait()

    return _Task(list(arrs), [_sds((N_DEV,) + a.shape, a.dtype) for a in arrs],
                 [pltpu.SemaphoreType.DMA((n, 7)), pltpu.SemaphoreType.DMA((n, 7)), pltpu.SemaphoreType.DMA((n,))],
                 start, finish)


def _sibling_task(arrs):
    n = len(arrs)

    def copies(ins, outs, sems):
        send_sems, recv_sems = sems
        x, y, c = _mesh_pos()
        return [pltpu.make_async_remote_copy(
            src_ref=ins[a].at[2 * q + (1 - c)], dst_ref=outs[a].at[q], send_sem=send_sems.at[a, q],
            recv_sem=recv_sems.at[a, q], device_id=(x, y, 1 - c), device_id_type=MESH)
            for a in range(n) for q in range(4)]

    def start(ins, outs, sems):
        for cp in copies(ins, outs, sems):
            cp.start()

    def finish(ins, outs, sems):
        for cp in copies(ins, outs, sems):
            cp.wait()

    return _Task(list(arrs), [_sds((4,) + a.shape[1:], a.dtype) for a in arrs],
                 [pltpu.SemaphoreType.DMA((n, 4)), pltpu.SemaphoreType.DMA((n, 4))], start, finish)


def _chips_task(arrs):
    n = len(arrs)

    def copies(ins, outs, sems):
        send_sems, recv_sems, local_sems = sems
        x, y, c = _mesh_pos()
        q_me = 2 * x + y
        chips = [(1 - x, y), (x, 1 - y), (1 - x, 1 - y)]
        own = [pltpu.make_async_copy(ins[a].at[q_me], outs[a].at[q_me], local_sems.at[a]) for a in range(n)]
        remote = [pltpu.make_async_remote_copy(
            src_ref=ins[a].at[2 * chip[0] + chip[1]], dst_ref=outs[a].at[q_me], send_sem=send_sems.at[a, j],
            recv_sem=recv_sems.at[a, j], device_id=(*chip, c), device_id_type=MESH)
            for a in range(n) for j, chip in enumerate(chips)]
        return own + remote

    def start(ins, outs, sems):
        for cp in copies(ins, outs, sems):
            cp.start()

    def finish(ins, outs, sems):
        for cp in copies(ins, outs, sems):
            cp.wait()

    return _Task(list(arrs), [_sds(a.shape, a.dtype) for a in arrs],
                 [pltpu.SemaphoreType.DMA((n, 3)), pltpu.SemaphoreType.DMA((n, 3)), pltpu.SemaphoreType.DMA((n,))],
                 start, finish)


def _sibling_sum(arr, land, core, name):
    _, rows, cols = arr.shape
    tr = _tile(rows, 512)
    arr4 = arr.reshape(4, 2, rows, cols)

    def body(c_ref, a_ref, l_ref, o_ref):
        o_ref[...] = (a_ref[...] + l_ref[...]).astype(BF16)

    grid_spec = pltpu.PrefetchScalarGridSpec(
        num_scalar_prefetch=1, grid=(4, rows // tr),
        in_specs=[pl.BlockSpec((None, None, tr, cols), lambda q, r, c_ref: (q, c_ref[0], r, 0)),
                  pl.BlockSpec((None, tr, cols), lambda q, r, c_ref: (q, r, 0))],
        out_specs=pl.BlockSpec((None, tr, cols), lambda q, r, c_ref: (q, r, 0)))
    return pl.pallas_call(body, name=name, grid_spec=grid_spec, out_shape=_sds((4, rows, cols), BF16),
                          compiler_params=_params("arbitrary", "arbitrary"))(core, arr4, land)


def _sibling_sums(arrs, land, core, tag):
    return [_sibling_sum(a, l, core, name=f"rs_sum_{tag}_{k}") for k, (a, l) in enumerate(zip(arrs, land))]


def _adam(gparts, w, m, v, name):
    n_l = len(gparts)
    p, rows, cols = gparts[0].shape
    tr = _tile(rows, 256)
    n_r = rows // tr
    c1 = 1.0 - ADAM_B1 ** ADAM_STEP
    c2 = 1.0 - ADAM_B2 ** ADAM_STEP

    def body(*refs):
        g_refs = refs[:n_l]
        w_ref, m_ref, v_ref, go_ref, d_ref, mo_ref, vo_ref = refs[n_l:]
        layer = pl.program_id(0)
        g = jnp.zeros((tr, cols), F32)
        for li, g_ref in enumerate(g_refs):
            gl = g_ref[0].astype(F32)
            for k in range(1, p):
                gl = gl + g_ref[k].astype(F32)
            g = gl if n_l == 1 else jnp.where(layer == li, gl, g)
        m_new = ADAM_B1 * m_ref[...] + (1.0 - ADAM_B1) * g
        v_new = ADAM_B2 * v_ref[...] + (1.0 - ADAM_B2) * (g * g)
        m_hat = m_new / c1
        v_hat = v_new / c2
        go_ref[...] = g
        d_ref[...] = -ADAM_LR * (m_hat / (jnp.sqrt(v_hat) + ADAM_EPS) + ADAM_WD * w_ref[...])
        mo_ref[...] = m_new
        vo_ref[...] = v_new

    def g_spec(li):
        def index(l, r):
            return (0, jnp.where(l == li, r, jnp.where(l < li, 0, n_r - 1)), 0)
        return pl.BlockSpec((p, tr, cols), index)

    blk = pl.BlockSpec((None, tr, cols), lambda l, r: (l, r, 0))
    return pl.pallas_call(
        body, name=name, grid=(n_l, n_r),
        in_specs=[g_spec(li) for li in range(n_l)] + [blk, blk, blk],
        out_specs=[blk] * 4, out_shape=[_sds((n_l, rows, cols), F32)] * 4,
        compiler_params=_params("arbitrary", "arbitrary"))(*gparts, w, m, v)


def _adam_nd(grad, w, m, v, name):
    shape = w.shape
    cols = shape[-1]
    rows = w.size // cols
    as_rows = lambda a: a.reshape(1, rows, cols)
    out = _adam([as_rows(grad)], as_rows(w), as_rows(m), as_rows(v), name)
    return [o.reshape(shape) for o in out]


def _norm_mod(x, g, sc, sh, name):
    t = x.shape[0]
    tm = _tile(t, 512)

    def body(x_ref, g_ref, sc_ref, sh_ref, h_ref):
        h_ref[...] = _modulated_norm(x_ref[...], g_ref[...], sc_ref[...], sh_ref[...])

    row = pl.BlockSpec((1, D), lambda i: (0, 0))
    blk = pl.BlockSpec((tm, D), lambda i: (i, 0))
    return pl.pallas_call(body, name=name, grid=(t // tm,), in_specs=[blk, row, row, row], out_specs=blk,
                          out_shape=_sds((t, D), BF16), compiler_params=_params("arbitrary"))(x, g, sc, sh)


def _with_tasks(res_per, tasks):
    return res_per if tasks else res_per[0]


def _mm_cols(a, b8, bias8, name, tasks=()):
    t, k = a.shape
    j, _, n = b8.shape
    tm = _tile(t, 1024)

    def body(a_ref, b_ref, bias_ref, o_ref):
        o_ref[...] = (_dot(a_ref[...], b_ref[...]) + bias_ref[...]).astype(BF16)

    return _with_tasks(_hosted(
        tasks, body, name=name, grid=(j, t // tm),
        in_specs=[pl.BlockSpec((tm, k), lambda jj, i: (i, 0)),
                  pl.BlockSpec((None, k, n), lambda jj, i: (jj, 0, 0)),
                  pl.BlockSpec((None, 1, n), lambda jj, i: (jj, 0, 0))],
        out_specs=pl.BlockSpec((None, tm, n), lambda jj, i: (jj, i, 0)),
        out_shape=_sds((j, t, n), BF16))(a, b8, bias8), tasks)


def _halo_before(tm, col):
    return pl.BlockSpec((None, HALO, D), lambda i: (col, jnp.maximum(i * (tm // HALO) - 1, 0), 0))


def _pool_forward(p_ext, t0, rows):
    t = t0 + lax.broadcasted_iota(jnp.int32, (rows, 1), 0)
    out = []
    for gi, win in enumerate(POOL_WINDOWS):
        e = p_ext[:, gi * POOL_GC:(gi + 1) * POOL_GC]
        s, sh = e, 1
        while sh < win:
            s = s + pltpu.roll(s, sh, 0)
            sh *= 2
        cnt = jnp.minimum(t + 1, win).astype(F32)
        out.append(s[HALO:] / cnt - e[HALO:])
    return out


def _fill_shift_bank(bank_ref, ext, causal):
    n = ext.shape[0]
    bank_ref[0] = ext
    for b in range(1, SUBLANE):
        bank_ref[b] = pltpu.roll(ext, b if causal else n - b, 0)


def _branches_fwd(z8, ln_g, ln_b, w_s, bs_b, pool_w, pool_scale, conv_w, conv_b, cln_g, cln_b, name, tasks=()):
    t = z8.shape[1]
    tm = _tile(t, 256)
    n_ext = tm + HALO

    def body(zu_ref, zv_ref, p_ref, a_ref, ag_ref, ph_ref, ah_ref, agh_ref, lng_ref, lnb_ref, ws_ref, bsb_ref,
             wp_ref, ps_ref, cw_ref, cb_ref, clg_ref, clb_ref, sa_ref, sb_ref, sc_ref, cv_ref, bank_ref):
        i = pl.program_id(0)
        has_past = (i > 0).astype(F32)
        u = _gelu(zu_ref[...].astype(F32))
        vhat, _ = _ln_stats(_gelu(zv_ref[...].astype(F32)))
        vb = (vhat * lng_ref[...] + lnb_ref[...]).astype(BF16)
        mask = _tril_mask()
        for g in range(SGU_G):
            cols = slice(g * CHUNK, (g + 1) * CHUNK)
            wm = (ws_ref[g] * mask).astype(BF16)
            for n in range(tm // CHUNK):
                rows = slice(n * CHUNK, (n + 1) * CHUNK)
                mixed = _dot(wm, vb[rows, cols]) + bsb_ref[g]
                sa_ref[rows, cols] = (u[rows, cols] * mixed).astype(BF16)
        p_ext = jnp.concatenate([ph_ref[...].astype(F32) * has_past, p_ref[...].astype(F32)], axis=0)
        pooled = _pool_forward(p_ext, i * tm, tm)
        for gi in range(len(POOL_WINDOWS)):
            cols = slice(gi * POOL_GC, (gi + 1) * POOL_GC)
            y = _dot(pooled[gi].astype(BF16), wp_ref[gi].astype(BF16))
            sb_ref[:, cols] = (y * ps_ref[:, cols]).astype(BF16)
        for cb in range(D // LANE):
            cols = slice(cb * LANE, (cb + 1) * LANE)
            zc = jnp.concatenate(
                [ah_ref[:, cols].astype(F32) * has_past * _sigmoid(agh_ref[:, cols].astype(F32)),
                 a_ref[:, cols].astype(F32) * _sigmoid(ag_ref[:, cols].astype(F32))], axis=0)
            _fill_shift_bank(bank_ref, zc, causal=True)
            for r0 in range(0, tm, CONV_STRIP):
                acc = jnp.zeros((CONV_STRIP, LANE), F32) + cb_ref[:, cols]
                for k in range(CONV_K):
                    hi, lo = divmod(CONV_K - 1 - k, SUBLANE)
                    acc = acc + cw_ref[k:k + 1, cols] * bank_ref[lo, pl.ds(HALO - SUBLANE * hi + r0, CONV_STRIP), :]
                cv_ref[r0:r0 + CONV_STRIP, cols] = acc
        cv = cv_ref[...]
        chat, _ = _ln_stats(cv)
        cl = chat * clg_ref[...] + clb_ref[...]
        sc_ref[...] = (cl * _sigmoid(cl)).astype(BF16)

    def col(j):
        return pl.BlockSpec((None, tm, D), lambda i: (j, i, 0))

    row = pl.BlockSpec((1, D), lambda i: (0, 0))
    full3 = lambda s: pl.BlockSpec(s, lambda i: (0, 0, 0))
    blk = pl.BlockSpec((tm, D), lambda i: (i, 0))
    return _with_tasks(_hosted(
        tasks, body, name=name, grid=(t // tm,),
        in_specs=[col(0), col(1), col(2), col(3), col(4), _halo_before(tm, 2), _halo_before(tm, 3),
                  _halo_before(tm, 4), row, row, full3((SGU_G, CHUNK, CHUNK)), full3((SGU_G, CHUNK, CHUNK)),
                  full3((4, POOL_GC, POOL_GC)), row, pl.BlockSpec((HALO, D), lambda i: (0, 0)), row, row, row],
        out_specs=[blk, blk, blk, blk],
        out_shape=[_sds((t, D), BF16)] * 3 + [_sds((t, D), F32)],
        scratch_shapes=[pltpu.VMEM((SUBLANE, n_ext, LANE), F32)],
    )(z8, z8, z8, z8, z8, z8, z8, z8, ln_g, ln_b, w_s, bs_b, pool_w, pool_scale, conv_w, conv_b, cln_g, cln_b), tasks)


def _proj_merge(sa, sb, sc, w_pa, w_pb, w_pc, z8, name, tasks=()):
    t = sa.shape[0]
    tm = _tile(t, 512)

    def body(sa_ref, sb_ref, sc_ref, wa_ref, wb_ref, wc_ref, g0_ref, g1_ref, g2_ref, ya_ref, yb_ref, yc_ref, m_ref):
        merged = jnp.zeros((tm, D), F32)
        for s_ref, w_ref, g_ref, y_ref in ((sa_ref, wa_ref, g0_ref, ya_ref), (sb_ref, wb_ref, g1_ref, yb_ref),
                                           (sc_ref, wc_ref, g2_ref, yc_ref)):
            y = _dot(s_ref[...], w_ref[...])
            y_ref[...] = y.astype(BF16)
            merged = merged + _sigmoid(g_ref[...].astype(F32)) * y
        m_ref[...] = merged.astype(BF16)

    blk = pl.BlockSpec((tm, D), lambda i: (i, 0))
    wspec = pl.BlockSpec((D, D), lambda i: (0, 0))
    gate = lambda j: pl.BlockSpec((None, tm, D), lambda i: (j, i, 0))
    return _with_tasks(_hosted(
        tasks, body, name=name, grid=(t // tm,),
        in_specs=[blk, blk, blk, wspec, wspec, wspec, gate(5), gate(6), gate(7)],
        out_specs=[blk] * 4, out_shape=[_sds((t, D), BF16)] * 4)(sa, sb, sc, w_pa, w_pb, w_pc, z8, z8, z8), tasks)


def _modulated_norm(xv, g, sc, sh):
    r = lax.rsqrt(jnp.mean(xv * xv, axis=-1, keepdims=True) + EPS)
    return (xv * r * g * (1.0 + sc) + sh).astype(BF16)


def _out_proj(merged, w_out, x, gt, g, sc, sh, name):
    t = x.shape[0]
    tm = _tile(t, 512)

    def body(m_ref, w_ref, x_ref, gt_ref, g_ref, sc_ref, sh_ref, om_ref, x1_ref, h2_ref):
        om = _dot(m_ref[...], w_ref[...])
        om_ref[...] = om
        x1 = x_ref[...] + gt_ref[...] * om
        x1_ref[...] = x1
        h2_ref[...] = _modulated_norm(x1, g_ref[...], sc_ref[...], sh_ref[...])

    blk = pl.BlockSpec((tm, D), lambda i: (i, 0))
    row = pl.BlockSpec((1, D), lambda i: (0, 0))
    return pl.pallas_call(
        body, name=name, grid=(t // tm,),
        in_specs=[blk, pl.BlockSpec((D, D), lambda i: (0, 0)), blk, row, row, row, row],
        out_specs=[blk, blk, blk], out_shape=[_sds((t, D), F32)] * 2 + [_sds((t, D), BF16)],
        compiler_params=_params("arbitrary"))(merged, w_out, x, gt, g, sc, sh)


def _ffn_in(h2, wfi, name, tasks=()):
    t = h2.shape[0]
    tm = _tile(t, 512)

    def body(h_ref, w_ref, gu_ref, f_ref):
        hv = h_ref[...]
        gp = _dot_nt(hv, w_ref[0])
        up = _dot_nt(hv, w_ref[1])
        gu_ref[0] = gp.astype(BF16)
        gu_ref[1] = up.astype(BF16)
        f_ref[...] = (gp * _sigmoid(gp) * up).astype(BF16)

    return _with_tasks(_hosted(
        tasks, body, name=name, grid=(4, t // tm),
        in_specs=[pl.BlockSpec((tm, D), lambda j, i: (i, 0)),
                  pl.BlockSpec((2, None, FF_BLK, D), lambda j, i: (0, j, 0, 0))],
        out_specs=[pl.BlockSpec((2, None, tm, FF_BLK), lambda j, i: (0, j, i, 0)),
                   pl.BlockSpec((None, tm, FF_BLK), lambda j, i: (j, i, 0))],
        out_shape=[_sds((2, 4, t, FF_BLK), BF16), _sds((4, t, FF_BLK), BF16)])(h2, wfi), tasks)


def _ffn_out(f4, wfo4, x1, gt, g, sc, sh, name, tasks=()):
    t = x1.shape[0]
    tm = _tile(t, 512)
    with_norm = g is not None

    def body(f_ref, w_ref, x_ref, gt_ref, *rest):
        o_ref, x2_ref = rest[-3:-1] if with_norm else rest[-2:]
        j = pl.program_id(1)

        @pl.when(j == 0)
        def _():
            o_ref[...] = jnp.zeros_like(o_ref)

        o_ref[...] += _dot(f_ref[...], w_ref[...])

        @pl.when(j == 3)
        def _():
            x2 = x_ref[...] + gt_ref[...] * o_ref[...]
            x2_ref[...] = x2
            if with_norm:
                g_ref, sc_ref, sh_ref = rest[:3]
                rest[-1][...] = _modulated_norm(x2, g_ref[...], sc_ref[...], sh_ref[...])

    blk = pl.BlockSpec((tm, D), lambda i, j: (i, 0))
    row = pl.BlockSpec((1, D), lambda i, j: (0, 0))
    norm_args = [g, sc, sh] if with_norm else []
    return _with_tasks(_hosted(
        tasks, body, name=name, grid=(t // tm, 4),
        in_specs=[pl.BlockSpec((None, tm, FF_BLK), lambda i, j: (j, i, 0)),
                  pl.BlockSpec((None, FF_BLK, D), lambda i, j: (j, 0, 0)), blk, row] + [row] * len(norm_args),
        out_specs=[blk, blk] + [blk] * with_norm,
        out_shape=[_sds((t, D), F32)] * 2 + [_sds((t, D), BF16)] * with_norm)(f4, wfo4, x1, gt, *norm_args), tasks)


def _gate_grads(dx, o_ref, gt_ref, do_ref, dgt_ref):
    do_ref[...] = (dx * gt_ref[...]).astype(BF16)
    dgt_ref[...] += _colsum(dx * o_ref[...])


def _final_loss(x, g, target, o, gt, name):
    t = x.shape[0]
    tm = _tile(t, 512)

    def body(x_ref, g_ref, t_ref, o_ref, gt_ref, loss_ref, dx_ref, dg_ref, do_ref, dgt_ref):
        @pl.when(pl.program_id(0) == 0)
        def _():
            for ref in (loss_ref, dg_ref, dgt_ref):
                ref[...] = jnp.zeros_like(ref)

        xv = x_ref[...]
        r = lax.rsqrt(jnp.mean(xv * xv, axis=-1, keepdims=True) + EPS)
        xn = xv * r
        diff = xn * g_ref[...] - t_ref[...]
        loss_ref[...] += 0.5 * jnp.sum(jnp.mean(diff * diff, axis=-1, keepdims=True))
        dy = diff * (1.0 / D)
        dg_ref[...] += _colsum(dy * xn)
        dxn = dy * g_ref[...]
        dx = r * (dxn - xn * jnp.mean(dxn * xn, axis=-1, keepdims=True))
        dx_ref[...] = dx
        _gate_grads(dx, o_ref, gt_ref, do_ref, dgt_ref)

    blk = pl.BlockSpec((tm, D), lambda i: (i, 0))
    row = pl.BlockSpec((1, D), lambda i: (0, 0))
    return pl.pallas_call(
        body, name=name, grid=(t // tm,), in_specs=[blk, row, blk, blk, row],
        out_specs=[pl.BlockSpec((8, 128), lambda i: (0, 0)), blk, row, blk, row],
        out_shape=[_sds((8, 128), F32), _sds((t, D), F32), _sds((1, D), F32), _sds((t, D), BF16), _sds((1, D), F32)],
        compiler_params=_params("arbitrary"))(x, g, target, o, gt)


def _norm_mod_bwd(dh, x, dres, g, sc, o, gt, name, tasks=()):
    t = x.shape[0]
    tm = _tile(t, 512)
    with_gate = o is not None

    def body(dh_ref, x_ref, dr_ref, g_ref, sc_ref, *rest):
        dx_ref, st_ref = rest[2:4] if with_gate else rest

        @pl.when(pl.program_id(0) == 0)
        def _():
            st_ref[...] = jnp.zeros_like(st_ref)
            if with_gate:
                rest[5][...] = jnp.zeros_like(rest[5])

        xv, dhv = x_ref[...], dh_ref[...]
        r = lax.rsqrt(jnp.mean(xv * xv, axis=-1, keepdims=True) + EPS)
        xn = xv * r
        gv, mod = g_ref[...], 1.0 + sc_ref[...]
        st_ref[0:1, :] += _colsum(dhv)
        st_ref[1:2, :] += _colsum(dhv * xn * gv)
        st_ref[2:3, :] += _colsum(dhv * xn * mod)
        dxn = dhv * gv * mod
        dx = dr_ref[...] + r * (dxn - xn * jnp.mean(dxn * xn, axis=-1, keepdims=True))
        dx_ref[...] = dx
        if with_gate:
            _gate_grads(dx, rest[0], rest[1], rest[4], rest[5])

    blk = pl.BlockSpec((tm, D), lambda i: (i, 0))
    row = pl.BlockSpec((1, D), lambda i: (0, 0))
    gate_args = [o, gt] if with_gate else []
    return _with_tasks(_hosted(
        tasks, body, name=name, grid=(t // tm,), in_specs=[blk, blk, blk, row, row] + [blk, row] * with_gate,
        out_specs=[blk, pl.BlockSpec((3, D), lambda i: (0, 0))] + [blk, row] * with_gate,
        out_shape=[_sds((t, D), F32), _sds((3, D), F32)] + [_sds((t, D), BF16), _sds((1, D), F32)] * with_gate,
    )(dh, x, dres, g, sc, *gate_args), tasks)


def _ffn_bwd_act(do, wfo4, gu, name, tasks=()):
    t = do.shape[0]
    tm = _tile(t, 512)

    def body(do_ref, w_ref, gu_ref, dgu_ref):
        df = _dot_nt(do_ref[...], w_ref[...])
        gp, up = gu_ref[0].astype(F32), gu_ref[1].astype(F32)
        sg = _sigmoid(gp)
        dgu_ref[0] = (df * up * (sg * (1.0 + gp * (1.0 - sg)))).astype(BF16)
        dgu_ref[1] = (df * (gp * sg)).astype(BF16)

    gu_spec = pl.BlockSpec((2, None, tm, FF_BLK), lambda j, i: (0, j, i, 0))
    return _with_tasks(_hosted(
        tasks, body, name=name, grid=(4, t // tm),
        in_specs=[pl.BlockSpec((tm, D), lambda j, i: (i, 0)),
                  pl.BlockSpec((None, FF_BLK, D), lambda j, i: (j, 0, 0)), gu_spec],
        out_specs=gu_spec, out_shape=_sds((2, 4, t, FF_BLK), BF16))(do, wfo4, gu), tasks)


def _mm_nt_sum(a8, b8, name, b_is_kn=False, tasks=()):
    j, t, k = a8.shape
    n = b8.shape[2] if b_is_kn else b8.shape[1]
    tm = _tile(t, 1024 if j > 1 else 512)
    out_dtype = F32 if j > 1 else BF16
    dot = _dot if b_is_kn else _dot_nt

    def body(a_ref, b_ref, o_ref):
        if j == 1:
            o_ref[...] = dot(a_ref[...], b_ref[...]).astype(out_dtype)
            return

        @pl.when(pl.program_id(1) == 0)
        def _():
            o_ref[...] = jnp.zeros_like(o_ref)

        o_ref[...] += dot(a_ref[...], b_ref[...])

    return _with_tasks(_hosted(
        tasks, body, name=name, grid=(t // tm, j),
        in_specs=[pl.BlockSpec((None, tm, k), lambda i, jj: (jj, i, 0)),
                  pl.BlockSpec((None,) + b8.shape[1:], lambda i, jj: (jj, 0, 0))],
        out_specs=pl.BlockSpec((tm, n), lambda i, jj: (i, 0)), out_shape=_sds((t, n), out_dtype))(a8, b8), tasks)


def _mm_tn(a8, b8, name):
    ja, t, m = a8.shape
    jb, _, n = b8.shape
    j = max(ja, jb)
    tk = _tile(t, 1024)

    def body(a_ref, b_ref, o_ref):
        @pl.when(pl.program_id(1) == 0)
        def _():
            o_ref[...] = jnp.zeros_like(o_ref)

        o_ref[...] += _dot_tn(a_ref[...], b_ref[...])

    return pl.pallas_call(
        body, name=name, grid=(j, t // tk),
        in_specs=[pl.BlockSpec((None, tk, m), (lambda jj, kk: (jj, kk, 0)) if ja > 1 else (lambda jj, kk: (0, kk, 0))),
                  pl.BlockSpec((None, tk, n), (lambda jj, kk: (jj, kk, 0)) if jb > 1 else (lambda jj, kk: (0, kk, 0)))],
        out_specs=pl.BlockSpec((None, m, n), lambda jj, kk: (jj, 0, 0)), out_shape=_sds((j, m, n), F32),
        compiler_params=_params("arbitrary", "arbitrary"))(a8, b8)


def _merge_bwd(dom, w_out, z8, ya, yb, yc, name, tasks=()):
    t = dom.shape[0]
    tm = _tile(t, 512)

    def body(dom_ref, w_ref, g0_ref, g1_ref, g2_ref, ya_ref, yb_ref, yc_ref, dya_ref, dyb_ref, dyc_ref, dzg_ref,
             db_ref):
        @pl.when(pl.program_id(0) == 0)
        def _():
            db_ref[...] = jnp.zeros_like(db_ref)

        dm = _dot_nt(dom_ref[...], w_ref[...])
        for k, (g_ref, y_ref, dy_ref) in enumerate(((g0_ref, ya_ref, dya_ref), (g1_ref, yb_ref, dyb_ref),
                                                    (g2_ref, yc_ref, dyc_ref))):
            sg = _sigmoid(g_ref[...].astype(F32))
            dy_ref[...] = (dm * sg).astype(BF16)
            dzg = dm * y_ref[...].astype(F32) * (sg * (1.0 - sg))
            dzg_ref[k] = dzg.astype(BF16)
            db_ref[k:k + 1, :] += _colsum(dzg)

    blk = pl.BlockSpec((tm, D), lambda i: (i, 0))
    gate = lambda j: pl.BlockSpec((None, tm, D), lambda i: (j, i, 0))
    return _with_tasks(_hosted(
        tasks, body, name=name, grid=(t // tm,),
        in_specs=[blk, pl.BlockSpec((D, D), lambda i: (0, 0)), gate(5), gate(6), gate(7), blk, blk, blk],
        out_specs=[blk, blk, blk, pl.BlockSpec((3, tm, D), lambda i: (0, i, 0)), pl.BlockSpec((3, D), lambda i: (0, 0))],
        out_shape=[_sds((t, D), BF16)] * 3 + [_sds((3, t, D), BF16), _sds((3, D), F32)],
    )(dom, w_out, z8, z8, z8, ya, yb, yc), tasks)


def _branches_bwd(z8, cv, dsa, dsb, dsc, dzg, ln_g, ln_b, w_s, bs_b, pool_w, pool_scale, conv_w, cln_g, cln_b, name,
                  tasks=()):
    t = z8.shape[1]
    tm = _tile(t, 128)
    n_ext = tm + HALO
    n_tiles = t // tm

    def body(zu_ref, zv_ref, p_ref, a_ref, ag_ref, ph_ref, ah_ref, agh_ref, cv_ref, cvf_ref, dsa_ref, dsb_ref,
             dsbf_ref, dsc_ref, dscf_ref, dzg_ref, lng_ref, lnb_ref, ws_ref, bsb_ref, wp_ref, ps_ref, cw_ref,
             clg_ref, clb_ref, dz_ref, dbin_ref, rows_ref, dws_ref, dbs_ref, dwp_ref, dcw_ref, mixed_scr, dvln_scr,
             dcv_scr, zbank_ref, dbank_ref, dcw8_scr):
        i = pl.program_id(0)

        @pl.when(i == 0)
        def _():
            for ref in (dbin_ref, rows_ref, dws_ref, dbs_ref, dwp_ref, dcw8_scr):
                ref[...] = jnp.zeros_like(ref)

        has_past = (i > 0).astype(F32)
        has_next = (i < n_tiles - 1).astype(F32)

        def emit(j, val):
            dz_ref[j] = val.astype(BF16)
            dbin_ref[j:j + 1, :] += _colsum(val)

        zu, zv = zu_ref[...].astype(F32), zv_ref[...].astype(F32)
        u = _gelu(zu)
        vhat, v_rstd = _ln_stats(_gelu(zv))
        vb = (vhat * lng_ref[...] + lnb_ref[...]).astype(BF16)
        dsa = dsa_ref[...].astype(F32)
        dmixed = dsa * u
        dmb = dmixed.astype(BF16)
        mask = _tril_mask()
        lane = lax.broadcasted_iota(jnp.int32, (CHUNK, CHUNK), 1)
        for g in range(SGU_G):
            cols = slice(g * CHUNK, (g + 1) * CHUNK)
            wm = (ws_ref[g] * mask).astype(BF16)
            dws = jnp.zeros((CHUNK, CHUNK), F32)
            dbs = jnp.zeros((CHUNK, 1), F32)
            for n in range(tm // CHUNK):
                rows = slice(n * CHUNK, (n + 1) * CHUNK)
                mixed_scr[rows, cols] = _dot(wm, vb[rows, cols]) + bsb_ref[g]
                dvln_scr[rows, cols] = _dot_tn(wm, dmb[rows, cols])
                dws = dws + _dot_nt(dmb[rows, cols], vb[rows, cols])
                dbs = dbs + jnp.sum(dmixed[rows, cols], axis=1, keepdims=True)
            dws_ref[g] += dws
            dbs_ref[...] += jnp.where(lane == g, dbs, 0.0)
        emit(0, dsa * mixed_scr[...] * _gelu_grad(zu))
        dvln = dvln_scr[...]
        rows_ref[0:1, :] += _colsum(dvln * vhat)
        rows_ref[1:2, :] += _colsum(dvln)
        emit(1, _ln_bwd(dvln * lng_ref[...], vhat, v_rstd) * _gelu_grad(zv))

        p_ext = jnp.concatenate([ph_ref[...].astype(F32) * has_past, p_ref[...].astype(F32)], axis=0)
        pooled = _pool_forward(p_ext, i * tm, tm)
        dsb = dsb_ref[...].astype(F32)
        dpl_ext = jnp.concatenate([dsb, dsbf_ref[...].astype(F32) * has_next], axis=0) * ps_ref[...]
        t_ext = i * tm + lax.broadcasted_iota(jnp.int32, (n_ext, 1), 0)
        dp_parts = []
        for gi, win in enumerate(POOL_WINDOWS):
            cols = slice(gi * POOL_GC, (gi + 1) * POOL_GC)
            pooled_b = pooled[gi].astype(BF16)
            wpb = wp_ref[gi].astype(BF16)
            rows_ref[2:3, cols] += _colsum(dsb[:, cols] * _dot(pooled_b, wpb))
            dplb = dpl_ext[:, cols].astype(BF16)
            dwp_ref[gi] += _dot_tn(pooled_b, dplb[:tm])
            dpooled = _dot_nt(dplb, wpb)
            s, sh = dpooled / jnp.minimum(t_ext + 1, win).astype(F32), 1
            while sh < win:
                s = s + pltpu.roll(s, n_ext - sh, 0)
                sh *= 2
            dp_parts.append(s[:tm] - dpooled[:tm])
        emit(2, jnp.concatenate(dp_parts, axis=1))

        cv_ext = jnp.concatenate([cv_ref[...], cvf_ref[...]], axis=0)
        chat, c_rstd = _ln_stats(cv_ext)
        cl = chat * clg_ref[...] + clb_ref[...]
        sg = _sigmoid(cl)
        dsc_ext = jnp.concatenate([dsc_ref[...].astype(F32), dscf_ref[...].astype(F32)], axis=0)
        dcl = dsc_ext * (sg * (1.0 + cl * (1.0 - sg)))
        rows_ref[4:5, :] += _colsum((dcl * chat)[:tm])
        rows_ref[5:6, :] += _colsum(dcl[:tm])
        in_seq = jnp.concatenate([jnp.ones((tm, 1), F32), jnp.zeros((HALO, 1), F32) + has_next], axis=0)
        dcv = jnp.where(in_seq > 0.0, _ln_bwd(dcl * clg_ref[...], chat, c_rstd), 0.0)
        rows_ref[3:4, :] += _colsum(dcv[:tm])
        dcv_scr[...] = dcv
        for cb in range(D // LANE):
            cols = slice(cb * LANE, (cb + 1) * LANE)
            zc = jnp.concatenate(
                [ah_ref[:, cols].astype(F32) * has_past * _sigmoid(agh_ref[:, cols].astype(F32)),
                 a_ref[:, cols].astype(F32) * _sigmoid(ag_ref[:, cols].astype(F32))], axis=0)
            _fill_shift_bank(zbank_ref, zc, causal=True)
            _fill_shift_bank(dbank_ref, dcv_scr[:, cols], causal=False)
            for r0 in range(0, tm, CONV_STRIP):
                rows = slice(r0, r0 + CONV_STRIP)
                dcv_s = dcv_scr[rows, cols]
                dzc = jnp.zeros((CONV_STRIP, LANE), F32)
                for k in range(CONV_K):
                    hi, lo = divmod(CONV_K - 1 - k, SUBLANE)
                    z_win = zbank_ref[lo, pl.ds(HALO - SUBLANE * hi + r0, CONV_STRIP), :]
                    dcw8_scr[k, :, cols] += jnp.sum((dcv_s * z_win).reshape(CONV_STRIP // SUBLANE, SUBLANE, LANE), axis=0)
                    dzc = dzc + cw_ref[k:k + 1, cols] * dbank_ref[lo, pl.ds(SUBLANE * hi + r0, CONV_STRIP), :]
                a_s = a_ref[rows, cols].astype(F32)
                sga = _sigmoid(ag_ref[rows, cols].astype(F32))
                dza = dzc * sga
                dzag = dzc * a_s * (sga * (1.0 - sga))
                dz_ref[3, rows, cols] = dza.astype(BF16)
                dz_ref[4, rows, cols] = dzag.astype(BF16)
                dbin_ref[3:4, cols] += _colsum(dza)
                dbin_ref[4:5, cols] += _colsum(dzag)
        for k in range(3):
            dz_ref[5 + k] = dzg_ref[k]

        @pl.when(i == n_tiles - 1)
        def _():
            dcw_ref[...] = jnp.sum(dcw8_scr[...], axis=1)

    def col(j):
        return pl.BlockSpec((None, tm, D), lambda i: (j, i, 0))

    blk = pl.BlockSpec((tm, D), lambda i: (i, 0))
    after = pl.BlockSpec((HALO, D), lambda i: (jnp.minimum((i + 1) * (tm // HALO), t // HALO - 1), 0))
    row = pl.BlockSpec((1, D), lambda i: (0, 0))
    full2 = lambda s: pl.BlockSpec(s, lambda i: (0, 0))
    full3 = lambda s: pl.BlockSpec(s, lambda i: (0, 0, 0))
    return _with_tasks(_hosted(
        tasks, body, name=name, grid=(n_tiles,),
        in_specs=[col(0), col(1), col(2), col(3), col(4), _halo_before(tm, 2), _halo_before(tm, 3),
                  _halo_before(tm, 4), blk, after, blk, blk, after, blk, after,
                  pl.BlockSpec((3, tm, D), lambda i: (0, i, 0)), row, row, full3((SGU_G, CHUNK, CHUNK)),
                  full3((SGU_G, CHUNK, CHUNK)), full3((4, POOL_GC, POOL_GC)), row, full2((HALO, D)), row, row],
        out_specs=[pl.BlockSpec((8, tm, D), lambda i: (0, i, 0)), full2((8, D)), full2((8, D)),
                   full3((SGU_G, CHUNK, CHUNK)), full2((CHUNK, CHUNK)), full3((4, POOL_GC, POOL_GC)),
                   full2((HALO, D))],
        out_shape=[_sds((8, t, D), BF16), _sds((8, D), F32), _sds((8, D), F32), _sds((SGU_G, CHUNK, CHUNK), F32),
                   _sds((CHUNK, CHUNK), F32), _sds((4, POOL_GC, POOL_GC), F32), _sds((HALO, D), F32)],
        scratch_shapes=[pltpu.VMEM((tm, D), F32), pltpu.VMEM((tm, D), F32), pltpu.VMEM((n_ext, D), F32),
                        pltpu.VMEM((SUBLANE, n_ext, LANE), F32), pltpu.VMEM((SUBLANE, n_ext, LANE), F32),
                        pltpu.VMEM((HALO, SUBLANE, D), F32)],
    )(z8, z8, z8, z8, z8, z8, z8, z8, cv, cv, dsa, dsb, dsb, dsc, dsc, dzg, ln_g, ln_b, w_s, bs_b, pool_w,
      pool_scale, conv_w, cln_g, cln_b), tasks)


def _ada_fwd(c_all, w_ada, b_loc, name):
    def body(c_ref, w_ref, b_ref, o_ref):
        cv = c_ref[...]
        ca = (cv * _sigmoid(cv)).astype(BF16)
        for l in range(DEPTH):
            o_ref[l] = _dot(ca, w_ref[l].astype(BF16)) + b_ref[l]

    return pl.pallas_call(body, name=name, out_shape=_sds((DEPTH, N_DEV, ADA_BLK), F32),
                          compiler_params=_params())(c_all, w_ada, b_loc)


def _ada_bwd(c_all_t, d_loc, name):
    def body(c_ref, d_ref, o_ref):
        cv = c_ref[...]
        ca = cv * _sigmoid(cv)
        for l in range(DEPTH):
            acc = jnp.zeros((D, ADA_BLK), F32)
            for j in range(N_DEV):
                acc = acc + ca[:, j:j + 1] * d_ref[l, j:j + 1, :]
            o_ref[l] = acc

    return pl.pallas_call(body, name=name, out_shape=_sds((DEPTH, D, ADA_BLK), F32),
                          compiler_params=_params())(c_all_t, d_loc)


def _sum8(g8, name):
    _, rows, cols = g8.shape
    tr = _tile(rows, 256)

    def body(g_ref, o_ref):
        acc = g_ref[0]
        for k in range(1, N_DEV):
            acc = acc + g_ref[k]
        o_ref[...] = acc

    return pl.pallas_call(body, name=name, grid=(rows // tr,),
                          in_specs=[pl.BlockSpec((N_DEV, tr, cols), lambda r: (0, r, 0))],
                          out_specs=pl.BlockSpec((tr, cols), lambda r: (r, 0)), out_shape=_sds((rows, cols), F32),
                          compiler_params=_params("arbitrary"))(g8)


PROJ = ("w_pa", "w_pb", "w_pc", "w_out")
FWD_GATHERS = {
    (0, "in_proj"): tuple((n, 0) for n in PROJ) + (("w_ffn_out", 0),),
    (0, "branches"): (("w_ffn_in", 0),),
    (0, "ffn_in"): (("w_in", 1),),
    (0, "ffn_out"): tuple((n, 1) for n in PROJ),
    (1, "in_proj"): (("w_ffn_in", 1),),
    (1, "branches"): (("w_ffn_out", 1),),
}
GRAD_GROUP = {"w_in": ("in", 0), "pool_w": ("in", 1), "w_pa": ("proj", 0), "w_pb": ("proj", 1), "w_pc": ("proj", 2),
              "w_out": ("proj", 3), "w_ffn_in": ("ffn", 0), "w_ffn_out": ("ffn", 1)}


def _run(factory, *args, tasks=(), **kw):
    out = factory(*args, tasks=tasks, **kw)
    return out if tasks else (out, [])


REPLICATED = ("b_ada", "g_mix", "b_in", "sgu_ln_g", "sgu_ln_b", "sgu_w_s", "sgu_b_s", "pool_scale", "conv_b",
              "conv_ln_g", "conv_ln_b", "g_ffn", "g_final")
WEIGHT_ORDER = ("w_ada", "b_ada", "g_mix", "w_in", "b_in", "sgu_ln_g", "sgu_ln_b", "sgu_w_s", "sgu_b_s", "w_pa",
                "pool_w", "pool_scale", "w_pb", "conv_w", "conv_b", "conv_ln_g", "conv_ln_b", "w_pc", "w_out",
                "g_ffn", "w_ffn_in", "w_ffn_out", "g_final")


def _rows(a):
    return a.reshape(-1, D)


def kernel(x, c, w_ada, b_ada, g_mix, w_in, b_in, sgu_ln_g, sgu_ln_b, sgu_w_s, sgu_b_s, w_pa, pool_w, pool_scale, w_pb, conv_w, conv_b, conv_ln_g, conv_ln_b, w_pc, w_out, g_ffn, w_ffn_in, w_ffn_out, g_final, loss_target, m_w_ada, m_b_ada, m_g_mix, m_w_in, m_b_in, m_sgu_ln_g, m_sgu_ln_b, m_sgu_w_s, m_sgu_b_s, m_w_pa, m_pool_w, m_pool_scale, m_w_pb, m_conv_w, m_conv_b, m_conv_ln_g, m_conv_ln_b, m_w_pc, m_w_out, m_g_ffn, m_w_ffn_in, m_w_ffn_out, m_g_final, v_w_ada, v_b_ada, v_g_mix, v_w_in, v_b_in, v_sgu_ln_g, v_sgu_ln_b, v_sgu_w_s, v_sgu_b_s, v_w_pa, v_pool_w, v_pool_scale, v_w_pb, v_conv_w, v_conv_b, v_conv_ln_g, v_conv_ln_b, v_w_pc, v_w_out, v_g_ffn, v_w_ffn_in, v_w_ffn_out, v_g_final):
    weights = dict(w_ada=w_ada, b_ada=b_ada, g_mix=g_mix, w_in=w_in, b_in=b_in, sgu_ln_g=sgu_ln_g, sgu_ln_b=sgu_ln_b,
                   sgu_w_s=sgu_w_s, sgu_b_s=sgu_b_s, w_pa=w_pa, pool_w=pool_w, pool_scale=pool_scale, w_pb=w_pb,
                   conv_w=conv_w, conv_b=conv_b, conv_ln_g=conv_ln_g, conv_ln_b=conv_ln_b, w_pc=w_pc, w_out=w_out,
                   g_ffn=g_ffn, w_ffn_in=w_ffn_in, w_ffn_out=w_ffn_out, g_final=g_final)
    mom1 = dict(w_ada=m_w_ada, b_ada=m_b_ada, g_mix=m_g_mix, w_in=m_w_in, b_in=m_b_in, sgu_ln_g=m_sgu_ln_g,
                sgu_ln_b=m_sgu_ln_b, sgu_w_s=m_sgu_w_s, sgu_b_s=m_sgu_b_s, w_pa=m_w_pa, pool_w=m_pool_w,
                pool_scale=m_pool_scale, w_pb=m_w_pb, conv_w=m_conv_w, conv_b=m_conv_b, conv_ln_g=m_conv_ln_g,
                conv_ln_b=m_conv_ln_b, w_pc=m_w_pc, w_out=m_w_out, g_ffn=m_g_ffn, w_ffn_in=m_w_ffn_in,
                w_ffn_out=m_w_ffn_out, g_final=m_g_final)
    mom2 = dict(w_ada=v_w_ada, b_ada=v_b_ada, g_mix=v_g_mix, w_in=v_w_in, b_in=v_b_in, sgu_ln_g=v_sgu_ln_g,
                sgu_ln_b=v_sgu_ln_b, sgu_w_s=v_sgu_w_s, sgu_b_s=v_sgu_b_s, w_pa=v_w_pa, pool_w=v_pool_w,
                pool_scale=v_pool_scale, w_pb=v_w_pb, conv_w=v_conv_w, conv_b=v_conv_b, conv_ln_g=v_conv_ln_g,
                conv_ln_b=v_conv_ln_b, w_pc=v_w_pc, w_out=v_w_out, g_ffn=v_g_ffn, w_ffn_in=v_w_ffn_in,
                w_ffn_out=v_w_ffn_out, g_final=v_g_final)

    for group in (weights, mom1, mom2):
        group["w_ffn_in"] = jnp.transpose(group["w_ffn_in"], (0, 2, 1))

    t = x.shape[1]
    xs = x.reshape(t, D)
    target = loss_target.reshape(t, D)
    me = 4 * lax.axis_index("x") + 2 * lax.axis_index("y") + lax.axis_index("c")
    core = lax.axis_index("c").astype(jnp.int32).reshape(1)

    bf = lambda n, l: weights[n][l].astype(BF16)
    (first,) = _transfer([_gather_task([bf("w_in", 0), c, pool_w, conv_w])], name="gather_first")
    w_in0, c_all, pool_all, conv_all = first
    full = [dict(w_in=w_in0)] + [dict() for _ in range(1, DEPTH)]

    def gather_at(l, stage):
        names = FWD_GATHERS.get((l, stage), ())
        return [_gather_task([bf(n, ll) for n, ll in names])] if names else []

    def landed(l, stage, per):
        for (n, ll), arr in zip(FWD_GATHERS.get((l, stage), ()), per[0] if per else ()):
            full[ll][n] = arr

    c_all = c_all.reshape(N_DEV, D)
    pool_full = jnp.transpose(pool_all, (1, 2, 0, 3, 4)).reshape(DEPTH, 4, POOL_GC, POOL_GC)
    conv_full = jnp.transpose(conv_all, (1, 2, 0, 3)).reshape(DEPTH, CONV_K, D)
    conv_full = jnp.pad(conv_full, ((0, 0), (0, HALO - CONV_K), (0, 0)))

    b_loc = lax.dynamic_slice_in_dim(b_ada, me * ADA_BLK, ADA_BLK, axis=1).reshape(DEPTH, 1, ADA_BLK)
    ada_part = _ada_fwd(c_all, w_ada, b_loc, name="ada_fwd")
    ((ada_all,),) = _transfer([_gather_task([ada_part])], name="gather_ada")
    ada = lax.dynamic_index_in_dim(ada_all, me, axis=2, keepdims=False)
    ada = jnp.transpose(ada, (1, 0, 2)).reshape(DEPTH, 6, 1, D)

    bs_b = jnp.broadcast_to(sgu_b_s[..., None], (DEPTH, SGU_G, CHUNK, CHUNK))

    saved = []
    xl = xs
    h = _norm_mod(xl, g_mix[0].reshape(1, D), ada[0, 1], ada[0, 0], name="norm_mix_0")
    for l in range(DEPTH):
        w = full[l]
        sh_m, sc_m, gt_m, sh_f, sc_f, gt_f = (ada[l, k] for k in range(6))
        row = lambda a: a[l].reshape(1, D)
        z8, per = _run(_mm_cols, h, w["w_in"], b_in[l].reshape(8, 1, D), name=f"in_proj_{l}",
                       tasks=gather_at(l, "in_proj"))
        landed(l, "in_proj", per)
        (sa, sb, sc, cv), per = _run(
            _branches_fwd, z8, row(sgu_ln_g), row(sgu_ln_b), sgu_w_s[l], bs_b[l], pool_full[l], row(pool_scale),
            conv_full[l], row(conv_b), row(conv_ln_g), row(conv_ln_b), name=f"branches_{l}",
            tasks=gather_at(l, "branches"))
        landed(l, "branches", per)
        wpa, wpb, wpc, wout = (w[n].reshape(D, D) for n in PROJ)
        (ya, yb, yc, merged), per = _run(_proj_merge, sa, sb, sc, wpa, wpb, wpc, z8, name=f"proj_merge_{l}",
                                         tasks=gather_at(l, "proj_merge"))
        landed(l, "proj_merge", per)
        om, x1, h2 = _out_proj(merged, wout, xl, gt_m, row(g_ffn), sc_f, sh_f, name=f"out_proj_{l}")
        wfi = w["w_ffn_in"].reshape(2, 4, FF_BLK, D)
        (gu, f4), per = _run(_ffn_in, h2, wfi, name=f"ffn_in_{l}", tasks=gather_at(l, "ffn_in"))
        landed(l, "ffn_in", per)
        wfo4 = w["w_ffn_out"].reshape(4, FF_BLK, D)
        nxt = (g_mix[l + 1].reshape(1, D), ada[l + 1, 1], ada[l + 1, 0]) if l + 1 < DEPTH else (None, None, None)
        res, per = _run(_ffn_out, f4, wfo4, x1, gt_f, *nxt, name=f"ffn_out_{l}", tasks=gather_at(l, "ffn_out"))
        landed(l, "ffn_out", per)
        o, x2 = res[0], res[1]
        saved.append(dict(x=xl, h=h, z8=z8, sa=sa, sb=sb, sc=sc, cv=cv, ya=ya, yb=yb, yc=yc, merged=merged, om=om,
                          x1=x1, h2=h2, gu=gu, f4=f4, o=o, wpa=wpa, wpb=wpb, wpc=wpc, wout=wout, wfi=wfi, wfo4=wfo4))
        xl = x2
        h = res[2] if l + 1 < DEPTH else None

    loss_tile, dx, dg_final, do, dgt_f = _final_loss(xl, g_final.reshape(1, D), target, saved[-1]["o"],
                                                     ada[DEPTH - 1, 5], name="final_loss")
    loss_row = jnp.broadcast_to(loss_tile[0:1, 0:1], (1, D))

    chip_parts = [dict() for _ in range(DEPTH)]
    small_buf = [None] * DEPTH
    small_all = [None] * DEPTH
    tril = jnp.tril(jnp.ones((CHUNK, CHUNK), F32))
    for l in reversed(range(DEPTH)):
        s, w = saved[l], full[l]
        above = l + 1 if l + 1 < DEPTH else None
        sh_m, sc_m, gt_m, sh_f, sc_f, gt_f = (ada[l, k] for k in range(6))
        row = lambda a: a[l].reshape(1, D)
        dgu, per = _run(_ffn_bwd_act, do, s["wfo4"], s["gu"], name=f"ffn_bwd_act_{l}",
                        tasks=[] if above is None else [_gather_task([small_buf[above]])])
        if above is not None:
            (small_all[above],) = per[0]
        d_wfo = _mm_tn(s["f4"], do[None], name=f"dw_ffn_out_{l}")
        dgu8 = dgu.reshape(8, t, FF_BLK)
        dh2, per = _run(_mm_nt_sum, dgu8, w["w_ffn_in"], name=f"dh_ffn_{l}", b_is_kn=True,
                        tasks=[] if above is None else [_chips_task(in_sums)])
        if above is not None:
            chip_parts[above]["in"] = per[0]
        d_wfi = _mm_tn(dgu8, s["h2"][None], name=f"dw_ffn_in_{l}")
        dx1, st_f, dom, dgt_m = _norm_mod_bwd(dh2, s["x1"], dx, row(g_ffn), sc_f, s["om"], gt_m,
                                              name=f"norm_ffn_bwd_{l}")
        ffn_group = [d_wfi, d_wfo.reshape(8, D_FF // 8, D)]
        (dya, dyb, dyc, dzg, db_gate), per = _run(_merge_bwd, dom, s["wout"], s["z8"], s["ya"], s["yb"], s["yc"],
                                                  name=f"merge_bwd_{l}", tasks=[_sibling_task(ffn_group)])
        ffn_sums = _sibling_sums(ffn_group, per[0], core, tag=f"ffn_{l}")
        d_wout = _mm_tn(s["merged"][None], dom[None], name=f"dw_out_{l}")
        d_wpa = _mm_tn(s["sa"][None], dya[None], name=f"dw_pa_{l}")
        d_wpb = _mm_tn(s["sb"][None], dyb[None], name=f"dw_pb_{l}")
        d_wpc = _mm_tn(s["sc"][None], dyc[None], name=f"dw_pc_{l}")
        dsa = _mm_nt_sum(dya[None], s["wpa"][None], name=f"ds_a_{l}")
        dsb = _mm_nt_sum(dyb[None], s["wpb"][None], name=f"ds_b_{l}")
        dsc = _mm_nt_sum(dyc[None], s["wpc"][None], name=f"ds_c_{l}")
        proj_group = [g.reshape(8, D // 8, D) for g in (d_wpa, d_wpb, d_wpc, d_wout)]
        (dz8, db_in5, rows6, dws, dbs, dwp, dcw), per = _run(
            _branches_bwd, s["z8"], s["cv"], dsa, dsb, dsc, dzg, row(sgu_ln_g), row(sgu_ln_b), sgu_w_s[l], bs_b[l],
            pool_full[l], row(pool_scale), conv_full[l], row(conv_ln_g), row(conv_ln_b), name=f"branches_bwd_{l}",
            tasks=[_chips_task(ffn_sums), _sibling_task(proj_group)])
        chip_parts[l]["ffn"] = per[0]
        proj_sums = _sibling_sums(proj_group, per[1], core, tag=f"proj_{l}")
        d_win = _mm_tn(s["h"][None], dz8, name=f"dw_in_{l}")
        d_pool = jnp.transpose(dwp.reshape(4, N_DEV, POOL_GC // N_DEV, POOL_GC), (1, 0, 2, 3))
        in_group = [d_win, d_pool.reshape(N_DEV, 4 * POOL_GC // N_DEV, POOL_GC)]
        dh, per = _run(_mm_nt_sum, dz8, w["w_in"], name=f"dh_in_{l}",
                       tasks=[_chips_task(proj_sums), _sibling_task(in_group)])
        chip_parts[l]["proj"] = per[0]
        in_sums = _sibling_sums(in_group, per[1], core, tag=f"in_{l}")
        gate = (saved[l - 1]["o"], ada[l - 1, 5]) if l > 0 else (None, None)
        res, per = _run(_norm_mod_bwd, dh, s["x"], dx1, row(g_mix), sc_m, *gate, name=f"norm_mix_bwd_{l}",
                        tasks=[_chips_task(in_sums)] if l == 0 else [])
        dx, st_m = res[0], res[1]
        if l == 0:
            chip_parts[0]["in"] = per[0]
        segs = [st_m[0:1], st_m[1:2], dgt_m, st_f[0:1], st_f[1:2], dgt_f]
        segs += [st_m[2:3], db_in5[0:5], db_gate, rows6[0:2], _rows(dws * tril), _rows(jnp.transpose(dbs[:, :SGU_G])),
                 rows6[2:6], st_f[2:3], dcw, dg_final if l == 0 else loss_row]
        small_buf[l] = jnp.concatenate(segs, axis=0)
        if l > 0:
            do, dgt_f = res[2], res[3]
    small_all[0] = _transfer([_gather_task([small_buf[0]])], name="gather_last")[0][0]

    red = [_sum8(small_all[l], name=f"sum_small_grads_{l}") for l in range(DEPTH)]
    layers = lambda lo, hi: jnp.stack([red[l][lo:hi] for l in range(DEPTH)], axis=0)
    small_rows = dict(b_ada=(0, 6), g_mix=(6, 7), b_in=(7, 15), sgu_ln_g=(15, 16), sgu_ln_b=(16, 17),
                      sgu_w_s=(17, 145), sgu_b_s=(145, 146), pool_scale=(146, 147), conv_b=(147, 148),
                      conv_ln_g=(148, 149), conv_ln_b=(149, 150), g_ffn=(150, 151))
    grads = {n: layers(lo, hi).reshape(weights[n].shape) for n, (lo, hi) in small_rows.items()}
    grads["g_final"] = red[0][183]
    loss = red[DEPTH - 1][183, 0]
    conv_g = layers(151, 151 + CONV_K)
    grads["conv_w"] = lax.dynamic_slice_in_dim(conv_g, me * (D // N_DEV), D // N_DEV, axis=2)
    d_ada_all = jnp.stack([small_all[l][:, 0:6] for l in range(DEPTH)], axis=1).reshape(N_DEV, DEPTH, 6 * D)
    d_loc = jnp.transpose(lax.dynamic_slice_in_dim(d_ada_all, me * ADA_BLK, ADA_BLK, axis=2), (1, 0, 2))
    grads["w_ada"] = _ada_bwd(jnp.transpose(c_all), d_loc, name="ada_bwd")

    out = {}
    for n in REPLICATED + ("conv_w", "w_ada"):
        out[n] = _adam_nd(grads[n], weights[n], mom1[n], mom2[n], name=f"adam_{n}")
    for n, (group, k) in GRAD_GROUP.items():
        parts = [chip_parts[l][group][k] for l in range(DEPTH)]
        shape = (DEPTH,) + parts[0].shape[1:]
        res = _adam(parts, weights[n].reshape(shape), mom1[n].reshape(shape), mom2[n].reshape(shape),
                    name=f"adam_{n}")
        out[n] = [r.reshape(weights[n].shape) for r in res]
    out["w_ffn_in"] = [jnp.transpose(r, (0, 2, 1)) for r in out["w_ffn_in"]]

    grad_x = dx.reshape(1, t, D)
    return (loss, grad_x, *[out[n][0] for n in WEIGHT_ORDER], *[out[n][1] for n in WEIGHT_ORDER],
            *[out[n][2] for n in WEIGHT_ORDER], *[out[n][3] for n in WEIGHT_ORDER])
```

```python
import math

import jax
import jax.numpy as jnp
from jax import lax
from jax.experimental import pallas as pl
from jax.experimental.pallas import tpu as pltpu

F32 = jnp.float32
BF16 = jnp.bfloat16
MESH = pl.DeviceIdType.MESH
AXES = ("x", "y", "c")
N_DEV = 8

D = 1024
DEPTH = 2
EPS = 1e-6
CHUNK = 128
SGU_G = 8
POOL_WINDOWS = (2, 4, 8, 16)
POOL_GC = 256
CONV_K = 31
HALO = 32
SUBLANE = 8
LANE = 128
CONV_STRIP = 128
D_FF = 2816
FF_BLK = D_FF // 4
ADA_BLK = 6 * D // N_DEV

ADAM_LR = 0.001
ADAM_B1 = 0.9
ADAM_B2 = 0.999
ADAM_EPS = 1e-08
ADAM_WD = 0.01
ADAM_STEP = 10

VMEM_LIMIT_V7X = 56 * 1024 * 1024
INV_SQRT2 = 1.0 / math.sqrt(2.0)
INV_SQRT_2PI = 1.0 / math.sqrt(2.0 * math.pi)


def _params(*sem):
    return pltpu.CompilerParams(dimension_semantics=sem if sem else None, vmem_limit_bytes=VMEM_LIMIT_V7X)


def _tile(n, pref):
    if n <= pref:
        return n
    for t in range(pref - pref % 8, 0, -8):
        if n % t == 0:
            return t
    raise ValueError((n, pref))


def _sds(shape, dtype):
    return jax.ShapeDtypeStruct(shape, dtype)


def _sigmoid(x):
    return 1.0 / (1.0 + jnp.exp(-x))


def _gelu(x):
    return 0.5 * x * (1.0 + lax.erf(x * INV_SQRT2))


def _gelu_grad(x):
    return 0.5 * (1.0 + lax.erf(x * INV_SQRT2)) + x * (INV_SQRT_2PI * jnp.exp(-0.5 * x * x))


def _ln_stats(v):
    mu = jnp.mean(v, axis=-1, keepdims=True)
    vc = v - mu
    rstd = lax.rsqrt(jnp.mean(vc * vc, axis=-1, keepdims=True) + EPS)
    return vc * rstd, rstd


def _ln_bwd(dvhat, vhat, rstd):
    return rstd * (dvhat - jnp.mean(dvhat, axis=-1, keepdims=True)
                   - vhat * jnp.mean(dvhat * vhat, axis=-1, keepdims=True))


def _colsum(v):
    return jnp.sum(v, axis=0, keepdims=True)


def _dot(a, b):
    return jnp.dot(a, b, preferred_element_type=F32)


def _dot_nt(a, b):
    return lax.dot_general(a, b, (((1,), (1,)), ((), ())), preferred_element_type=F32)


def _dot_tn(a, b):
    return lax.dot_general(a, b, (((0,), (0,)), ((), ())), preferred_element_type=F32)


def _tril_mask():
    r = lax.broadcasted_iota(jnp.int32, (CHUNK, CHUNK), 0)
    c = lax.broadcasted_iota(jnp.int32, (CHUNK, CHUNK), 1)
    return (r >= c).astype(F32)


def _mesh_pos():
    return tuple(lax.axis_index(a) for a in AXES)


class _Task:
    def __init__(self, arrays, out_shapes, scratch, start, finish):
        self.arrays, self.out_shapes, self.scratch, self.start, self.finish = arrays, out_shapes, scratch, start, finish


def _hosted(tasks, body, *, name, grid, in_specs, out_specs, out_shape, scratch_shapes=()):
    single = not isinstance(out_shape, (list, tuple))
    out_shape, out_specs = ([out_shape], [out_specs]) if single else (list(out_shape), list(out_specs))
    n_in, n_out, n_scr = len(in_specs), len(out_shape), len(scratch_shapes)
    sizes = [(len(t.arrays), len(t.out_shapes), len(t.scratch)) for t in tasks]
    t_in, t_out, t_scr = (sum(s[k] for s in sizes) for k in range(3))
    any_spec = pl.BlockSpec(memory_space=pl.ANY)

    def wrapped(*refs):
        refs = list(refs)
        ins, refs = refs[:n_in + t_in], refs[n_in + t_in:]
        outs, scr = refs[:n_out + t_out], refs[n_out + t_out:]

        def per_task(fn_name):
            i0, o0, s0 = n_in, n_out, n_scr
            for t, (ni, no, ns) in zip(tasks, sizes):
                getattr(t, fn_name)(ins[i0:i0 + ni], outs[o0:o0 + no], scr[s0:s0 + ns])
                i0, o0, s0 = i0 + ni, o0 + no, s0 + ns

        if tasks and grid:
            first, last = None, None
            for d, g in enumerate(grid):
                f, e = pl.program_id(d) == 0, pl.program_id(d) == g - 1
                first, last = (f, e) if first is None else (first & f, last & e)
            pl.when(first)(lambda: per_task("start"))
        elif tasks:
            per_task("start")
        body(*ins[:n_in], *outs[:n_out], *scr[:n_scr])
        if tasks and grid:
            pl.when(last)(lambda: per_task("finish"))
        elif tasks:
            per_task("finish")

    call = pl.pallas_call(
        wrapped, name=name, grid=grid,
        in_specs=list(in_specs) + [any_spec] * t_in, out_specs=out_specs + [any_spec] * t_out,
        out_shape=out_shape + [s for t in tasks for s in t.out_shapes],
        scratch_shapes=list(scratch_shapes) + [s for t in tasks for s in t.scratch],
        compiler_params=_params(*(("arbitrary",) * len(grid))))

    def run(*operands):
        res = list(call(*operands, *[a for t in tasks for a in t.arrays]))
        host, rest, per = res[:n_out], res[n_out:], []
        for _, no, _ in sizes:
            per.append(rest[:no])
            rest = rest[no:]
        return (host[0] if single else host), per

    return run


def _transfer(tasks, name):
    return _hosted(tasks, lambda: None, name=name, grid=(), in_specs=[], out_specs=[], out_shape=[])()[1]


def _gather_task(arrs):
    n = len(arrs)

    def plan(ins, outs, sems):
        send_sems, recv_sems, local_sems = sems
        x, y, c = _mesh_pos()
        me, sibling = (x, y, c), (x, y, 1 - c)
        chips = [(1 - x, y), (x, 1 - y), (1 - x, 1 - y)]

        def slot(a, p):
            return outs[a].at[4 * p[0] + 2 * p[1] + p[2]]

        def copy(a, k, block, to, src=None):
            dst = slot(a, block)
            return pltpu.make_async_remote_copy(
                src_ref=dst if src is None else src, dst_ref=dst, send_sem=send_sems.at[a, k],
                recv_sem=recv_sems.at[a, k], device_id=to, device_id_type=MESH)

        mine = [pltpu.make_async_copy(ins[a], slot(a, me), local_sems.at[a]) for a in range(n)]
        first = []
        for a in range(n):
            first.append(copy(a, 0, me, sibling, src=ins[a]))
            first += [copy(a, 1 + j, me, (*chip, c), src=ins[a]) for j, chip in enumerate(chips)]
        return c, me, sibling, chips, copy, mine, first

    def start(ins, outs, sems):
        *_, mine, first = plan(ins, outs, sems)
        for cp in mine + first:
            cp.start()

    def finish(ins, outs, sems):
        c, me, sibling, chips, copy, mine, first = plan(ins, outs, sems)
        passed = []
        for j, chip in enumerate(chips):
            for a in range(n):
                copy(a, 1 + j, (*chip, c), me).wait_recv()
                fwd = copy(a, 4 + j, (*chip, c), sibling)
                fwd.start()
                passed.append(fwd)
        for a in range(n):
            copy(a, 0, sibling, me).wait_recv()
            for j, chip in enumerate(chips):
                copy(a, 4 + j, (*chip, 1 - c), me).wait_recv()
        for cp in first + passed:
            cp.wait_send()
        for m in mine:
            m.wait()

    return _Task(list(arrs), [_sds((N_DEV,) + a.shape, a.dtype) for a in arrs],
                 [pltpu.SemaphoreType.DMA((n, 7)), pltpu.SemaphoreType.DMA((n, 7)), pltpu.SemaphoreType.DMA((n,))],
                 start, finish)


def _sibling_task(arrs):
    n = len(arrs)

    def copies(ins, outs, sems):
        send_sems, recv_sems = sems
        x, y, c = _mesh_pos()
        return [pltpu.make_async_remote_copy(
            src_ref=ins[a].at[2 * q + (1 - c)], dst_ref=outs[a].at[q], send_sem=send_sems.at[a, q],
            recv_sem=recv_sems.at[a, q], device_id=(x, y, 1 - c), device_id_type=MESH)
            for a in range(n) for q in range(4)]

    def start(ins, outs, sems):
        for cp in copies(ins, outs, sems):
            cp.start()

    def finish(ins, outs, sems):
        for cp in copies(ins, outs, sems):
            cp.wait()

    return _Task(list(arrs), [_sds((4,) + a.shape[1:], a.dtype) for a in arrs],
                 [pltpu.SemaphoreType.DMA((n, 4)), pltpu.SemaphoreType.DMA((n, 4))], start, finish)


def _chips_task(arrs):
    n = len(arrs)

    def copies(ins, outs, sems):
        send_sems, recv_sems, local_sems = sems
        x, y, c = _mesh_pos()
        q_me = 2 * x + y
        chips = [(1 - x, y), (x, 1 - y), (1 - x, 1 - y)]
        own = [pltpu.make_async_copy(ins[a].at[q_me], outs[a].at[q_me], local_sems.at[a]) for a in range(n)]
        remote = [pltpu.make_async_remote_copy(
            src_ref=ins[a].at[2 * chip[0] + chip[1]], dst_ref=outs[a].at[q_me], send_sem=send_sems.at[a, j],
            recv_sem=recv_sems.at[a, j], device_id=(*chip, c), device_id_type=MESH)
            for a in range(n) for j, chip in enumerate(chips)]
        return own + remote

    def start(ins, outs, sems):
        for cp in copies(ins, outs, sems):
            cp.start()

    def finish(ins, outs, sems):
        for cp in copies(ins, outs, sems):
            cp.wait()

    return _Task(list(arrs), [_sds(a.shape, a.dtype) for a in arrs],
                 [pltpu.SemaphoreType.DMA((n, 3)), pltpu.SemaphoreType.DMA((n, 3)), pltpu.SemaphoreType.DMA((n,))],
                 start, finish)


def _sibling_sum(arr, land, core, name):
    _, rows, cols = arr.shape
    tr = _tile(rows, 512)
    arr4 = arr.reshape(4, 2, rows, cols)

    def body(c_ref, a_ref, l_ref, o_ref):
        o_ref[...] = (a_ref[...] + l_ref[...]).astype(BF16)

    grid_spec = pltpu.PrefetchScalarGridSpec(
        num_scalar_prefetch=1, grid=(4, rows // tr),
        in_specs=[pl.BlockSpec((None, None, tr, cols), lambda q, r, c_ref: (q, c_ref[0], r, 0)),
                  pl.BlockSpec((None, tr, cols), lambda q, r, c_ref: (q, r, 0))],
        out_specs=pl.BlockSpec((None, tr, cols), lambda q, r, c_ref: (q, r, 0)))
    return pl.pallas_call(body, name=name, grid_spec=grid_spec, out_shape=_sds((4, rows, cols), BF16),
                          compiler_params=_params("arbitrary", "arbitrary"))(core, arr4, land)


def _sibling_sums(arrs, land, core, tag):
    return [_sibling_sum(a, l, core, name=f"rs_sum_{tag}_{k}") for k, (a, l) in enumerate(zip(arrs, land))]


def _adam(gparts, w, m, v, name):
    n_l = len(gparts)
    p, rows, cols = gparts[0].shape
    tr = _tile(rows, 256)
    n_r = rows // tr
    c1 = 1.0 - ADAM_B1 ** ADAM_STEP
    c2 = 1.0 - ADAM_B2 ** ADAM_STEP

    def body(*refs):
        g_refs = refs[:n_l]
        w_ref, m_ref, v_ref, go_ref, d_ref, mo_ref, vo_ref = refs[n_l:]
        layer = pl.program_id(0)
        g = jnp.zeros((tr, cols), F32)
        for li, g_ref in enumerate(g_refs):
            gl = g_ref[0].astype(F32)
            for k in range(1, p):
                gl = gl + g_ref[k].astype(F32)
            g = gl if n_l == 1 else jnp.where(layer == li, gl, g)
        m_new = ADAM_B1 * m_ref[...] + (1.0 - ADAM_B1) * g
        v_new = ADAM_B2 * v_ref[...] + (1.0 - ADAM_B2) * (g * g)
        m_hat = m_new / c1
        v_hat = v_new / c2
        go_ref[...] = g
        d_ref[...] = -ADAM_LR * (m_hat / (jnp.sqrt(v_hat) + ADAM_EPS) + ADAM_WD * w_ref[...])
        mo_ref[...] = m_new
        vo_ref[...] = v_new

    def g_spec(li):
        def index(l, r):
            return (0, jnp.where(l == li, r, jnp.where(l < li, 0, n_r - 1)), 0)
        return pl.BlockSpec((p, tr, cols), index)

    blk = pl.BlockSpec((None, tr, cols), lambda l, r: (l, r, 0))
    return pl.pallas_call(
        body, name=name, grid=(n_l, n_r),
        in_specs=[g_spec(li) for li in range(n_l)] + [blk, blk, blk],
        out_specs=[blk] * 4, out_shape=[_sds((n_l, rows, cols), F32)] * 4,
        compiler_params=_params("arbitrary", "arbitrary"))(*gparts, w, m, v)


def _adam_nd(grad, w, m, v, name):
    shape = w.shape
    cols = shape[-1]
    rows = w.size // cols
    as_rows = lambda a: a.reshape(1, rows, cols)
    out = _adam([as_rows(grad)], as_rows(w), as_rows(m), as_rows(v), name)
    return [o.reshape(shape) for o in out]


def _norm_mod(x, g, sc, sh, name):
    t = x.shape[0]
    tm = _tile(t, 512)

    def body(x_ref, g_ref, sc_ref, sh_ref, h_ref):
        h_ref[...] = _modulated_norm(x_ref[...], g_ref[...], sc_ref[...], sh_ref[...])

    row = pl.BlockSpec((1, D), lambda i: (0, 0))
    blk = pl.BlockSpec((tm, D), lambda i: (i, 0))
    return pl.pallas_call(body, name=name, grid=(t // tm,), in_specs=[blk, row, row, row], out_specs=blk,
                          out_shape=_sds((t, D), BF16), compiler_params=_params("arbitrary"))(x, g, sc, sh)


def _with_tasks(res_per, tasks):
    return res_per if tasks else res_per[0]


def _mm_cols(a, b8, bias8, name, tasks=()):
    t, k = a.shape
    j, _, n = b8.shape
    tm = _tile(t, 1024)

    def body(a_ref, b_ref, bias_ref, o_ref):
        o_ref[...] = (_dot(a_ref[...], b_ref[...]) + bias_ref[...]).astype(BF16)

    return _with_tasks(_hosted(
        tasks, body, name=name, grid=(j, t // tm),
        in_specs=[pl.BlockSpec((tm, k), lambda jj, i: (i, 0)),
                  pl.BlockSpec((None, k, n), lambda jj, i: (jj, 0, 0)),
                  pl.BlockSpec((None, 1, n), lambda jj, i: (jj, 0, 0))],
        out_specs=pl.BlockSpec((None, tm, n), lambda jj, i: (jj, i, 0)),
        out_shape=_sds((j, t, n), BF16))(a, b8, bias8), tasks)


def _halo_before(tm, col):
    return pl.BlockSpec((None, HALO, D), lambda i: (col, jnp.maximum(i * (tm // HALO) - 1, 0), 0))


def _pool_forward(p_ext, t0, rows):
    t = t0 + lax.broadcasted_iota(jnp.int32, (rows, 1), 0)
    out = []
    for gi, win in enumerate(POOL_WINDOWS):
        e = p_ext[:, gi * POOL_GC:(gi + 1) * POOL_GC]
        s, sh = e, 1
        while sh < win:
            s = s + pltpu.roll(s, sh, 0)
            sh *= 2
        cnt = jnp.minimum(t + 1, win).astype(F32)
        out.append(s[HALO:] / cnt - e[HALO:])
    return out


def _fill_shift_bank(bank_ref, ext, causal):
    n = ext.shape[0]
    bank_ref[0] = ext
    for b in range(1, SUBLANE):
        bank_ref[b] = pltpu.roll(ext, b if causal else n - b, 0)


def _branches_fwd(z8, ln_g, ln_b, w_s, bs_b, pool_w, pool_scale, conv_w, conv_b, cln_g, cln_b, name, tasks=()):
    t = z8.shape[1]
    tm = _tile(t, 256)
    n_ext = tm + HALO

    def body(zu_ref, zv_ref, p_ref, a_ref, ag_ref, ph_ref, ah_ref, agh_ref, lng_ref, lnb_ref, ws_ref, bsb_ref,
             wp_ref, ps_ref, cw_ref, cb_ref, clg_ref, clb_ref, sa_ref, sb_ref, sc_ref, cv_ref, bank_ref):
        i = pl.program_id(0)
        has_past = (i > 0).astype(F32)
        u = _gelu(zu_ref[...].astype(F32))
        vhat, _ = _ln_stats(_gelu(zv_ref[...].astype(F32)))
        vb = (vhat * lng_ref[...] + lnb_ref[...]).astype(BF16)
        mask = _tril_mask()
        for g in range(SGU_G):
            cols = slice(g * CHUNK, (g + 1) * CHUNK)
            wm = (ws_ref[g] * mask).astype(BF16)
            for n in range(tm // CHUNK):
                rows = slice(n * CHUNK, (n + 1) * CHUNK)
                mixed = _dot(wm, vb[rows, cols]) + bsb_ref[g]
                sa_ref[rows, cols] = (u[rows, cols] * mixed).astype(BF16)
        p_ext = jnp.concatenate([ph_ref[...].astype(F32) * has_past, p_ref[...].astype(F32)], axis=0)
        pooled = _pool_forward(p_ext, i * tm, tm)
        for gi in range(len(POOL_WINDOWS)):
            cols = slice(gi * POOL_GC, (gi + 1) * POOL_GC)
            y = _dot(pooled[gi].astype(BF16), wp_ref[gi].astype(BF16))
            sb_ref[:, cols] = (y * ps_ref[:, cols]).astype(BF16)
        for cb in range(D // LANE):
            cols = slice(cb * LANE, (cb + 1) * LANE)
            zc = jnp.concatenate(
                [ah_ref[:, cols].astype(F32) * has_past * _sigmoid(agh_ref[:, cols].astype(F32)),
                 a_ref[:, cols].astype(F32) * _sigmoid(ag_ref[:, cols].astype(F32))], axis=0)
            _fill_shift_bank(bank_ref, zc, causal=True)
            for r0 in range(0, tm, CONV_STRIP):
                acc = jnp.zeros((CONV_STRIP, LANE), F32) + cb_ref[:, cols]
                for k in range(CONV_K):
                    hi, lo = divmod(CONV_K - 1 - k, SUBLANE)
                    acc = acc + cw_ref[k:k + 1, cols] * bank_ref[lo, pl.ds(HALO - SUBLANE * hi + r0, CONV_STRIP), :]
                cv_ref[r0:r0 + CONV_STRIP, cols] = acc
        cv = cv_ref[...]
        chat, _ = _ln_stats(cv)
        cl = chat * clg_ref[...] + clb_ref[...]
        sc_ref[...] = (cl * _sigmoid(cl)).astype(BF16)

    def col(j):
        return pl.BlockSpec((None, tm, D), lambda i: (j, i, 0))

    row = pl.BlockSpec((1, D), lambda i: (0, 0))
    full3 = lambda s: pl.BlockSpec(s, lambda i: (0, 0, 0))
    blk = pl.BlockSpec((tm, D), lambda i: (i, 0))
    return _with_tasks(_hosted(
        tasks, body, name=name, grid=(t // tm,),
        in_specs=[col(0), col(1), col(2), col(3), col(4), _halo_before(tm, 2), _halo_before(tm, 3),
                  _halo_before(tm, 4), row, row, full3((SGU_G, CHUNK, CHUNK)), full3((SGU_G, CHUNK, CHUNK)),
                  full3((4, POOL_GC, POOL_GC)), row, pl.BlockSpec((HALO, D), lambda i: (0, 0)), row, row, row],
        out_specs=[blk, blk, blk, blk],
        out_shape=[_sds((t, D), BF16)] * 3 + [_sds((t, D), F32)],
        scratch_shapes=[pltpu.VMEM((SUBLANE, n_ext, LANE), F32)],
    )(z8, z8, z8, z8, z8, z8, z8, z8, ln_g, ln_b, w_s, bs_b, pool_w, pool_scale, conv_w, conv_b, cln_g, cln_b), tasks)


def _proj_merge(sa, sb, sc, w_pa, w_pb, w_pc, z8, name, tasks=()):
    t = sa.shape[0]
    tm = _tile(t, 512)

    def body(sa_ref, sb_ref, sc_ref, wa_ref, wb_ref, wc_ref, g0_ref, g1_ref, g2_ref, ya_ref, yb_ref, yc_ref, m_ref):
        merged = jnp.zeros((tm, D), F32)
        for s_ref, w_ref, g_ref, y_ref in ((sa_ref, wa_ref, g0_ref, ya_ref), (sb_ref, wb_ref, g1_ref, yb_ref),
                                           (sc_ref, wc_ref, g2_ref, yc_ref)):
            y = _dot(s_ref[...], w_ref[...])
            y_ref[...] = y.astype(BF16)
            merged = merged + _sigmoid(g_ref[...].astype(F32)) * y
        m_ref[...] = merged.astype(BF16)

    blk = pl.BlockSpec((tm, D), lambda i: (i, 0))
    wspec = pl.BlockSpec((D, D), lambda i: (0, 0))
    gate = lambda j: pl.BlockSpec((None, tm, D), lambda i: (j, i, 0))
    return _with_tasks(_hosted(
        tasks, body, name=name, grid=(t // tm,),
        in_specs=[blk, blk, blk, wspec, wspec, wspec, gate(5), gate(6), gate(7)],
        out_specs=[blk] * 4, out_shape=[_sds((t, D), BF16)] * 4)(sa, sb, sc, w_pa, w_pb, w_pc, z8, z8, z8), tasks)


def _modulated_norm(xv, g, sc, sh):
    r = lax.rsqrt(jnp.mean(xv * xv, axis=-1, keepdims=True) + EPS)
    return (xv * r * g * (1.0 + sc) + sh).astype(BF16)


def _out_proj(merged, w_out, x, gt, g, sc, sh, name):
    t = x.shape[0]
    tm = _tile(t, 512)

    def body(m_ref, w_ref, x_ref, gt_ref, g_ref, sc_ref, sh_ref, om_ref, x1_ref, h2_ref):
        om = _dot(m_ref[...], w_ref[...])
        om_ref[...] = om
        x1 = x_ref[...] + gt_ref[...] * om
        x1_ref[...] = x1
        h2_ref[...] = _modulated_norm(x1, g_ref[...], sc_ref[...], sh_ref[...])

    blk = pl.BlockSpec((tm, D), lambda i: (i, 0))
    row = pl.BlockSpec((1, D), lambda i: (0, 0))
    return pl.pallas_call(
        body, name=name, grid=(t // tm,),
        in_specs=[blk, pl.BlockSpec((D, D), lambda i: (0, 0)), blk, row, row, row, row],
        out_specs=[blk, blk, blk], out_shape=[_sds((t, D), F32)] * 2 + [_sds((t, D), BF16)],
        compiler_params=_params("arbitrary"))(merged, w_out, x, gt, g, sc, sh)


def _ffn_in(h2, wfi, name, tasks=()):
    t = h2.shape[0]
    tm = _tile(t, 512)

    def body(h_ref, w_ref, gu_ref, f_ref):
        hv = h_ref[...]
        gp = _dot_nt(hv, w_ref[0])
        up = _dot_nt(hv, w_ref[1])
        gu_ref[0] = gp.astype(BF16)
        gu_ref[1] = up.astype(BF16)
        f_ref[...] = (gp * _sigmoid(gp) * up).astype(BF16)

    return _with_tasks(_hosted(
        tasks, body, name=name, grid=(4, t // tm),
        in_specs=[pl.BlockSpec((tm, D), lambda j, i: (i, 0)),
                  pl.BlockSpec((2, None, FF_BLK, D), lambda j, i: (0, j, 0, 0))],
        out_specs=[pl.BlockSpec((2, None, tm, FF_BLK), lambda j, i: (0, j, i, 0)),
                   pl.BlockSpec((None, tm, FF_BLK), lambda j, i: (j, i, 0))],
        out_shape=[_sds((2, 4, t, FF_BLK), BF16), _sds((4, t, FF_BLK), BF16)])(h2, wfi), tasks)


def _ffn_out(f4, wfo4, x1, gt, g, sc, sh, name, tasks=()):
    t = x1.shape[0]
    tm = _tile(t, 512)
    with_norm = g is not None

    def body(f_ref, w_ref, x_ref, gt_ref, *rest):
        o_ref, x2_ref = rest[-3:-1] if with_norm else rest[-2:]
        j = pl.program_id(1)

        @pl.when(j == 0)
        def _():
            o_ref[...] = jnp.zeros_like(o_ref)

        o_ref[...] += _dot(f_ref[...], w_ref[...])

        @pl.when(j == 3)
        def _():
            x2 = x_ref[...] + gt_ref[...] * o_ref[...]
            x2_ref[...] = x2
            if with_norm:
                g_ref, sc_ref, sh_ref = rest[:3]
                rest[-1][...] = _modulated_norm(x2, g_ref[...], sc_ref[...], sh_ref[...])

    blk = pl.BlockSpec((tm, D), lambda i, j: (i, 0))
    row = pl.BlockSpec((1, D), lambda i, j: (0, 0))
    norm_args = [g, sc, sh] if with_norm else []
    return _with_tasks(_hosted(
        tasks, body, name=name, grid=(t // tm, 4),
        in_specs=[pl.BlockSpec((None, tm, FF_BLK), lambda i, j: (j, i, 0)),
                  pl.BlockSpec((None, FF_BLK, D), lambda i, j: (j, 0, 0)), blk, row] + [row] * len(norm_args),
        out_specs=[blk, blk] + [blk] * with_norm,
        out_shape=[_sds((t, D), F32)] * 2 + [_sds((t, D), BF16)] * with_norm)(f4, wfo4, x1, gt, *norm_args), tasks)


def _gate_grads(dx, o_ref, gt_ref, do_ref, dgt_ref):
    do_ref[...] = (dx * gt_ref[...]).astype(BF16)
    dgt_ref[...] += _colsum(dx * o_ref[...])


def _final_loss(x, g, target, o, gt, name):
    t = x.shape[0]
    tm = _tile(t, 512)

    def body(x_ref, g_ref, t_ref, o_ref, gt_ref, loss_ref, dx_ref, dg_ref, do_ref, dgt_ref):
        @pl.when(pl.program_id(0) == 0)
        def _():
            for ref in (loss_ref, dg_ref, dgt_ref):
                ref[...] = jnp.zeros_like(ref)

        xv = x_ref[...]
        r = lax.rsqrt(jnp.mean(xv * xv, axis=-1, keepdims=True) + EPS)
        xn = xv * r
        diff = xn * g_ref[...] - t_ref[...]
        loss_ref[...] += 0.5 * jnp.sum(jnp.mean(diff * diff, axis=-1, keepdims=True))
        dy = diff * (1.0 / D)
        dg_ref[...] += _colsum(dy * xn)
        dxn = dy * g_ref[...]
        dx = r * (dxn - xn * jnp.mean(dxn * xn, axis=-1, keepdims=True))
        dx_ref[...] = dx
        _gate_grads(dx, o_ref, gt_ref, do_ref, dgt_ref)

    blk = pl.BlockSpec((tm, D), lambda i: (i, 0))
    row = pl.BlockSpec((1, D), lambda i: (0, 0))
    return pl.pallas_call(
        body, name=name, grid=(t // tm,), in_specs=[blk, row, blk, blk, row],
        out_specs=[pl.BlockSpec((8, 128), lambda i: (0, 0)), blk, row, blk, row],
        out_shape=[_sds((8, 128), F32), _sds((t, D), F32), _sds((1, D), F32), _sds((t, D), BF16), _sds((1, D), F32)],
        compiler_params=_params("arbitrary"))(x, g, target, o, gt)


def _norm_mod_bwd(dh, x, dres, g, sc, o, gt, name, tasks=()):
    t = x.shape[0]
    tm = _tile(t, 512)
    with_gate = o is not None

    def body(dh_ref, x_ref, dr_ref, g_ref, sc_ref, *rest):
        dx_ref, st_ref = rest[2:4] if with_gate else rest

        @pl.when(pl.program_id(0) == 0)
        def _():
            st_ref[...] = jnp.zeros_like(st_ref)
            if with_gate:
                rest[5][...] = jnp.zeros_like(rest[5])

        xv, dhv = x_ref[...], dh_ref[...]
        r = lax.rsqrt(jnp.mean(xv * xv, axis=-1, keepdims=True) + EPS)
        xn = xv * r
        gv, mod = g_ref[...], 1.0 + sc_ref[...]
        st_ref[0:1, :] += _colsum(dhv)
        st_ref[1:2, :] += _colsum(dhv * xn * gv)
        st_ref[2:3, :] += _colsum(dhv * xn * mod)
        dxn = dhv * gv * mod
        dx = dr_ref[...] + r * (dxn - xn * jnp.mean(dxn * xn, axis=-1, keepdims=True))
        dx_ref[...] = dx
        if with_gate:
            _gate_grads(dx, rest[0], rest[1], rest[4], rest[5])

    blk = pl.BlockSpec((tm, D), lambda i: (i, 0))
    row = pl.BlockSpec((1, D), lambda i: (0, 0))
    gate_args = [o, gt] if with_gate else []
    return _with_tasks(_hosted(
        tasks, body, name=name, grid=(t // tm,), in_specs=[blk, blk, blk, row, row] + [blk, row] * with_gate,
        out_specs=[blk, pl.BlockSpec((3, D), lambda i: (0, 0))] + [blk, row] * with_gate,
        out_shape=[_sds((t, D), F32), _sds((3, D), F32)] + [_sds((t, D), BF16), _sds((1, D), F32)] * with_gate,
    )(dh, x, dres, g, sc, *gate_args), tasks)


def _ffn_bwd_act(do, wfo4, gu, name, tasks=()):
    t = do.shape[0]
    tm = _tile(t, 512)

    def body(do_ref, w_ref, gu_ref, dgu_ref):
        df = _dot_nt(do_ref[...], w_ref[...])
        gp, up = gu_ref[0].astype(F32), gu_ref[1].astype(F32)
        sg = _sigmoid(gp)
        dgu_ref[0] = (df * up * (sg * (1.0 + gp * (1.0 - sg)))).astype(BF16)
        dgu_ref[1] = (df * (gp * sg)).astype(BF16)

    gu_spec = pl.BlockSpec((2, None, tm, FF_BLK), lambda j, i: (0, j, i, 0))
    return _with_tasks(_hosted(
        tasks, body, name=name, grid=(4, t // tm),
        in_specs=[pl.BlockSpec((tm, D), lambda j, i: (i, 0)),
                  pl.BlockSpec((None, FF_BLK, D), lambda j, i: (j, 0, 0)), gu_spec],
        out_specs=gu_spec, out_shape=_sds((2, 4, t, FF_BLK), BF16))(do, wfo4, gu), tasks)


def _mm_nt_sum(a8, b8, name, b_is_kn=False, tasks=()):
    j, t, k = a8.shape
    n = b8.shape[2] if b_is_kn else b8.shape[1]
    tm = _tile(t, 1024 if j > 1 else 512)
    out_dtype = F32 if j > 1 else BF16
    dot = _dot if b_is_kn else _dot_nt

    def body(a_ref, b_ref, o_ref):
        if j == 1:
            o_ref[...] = dot(a_ref[...], b_ref[...]).astype(out_dtype)
            return

        @pl.when(pl.program_id(1) == 0)
        def _():
            o_ref[...] = jnp.zeros_like(o_ref)

        o_ref[...] += dot(a_ref[...], b_ref[...])

    return _with_tasks(_hosted(
        tasks, body, name=name, grid=(t // tm, j),
        in_specs=[pl.BlockSpec((None, tm, k), lambda i, jj: (jj, i, 0)),
                  pl.BlockSpec((None,) + b8.shape[1:], lambda i, jj: (jj, 0, 0))],
        out_specs=pl.BlockSpec((tm, n), lambda i, jj: (i, 0)), out_shape=_sds((t, n), out_dtype))(a8, b8), tasks)


def _mm_tn(a8, b8, name):
    ja, t, m = a8.shape
    jb, _, n = b8.shape
    j = max(ja, jb)
    tk = _tile(t, 1024)

    def body(a_ref, b_ref, o_ref):
        @pl.when(pl.program_id(1) == 0)
        def _():
            o_ref[...] = jnp.zeros_like(o_ref)

        o_ref[...] += _dot_tn(a_ref[...], b_ref[...])

    return pl.pallas_call(
        body, name=name, grid=(j, t // tk),
        in_specs=[pl.BlockSpec((None, tk, m), (lambda jj, kk: (jj, kk, 0)) if ja > 1 else (lambda jj, kk: (0, kk, 0))),
                  pl.BlockSpec((None, tk, n), (lambda jj, kk: (jj, kk, 0)) if jb > 1 else (lambda jj, kk: (0, kk, 0)))],
        out_specs=pl.BlockSpec((None, m, n), lambda jj, kk: (jj, 0, 0)), out_shape=_sds((j, m, n), F32),
        compiler_params=_params("arbitrary", "arbitrary"))(a8, b8)


def _merge_bwd(dom, w_out, z8, ya, yb, yc, name, tasks=()):
    t = dom.shape[0]
    tm = _tile(t, 512)

    def body(dom_ref, w_ref, g0_ref, g1_ref, g2_ref, ya_ref, yb_ref, yc_ref, dya_ref, dyb_ref, dyc_ref, dzg_ref,
             db_ref):
        @pl.when(pl.program_id(0) == 0)
        def _():
            db_ref[...] = jnp.zeros_like(db_ref)

        dm = _dot_nt(dom_ref[...], w_ref[...])
        for k, (g_ref, y_ref, dy_ref) in enumerate(((g0_ref, ya_ref, dya_ref), (g1_ref, yb_ref, dyb_ref),
                                                    (g2_ref, yc_ref, dyc_ref))):
            sg = _sigmoid(g_ref[...].astype(F32))
            dy_ref[...] = (dm * sg).astype(BF16)
            dzg = dm * y_ref[...].astype(F32) * (sg * (1.0 - sg))
            dzg_ref[k] = dzg.astype(BF16)
            db_ref[k:k + 1, :] += _colsum(dzg)

    blk = pl.BlockSpec((tm, D), lambda i: (i, 0))
    gate = lambda j: pl.BlockSpec((None, tm, D), lambda i: (j, i, 0))
    return _with_tasks(_hosted(
        tasks, body, name=name, grid=(t // tm,),
        in_specs=[blk, pl.BlockSpec((D, D), lambda i: (0, 0)), gate(5), gate(6), gate(7), blk, blk, blk],
        out_specs=[blk, blk, blk, pl.BlockSpec((3, tm, D), lambda i: (0, i, 0)), pl.BlockSpec((3, D), lambda i: (0, 0))],
        out_shape=[_sds((t, D), BF16)] * 3 + [_sds((3, t, D), BF16), _sds((3, D), F32)],
    )(dom, w_out, z8, z8, z8, ya, yb, yc), tasks)


def _branches_bwd(z8, cv, dsa, dsb, dsc, dzg, ln_g, ln_b, w_s, bs_b, pool_w, pool_scale, conv_w, cln_g, cln_b, name,
                  tasks=()):
    t = z8.shape[1]
    tm = _tile(t, 256)
    n_ext = tm + HALO
    n_tiles = t // tm

    def body(zu_ref, zv_ref, p_ref, a_ref, ag_ref, ph_ref, ah_ref, agh_ref, cv_ref, cvf_ref, dsa_ref, dsb_ref,
             dsbf_ref, dsc_ref, dscf_ref, dzg_ref, lng_ref, lnb_ref, ws_ref, bsb_ref, wp_ref, ps_ref, cw_ref,
             clg_ref, clb_ref, dz_ref, dbin_ref, rows_ref, dws_ref, dbs_ref, dwp_ref, dcw_ref, mixed_scr, dvln_scr,
             dcv_scr, zbank_ref, dbank_ref, dcw8_scr):
        i = pl.program_id(0)

        @pl.when(i == 0)
        def _():
            for ref in (dbin_ref, rows_ref, dws_ref, dbs_ref, dwp_ref, dcw8_scr):
                ref[...] = jnp.zeros_like(ref)

        has_past = (i > 0).astype(F32)
        has_next = (i < n_tiles - 1).astype(F32)

        def emit(j, val):
            dz_ref[j] = val.astype(BF16)
            dbin_ref[j:j + 1, :] += _colsum(val)

        zu, zv = zu_ref[...].astype(F32), zv_ref[...].astype(F32)
        u = _gelu(zu)
        vhat, v_rstd = _ln_stats(_gelu(zv))
        vb = (vhat * lng_ref[...] + lnb_ref[...]).astype(BF16)
        dsa = dsa_ref[...].astype(F32)
        dmixed = dsa * u
        dmb = dmixed.astype(BF16)
        mask = _tril_mask()
        lane = lax.broadcasted_iota(jnp.int32, (CHUNK, CHUNK), 1)
        for g in range(SGU_G):
            cols = slice(g * CHUNK, (g + 1) * CHUNK)
            wm = (ws_ref[g] * mask).astype(BF16)
            dws = jnp.zeros((CHUNK, CHUNK), F32)
            dbs = jnp.zeros((CHUNK, 1), F32)
            for n in range(tm // CHUNK):
                rows = slice(n * CHUNK, (n + 1) * CHUNK)
                mixed_scr[rows, cols] = _dot(wm, vb[rows, cols]) + bsb_ref[g]
                dvln_scr[rows, cols] = _dot_tn(wm, dmb[rows, cols])
                dws = dws + _dot_nt(dmb[rows, cols], vb[rows, cols])
                dbs = dbs + jnp.sum(dmixed[rows, cols], axis=1, keepdims=True)
            dws_ref[g] += dws
            dbs_ref[...] += jnp.where(lane == g, dbs, 0.0)
        emit(0, dsa * mixed_scr[...] * _gelu_grad(zu))
        dvln = dvln_scr[...]
        rows_ref[0:1, :] += _colsum(dvln * vhat)
        rows_ref[1:2, :] += _colsum(dvln)
        emit(1, _ln_bwd(dvln * lng_ref[...], vhat, v_rstd) * _gelu_grad(zv))

        p_ext = jnp.concatenate([ph_ref[...].astype(F32) * has_past, p_ref[...].astype(F32)], axis=0)
        pooled = _pool_forward(p_ext, i * tm, tm)
        dsb = dsb_ref[...].astype(F32)
        dpl_ext = jnp.concatenate([dsb, dsbf_ref[...].astype(F32) * has_next], axis=0) * ps_ref[...]
        t_ext = i * tm + lax.broadcasted_iota(jnp.int32, (n_ext, 1), 0)
        dp_parts = []
        for gi, win in enumerate(POOL_WINDOWS):
            cols = slice(gi * POOL_GC, (gi + 1) * POOL_GC)
            pooled_b = pooled[gi].astype(BF16)
            wpb = wp_ref[gi].astype(BF16)
            rows_ref[2:3, cols] += _colsum(dsb[:, cols] * _dot(pooled_b, wpb))
            dplb = dpl_ext[:, cols].astype(BF16)
            dwp_ref[gi] += _dot_tn(pooled_b, dplb[:tm])
            dpooled = _dot_nt(dplb, wpb)
            s, sh = dpooled / jnp.minimum(t_ext + 1, win).astype(F32), 1
            while sh < win:
                s = s + pltpu.roll(s, n_ext - sh, 0)
                sh *= 2
            dp_parts.append(s[:tm] - dpooled[:tm])
        emit(2, jnp.concatenate(dp_parts, axis=1))

        cv_ext = jnp.concatenate([cv_ref[...], cvf_ref[...]], axis=0)
        chat, c_rstd = _ln_stats(cv_ext)
        cl = chat * clg_ref[...] + clb_ref[...]
        sg = _sigmoid(cl)
        dsc_ext = jnp.concatenate([dsc_ref[...].astype(F32), dscf_ref[...].astype(F32)], axis=0)
        dcl = dsc_ext * (sg * (1.0 + cl * (1.0 - sg)))
        rows_ref[4:5, :] += _colsum((dcl * chat)[:tm])
        rows_ref[5:6, :] += _colsum(dcl[:tm])
        in_seq = jnp.concatenate([jnp.ones((tm, 1), F32), jnp.zeros((HALO, 1), F32) + has_next], axis=0)
        dcv = jnp.where(in_seq > 0.0, _ln_bwd(dcl * clg_ref[...], chat, c_rstd), 0.0)
        rows_ref[3:4, :] += _colsum(dcv[:tm])
        dcv_scr[...] = dcv
        for cb in range(D // LANE):
            cols = slice(cb * LANE, (cb + 1) * LANE)
            zc = jnp.concatenate(
                [ah_ref[:, cols].astype(F32) * has_past * _sigmoid(agh_ref[:, cols].astype(F32)),
                 a_ref[:, cols].astype(F32) * _sigmoid(ag_ref[:, cols].astype(F32))], axis=0)
            _fill_shift_bank(zbank_ref, zc, causal=True)
            _fill_shift_bank(dbank_ref, dcv_scr[:, cols], causal=False)
            for r0 in range(0, tm, CONV_STRIP):
                rows = slice(r0, r0 + CONV_STRIP)
                dcv_s = dcv_scr[rows, cols]
                dzc = jnp.zeros((CONV_STRIP, LANE), F32)
                for k in range(CONV_K):
                    hi, lo = divmod(CONV_K - 1 - k, SUBLANE)
                    z_win = zbank_ref[lo, pl.ds(HALO - SUBLANE * hi + r0, CONV_STRIP), :]
                    dcw8_scr[k, :, cols] += jnp.sum((dcv_s * z_win).reshape(CONV_STRIP // SUBLANE, SUBLANE, LANE), axis=0)
                    dzc = dzc + cw_ref[k:k + 1, cols] * dbank_ref[lo, pl.ds(SUBLANE * hi + r0, CONV_STRIP), :]
                a_s = a_ref[rows, cols].astype(F32)
                sga = _sigmoid(ag_ref[rows, cols].astype(F32))
                dza = dzc * sga
                dzag = dzc * a_s * (sga * (1.0 - sga))
                dz_ref[3, rows, cols] = dza.astype(BF16)
                dz_ref[4, rows, cols] = dzag.astype(BF16)
                dbin_ref[3:4, cols] += _colsum(dza)
                dbin_ref[4:5, cols] += _colsum(dzag)
        for k in range(3):
            dz_ref[5 + k] = dzg_ref[k]

        @pl.when(i == n_tiles - 1)
        def _():
            dcw_ref[...] = jnp.sum(dcw8_scr[...], axis=1)

    def col(j):
        return pl.BlockSpec((None, tm, D), lambda i: (j, i, 0))

    blk = pl.BlockSpec((tm, D), lambda i: (i, 0))
    after = pl.BlockSpec((HALO, D), lambda i: (jnp.minimum((i + 1) * (tm // HALO), t // HALO - 1), 0))
    row = pl.BlockSpec((1, D), lambda i: (0, 0))
    full2 = lambda s: pl.BlockSpec(s, lambda i: (0, 0))
    full3 = lambda s: pl.BlockSpec(s, lambda i: (0, 0, 0))
    return _with_tasks(_hosted(
        tasks, body, name=name, grid=(n_tiles,),
        in_specs=[col(0), col(1), col(2), col(3), col(4), _halo_before(tm, 2), _halo_before(tm, 3),
                  _halo_before(tm, 4), blk, after, blk, blk, after, blk, after,
                  pl.BlockSpec((3, tm, D), lambda i: (0, i, 0)), row, row, full3((SGU_G, CHUNK, CHUNK)),
                  full3((SGU_G, CHUNK, CHUNK)), full3((4, POOL_GC, POOL_GC)), row, full2((HALO, D)), row, row],
        out_specs=[pl.BlockSpec((8, tm, D), lambda i: (0, i, 0)), full2((8, D)), full2((8, D)),
                   full3((SGU_G, CHUNK, CHUNK)), full2((CHUNK, CHUNK)), full3((4, POOL_GC, POOL_GC)),
                   full2((HALO, D))],
        out_shape=[_sds((8, t, D), BF16), _sds((8, D), F32), _sds((8, D), F32), _sds((SGU_G, CHUNK, CHUNK), F32),
                   _sds((CHUNK, CHUNK), F32), _sds((4, POOL_GC, POOL_GC), F32), _sds((HALO, D), F32)],
        scratch_shapes=[pltpu.VMEM((tm, D), F32), pltpu.VMEM((tm, D), F32), pltpu.VMEM((n_ext, D), F32),
                        pltpu.VMEM((SUBLANE, n_ext, LANE), F32), pltpu.VMEM((SUBLANE, n_ext, LANE), F32),
                        pltpu.VMEM((HALO, SUBLANE, D), F32)],
    )(z8, z8, z8, z8, z8, z8, z8, z8, cv, cv, dsa, dsb, dsb, dsc, dsc, dzg, ln_g, ln_b, w_s, bs_b, pool_w,
      pool_scale, conv_w, cln_g, cln_b), tasks)


def _ada_fwd(c_all, w_ada, b_loc, name):
    def body(c_ref, w_ref, b_ref, o_ref):
        cv = c_ref[...]
        ca = (cv * _sigmoid(cv)).astype(BF16)
        for l in range(DEPTH):
            o_ref[l] = _dot(ca, w_ref[l].astype(BF16)) + b_ref[l]

    return pl.pallas_call(body, name=name, out_shape=_sds((DEPTH, N_DEV, ADA_BLK), F32),
                          compiler_params=_params())(c_all, w_ada, b_loc)


def _ada_bwd(c_all_t, d_loc, name):
    def body(c_ref, d_ref, o_ref):
        cv = c_ref[...]
        ca = cv * _sigmoid(cv)
        for l in range(DEPTH):
            acc = jnp.zeros((D, ADA_BLK), F32)
            for j in range(N_DEV):
                acc = acc + ca[:, j:j + 1] * d_ref[l, j:j + 1, :]
            o_ref[l] = acc

    return pl.pallas_call(body, name=name, out_shape=_sds((DEPTH, D, ADA_BLK), F32),
                          compiler_params=_params())(c_all_t, d_loc)


def _sum8(g8, name):
    _, rows, cols = g8.shape
    tr = _tile(rows, 256)

    def body(g_ref, o_ref):
        acc = g_ref[0]
        for k in range(1, N_DEV):
            acc = acc + g_ref[k]
        o_ref[...] = acc

    return pl.pallas_call(body, name=name, grid=(rows // tr,),
                          in_specs=[pl.BlockSpec((N_DEV, tr, cols), lambda r: (0, r, 0))],
                          out_specs=pl.BlockSpec((tr, cols), lambda r: (r, 0)), out_shape=_sds((rows, cols), F32),
                          compiler_params=_params("arbitrary"))(g8)


PROJ = ("w_pa", "w_pb", "w_pc", "w_out")
FWD_GATHERS = {
    (0, "in_proj"): tuple((n, 0) for n in PROJ) + (("w_ffn_out", 0),),
    (0, "branches"): (("w_ffn_in", 0),),
    (0, "ffn_in"): (("w_in", 1),),
    (0, "ffn_out"): tuple((n, 1) for n in PROJ),
    (1, "in_proj"): (("w_ffn_in", 1),),
    (1, "branches"): (("w_ffn_out", 1),),
}
GRAD_GROUP = {"w_in": ("in", 0), "pool_w": ("in", 1), "w_pa": ("proj", 0), "w_pb": ("proj", 1), "w_pc": ("proj", 2),
              "w_out": ("proj", 3), "w_ffn_in": ("ffn", 0), "w_ffn_out": ("ffn", 1)}


EARLY_ROWS = dict(b_in=(0, 8), sgu_ln_g=(8, 9), sgu_ln_b=(9, 10), sgu_w_s=(10, 138), sgu_b_s=(138, 139),
                  pool_scale=(139, 140), conv_b=(140, 141), conv_ln_g=(141, 142), conv_ln_b=(142, 143), g_ffn=(143, 144))
CONV_ROW = 144
LATE_ROWS = dict(b_ada=(0, 6), g_mix=(6, 7))


def _run(factory, *args, tasks=(), **kw):
    out = factory(*args, tasks=tasks, **kw)
    return out if tasks else (out, [])


REPLICATED = ("b_ada", "g_mix", "b_in", "sgu_ln_g", "sgu_ln_b", "sgu_w_s", "sgu_b_s", "pool_scale", "conv_b",
              "conv_ln_g", "conv_ln_b", "g_ffn", "g_final")
WEIGHT_ORDER = ("w_ada", "b_ada", "g_mix", "w_in", "b_in", "sgu_ln_g", "sgu_ln_b", "sgu_w_s", "sgu_b_s", "w_pa",
                "pool_w", "pool_scale", "w_pb", "conv_w", "conv_b", "conv_ln_g", "conv_ln_b", "w_pc", "w_out",
                "g_ffn", "w_ffn_in", "w_ffn_out", "g_final")


def _rows(a):
    return a.reshape(-1, D)


def kernel(x, c, w_ada, b_ada, g_mix, w_in, b_in, sgu_ln_g, sgu_ln_b, sgu_w_s, sgu_b_s, w_pa, pool_w, pool_scale, w_pb, conv_w, conv_b, conv_ln_g, conv_ln_b, w_pc, w_out, g_ffn, w_ffn_in, w_ffn_out, g_final, loss_target, m_w_ada, m_b_ada, m_g_mix, m_w_in, m_b_in, m_sgu_ln_g, m_sgu_ln_b, m_sgu_w_s, m_sgu_b_s, m_w_pa, m_pool_w, m_pool_scale, m_w_pb, m_conv_w, m_conv_b, m_conv_ln_g, m_conv_ln_b, m_w_pc, m_w_out, m_g_ffn, m_w_ffn_in, m_w_ffn_out, m_g_final, v_w_ada, v_b_ada, v_g_mix, v_w_in, v_b_in, v_sgu_ln_g, v_sgu_ln_b, v_sgu_w_s, v_sgu_b_s, v_w_pa, v_pool_w, v_pool_scale, v_w_pb, v_conv_w, v_conv_b, v_conv_ln_g, v_conv_ln_b, v_w_pc, v_w_out, v_g_ffn, v_w_ffn_in, v_w_ffn_out, v_g_final):
    weights = dict(w_ada=w_ada, b_ada=b_ada, g_mix=g_mix, w_in=w_in, b_in=b_in, sgu_ln_g=sgu_ln_g, sgu_ln_b=sgu_ln_b,
                   sgu_w_s=sgu_w_s, sgu_b_s=sgu_b_s, w_pa=w_pa, pool_w=pool_w, pool_scale=pool_scale, w_pb=w_pb,
                   conv_w=conv_w, conv_b=conv_b, conv_ln_g=conv_ln_g, conv_ln_b=conv_ln_b, w_pc=w_pc, w_out=w_out,
                   g_ffn=g_ffn, w_ffn_in=w_ffn_in, w_ffn_out=w_ffn_out, g_final=g_final)
    mom1 = dict(w_ada=m_w_ada, b_ada=m_b_ada, g_mix=m_g_mix, w_in=m_w_in, b_in=m_b_in, sgu_ln_g=m_sgu_ln_g,
                sgu_ln_b=m_sgu_ln_b, sgu_w_s=m_sgu_w_s, sgu_b_s=m_sgu_b_s, w_pa=m_w_pa, pool_w=m_pool_w,
                pool_scale=m_pool_scale, w_pb=m_w_pb, conv_w=m_conv_w, conv_b=m_conv_b, conv_ln_g=m_conv_ln_g,
                conv_ln_b=m_conv_ln_b, w_pc=m_w_pc, w_out=m_w_out, g_ffn=m_g_ffn, w_ffn_in=m_w_ffn_in,
                w_ffn_out=m_w_ffn_out, g_final=m_g_final)
    mom2 = dict(w_ada=v_w_ada, b_ada=v_b_ada, g_mix=v_g_mix, w_in=v_w_in, b_in=v_b_in, sgu_ln_g=v_sgu_ln_g,
                sgu_ln_b=v_sgu_ln_b, sgu_w_s=v_sgu_w_s, sgu_b_s=v_sgu_b_s, w_pa=v_w_pa, pool_w=v_pool_w,
                pool_scale=v_pool_scale, w_pb=v_w_pb, conv_w=v_conv_w, conv_b=v_conv_b, conv_ln_g=v_conv_ln_g,
                conv_ln_b=v_conv_ln_b, w_pc=v_w_pc, w_out=v_w_out, g_ffn=v_g_ffn, w_ffn_in=v_w_ffn_in,
                w_ffn_out=v_w_ffn_out, g_final=v_g_final)

    for group in (weights, mom1, mom2):
        group["w_ffn_in"] = jnp.transpose(group["w_ffn_in"], (0, 2, 1))

    t = x.shape[1]
    xs = x.reshape(t, D)
    target = loss_target.reshape(t, D)
    me = 4 * lax.axis_index("x") + 2 * lax.axis_index("y") + lax.axis_index("c")
    core = lax.axis_index("c").astype(jnp.int32).reshape(1)

    bf = lambda n, l: weights[n][l].astype(BF16)
    (first,) = _transfer([_gather_task([bf("w_in", 0), c, pool_w, conv_w])], name="gather_first")
    w_in0, c_all, pool_all, conv_all = first
    full = [dict(w_in=w_in0)] + [dict() for _ in range(1, DEPTH)]

    def gather_at(l, stage):
        names = FWD_GATHERS.get((l, stage), ())
        return [_gather_task([bf(n, ll) for n, ll in names])] if names else []

    def landed(l, stage, per):
        for (n, ll), arr in zip(FWD_GATHERS.get((l, stage), ()), per[0] if per else ()):
            full[ll][n] = arr

    c_all = c_all.reshape(N_DEV, D)
    pool_full = jnp.transpose(pool_all, (1, 2, 0, 3, 4)).reshape(DEPTH, 4, POOL_GC, POOL_GC)
    conv_full = jnp.transpose(conv_all, (1, 2, 0, 3)).reshape(DEPTH, CONV_K, D)
    conv_full = jnp.pad(conv_full, ((0, 0), (0, HALO - CONV_K), (0, 0)))

    b_loc = lax.dynamic_slice_in_dim(b_ada, me * ADA_BLK, ADA_BLK, axis=1).reshape(DEPTH, 1, ADA_BLK)
    ada_part = _ada_fwd(c_all, w_ada, b_loc, name="ada_fwd")
    ((ada_all,),) = _transfer([_gather_task([ada_part])], name="gather_ada")
    ada = lax.dynamic_index_in_dim(ada_all, me, axis=2, keepdims=False)
    ada = jnp.transpose(ada, (1, 0, 2)).reshape(DEPTH, 6, 1, D)

    bs_b = jnp.broadcast_to(sgu_b_s[..., None], (DEPTH, SGU_G, CHUNK, CHUNK))

    saved = []
    xl = xs
    h = _norm_mod(xl, g_mix[0].reshape(1, D), ada[0, 1], ada[0, 0], name="norm_mix_0")
    for l in range(DEPTH):
        w = full[l]
        sh_m, sc_m, gt_m, sh_f, sc_f, gt_f = (ada[l, k] for k in range(6))
        row = lambda a: a[l].reshape(1, D)
        z8, per = _run(_mm_cols, h, w["w_in"], b_in[l].reshape(8, 1, D), name=f"in_proj_{l}",
                       tasks=gather_at(l, "in_proj"))
        landed(l, "in_proj", per)
        (sa, sb, sc, cv), per = _run(
            _branches_fwd, z8, row(sgu_ln_g), row(sgu_ln_b), sgu_w_s[l], bs_b[l], pool_full[l], row(pool_scale),
            conv_full[l], row(conv_b), row(conv_ln_g), row(conv_ln_b), name=f"branches_{l}",
            tasks=gather_at(l, "branches"))
        landed(l, "branches", per)
        wpa, wpb, wpc, wout = (w[n].reshape(D, D) for n in PROJ)
        (ya, yb, yc, merged), per = _run(_proj_merge, sa, sb, sc, wpa, wpb, wpc, z8, name=f"proj_merge_{l}",
                                         tasks=gather_at(l, "proj_merge"))
        landed(l, "proj_merge", per)
        om, x1, h2 = _out_proj(merged, wout, xl, gt_m, row(g_ffn), sc_f, sh_f, name=f"out_proj_{l}")
        wfi = w["w_ffn_in"].reshape(2, 4, FF_BLK, D)
        (gu, f4), per = _run(_ffn_in, h2, wfi, name=f"ffn_in_{l}", tasks=gather_at(l, "ffn_in"))
        landed(l, "ffn_in", per)
        wfo4 = w["w_ffn_out"].reshape(4, FF_BLK, D)
        nxt = (g_mix[l + 1].reshape(1, D), ada[l + 1, 1], ada[l + 1, 0]) if l + 1 < DEPTH else (None, None, None)
        res, per = _run(_ffn_out, f4, wfo4, x1, gt_f, *nxt, name=f"ffn_out_{l}", tasks=gather_at(l, "ffn_out"))
        landed(l, "ffn_out", per)
        o, x2 = res[0], res[1]
        saved.append(dict(x=xl, h=h, z8=z8, sa=sa, sb=sb, sc=sc, cv=cv, ya=ya, yb=yb, yc=yc, merged=merged, om=om,
                          x1=x1, h2=h2, gu=gu, f4=f4, o=o, wpa=wpa, wpb=wpb, wpc=wpc, wout=wout, wfi=wfi, wfo4=wfo4))
        xl = x2
        h = res[2] if l + 1 < DEPTH else None

    loss_tile, dx, dg_final, do, dgt_f = _final_loss(xl, g_final.reshape(1, D), target, saved[-1]["o"],
                                                     ada[DEPTH - 1, 5], name="final_loss")
    loss_row = jnp.broadcast_to(loss_tile[0:1, 0:1], (1, D))

    chip_parts = [dict() for _ in range(DEPTH)]
    early_buf, late_buf = [None] * DEPTH, [None] * DEPTH
    early_all, late_all = [None] * DEPTH, [None] * DEPTH
    tril = jnp.tril(jnp.ones((CHUNK, CHUNK), F32))
    for l in reversed(range(DEPTH)):
        s, w = saved[l], full[l]
        above = l + 1 if l + 1 < DEPTH else None
        sh_m, sc_m, gt_m, sh_f, sc_f, gt_f = (ada[l, k] for k in range(6))
        row = lambda a: a[l].reshape(1, D)
        dgu, per = _run(_ffn_bwd_act, do, s["wfo4"], s["gu"], name=f"ffn_bwd_act_{l}",
                        tasks=[] if above is None else [_gather_task([early_buf[above], late_buf[above]])])
        if above is not None:
            early_all[above], late_all[above] = per[0]
        d_wfo = _mm_tn(s["f4"], do[None], name=f"dw_ffn_out_{l}")
        dgu8 = dgu.reshape(8, t, FF_BLK)
        dh2 = _mm_nt_sum(dgu8, w["w_ffn_in"], name=f"dh_ffn_{l}", b_is_kn=True)
        d_wfi = _mm_tn(dgu8, s["h2"][None], name=f"dw_ffn_in_{l}")
        dx1, st_f, dom, dgt_m = _norm_mod_bwd(dh2, s["x1"], dx, row(g_ffn), sc_f, s["om"], gt_m,
                                              name=f"norm_ffn_bwd_{l}")
        ffn_group = [d_wfi, d_wfo.reshape(8, D_FF // 8, D)]
        (dya, dyb, dyc, dzg, db_gate), per = _run(_merge_bwd, dom, s["wout"], s["z8"], s["ya"], s["yb"], s["yc"],
                                                  name=f"merge_bwd_{l}", tasks=[_sibling_task(ffn_group)])
        ffn_sums = _sibling_sums(ffn_group, per[0], core, tag=f"ffn_{l}")
        d_wout = _mm_tn(s["merged"][None], dom[None], name=f"dw_out_{l}")
        d_wpa = _mm_tn(s["sa"][None], dya[None], name=f"dw_pa_{l}")
        d_wpb = _mm_tn(s["sb"][None], dyb[None], name=f"dw_pb_{l}")
        d_wpc = _mm_tn(s["sc"][None], dyc[None], name=f"dw_pc_{l}")
        dsa = _mm_nt_sum(dya[None], s["wpa"][None], name=f"ds_a_{l}")
        dsb = _mm_nt_sum(dyb[None], s["wpb"][None], name=f"ds_b_{l}")
        dsc = _mm_nt_sum(dyc[None], s["wpc"][None], name=f"ds_c_{l}")
        proj_group = [g.reshape(8, D // 8, D) for g in (d_wpa, d_wpb, d_wpc, d_wout)]
        (dz8, db_in5, rows6, dws, dbs, dwp, dcw), per = _run(
            _branches_bwd, s["z8"], s["cv"], dsa, dsb, dsc, dzg, row(sgu_ln_g), row(sgu_ln_b), sgu_w_s[l], bs_b[l],
            pool_full[l], row(pool_scale), conv_full[l], row(conv_ln_g), row(conv_ln_b), name=f"branches_bwd_{l}",
            tasks=[_chips_task(ffn_sums), _sibling_task(proj_group)]
            + ([] if above is None else [_chips_task(in_sums)]))
        chip_parts[l]["ffn"] = per[0]
        proj_sums = _sibling_sums(proj_group, per[1], core, tag=f"proj_{l}")
        if above is not None:
            chip_parts[above]["in"] = per[2]
        d_win = _mm_tn(s["h"][None], dz8, name=f"dw_in_{l}")
        d_pool = jnp.transpose(dwp.reshape(4, N_DEV, POOL_GC // N_DEV, POOL_GC), (1, 0, 2, 3))
        in_group = [d_win, d_pool.reshape(N_DEV, 4 * POOL_GC // N_DEV, POOL_GC)]
        early_buf[l] = jnp.concatenate([db_in5[0:5], db_gate, rows6[0:2], _rows(dws * tril),
                                        _rows(jnp.transpose(dbs[:, :SGU_G])), rows6[2:6], st_f[2:3], dcw], axis=0)
        dh, per = _run(_mm_nt_sum, dz8, w["w_in"], name=f"dh_in_{l}",
                       tasks=[_chips_task(proj_sums), _sibling_task(in_group)]
                       + ([_gather_task([early_buf[0]])] if l == 0 else []))
        chip_parts[l]["proj"] = per[0]
        in_sums = _sibling_sums(in_group, per[1], core, tag=f"in_{l}")
        if l == 0:
            (early_all[0],) = per[2]
        gate = (saved[l - 1]["o"], ada[l - 1, 5]) if l > 0 else (None, None)
        res, per = _run(_norm_mod_bwd, dh, s["x"], dx1, row(g_mix), sc_m, *gate, name=f"norm_mix_bwd_{l}",
                        tasks=[_chips_task(in_sums)] if l == 0 else [])
        dx, st_m = res[0], res[1]
        if l == 0:
            chip_parts[0]["in"] = per[0]
        late_buf[l] = jnp.concatenate([st_m[0:1], st_m[1:2], dgt_m, st_f[0:1], st_f[1:2], dgt_f, st_m[2:3],
                                       dg_final if l == 0 else loss_row], axis=0)
        if l > 0:
            do, dgt_f = res[2], res[3]
    late_all[0] = _transfer([_gather_task([late_buf[0]])], name="gather_last")[0][0]

    early = [_sum8(early_all[l], name=f"sum_early_grads_{l}") for l in range(DEPTH)]
    late = [_sum8(late_all[l], name=f"sum_late_grads_{l}") for l in range(DEPTH)]
    layers = lambda red, lo, hi: jnp.stack([red[l][lo:hi] for l in range(DEPTH)], axis=0)
    grads = {n: layers(early, lo, hi).reshape(weights[n].shape) for n, (lo, hi) in EARLY_ROWS.items()}
    grads.update({n: layers(late, lo, hi).reshape(weights[n].shape) for n, (lo, hi) in LATE_ROWS.items()})
    grads["g_final"] = late[0][7]
    loss = late[DEPTH - 1][7, 0]
    conv_g = layers(early, CONV_ROW, CONV_ROW + CONV_K)
    grads["conv_w"] = lax.dynamic_slice_in_dim(conv_g, me * (D // N_DEV), D // N_DEV, axis=2)
    d_ada_all = jnp.stack([late_all[l][:, 0:6] for l in range(DEPTH)], axis=1).reshape(N_DEV, DEPTH, 6 * D)
    d_loc = jnp.transpose(lax.dynamic_slice_in_dim(d_ada_all, me * ADA_BLK, ADA_BLK, axis=2), (1, 0, 2))
    grads["w_ada"] = _ada_bwd(jnp.transpose(c_all), d_loc, name="ada_bwd")

    out = {}
    for n in REPLICATED + ("conv_w", "w_ada"):
        out[n] = _adam_nd(grads[n], weights[n], mom1[n], mom2[n], name=f"adam_{n}")
    for n, (group, k) in GRAD_GROUP.items():
        parts = [chip_parts[l][group][k] for l in range(DEPTH)]
        shape = (DEPTH,) + parts[0].shape[1:]
        res = _adam(parts, weights[n].reshape(shape), mom1[n].reshape(shape), mom2[n].reshape(shape),
                    name=f"adam_{n}")
        out[n] = [r.reshape(weights[n].shape) for r in res]
    out["w_ffn_in"] = [jnp.transpose(r, (0, 2, 1)) for r in out["w_ffn_in"]]

    grad_x = dx.reshape(1, t, D)
    return (loss, grad_x, *[out[n][0] for n in WEIGHT_ORDER], *[out[n][1] for n in WEIGHT_ORDER],
            *[out[n][2] for n in WEIGHT_ORDER], *[out[n][3] for n in WEIGHT_ORDER])
```

```python
import math

import jax
import jax.numpy as jnp
from jax import lax
from jax.experimental import pallas as pl
from jax.experimental.pallas import tpu as pltpu

F32 = jnp.float32
BF16 = jnp.bfloat16
MESH = pl.DeviceIdType.MESH
AXES = ("x", "y", "c")
N_DEV = 8

D = 1024
DEPTH = 2
EPS = 1e-6
CHUNK = 128
SGU_G = 8
POOL_WINDOWS = (2, 4, 8, 16)
POOL_GC = 256
CONV_K = 31
HALO = 32
SUBLANE = 8
LANE = 128
CONV_STRIP = 128
D_FF = 2816
FF_BLK = D_FF // 4
ADA_BLK = 6 * D // N_DEV

ADAM_LR = 0.001
ADAM_B1 = 0.9
ADAM_B2 = 0.999
ADAM_EPS = 1e-08
ADAM_WD = 0.01
ADAM_STEP = 10

VMEM_LIMIT_V7X = 56 * 1024 * 1024
INV_SQRT2 = 1.0 / math.sqrt(2.0)
INV_SQRT_2PI = 1.0 / math.sqrt(2.0 * math.pi)


def _params(*sem):
    return pltpu.CompilerParams(dimension_semantics=sem if sem else None, vmem_limit_bytes=VMEM_LIMIT_V7X)


def _tile(n, pref):
    if n <= pref:
        return n
    for t in range(pref - pref % 8, 0, -8):
        if n % t == 0:
            return t
    raise ValueError((n, pref))


def _sds(shape, dtype):
    return jax.ShapeDtypeStruct(shape, dtype)


def _sigmoid(x):
    return 1.0 / (1.0 + jnp.exp(-x))


def _gelu(x):
    return 0.5 * x * (1.0 + lax.erf(x * INV_SQRT2))


def _gelu_grad(x):
    return 0.5 * (1.0 + lax.erf(x * INV_SQRT2)) + x * (INV_SQRT_2PI * jnp.exp(-0.5 * x * x))


def _ln_stats(v):
    mu = jnp.mean(v, axis=-1, keepdims=True)
    vc = v - mu
    rstd = lax.rsqrt(jnp.mean(vc * vc, axis=-1, keepdims=True) + EPS)
    return vc * rstd, rstd


def _ln_bwd(dvhat, vhat, rstd):
    return rstd * (dvhat - jnp.mean(dvhat, axis=-1, keepdims=True)
                   - vhat * jnp.mean(dvhat * vhat, axis=-1, keepdims=True))


def _colsum(v):
    return jnp.sum(v, axis=0, keepdims=True)


def _dot(a, b):
    return jnp.dot(a, b, preferred_element_type=F32)


def _dot_nt(a, b):
    return lax.dot_general(a, b, (((1,), (1,)), ((), ())), preferred_element_type=F32)


def _dot_tn(a, b):
    return lax.dot_general(a, b, (((0,), (0,)), ((), ())), preferred_element_type=F32)


def _tril_mask():
    r = lax.broadcasted_iota(jnp.int32, (CHUNK, CHUNK), 0)
    c = lax.broadcasted_iota(jnp.int32, (CHUNK, CHUNK), 1)
    return (r >= c).astype(F32)


def _mesh_pos():
    return tuple(lax.axis_index(a) for a in AXES)


class _Task:
    def __init__(self, arrays, out_shapes, scratch, start, finish, relay=None, relay_at=1.0):
        self.arrays, self.out_shapes, self.scratch = arrays, out_shapes, scratch
        self.start, self.finish, self.relay = start, finish, relay or (lambda ins, outs, sems: None)
        self.relay_at = relay_at


def _hosted(tasks, body, *, name, grid, in_specs, out_specs, out_shape, scratch_shapes=()):
    single = not isinstance(out_shape, (list, tuple))
    out_shape, out_specs = ([out_shape], [out_specs]) if single else (list(out_shape), list(out_specs))
    n_in, n_out, n_scr = len(in_specs), len(out_shape), len(scratch_shapes)
    sizes = [(len(t.arrays), len(t.out_shapes), len(t.scratch)) for t in tasks]
    t_in, t_out, t_scr = (sum(s[k] for s in sizes) for k in range(3))
    any_spec = pl.BlockSpec(memory_space=pl.ANY)

    def wrapped(*refs):
        refs = list(refs)
        ins, refs = refs[:n_in + t_in], refs[n_in + t_in:]
        outs, scr = refs[:n_out + t_out], refs[n_out + t_out:]

        def per_task(fn_name, only=None):
            i0, o0, s0 = n_in, n_out, n_scr
            for t, (ni, no, ns) in zip(tasks, sizes):
                if only is None or t is only:
                    getattr(t, fn_name)(ins[i0:i0 + ni], outs[o0:o0 + no], scr[s0:s0 + ns])
                i0, o0, s0 = i0 + ni, o0 + no, s0 + ns

        if tasks and grid:
            first, last, step, total = None, None, 0, 1
            for d, g in enumerate(grid):
                f, e = pl.program_id(d) == 0, pl.program_id(d) == g - 1
                first, last = (f, e) if first is None else (first & f, last & e)
                step, total = step * g + pl.program_id(d), total * g
            pl.when(first)(lambda: per_task("start"))
            for t in tasks:
                pl.when(step == min(int(t.relay_at * total), total - 1))(lambda t=t: per_task("relay", only=t))
        elif tasks:
            per_task("start")
            per_task("relay")
        body(*ins[:n_in], *outs[:n_out], *scr[:n_scr])
        if tasks and grid:
            pl.when(last)(lambda: per_task("finish"))
        elif tasks:
            per_task("finish")

    call = pl.pallas_call(
        wrapped, name=name, grid=grid,
        in_specs=list(in_specs) + [any_spec] * t_in, out_specs=out_specs + [any_spec] * t_out,
        out_shape=out_shape + [s for t in tasks for s in t.out_shapes],
        scratch_shapes=list(scratch_shapes) + [s for t in tasks for s in t.scratch],
        compiler_params=_params(*(("arbitrary",) * len(grid))))

    def run(*operands):
        res = list(call(*operands, *[a for t in tasks for a in t.arrays]))
        host, rest, per = res[:n_out], res[n_out:], []
        for _, no, _ in sizes:
            per.append(rest[:no])
            rest = rest[no:]
        return (host[0] if single else host), per

    return run


def _transfer(tasks, name):
    return _hosted(tasks, lambda: None, name=name, grid=(), in_specs=[], out_specs=[], out_shape=[])()[1]


def _gather_task(arrs, relay_at=0.75):
    n = len(arrs)

    def plan(ins, outs, sems):
        send_sems, recv_sems, local_sems = sems
        x, y, c = _mesh_pos()
        me, sibling = (x, y, c), (x, y, 1 - c)
        chips = [(1 - x, y), (x, 1 - y), (1 - x, 1 - y)]

        def slot(a, p):
            return outs[a].at[4 * p[0] + 2 * p[1] + p[2]]

        def copy(a, k, block, to, src=None):
            dst = slot(a, block)
            return pltpu.make_async_remote_copy(
                src_ref=dst if src is None else src, dst_ref=dst, send_sem=send_sems.at[a, k],
                recv_sem=recv_sems.at[a, k], device_id=to, device_id_type=MESH)

        def own_block_copies():
            mine = [pltpu.make_async_copy(ins[a], slot(a, me), local_sems.at[a]) for a in range(n)]
            first = []
            for a in range(n):
                first.append(copy(a, 0, me, sibling, src=ins[a]))
                first += [copy(a, 1 + j, me, (*chip, c), src=ins[a]) for j, chip in enumerate(chips)]
            return mine, first

        return c, me, sibling, chips, copy, own_block_copies

    def start(ins, outs, sems):
        mine, first = plan(ins, outs, sems)[-1]()
        for cp in mine + first:
            cp.start()

    def relay(ins, outs, sems):
        c, me, sibling, chips, copy, _ = plan(ins, outs, sems)
        for j, chip in enumerate(chips):
            for a in range(n):
                copy(a, 1 + j, (*chip, c), me).wait_recv()
                copy(a, 4 + j, (*chip, c), sibling).start()

    def finish(ins, outs, sems):
        c, me, sibling, chips, copy, own_block_copies = plan(ins, outs, sems)
        mine, first = own_block_copies()
        passed = [copy(a, 4 + j, (*chip, c), sibling) for j, chip in enumerate(chips) for a in range(n)]
        for a in range(n):
            copy(a, 0, sibling, me).wait_recv()
            for j, chip in enumerate(chips):
                copy(a, 4 + j, (*chip, 1 - c), me).wait_recv()
        for cp in first + passed:
            cp.wait_send()
        for m in mine:
            m.wait()

    return _Task(list(arrs), [_sds((N_DEV,) + a.shape, a.dtype) for a in arrs],
                 [pltpu.SemaphoreType.DMA((n, 7)), pltpu.SemaphoreType.DMA((n, 7)), pltpu.SemaphoreType.DMA((n,))],
                 start, finish, relay, relay_at)


def _sibling_task(arrs):
    n = len(arrs)

    def copies(ins, outs, sems):
        send_sems, recv_sems = sems
        x, y, c = _mesh_pos()
        return [pltpu.make_async_remote_copy(
            src_ref=ins[a].at[2 * q + (1 - c)], dst_ref=outs[a].at[q], send_sem=send_sems.at[a, q],
            recv_sem=recv_sems.at[a, q], device_id=(x, y, 1 - c), device_id_type=MESH)
            for a in range(n) for q in range(4)]

    def start(ins, outs, sems):
        for cp in copies(ins, outs, sems):
            cp.start()

    def finish(ins, outs, sems):
        for cp in copies(ins, outs, sems):
            cp.wait()

    return _Task(list(arrs), [_sds((4,) + a.shape[1:], a.dtype) for a in arrs],
                 [pltpu.SemaphoreType.DMA((n, 4)), pltpu.SemaphoreType.DMA((n, 4))], start, finish)


def _chips_task(arrs):
    n = len(arrs)

    def copies(ins, outs, sems):
        send_sems, recv_sems, local_sems = sems
        x, y, c = _mesh_pos()
        q_me = 2 * x + y
        chips = [(1 - x, y), (x, 1 - y), (1 - x, 1 - y)]
        own = [pltpu.make_async_copy(ins[a].at[q_me], outs[a].at[q_me], local_sems.at[a]) for a in range(n)]
        remote = [pltpu.make_async_remote_copy(
            src_ref=ins[a].at[2 * chip[0] + chip[1]], dst_ref=outs[a].at[q_me], send_sem=send_sems.at[a, j],
            recv_sem=recv_sems.at[a, j], device_id=(*chip, c), device_id_type=MESH)
            for a in range(n) for j, chip in enumerate(chips)]
        return own + remote

    def start(ins, outs, sems):
        for cp in copies(ins, outs, sems):
            cp.start()

    def finish(ins, outs, sems):
        for cp in copies(ins, outs, sems):
            cp.wait()

    return _Task(list(arrs), [_sds(a.shape, a.dtype) for a in arrs],
                 [pltpu.SemaphoreType.DMA((n, 3)), pltpu.SemaphoreType.DMA((n, 3)), pltpu.SemaphoreType.DMA((n,))],
                 start, finish)


def _sibling_sum(arr, land, core, name):
    _, rows, cols = arr.shape
    tr = _tile(rows, 512)
    arr4 = arr.reshape(4, 2, rows, cols)

    def body(c_ref, a_ref, l_ref, o_ref):
        o_ref[...] = (a_ref[...] + l_ref[...]).astype(BF16)

    grid_spec = pltpu.PrefetchScalarGridSpec(
        num_scalar_prefetch=1, grid=(4, rows // tr),
        in_specs=[pl.BlockSpec((None, None, tr, cols), lambda q, r, c_ref: (q, c_ref[0], r, 0)),
                  pl.BlockSpec((None, tr, cols), lambda q, r, c_ref: (q, r, 0))],
        out_specs=pl.BlockSpec((None, tr, cols), lambda q, r, c_ref: (q, r, 0)))
    return pl.pallas_call(body, name=name, grid_spec=grid_spec, out_shape=_sds((4, rows, cols), BF16),
                          compiler_params=_params("arbitrary", "arbitrary"))(core, arr4, land)


def _sibling_sums(arrs, land, core, tag):
    return [_sibling_sum(a, l, core, name=f"rs_sum_{tag}_{k}") for k, (a, l) in enumerate(zip(arrs, land))]


def _adam(gparts, w, m, v, name):
    n_l = len(gparts)
    p, rows, cols = gparts[0].shape
    tr = _tile(rows, 256)
    n_r = rows // tr
    c1 = 1.0 - ADAM_B1 ** ADAM_STEP
    c2 = 1.0 - ADAM_B2 ** ADAM_STEP

    def body(*refs):
        g_refs = refs[:n_l]
        w_ref, m_ref, v_ref, go_ref, d_ref, mo_ref, vo_ref = refs[n_l:]
        layer = pl.program_id(0)
        g = jnp.zeros((tr, cols), F32)
        for li, g_ref in enumerate(g_refs):
            gl = g_ref[0].astype(F32)
            for k in range(1, p):
                gl = gl + g_ref[k].astype(F32)
            g = gl if n_l == 1 else jnp.where(layer == li, gl, g)
        m_new = ADAM_B1 * m_ref[...] + (1.0 - ADAM_B1) * g
        v_new = ADAM_B2 * v_ref[...] + (1.0 - ADAM_B2) * (g * g)
        m_hat = m_new / c1
        v_hat = v_new / c2
        go_ref[...] = g
        d_ref[...] = -ADAM_LR * (m_hat / (jnp.sqrt(v_hat) + ADAM_EPS) + ADAM_WD * w_ref[...])
        mo_ref[...] = m_new
        vo_ref[...] = v_new

    def g_spec(li):
        def index(l, r):
            return (0, jnp.where(l == li, r, jnp.where(l < li, 0, n_r - 1)), 0)
        return pl.BlockSpec((p, tr, cols), index)

    blk = pl.BlockSpec((None, tr, cols), lambda l, r: (l, r, 0))
    return pl.pallas_call(
        body, name=name, grid=(n_l, n_r),
        in_specs=[g_spec(li) for li in range(n_l)] + [blk, blk, blk],
        out_specs=[blk] * 4, out_shape=[_sds((n_l, rows, cols), F32)] * 4,
        compiler_params=_params("arbitrary", "arbitrary"))(*gparts, w, m, v)


def _adam_nd(grad, w, m, v, name):
    shape = w.shape
    cols = shape[-1]
    rows = w.size // cols
    as_rows = lambda a: a.reshape(1, rows, cols)
    out = _adam([as_rows(grad)], as_rows(w), as_rows(m), as_rows(v), name)
    return [o.reshape(shape) for o in out]


def _norm_mod(x, g, sc, sh, name):
    t = x.shape[0]
    tm = _tile(t, 512)

    def body(x_ref, g_ref, sc_ref, sh_ref, h_ref):
        h_ref[...] = _modulated_norm(x_ref[...], g_ref[...], sc_ref[...], sh_ref[...])

    row = pl.BlockSpec((1, D), lambda i: (0, 0))
    blk = pl.BlockSpec((tm, D), lambda i: (i, 0))
    return pl.pallas_call(body, name=name, grid=(t // tm,), in_specs=[blk, row, row, row], out_specs=blk,
                          out_shape=_sds((t, D), BF16), compiler_params=_params("arbitrary"))(x, g, sc, sh)


def _with_tasks(res_per, tasks):
    return res_per if tasks else res_per[0]


def _mm_cols(a, b8, bias8, name, tasks=()):
    t, k = a.shape
    j, _, n = b8.shape
    tm = _tile(t, 1024)

    def body(a_ref, b_ref, bias_ref, o_ref):
        o_ref[...] = (_dot(a_ref[...], b_ref[...]) + bias_ref[...]).astype(BF16)

    return _with_tasks(_hosted(
        tasks, body, name=name, grid=(j, t // tm),
        in_specs=[pl.BlockSpec((tm, k), lambda jj, i: (i, 0)),
                  pl.BlockSpec((None, k, n), lambda jj, i: (jj, 0, 0)),
                  pl.BlockSpec((None, 1, n), lambda jj, i: (jj, 0, 0))],
        out_specs=pl.BlockSpec((None, tm, n), lambda jj, i: (jj, i, 0)),
        out_shape=_sds((j, t, n), BF16))(a, b8, bias8), tasks)


def _halo_before(tm, col):
    return pl.BlockSpec((None, HALO, D), lambda i: (col, jnp.maximum(i * (tm // HALO) - 1, 0), 0))


def _pool_forward(p_ext, t0, rows):
    t = t0 + lax.broadcasted_iota(jnp.int32, (rows, 1), 0)
    out = []
    for gi, win in enumerate(POOL_WINDOWS):
        e = p_ext[:, gi * POOL_GC:(gi + 1) * POOL_GC]
        s, sh = e, 1
        while sh < win:
            s = s + pltpu.roll(s, sh, 0)
            sh *= 2
        cnt = jnp.minimum(t + 1, win).astype(F32)
        out.append(s[HALO:] / cnt - e[HALO:])
    return out


def _fill_shift_bank(bank_ref, ext, causal):
    n = ext.shape[0]
    bank_ref[0] = ext
    for b in range(1, SUBLANE):
        bank_ref[b] = pltpu.roll(ext, b if causal else n - b, 0)


def _branches_fwd(z8, ln_g, ln_b, w_s, bs_b, pool_w, pool_scale, conv_w, conv_b, cln_g, cln_b, name, tasks=()):
    t = z8.shape[1]
    tm = _tile(t, 256)
    n_ext = tm + HALO

    def body(zu_ref, zv_ref, p_ref, a_ref, ag_ref, ph_ref, ah_ref, agh_ref, lng_ref, lnb_ref, ws_ref, bsb_ref,
             wp_ref, ps_ref, cw_ref, cb_ref, clg_ref, clb_ref, sa_ref, sb_ref, sc_ref, cv_ref, bank_ref):
        i = pl.program_id(0)
        has_past = (i > 0).astype(F32)
        u = _gelu(zu_ref[...].astype(F32))
        vhat, _ = _ln_stats(_gelu(zv_ref[...].astype(F32)))
        vb = (vhat * lng_ref[...] + lnb_ref[...]).astype(BF16)
        mask = _tril_mask()
        for g in range(SGU_G):
            cols = slice(g * CHUNK, (g + 1) * CHUNK)
            wm = (ws_ref[g] * mask).astype(BF16)
            for n in range(tm // CHUNK):
                rows = slice(n * CHUNK, (n + 1) * CHUNK)
                mixed = _dot(wm, vb[rows, cols]) + bsb_ref[g]
                sa_ref[rows, cols] = (u[rows, cols] * mixed).astype(BF16)
        p_ext = jnp.concatenate([ph_ref[...].astype(F32) * has_past, p_ref[...].astype(F32)], axis=0)
        pooled = _pool_forward(p_ext, i * tm, tm)
        for gi in range(len(POOL_WINDOWS)):
            cols = slice(gi * POOL_GC, (gi + 1) * POOL_GC)
            y = _dot(pooled[gi].astype(BF16), wp_ref[gi].astype(BF16))
            sb_ref[:, cols] = (y * ps_ref[:, cols]).astype(BF16)
        for cb in range(D // LANE):
            cols = slice(cb * LANE, (cb + 1) * LANE)
            zc = jnp.concatenate(
                [ah_ref[:, cols].astype(F32) * has_past * _sigmoid(agh_ref[:, cols].astype(F32)),
                 a_ref[:, cols].astype(F32) * _sigmoid(ag_ref[:, cols].astype(F32))], axis=0)
            _fill_shift_bank(bank_ref, zc, causal=True)
            for r0 in range(0, tm, CONV_STRIP):
                acc = jnp.zeros((CONV_STRIP, LANE), F32) + cb_ref[:, cols]
                for k in range(CONV_K):
                    hi, lo = divmod(CONV_K - 1 - k, SUBLANE)
                    acc = acc + cw_ref[k:k + 1, cols] * bank_ref[lo, pl.ds(HALO - SUBLANE * hi + r0, CONV_STRIP), :]
                cv_ref[r0:r0 + CONV_STRIP, cols] = acc
        cv = cv_ref[...]
        chat, _ = _ln_stats(cv)
        cl = chat * clg_ref[...] + clb_ref[...]
        sc_ref[...] = (cl * _sigmoid(cl)).astype(BF16)

    def col(j):
        return pl.BlockSpec((None, tm, D), lambda i: (j, i, 0))

    row = pl.BlockSpec((1, D), lambda i: (0, 0))
    full3 = lambda s: pl.BlockSpec(s, lambda i: (0, 0, 0))
    blk = pl.BlockSpec((tm, D), lambda i: (i, 0))
    return _with_tasks(_hosted(
        tasks, body, name=name, grid=(t // tm,),
        in_specs=[col(0), col(1), col(2), col(3), col(4), _halo_before(tm, 2), _halo_before(tm, 3),
                  _halo_before(tm, 4), row, row, full3((SGU_G, CHUNK, CHUNK)), full3((SGU_G, CHUNK, CHUNK)),
                  full3((4, POOL_GC, POOL_GC)), row, pl.BlockSpec((HALO, D), lambda i: (0, 0)), row, row, row],
        out_specs=[blk, blk, blk, blk],
        out_shape=[_sds((t, D), BF16)] * 3 + [_sds((t, D), F32)],
        scratch_shapes=[pltpu.VMEM((SUBLANE, n_ext, LANE), F32)],
    )(z8, z8, z8, z8, z8, z8, z8, z8, ln_g, ln_b, w_s, bs_b, pool_w, pool_scale, conv_w, conv_b, cln_g, cln_b), tasks)


def _proj_merge(sa, sb, sc, w_pa, w_pb, w_pc, z8, name, tasks=()):
    t = sa.shape[0]
    tm = _tile(t, 512)

    def body(sa_ref, sb_ref, sc_ref, wa_ref, wb_ref, wc_ref, g0_ref, g1_ref, g2_ref, ya_ref, yb_ref, yc_ref, m_ref):
        merged = jnp.zeros((tm, D), F32)
        for s_ref, w_ref, g_ref, y_ref in ((sa_ref, wa_ref, g0_ref, ya_ref), (sb_ref, wb_ref, g1_ref, yb_ref),
                                           (sc_ref, wc_ref, g2_ref, yc_ref)):
            y = _dot(s_ref[...], w_ref[...])
            y_ref[...] = y.astype(BF16)
            merged = merged + _sigmoid(g_ref[...].astype(F32)) * y
        m_ref[...] = merged.astype(BF16)

    blk = pl.BlockSpec((tm, D), lambda i: (i, 0))
    wspec = pl.BlockSpec((D, D), lambda i: (0, 0))
    gate = lambda j: pl.BlockSpec((None, tm, D), lambda i: (j, i, 0))
    return _with_tasks(_hosted(
        tasks, body, name=name, grid=(t // tm,),
        in_specs=[blk, blk, blk, wspec, wspec, wspec, gate(5), gate(6), gate(7)],
        out_specs=[blk] * 4, out_shape=[_sds((t, D), BF16)] * 4)(sa, sb, sc, w_pa, w_pb, w_pc, z8, z8, z8), tasks)


def _modulated_norm(xv, g, sc, sh):
    r = lax.rsqrt(jnp.mean(xv * xv, axis=-1, keepdims=True) + EPS)
    return (xv * r * g * (1.0 + sc) + sh).astype(BF16)


def _out_proj(merged, w_out, x, gt, g, sc, sh, name):
    t = x.shape[0]
    tm = _tile(t, 512)

    def body(m_ref, w_ref, x_ref, gt_ref, g_ref, sc_ref, sh_ref, om_ref, x1_ref, h2_ref):
        om = _dot(m_ref[...], w_ref[...])
        om_ref[...] = om
        x1 = x_ref[...] + gt_ref[...] * om
        x1_ref[...] = x1
        h2_ref[...] = _modulated_norm(x1, g_ref[...], sc_ref[...], sh_ref[...])

    blk = pl.BlockSpec((tm, D), lambda i: (i, 0))
    row = pl.BlockSpec((1, D), lambda i: (0, 0))
    return pl.pallas_call(
        body, name=name, grid=(t // tm,),
        in_specs=[blk, pl.BlockSpec((D, D), lambda i: (0, 0)), blk, row, row, row, row],
        out_specs=[blk, blk, blk], out_shape=[_sds((t, D), F32)] * 2 + [_sds((t, D), BF16)],
        compiler_params=_params("arbitrary"))(merged, w_out, x, gt, g, sc, sh)


def _ffn_in(h2, wfi, name, tasks=()):
    t = h2.shape[0]
    tm = _tile(t, 512)

    def body(h_ref, w_ref, gu_ref, f_ref):
        hv = h_ref[...]
        gp = _dot_nt(hv, w_ref[0])
        up = _dot_nt(hv, w_ref[1])
        gu_ref[0] = gp.astype(BF16)
        gu_ref[1] = up.astype(BF16)
        f_ref[...] = (gp * _sigmoid(gp) * up).astype(BF16)

    return _with_tasks(_hosted(
        tasks, body, name=name, grid=(4, t // tm),
        in_specs=[pl.BlockSpec((tm, D), lambda j, i: (i, 0)),
                  pl.BlockSpec((2, None, FF_BLK, D), lambda j, i: (0, j, 0, 0))],
        out_specs=[pl.BlockSpec((2, None, tm, FF_BLK), lambda j, i: (0, j, i, 0)),
                   pl.BlockSpec((None, tm, FF_BLK), lambda j, i: (j, i, 0))],
        out_shape=[_sds((2, 4, t, FF_BLK), BF16), _sds((4, t, FF_BLK), BF16)])(h2, wfi), tasks)


def _ffn_out(f4, wfo4, x1, gt, g, sc, sh, name, tasks=()):
    t = x1.shape[0]
    tm = _tile(t, 512)
    with_norm = g is not None

    def body(f_ref, w_ref, x_ref, gt_ref, *rest):
        o_ref, x2_ref = rest[-3:-1] if with_norm else rest[-2:]
        j = pl.program_id(1)

        @pl.when(j == 0)
        def _():
            o_ref[...] = jnp.zeros_like(o_ref)

        o_ref[...] += _dot(f_ref[...], w_ref[...])

        @pl.when(j == 3)
        def _():
            x2 = x_ref[...] + gt_ref[...] * o_ref[...]
            x2_ref[...] = x2
            if with_norm:
                g_ref, sc_ref, sh_ref = rest[:3]
                rest[-1][...] = _modulated_norm(x2, g_ref[...], sc_ref[...], sh_ref[...])

    blk = pl.BlockSpec((tm, D), lambda i, j: (i, 0))
    row = pl.BlockSpec((1, D), lambda i, j: (0, 0))
    norm_args = [g, sc, sh] if with_norm else []
    return _with_tasks(_hosted(
        tasks, body, name=name, grid=(t // tm, 4),
        in_specs=[pl.BlockSpec((None, tm, FF_BLK), lambda i, j: (j, i, 0)),
                  pl.BlockSpec((None, FF_BLK, D), lambda i, j: (j, 0, 0)), blk, row] + [row] * len(norm_args),
        out_specs=[blk, blk] + [blk] * with_norm,
        out_shape=[_sds((t, D), F32)] * 2 + [_sds((t, D), BF16)] * with_norm)(f4, wfo4, x1, gt, *norm_args), tasks)


def _gate_grads(dx, o_ref, gt_ref, do_ref, dgt_ref):
    do_ref[...] = (dx * gt_ref[...]).astype(BF16)
    dgt_ref[...] += _colsum(dx * o_ref[...])


def _final_loss(x, g, target, o, gt, name):
    t = x.shape[0]
    tm = _tile(t, 512)

    def body(x_ref, g_ref, t_ref, o_ref, gt_ref, loss_ref, dx_ref, dg_ref, do_ref, dgt_ref):
        @pl.when(pl.program_id(0) == 0)
        def _():
            for ref in (loss_ref, dg_ref, dgt_ref):
                ref[...] = jnp.zeros_like(ref)

        xv = x_ref[...]
        r = lax.rsqrt(jnp.mean(xv * xv, axis=-1, keepdims=True) + EPS)
        xn = xv * r
        diff = xn * g_ref[...] - t_ref[...]
        loss_ref[...] += 0.5 * jnp.sum(jnp.mean(diff * diff, axis=-1, keepdims=True))
        dy = diff * (1.0 / D)
        dg_ref[...] += _colsum(dy * xn)
        dxn = dy * g_ref[...]
        dx = r * (dxn - xn * jnp.mean(dxn * xn, axis=-1, keepdims=True))
        dx_ref[...] = dx
        _gate_grads(dx, o_ref, gt_ref, do_ref, dgt_ref)

    blk = pl.BlockSpec((tm, D), lambda i: (i, 0))
    row = pl.BlockSpec((1, D), lambda i: (0, 0))
    return pl.pallas_call(
        body, name=name, grid=(t // tm,), in_specs=[blk, row, blk, blk, row],
        out_specs=[pl.BlockSpec((8, 128), lambda i: (0, 0)), blk, row, blk, row],
        out_shape=[_sds((8, 128), F32), _sds((t, D), F32), _sds((1, D), F32), _sds((t, D), BF16), _sds((1, D), F32)],
        compiler_params=_params("arbitrary"))(x, g, target, o, gt)


def _norm_mod_bwd(dh, x, dres, g, sc, o, gt, name, tasks=()):
    t = x.shape[0]
    tm = _tile(t, 512)
    with_gate = o is not None

    def body(dh_ref, x_ref, dr_ref, g_ref, sc_ref, *rest):
        dx_ref, st_ref = rest[2:4] if with_gate else rest

        @pl.when(pl.program_id(0) == 0)
        def _():
            st_ref[...] = jnp.zeros_like(st_ref)
            if with_gate:
                rest[5][...] = jnp.zeros_like(rest[5])

        xv, dhv = x_ref[...], dh_ref[...]
        r = lax.rsqrt(jnp.mean(xv * xv, axis=-1, keepdims=True) + EPS)
        xn = xv * r
        gv, mod = g_ref[...], 1.0 + sc_ref[...]
        st_ref[0:1, :] += _colsum(dhv)
        st_ref[1:2, :] += _colsum(dhv * xn * gv)
        st_ref[2:3, :] += _colsum(dhv * xn * mod)
        dxn = dhv * gv * mod
        dx = dr_ref[...] + r * (dxn - xn * jnp.mean(dxn * xn, axis=-1, keepdims=True))
        dx_ref[...] = dx
        if with_gate:
            _gate_grads(dx, rest[0], rest[1], rest[4], rest[5])

    blk = pl.BlockSpec((tm, D), lambda i: (i, 0))
    row = pl.BlockSpec((1, D), lambda i: (0, 0))
    gate_args = [o, gt] if with_gate else []
    return _with_tasks(_hosted(
        tasks, body, name=name, grid=(t // tm,), in_specs=[blk, blk, blk, row, row] + [blk, row] * with_gate,
        out_specs=[blk, pl.BlockSpec((3, D), lambda i: (0, 0))] + [blk, row] * with_gate,
        out_shape=[_sds((t, D), F32), _sds((3, D), F32)] + [_sds((t, D), BF16), _sds((1, D), F32)] * with_gate,
    )(dh, x, dres, g, sc, *gate_args), tasks)


def _ffn_bwd_act(do, wfo4, gu, name, tasks=()):
    t = do.shape[0]
    tm = _tile(t, 512)

    def body(do_ref, w_ref, gu_ref, dgu_ref):
        df = _dot_nt(do_ref[...], w_ref[...])
        gp, up = gu_ref[0].astype(F32), gu_ref[1].astype(F32)
        sg = _sigmoid(gp)
        dgu_ref[0] = (df * up * (sg * (1.0 + gp * (1.0 - sg)))).astype(BF16)
        dgu_ref[1] = (df * (gp * sg)).astype(BF16)

    gu_spec = pl.BlockSpec((2, None, tm, FF_BLK), lambda j, i: (0, j, i, 0))
    return _with_tasks(_hosted(
        tasks, body, name=name, grid=(4, t // tm),
        in_specs=[pl.BlockSpec((tm, D), lambda j, i: (i, 0)),
                  pl.BlockSpec((None, FF_BLK, D), lambda j, i: (j, 0, 0)), gu_spec],
        out_specs=gu_spec, out_shape=_sds((2, 4, t, FF_BLK), BF16))(do, wfo4, gu), tasks)


def _mm_nt_sum(a8, b8, name, b_is_kn=False, tasks=()):
    j, t, k = a8.shape
    n = b8.shape[2] if b_is_kn else b8.shape[1]
    tm = _tile(t, 1024 if j > 1 else 512)
    out_dtype = F32 if j > 1 else BF16
    dot = _dot if b_is_kn else _dot_nt

    def body(a_ref, b_ref, o_ref):
        if j == 1:
            o_ref[...] = dot(a_ref[...], b_ref[...]).astype(out_dtype)
            return

        @pl.when(pl.program_id(1) == 0)
        def _():
            o_ref[...] = jnp.zeros_like(o_ref)

        o_ref[...] += dot(a_ref[...], b_ref[...])

    return _with_tasks(_hosted(
        tasks, body, name=name, grid=(t // tm, j),
        in_specs=[pl.BlockSpec((None, tm, k), lambda i, jj: (jj, i, 0)),
                  pl.BlockSpec((None,) + b8.shape[1:], lambda i, jj: (jj, 0, 0))],
        out_specs=pl.BlockSpec((tm, n), lambda i, jj: (i, 0)), out_shape=_sds((t, n), out_dtype))(a8, b8), tasks)


def _mm_tn(a8, b8, name):
    ja, t, m = a8.shape
    jb, _, n = b8.shape
    j = max(ja, jb)
    tk = _tile(t, 1024)

    def body(a_ref, b_ref, o_ref):
        @pl.when(pl.program_id(1) == 0)
        def _():
            o_ref[...] = jnp.zeros_like(o_ref)

        o_ref[...] += _dot_tn(a_ref[...], b_ref[...])

    return pl.pallas_call(
        body, name=name, grid=(j, t // tk),
        in_specs=[pl.BlockSpec((None, tk, m), (lambda jj, kk: (jj, kk, 0)) if ja > 1 else (lambda jj, kk: (0, kk, 0))),
                  pl.BlockSpec((None, tk, n), (lambda jj, kk: (jj, kk, 0)) if jb > 1 else (lambda jj, kk: (0, kk, 0)))],
        out_specs=pl.BlockSpec((None, m, n), lambda jj, kk: (jj, 0, 0)), out_shape=_sds((j, m, n), F32),
        compiler_params=_params("arbitrary", "arbitrary"))(a8, b8)


def _merge_bwd(dom, w_out, z8, ya, yb, yc, name, tasks=()):
    t = dom.shape[0]
    tm = _tile(t, 512)

    def body(dom_ref, w_ref, g0_ref, g1_ref, g2_ref, ya_ref, yb_ref, yc_ref, dya_ref, dyb_ref, dyc_ref, dzg_ref,
             db_ref):
        @pl.when(pl.program_id(0) == 0)
        def _():
            db_ref[...] = jnp.zeros_like(db_ref)

        dm = _dot_nt(dom_ref[...], w_ref[...])
        for k, (g_ref, y_ref, dy_ref) in enumerate(((g0_ref, ya_ref, dya_ref), (g1_ref, yb_ref, dyb_ref),
                                                    (g2_ref, yc_ref, dyc_ref))):
            sg = _sigmoid(g_ref[...].astype(F32))
            dy_ref[...] = (dm * sg).astype(BF16)
            dzg = dm * y_ref[...].astype(F32) * (sg * (1.0 - sg))
            dzg_ref[k] = dzg.astype(BF16)
            db_ref[k:k + 1, :] += _colsum(dzg)

    blk = pl.BlockSpec((tm, D), lambda i: (i, 0))
    gate = lambda j: pl.BlockSpec((None, tm, D), lambda i: (j, i, 0))
    return _with_tasks(_hosted(
        tasks, body, name=name, grid=(t // tm,),
        in_specs=[blk, pl.BlockSpec((D, D), lambda i: (0, 0)), gate(5), gate(6), gate(7), blk, blk, blk],
        out_specs=[blk, blk, blk, pl.BlockSpec((3, tm, D), lambda i: (0, i, 0)), pl.BlockSpec((3, D), lambda i: (0, 0))],
        out_shape=[_sds((t, D), BF16)] * 3 + [_sds((3, t, D), BF16), _sds((3, D), F32)],
    )(dom, w_out, z8, z8, z8, ya, yb, yc), tasks)


def _branches_bwd(z8, cv, dsa, dsb, dsc, dzg, ln_g, ln_b, w_s, bs_b, pool_w, pool_scale, conv_w, cln_g, cln_b, name,
                  tasks=()):
    t = z8.shape[1]
    tm = _tile(t, 256)
    n_ext = tm + HALO
    n_tiles = t // tm

    def body(zu_ref, zv_ref, p_ref, a_ref, ag_ref, ph_ref, ah_ref, agh_ref, cv_ref, cvf_ref, dsa_ref, dsb_ref,
             dsbf_ref, dsc_ref, dscf_ref, dzg_ref, lng_ref, lnb_ref, ws_ref, bsb_ref, wp_ref, ps_ref, cw_ref,
             clg_ref, clb_ref, dz_ref, dbin_ref, rows_ref, dws_ref, dbs_ref, dwp_ref, dcw_ref, mixed_scr, dvln_scr,
             dcv_scr, zbank_ref, dbank_ref, dcw8_scr):
        i = pl.program_id(0)

        @pl.when(i == 0)
        def _():
            for ref in (dbin_ref, rows_ref, dws_ref, dbs_ref, dwp_ref, dcw8_scr):
                ref[...] = jnp.zeros_like(ref)

        has_past = (i > 0).astype(F32)
        has_next = (i < n_tiles - 1).astype(F32)

        def emit(j, val):
            dz_ref[j] = val.astype(BF16)
            dbin_ref[j:j + 1, :] += _colsum(val)

        zu, zv = zu_ref[...].astype(F32), zv_ref[...].astype(F32)
        u = _gelu(zu)
        vhat, v_rstd = _ln_stats(_gelu(zv))
        vb = (vhat * lng_ref[...] + lnb_ref[...]).astype(BF16)
        dsa = dsa_ref[...].astype(F32)
        dmixed = dsa * u
        dmb = dmixed.astype(BF16)
        mask = _tril_mask()
        lane = lax.broadcasted_iota(jnp.int32, (CHUNK, CHUNK), 1)
        for g in range(SGU_G):
            cols = slice(g * CHUNK, (g + 1) * CHUNK)
            wm = (ws_ref[g] * mask).astype(BF16)
            dws = jnp.zeros((CHUNK, CHUNK), F32)
            dbs = jnp.zeros((CHUNK, 1), F32)
            for n in range(tm // CHUNK):
                rows = slice(n * CHUNK, (n + 1) * CHUNK)
                mixed_scr[rows, cols] = _dot(wm, vb[rows, cols]) + bsb_ref[g]
                dvln_scr[rows, cols] = _dot_tn(wm, dmb[rows, cols])
                dws = dws + _dot_nt(dmb[rows, cols], vb[rows, cols])
                dbs = dbs + jnp.sum(dmixed[rows, cols], axis=1, keepdims=True)
            dws_ref[g] += dws
            dbs_ref[...] += jnp.where(lane == g, dbs, 0.0)
        emit(0, dsa * mixed_scr[...] * _gelu_grad(zu))
        dvln = dvln_scr[...]
        rows_ref[0:1, :] += _colsum(dvln * vhat)
        rows_ref[1:2, :] += _colsum(dvln)
        emit(1, _ln_bwd(dvln * lng_ref[...], vhat, v_rstd) * _gelu_grad(zv))

        p_ext = jnp.concatenate([ph_ref[...].astype(F32) * has_past, p_ref[...].astype(F32)], axis=0)
        pooled = _pool_forward(p_ext, i * tm, tm)
        dsb = dsb_ref[...].astype(F32)
        dpl_ext = jnp.concatenate([dsb, dsbf_ref[...].astype(F32) * has_next], axis=0) * ps_ref[...]
        t_ext = i * tm + lax.broadcasted_iota(jnp.int32, (n_ext, 1), 0)
        dp_parts = []
        for gi, win in enumerate(POOL_WINDOWS):
            cols = slice(gi * POOL_GC, (gi + 1) * POOL_GC)
            pooled_b = pooled[gi].astype(BF16)
            wpb = wp_ref[gi].astype(BF16)
            rows_ref[2:3, cols] += _colsum(dsb[:, cols] * _dot(pooled_b, wpb))
            dplb = dpl_ext[:, cols].astype(BF16)
            dwp_ref[gi] += _dot_tn(pooled_b, dplb[:tm])
            dpooled = _dot_nt(dplb, wpb)
            s, sh = dpooled / jnp.minimum(t_ext + 1, win).astype(F32), 1
            while sh < win:
                s = s + pltpu.roll(s, n_ext - sh, 0)
                sh *= 2
            dp_parts.append(s[:tm] - dpooled[:tm])
        emit(2, jnp.concatenate(dp_parts, axis=1))

        cv_ext = jnp.concatenate([cv_ref[...], cvf_ref[...]], axis=0)
        chat, c_rstd = _ln_stats(cv_ext)
        cl = chat * clg_ref[...] + clb_ref[...]
        sg = _sigmoid(cl)
        dsc_ext = jnp.concatenate([dsc_ref[...].astype(F32), dscf_ref[...].astype(F32)], axis=0)
        dcl = dsc_ext * (sg * (1.0 + cl * (1.0 - sg)))
        rows_ref[4:5, :] += _colsum((dcl * chat)[:tm])
        rows_ref[5:6, :] += _colsum(dcl[:tm])
        in_seq = jnp.concatenate([jnp.ones((tm, 1), F32), jnp.zeros((HALO, 1), F32) + has_next], axis=0)
        dcv = jnp.where(in_seq > 0.0, _ln_bwd(dcl * clg_ref[...], chat, c_rstd), 0.0)
        rows_ref[3:4, :] += _colsum(dcv[:tm])
        dcv_scr[...] = dcv
        for cb in range(D // LANE):
            cols = slice(cb * LANE, (cb + 1) * LANE)
            zc = jnp.concatenate(
                [ah_ref[:, cols].astype(F32) * has_past * _sigmoid(agh_ref[:, cols].astype(F32)),
                 a_ref[:, cols].astype(F32) * _sigmoid(ag_ref[:, cols].astype(F32))], axis=0)
            _fill_shift_bank(zbank_ref, zc, causal=True)
            _fill_shift_bank(dbank_ref, dcv_scr[:, cols], causal=False)
            for r0 in range(0, tm, CONV_STRIP):
                rows = slice(r0, r0 + CONV_STRIP)
                dcv_s = dcv_scr[rows, cols]
                dzc = jnp.zeros((CONV_STRIP, LANE), F32)
                for k in range(CONV_K):
                    hi, lo = divmod(CONV_K - 1 - k, SUBLANE)
                    z_win = zbank_ref[lo, pl.ds(HALO - SUBLANE * hi + r0, CONV_STRIP), :]
                    dcw8_scr[k, :, cols] += jnp.sum((dcv_s * z_win).reshape(CONV_STRIP // SUBLANE, SUBLANE, LANE), axis=0)
                    dzc = dzc + cw_ref[k:k + 1, cols] * dbank_ref[lo, pl.ds(SUBLANE * hi + r0, CONV_STRIP), :]
                a_s = a_ref[rows, cols].astype(F32)
                sga = _sigmoid(ag_ref[rows, cols].astype(F32))
                dza = dzc * sga
                dzag = dzc * a_s * (sga * (1.0 - sga))
                dz_ref[3, rows, cols] = dza.astype(BF16)
                dz_ref[4, rows, cols] = dzag.astype(BF16)
                dbin_ref[3:4, cols] += _colsum(dza)
                dbin_ref[4:5, cols] += _colsum(dzag)
        for k in range(3):
            dz_ref[5 + k] = dzg_ref[k]

        @pl.when(i == n_tiles - 1)
        def _():
            dcw_ref[...] = jnp.sum(dcw8_scr[...], axis=1)

    def col(j):
        return pl.BlockSpec((None, tm, D), lambda i: (j, i, 0))

    blk = pl.BlockSpec((tm, D), lambda i: (i, 0))
    after = pl.BlockSpec((HALO, D), lambda i: (jnp.minimum((i + 1) * (tm // HALO), t // HALO - 1), 0))
    row = pl.BlockSpec((1, D), lambda i: (0, 0))
    full2 = lambda s: pl.BlockSpec(s, lambda i: (0, 0))
    full3 = lambda s: pl.BlockSpec(s, lambda i: (0, 0, 0))
    return _with_tasks(_hosted(
        tasks, body, name=name, grid=(n_tiles,),
        in_specs=[col(0), col(1), col(2), col(3), col(4), _halo_before(tm, 2), _halo_before(tm, 3),
                  _halo_before(tm, 4), blk, after, blk, blk, after, blk, after,
                  pl.BlockSpec((3, tm, D), lambda i: (0, i, 0)), row, row, full3((SGU_G, CHUNK, CHUNK)),
                  full3((SGU_G, CHUNK, CHUNK)), full3((4, POOL_GC, POOL_GC)), row, full2((HALO, D)), row, row],
        out_specs=[pl.BlockSpec((8, tm, D), lambda i: (0, i, 0)), full2((8, D)), full2((8, D)),
                   full3((SGU_G, CHUNK, CHUNK)), full2((CHUNK, CHUNK)), full3((4, POOL_GC, POOL_GC)),
                   full2((HALO, D))],
        out_shape=[_sds((8, t, D), BF16), _sds((8, D), F32), _sds((8, D), F32), _sds((SGU_G, CHUNK, CHUNK), F32),
                   _sds((CHUNK, CHUNK), F32), _sds((4, POOL_GC, POOL_GC), F32), _sds((HALO, D), F32)],
        scratch_shapes=[pltpu.VMEM((tm, D), F32), pltpu.VMEM((tm, D), F32), pltpu.VMEM((n_ext, D), F32),
                        pltpu.VMEM((SUBLANE, n_ext, LANE), F32), pltpu.VMEM((SUBLANE, n_ext, LANE), F32),
                        pltpu.VMEM((HALO, SUBLANE, D), F32)],
    )(z8, z8, z8, z8, z8, z8, z8, z8, cv, cv, dsa, dsb, dsb, dsc, dsc, dzg, ln_g, ln_b, w_s, bs_b, pool_w,
      pool_scale, conv_w, cln_g, cln_b), tasks)


def _ada_fwd(c_all, w_ada, b_loc, name):
    def body(c_ref, w_ref, b_ref, o_ref):
        cv = c_ref[...]
        ca = (cv * _sigmoid(cv)).astype(BF16)
        for l in range(DEPTH):
            o_ref[l] = _dot(ca, w_ref[l].astype(BF16)) + b_ref[l]

    return pl.pallas_call(body, name=name, out_shape=_sds((DEPTH, N_DEV, ADA_BLK), F32),
                          compiler_params=_params())(c_all, w_ada, b_loc)


def _ada_bwd(c_all_t, d_loc, name):
    def body(c_ref, d_ref, o_ref):
        cv = c_ref[...]
        ca = cv * _sigmoid(cv)
        for l in range(DEPTH):
            acc = jnp.zeros((D, ADA_BLK), F32)
            for j in range(N_DEV):
                acc = acc + ca[:, j:j + 1] * d_ref[l, j:j + 1, :]
            o_ref[l] = acc

    return pl.pallas_call(body, name=name, out_shape=_sds((DEPTH, D, ADA_BLK), F32),
                          compiler_params=_params())(c_all_t, d_loc)


def _sum8(g8, name):
    _, rows, cols = g8.shape
    tr = _tile(rows, 256)

    def body(g_ref, o_ref):
        acc = g_ref[0]
        for k in range(1, N_DEV):
            acc = acc + g_ref[k]
        o_ref[...] = acc

    return pl.pallas_call(body, name=name, grid=(rows // tr,),
                          in_specs=[pl.BlockSpec((N_DEV, tr, cols), lambda r: (0, r, 0))],
                          out_specs=pl.BlockSpec((tr, cols), lambda r: (r, 0)), out_shape=_sds((rows, cols), F32),
                          compiler_params=_params("arbitrary"))(g8)


PROJ = ("w_pa", "w_pb", "w_pc", "w_out")
FWD_GATHERS = {
    (0, "in_proj"): tuple((n, 0) for n in PROJ) + (("w_ffn_out", 0),),
    (0, "branches"): (("w_ffn_in", 0),),
    (0, "ffn_in"): (("w_in", 1),),
    (0, "ffn_out"): tuple((n, 1) for n in PROJ),
    (1, "in_proj"): (("w_ffn_in", 1),),
    (1, "branches"): (("w_ffn_out", 1),),
}
FWD_GATHER_FILLS_KERNEL = ((0, "in_proj"), (0, "ffn_in"))
GRAD_GROUP = {"w_in": ("in", 0), "pool_w": ("in", 1), "w_pa": ("proj", 0), "w_pb": ("proj", 1), "w_pc": ("proj", 2),
              "w_out": ("proj", 3), "w_ffn_in": ("ffn", 0), "w_ffn_out": ("ffn", 1)}


EARLY_ROWS = dict(b_in=(0, 8), sgu_ln_g=(8, 9), sgu_ln_b=(9, 10), sgu_w_s=(10, 138), sgu_b_s=(138, 139),
                  pool_scale=(139, 140), conv_b=(140, 141), conv_ln_g=(141, 142), conv_ln_b=(142, 143), g_ffn=(143, 144))
CONV_ROW = 144
LATE_ROWS = dict(b_ada=(0, 6), g_mix=(6, 7))


def _run(factory, *args, tasks=(), **kw):
    out = factory(*args, tasks=tasks, **kw)
    return out if tasks else (out, [])


REPLICATED = ("b_ada", "g_mix", "b_in", "sgu_ln_g", "sgu_ln_b", "sgu_w_s", "sgu_b_s", "pool_scale", "conv_b",
              "conv_ln_g", "conv_ln_b", "g_ffn", "g_final")
WEIGHT_ORDER = ("w_ada", "b_ada", "g_mix", "w_in", "b_in", "sgu_ln_g", "sgu_ln_b", "sgu_w_s", "sgu_b_s", "w_pa",
                "pool_w", "pool_scale", "w_pb", "conv_w", "conv_b", "conv_ln_g", "conv_ln_b", "w_pc", "w_out",
                "g_ffn", "w_ffn_in", "w_ffn_out", "g_final")


def _rows(a):
    return a.reshape(-1, D)


def kernel(x, c, w_ada, b_ada, g_mix, w_in, b_in, sgu_ln_g, sgu_ln_b, sgu_w_s, sgu_b_s, w_pa, pool_w, pool_scale, w_pb, conv_w, conv_b, conv_ln_g, conv_ln_b, w_pc, w_out, g_ffn, w_ffn_in, w_ffn_out, g_final, loss_target, m_w_ada, m_b_ada, m_g_mix, m_w_in, m_b_in, m_sgu_ln_g, m_sgu_ln_b, m_sgu_w_s, m_sgu_b_s, m_w_pa, m_pool_w, m_pool_scale, m_w_pb, m_conv_w, m_conv_b, m_conv_ln_g, m_conv_ln_b, m_w_pc, m_w_out, m_g_ffn, m_w_ffn_in, m_w_ffn_out, m_g_final, v_w_ada, v_b_ada, v_g_mix, v_w_in, v_b_in, v_sgu_ln_g, v_sgu_ln_b, v_sgu_w_s, v_sgu_b_s, v_w_pa, v_pool_w, v_pool_scale, v_w_pb, v_conv_w, v_conv_b, v_conv_ln_g, v_conv_ln_b, v_w_pc, v_w_out, v_g_ffn, v_w_ffn_in, v_w_ffn_out, v_g_final):
    weights = dict(w_ada=w_ada, b_ada=b_ada, g_mix=g_mix, w_in=w_in, b_in=b_in, sgu_ln_g=sgu_ln_g, sgu_ln_b=sgu_ln_b,
                   sgu_w_s=sgu_w_s, sgu_b_s=sgu_b_s, w_pa=w_pa, pool_w=pool_w, pool_scale=pool_scale, w_pb=w_pb,
                   conv_w=conv_w, conv_b=conv_b, conv_ln_g=conv_ln_g, conv_ln_b=conv_ln_b, w_pc=w_pc, w_out=w_out,
                   g_ffn=g_ffn, w_ffn_in=w_ffn_in, w_ffn_out=w_ffn_out, g_final=g_final)
    mom1 = dict(w_ada=m_w_ada, b_ada=m_b_ada, g_mix=m_g_mix, w_in=m_w_in, b_in=m_b_in, sgu_ln_g=m_sgu_ln_g,
                sgu_ln_b=m_sgu_ln_b, sgu_w_s=m_sgu_w_s, sgu_b_s=m_sgu_b_s, w_pa=m_w_pa, pool_w=m_pool_w,
                pool_scale=m_pool_scale, w_pb=m_w_pb, conv_w=m_conv_w, conv_b=m_conv_b, conv_ln_g=m_conv_ln_g,
                conv_ln_b=m_conv_ln_b, w_pc=m_w_pc, w_out=m_w_out, g_ffn=m_g_ffn, w_ffn_in=m_w_ffn_in,
                w_ffn_out=m_w_ffn_out, g_final=m_g_final)
    mom2 = dict(w_ada=v_w_ada, b_ada=v_b_ada, g_mix=v_g_mix, w_in=v_w_in, b_in=v_b_in, sgu_ln_g=v_sgu_ln_g,
                sgu_ln_b=v_sgu_ln_b, sgu_w_s=v_sgu_w_s, sgu_b_s=v_sgu_b_s, w_pa=v_w_pa, pool_w=v_pool_w,
                pool_scale=v_pool_scale, w_pb=v_w_pb, conv_w=v_conv_w, conv_b=v_conv_b, conv_ln_g=v_conv_ln_g,
                conv_ln_b=v_conv_ln_b, w_pc=v_w_pc, w_out=v_w_out, g_ffn=v_g_ffn, w_ffn_in=v_w_ffn_in,
                w_ffn_out=v_w_ffn_out, g_final=v_g_final)

    for group in (weights, mom1, mom2):
        group["w_ffn_in"] = jnp.transpose(group["w_ffn_in"], (0, 2, 1))

    t = x.shape[1]
    xs = x.reshape(t, D)
    target = loss_target.reshape(t, D)
    me = 4 * lax.axis_index("x") + 2 * lax.axis_index("y") + lax.axis_index("c")
    core = lax.axis_index("c").astype(jnp.int32).reshape(1)

    bf = lambda n, l: weights[n][l].astype(BF16)
    (first,) = _transfer([_gather_task([bf("w_in", 0), c, pool_w, conv_w])], name="gather_first")
    w_in0, c_all, pool_all, conv_all = first
    full = [dict(w_in=w_in0)] + [dict() for _ in range(1, DEPTH)]

    def gather_at(l, stage):
        names = FWD_GATHERS.get((l, stage), ())
        relay_at = 1.0 if (l, stage) in FWD_GATHER_FILLS_KERNEL else 0.75
        return [_gather_task([bf(n, ll) for n, ll in names], relay_at)] if names else []

    def landed(l, stage, per):
        for (n, ll), arr in zip(FWD_GATHERS.get((l, stage), ()), per[0] if per else ()):
            full[ll][n] = arr

    c_all = c_all.reshape(N_DEV, D)
    pool_full = jnp.transpose(pool_all, (1, 2, 0, 3, 4)).reshape(DEPTH, 4, POOL_GC, POOL_GC)
    conv_full = jnp.transpose(conv_all, (1, 2, 0, 3)).reshape(DEPTH, CONV_K, D)
    conv_full = jnp.pad(conv_full, ((0, 0), (0, HALO - CONV_K), (0, 0)))

    b_loc = lax.dynamic_slice_in_dim(b_ada, me * ADA_BLK, ADA_BLK, axis=1).reshape(DEPTH, 1, ADA_BLK)
    ada_part = _ada_fwd(c_all, w_ada, b_loc, name="ada_fwd")
    ((ada_all,),) = _transfer([_gather_task([ada_part])], name="gather_ada")
    ada = lax.dynamic_index_in_dim(ada_all, me, axis=2, keepdims=False)
    ada = jnp.transpose(ada, (1, 0, 2)).reshape(DEPTH, 6, 1, D)

    bs_b = jnp.broadcast_to(sgu_b_s[..., None], (DEPTH, SGU_G, CHUNK, CHUNK))

    saved = []
    xl = xs
    h = _norm_mod(xl, g_mix[0].reshape(1, D), ada[0, 1], ada[0, 0], name="norm_mix_0")
    for l in range(DEPTH):
        w = full[l]
        sh_m, sc_m, gt_m, sh_f, sc_f, gt_f = (ada[l, k] for k in range(6))
        row = lambda a: a[l].reshape(1, D)
        z8, per = _run(_mm_cols, h, w["w_in"], b_in[l].reshape(8, 1, D), name=f"in_proj_{l}",
                       tasks=gather_at(l, "in_proj"))
        landed(l, "in_proj", per)
        (sa, sb, sc, cv), per = _run(
            _branches_fwd, z8, row(sgu_ln_g), row(sgu_ln_b), sgu_w_s[l], bs_b[l], pool_full[l], row(pool_scale),
            conv_full[l], row(conv_b), row(conv_ln_g), row(conv_ln_b), name=f"branches_{l}",
            tasks=gather_at(l, "branches"))
        landed(l, "branches", per)
        wpa, wpb, wpc, wout = (w[n].reshape(D, D) for n in PROJ)
        (ya, yb, yc, merged), per = _run(_proj_merge, sa, sb, sc, wpa, wpb, wpc, z8, name=f"proj_merge_{l}",
                                         tasks=gather_at(l, "proj_merge"))
        landed(l, "proj_merge", per)
        om, x1, h2 = _out_proj(merged, wout, xl, gt_m, row(g_ffn), sc_f, sh_f, name=f"out_proj_{l}")
        wfi = w["w_ffn_in"].reshape(2, 4, FF_BLK, D)
        (gu, f4), per = _run(_ffn_in, h2, wfi, name=f"ffn_in_{l}", tasks=gather_at(l, "ffn_in"))
        landed(l, "ffn_in", per)
        wfo4 = w["w_ffn_out"].reshape(4, FF_BLK, D)
        nxt = (g_mix[l + 1].reshape(1, D), ada[l + 1, 1], ada[l + 1, 0]) if l + 1 < DEPTH else (None, None, None)
        res, per = _run(_ffn_out, f4, wfo4, x1, gt_f, *nxt, name=f"ffn_out_{l}", tasks=gather_at(l, "ffn_out"))
        landed(l, "ffn_out", per)
        o, x2 = res[0], res[1]
        saved.append(dict(x=xl, h=h, z8=z8, sa=sa, sb=sb, sc=sc, cv=cv, ya=ya, yb=yb, yc=yc, merged=merged, om=om,
                          x1=x1, h2=h2, gu=gu, f4=f4, o=o, wpa=wpa, wpb=wpb, wpc=wpc, wout=wout, wfi=wfi, wfo4=wfo4))
        xl = x2
        h = res[2] if l + 1 < DEPTH else None

    loss_tile, dx, dg_final, do, dgt_f = _final_loss(xl, g_final.reshape(1, D), target, saved[-1]["o"],
                                                     ada[DEPTH - 1, 5], name="final_loss")
    loss_row = jnp.broadcast_to(loss_tile[0:1, 0:1], (1, D))

    chip_parts = [dict() for _ in range(DEPTH)]
    early_buf, late_buf = [None] * DEPTH, [None] * DEPTH
    early_all, late_all = [None] * DEPTH, [None] * DEPTH
    tril = jnp.tril(jnp.ones((CHUNK, CHUNK), F32))
    for l in reversed(range(DEPTH)):
        s, w = saved[l], full[l]
        above = l + 1 if l + 1 < DEPTH else None
        sh_m, sc_m, gt_m, sh_f, sc_f, gt_f = (ada[l, k] for k in range(6))
        row = lambda a: a[l].reshape(1, D)
        dgu, per = _run(_ffn_bwd_act, do, s["wfo4"], s["gu"], name=f"ffn_bwd_act_{l}",
                        tasks=[] if above is None else [_gather_task([early_buf[above], late_buf[above]])])
        if above is not None:
            early_all[above], late_all[above] = per[0]
        d_wfo = _mm_tn(s["f4"], do[None], name=f"dw_ffn_out_{l}")
        dgu8 = dgu.reshape(8, t, FF_BLK)
        dh2 = _mm_nt_sum(dgu8, w["w_ffn_in"], name=f"dh_ffn_{l}", b_is_kn=True)
        d_wfi = _mm_tn(dgu8, s["h2"][None], name=f"dw_ffn_in_{l}")
        dx1, st_f, dom, dgt_m = _norm_mod_bwd(dh2, s["x1"], dx, row(g_ffn), sc_f, s["om"], gt_m,
                                              name=f"norm_ffn_bwd_{l}")
        ffn_group = [d_wfi, d_wfo.reshape(8, D_FF // 8, D)]
        (dya, dyb, dyc, dzg, db_gate), per = _run(_merge_bwd, dom, s["wout"], s["z8"], s["ya"], s["yb"], s["yc"],
                                                  name=f"merge_bwd_{l}", tasks=[_sibling_task(ffn_group)])
        ffn_sums = _sibling_sums(ffn_group, per[0], core, tag=f"ffn_{l}")
        d_wout = _mm_tn(s["merged"][None], dom[None], name=f"dw_out_{l}")
        d_wpa = _mm_tn(s["sa"][None], dya[None], name=f"dw_pa_{l}")
        d_wpb = _mm_tn(s["sb"][None], dyb[None], name=f"dw_pb_{l}")
        d_wpc = _mm_tn(s["sc"][None], dyc[None], name=f"dw_pc_{l}")
        dsa = _mm_nt_sum(dya[None], s["wpa"][None], name=f"ds_a_{l}")
        dsb = _mm_nt_sum(dyb[None], s["wpb"][None], name=f"ds_b_{l}")
        dsc = _mm_nt_sum(dyc[None], s["wpc"][None], name=f"ds_c_{l}")
        proj_group = [g.reshape(8, D // 8, D) for g in (d_wpa, d_wpb, d_wpc, d_wout)]
        (dz8, db_in5, rows6, dws, dbs, dwp, dcw), per = _run(
            _branches_bwd, s["z8"], s["cv"], dsa, dsb, dsc, dzg, row(sgu_ln_g), row(sgu_ln_b), sgu_w_s[l], bs_b[l],
            pool_full[l], row(pool_scale), conv_full[l], row(conv_ln_g), row(conv_ln_b), name=f"branches_bwd_{l}",
            tasks=[_chips_task(ffn_sums), _sibling_task(proj_group)]
            + ([] if above is None else [_chips_task(in_sums)]))
        chip_parts[l]["ffn"] = per[0]
        proj_sums = _sibling_sums(proj_group, per[1], core, tag=f"proj_{l}")
        if above is not None:
            chip_parts[above]["in"] = per[2]
        d_win = _mm_tn(s["h"][None], dz8, name=f"dw_in_{l}")
        d_pool = jnp.transpose(dwp.reshape(4, N_DEV, POOL_GC // N_DEV, POOL_GC), (1, 0, 2, 3))
        in_group = [d_win, d_pool.reshape(N_DEV, 4 * POOL_GC // N_DEV, POOL_GC)]
        early_buf[l] = jnp.concatenate([db_in5[0:5], db_gate, rows6[0:2], _rows(dws * tril),
                                        _rows(jnp.transpose(dbs[:, :SGU_G])), rows6[2:6], st_f[2:3], dcw], axis=0)
        dh, per = _run(_mm_nt_sum, dz8, w["w_in"], name=f"dh_in_{l}",
                       tasks=[_chips_task(proj_sums), _sibling_task(in_group)]
                       + ([_gather_task([early_buf[0]])] if l == 0 else []))
        chip_parts[l]["proj"] = per[0]
        in_sums = _sibling_sums(in_group, per[1], core, tag=f"in_{l}")
        if l == 0:
            (early_all[0],) = per[2]
        gate = (saved[l - 1]["o"], ada[l - 1, 5]) if l > 0 else (None, None)
        res, per = _run(_norm_mod_bwd, dh, s["x"], dx1, row(g_mix), sc_m, *gate, name=f"norm_mix_bwd_{l}",
                        tasks=[_chips_task(in_sums)] if l == 0 else [])
        dx, st_m = res[0], res[1]
        if l == 0:
            chip_parts[0]["in"] = per[0]
        late_buf[l] = jnp.concatenate([st_m[0:1], st_m[1:2], dgt_m, st_f[0:1], st_f[1:2], dgt_f, st_m[2:3],
                                       dg_final if l == 0 else loss_row], axis=0)
        if l > 0:
            do, dgt_f = res[2], res[3]
    late_all[0] = _transfer([_gather_task([late_buf[0]])], name="gather_last")[0][0]

    early = [_sum8(early_all[l], name=f"sum_early_grads_{l}") for l in range(DEPTH)]
    late = [_sum8(late_all[l], name=f"sum_late_grads_{l}") for l in range(DEPTH)]
    layers = lambda red, lo, hi: jnp.stack([red[l][lo:hi] for l in range(DEPTH)], axis=0)
    grads = {n: layers(early, lo, hi).reshape(weights[n].shape) for n, (lo, hi) in EARLY_ROWS.items()}
    grads.update({n: layers(late, lo, hi).reshape(weights[n].shape) for n, (lo, hi) in LATE_ROWS.items()})
    grads["g_final"] = late[0][7]
    loss = late[DEPTH - 1][7, 0]
    conv_g = layers(early, CONV_ROW, CONV_ROW + CONV_K)
    grads["conv_w"] = lax.dynamic_slice_in_dim(conv_g, me * (D // N_DEV), D // N_DEV, axis=2)
    d_ada_all = jnp.stack([late_all[l][:, 0:6] for l in range(DEPTH)], axis=1).reshape(N_DEV, DEPTH, 6 * D)
    d_loc = jnp.transpose(lax.dynamic_slice_in_dim(d_ada_all, me * ADA_BLK, ADA_BLK, axis=2), (1, 0, 2))
    grads["w_ada"] = _ada_bwd(jnp.transpose(c_all), d_loc, name="ada_bwd")

    out = {}
    for n in REPLICATED + ("conv_w", "w_ada"):
        out[n] = _adam_nd(grads[n], weights[n], mom1[n], mom2[n], name=f"adam_{n}")
    for n, (group, k) in GRAD_GROUP.items():
        parts = [chip_parts[l][group][k] for l in range(DEPTH)]
        shape = (DEPTH,) + parts[0].shape[1:]
        res = _adam(parts, weights[n].reshape(shape), mom1[n].reshape(shape), mom2[n].reshape(shape),
                    name=f"adam_{n}")
        out[n] = [r.reshape(weights[n].shape) for r in res]
    out["w_ffn_in"] = [jnp.transpose(r, (0, 2, 1)) for r in out["w_ffn_in"]]

    grad_x = dx.reshape(1, t, D)
    return (loss, grad_x, *[out[n][0] for n in WEIGHT_ORDER], *[out[n][1] for n in WEIGHT_ORDER],
            *[out[n][2] for n in WEIGHT_ORDER], *[out[n][3] for n in WEIGHT_ORDER])
```

```python
import math

import jax
import jax.numpy as jnp
from jax import lax
from jax.experimental import pallas as pl
from jax.experimental.pallas import tpu as pltpu

F32 = jnp.float32
BF16 = jnp.bfloat16
MESH = pl.DeviceIdType.MESH
AXES = ("x", "y", "c")
N_DEV = 8

D = 1024
DEPTH = 2
EPS = 1e-6
CHUNK = 128
SGU_G = 8
POOL_WINDOWS = (2, 4, 8, 16)
POOL_GC = 256
CONV_K = 31
HALO = 32
SUBLANE = 8
LANE = 128
CONV_STRIP = 128
D_FF = 2816
FF_BLK = D_FF // 4
ADA_BLK = 6 * D // N_DEV

ADAM_LR = 0.001
ADAM_B1 = 0.9
ADAM_B2 = 0.999
ADAM_EPS = 1e-08
ADAM_WD = 0.01
ADAM_STEP = 10

VMEM_LIMIT_V7X = 56 * 1024 * 1024
INV_SQRT2 = 1.0 / math.sqrt(2.0)
INV_SQRT_2PI = 1.0 / math.sqrt(2.0 * math.pi)


def _params(*sem):
    return pltpu.CompilerParams(dimension_semantics=sem if sem else None, vmem_limit_bytes=VMEM_LIMIT_V7X)


def _tile(n, pref):
    if n <= pref:
        return n
    for t in range(pref - pref % 8, 0, -8):
        if n % t == 0:
            return t
    raise ValueError((n, pref))


def _sds(shape, dtype):
    return jax.ShapeDtypeStruct(shape, dtype)


def _sigmoid(x):
    return 1.0 / (1.0 + jnp.exp(-x))


def _gelu(x):
    return 0.5 * x * (1.0 + lax.erf(x * INV_SQRT2))


def _gelu_grad(x):
    return 0.5 * (1.0 + lax.erf(x * INV_SQRT2)) + x * (INV_SQRT_2PI * jnp.exp(-0.5 * x * x))


def _ln_stats(v):
    mu = jnp.mean(v, axis=-1, keepdims=True)
    vc = v - mu
    rstd = lax.rsqrt(jnp.mean(vc * vc, axis=-1, keepdims=True) + EPS)
    return vc * rstd, rstd


def _ln_bwd(dvhat, vhat, rstd):
    return rstd * (dvhat - jnp.mean(dvhat, axis=-1, keepdims=True)
                   - vhat * jnp.mean(dvhat * vhat, axis=-1, keepdims=True))


def _colsum(v):
    return jnp.sum(v, axis=0, keepdims=True)


def _dot(a, b):
    return jnp.dot(a, b, preferred_element_type=F32)


def _dot_nt(a, b):
    return lax.dot_general(a, b, (((1,), (1,)), ((), ())), preferred_element_type=F32)


def _dot_tn(a, b):
    return lax.dot_general(a, b, (((0,), (0,)), ((), ())), preferred_element_type=F32)


def _tril_mask():
    r = lax.broadcasted_iota(jnp.int32, (CHUNK, CHUNK), 0)
    c = lax.broadcasted_iota(jnp.int32, (CHUNK, CHUNK), 1)
    return (r >= c).astype(F32)


def _mesh_pos():
    return tuple(lax.axis_index(a) for a in AXES)


class _Task:
    def __init__(self, arrays, out_shapes, scratch, start, finish, relay=None, relay_at=1.0):
        self.arrays, self.out_shapes, self.scratch = arrays, out_shapes, scratch
        self.start, self.finish, self.relay = start, finish, relay or (lambda ins, outs, sems: None)
        self.relay_at = relay_at


def _hosted(tasks, body, *, name, grid, in_specs, out_specs, out_shape, scratch_shapes=()):
    single = not isinstance(out_shape, (list, tuple))
    out_shape, out_specs = ([out_shape], [out_specs]) if single else (list(out_shape), list(out_specs))
    n_in, n_out, n_scr = len(in_specs), len(out_shape), len(scratch_shapes)
    sizes = [(len(t.arrays), len(t.out_shapes), len(t.scratch)) for t in tasks]
    t_in, t_out, t_scr = (sum(s[k] for s in sizes) for k in range(3))
    any_spec = pl.BlockSpec(memory_space=pl.ANY)

    def wrapped(*refs):
        refs = list(refs)
        ins, refs = refs[:n_in + t_in], refs[n_in + t_in:]
        outs, scr = refs[:n_out + t_out], refs[n_out + t_out:]

        def per_task(fn_name, only=None):
            i0, o0, s0 = n_in, n_out, n_scr
            for t, (ni, no, ns) in zip(tasks, sizes):
                if only is None or t is only:
                    getattr(t, fn_name)(ins[i0:i0 + ni], outs[o0:o0 + no], scr[s0:s0 + ns])
                i0, o0, s0 = i0 + ni, o0 + no, s0 + ns

        if tasks and grid:
            first, last, step, total = None, None, 0, 1
            for d, g in enumerate(grid):
                f, e = pl.program_id(d) == 0, pl.program_id(d) == g - 1
                first, last = (f, e) if first is None else (first & f, last & e)
                step, total = step * g + pl.program_id(d), total * g
            pl.when(first)(lambda: per_task("start"))
            for t in tasks:
                pl.when(step == min(int(t.relay_at * total), total - 1))(lambda t=t: per_task("relay", only=t))
        elif tasks:
            per_task("start")
            per_task("relay")
        body(*ins[:n_in], *outs[:n_out], *scr[:n_scr])
        if tasks and grid:
            pl.when(last)(lambda: per_task("finish"))
        elif tasks:
            per_task("finish")

    call = pl.pallas_call(
        wrapped, name=name, grid=grid,
        in_specs=list(in_specs) + [any_spec] * t_in, out_specs=out_specs + [any_spec] * t_out,
        out_shape=out_shape + [s for t in tasks for s in t.out_shapes],
        scratch_shapes=list(scratch_shapes) + [s for t in tasks for s in t.scratch],
        compiler_params=_params(*(("arbitrary",) * len(grid))))

    def run(*operands):
        res = list(call(*operands, *[a for t in tasks for a in t.arrays]))
        host, rest, per = res[:n_out], res[n_out:], []
        for _, no, _ in sizes:
            per.append(rest[:no])
            rest = rest[no:]
        return (host[0] if single else host), per

    return run


def _transfer(tasks, name):
    return _hosted(tasks, lambda: None, name=name, grid=(), in_specs=[], out_specs=[], out_shape=[])()[1]


def _gather_task(arrs, relay_at=0.75):
    n = len(arrs)

    def plan(ins, outs, sems):
        send_sems, recv_sems, local_sems = sems
        x, y, c = _mesh_pos()
        me, sibling = (x, y, c), (x, y, 1 - c)
        chips = [(1 - x, y), (x, 1 - y), (1 - x, 1 - y)]

        def slot(a, p):
            return outs[a].at[4 * p[0] + 2 * p[1] + p[2]]

        def copy(a, k, block, to, src=None):
            dst = slot(a, block)
            return pltpu.make_async_remote_copy(
                src_ref=dst if src is None else src, dst_ref=dst, send_sem=send_sems.at[a, k],
                recv_sem=recv_sems.at[a, k], device_id=to, device_id_type=MESH)

        def own_block_copies():
            mine = [pltpu.make_async_copy(ins[a], slot(a, me), local_sems.at[a]) for a in range(n)]
            first = []
            for a in range(n):
                first.append(copy(a, 0, me, sibling, src=ins[a]))
                first += [copy(a, 1 + j, me, (*chip, c), src=ins[a]) for j, chip in enumerate(chips)]
            return mine, first

        return c, me, sibling, chips, copy, own_block_copies

    def start(ins, outs, sems):
        mine, first = plan(ins, outs, sems)[-1]()
        for cp in mine + first:
            cp.start()

    def relay(ins, outs, sems):
        c, me, sibling, chips, copy, _ = plan(ins, outs, sems)
        for j, chip in enumerate(chips):
            for a in range(n):
                copy(a, 1 + j, (*chip, c), me).wait_recv()
                copy(a, 4 + j, (*chip, c), sibling).start()

    def finish(ins, outs, sems):
        c, me, sibling, chips, copy, own_block_copies = plan(ins, outs, sems)
        mine, first = own_block_copies()
        passed = [copy(a, 4 + j, (*chip, c), sibling) for j, chip in enumerate(chips) for a in range(n)]
        for a in range(n):
            copy(a, 0, sibling, me).wait_recv()
            for j, chip in enumerate(chips):
                copy(a, 4 + j, (*chip, 1 - c), me).wait_recv()
        for cp in first + passed:
            cp.wait_send()
        for m in mine:
            m.wait()

    return _Task(list(arrs), [_sds((N_DEV,) + a.shape, a.dtype) for a in arrs],
                 [pltpu.SemaphoreType.DMA((n, 7)), pltpu.SemaphoreType.DMA((n, 7)), pltpu.SemaphoreType.DMA((n,))],
                 start, finish, relay, relay_at)


def _sibling_task(arrs):
    n = len(arrs)

    def copies(ins, outs, sems):
        send_sems, recv_sems = sems
        x, y, c = _mesh_pos()
        return [pltpu.make_async_remote_copy(
            src_ref=ins[a].at[2 * q + (1 - c)], dst_ref=outs[a].at[q], send_sem=send_sems.at[a, q],
            recv_sem=recv_sems.at[a, q], device_id=(x, y, 1 - c), device_id_type=MESH)
            for a in range(n) for q in range(4)]

    def start(ins, outs, sems):
        for cp in copies(ins, outs, sems):
            cp.start()

    def finish(ins, outs, sems):
        for cp in copies(ins, outs, sems):
            cp.wait()

    return _Task(list(arrs), [_sds((4,) + a.shape[1:], a.dtype) for a in arrs],
                 [pltpu.SemaphoreType.DMA((n, 4)), pltpu.SemaphoreType.DMA((n, 4))], start, finish)


def _chips_task(arrs):
    n = len(arrs)

    def copies(ins, outs, sems):
        send_sems, recv_sems, local_sems = sems
        x, y, c = _mesh_pos()
        q_me = 2 * x + y
        chips = [(1 - x, y), (x, 1 - y), (1 - x, 1 - y)]
        own = [pltpu.make_async_copy(ins[a].at[q_me], outs[a].at[q_me], local_sems.at[a]) for a in range(n)]
        remote = [pltpu.make_async_remote_copy(
            src_ref=ins[a].at[2 * chip[0] + chip[1]], dst_ref=outs[a].at[q_me], send_sem=send_sems.at[a, j],
            recv_sem=recv_sems.at[a, j], device_id=(*chip, c), device_id_type=MESH)
            for a in range(n) for j, chip in enumerate(chips)]
        return own + remote

    def start(ins, outs, sems):
        for cp in copies(ins, outs, sems):
            cp.start()

    def finish(ins, outs, sems):
        for cp in copies(ins, outs, sems):
            cp.wait()

    return _Task(list(arrs), [_sds(a.shape, a.dtype) for a in arrs],
                 [pltpu.SemaphoreType.DMA((n, 3)), pltpu.SemaphoreType.DMA((n, 3)), pltpu.SemaphoreType.DMA((n,))],
                 start, finish)


def _sibling_sum(arr, land, core, name):
    _, rows, cols = arr.shape
    tr = _tile(rows, 512)
    arr4 = arr.reshape(4, 2, rows, cols)

    def body(c_ref, a_ref, l_ref, o_ref):
        o_ref[...] = (a_ref[...] + l_ref[...]).astype(BF16)

    grid_spec = pltpu.PrefetchScalarGridSpec(
        num_scalar_prefetch=1, grid=(4, rows // tr),
        in_specs=[pl.BlockSpec((None, None, tr, cols), lambda q, r, c_ref: (q, c_ref[0], r, 0)),
                  pl.BlockSpec((None, tr, cols), lambda q, r, c_ref: (q, r, 0))],
        out_specs=pl.BlockSpec((None, tr, cols), lambda q, r, c_ref: (q, r, 0)))
    return pl.pallas_call(body, name=name, grid_spec=grid_spec, out_shape=_sds((4, rows, cols), BF16),
                          compiler_params=_params("arbitrary", "arbitrary"))(core, arr4, land)


def _sibling_sums(arrs, land, core, tag):
    return [_sibling_sum(a, l, core, name=f"rs_sum_{tag}_{k}") for k, (a, l) in enumerate(zip(arrs, land))]


def _adam(gparts, w, m, v, name):
    n_l = len(gparts)
    p, rows, cols = gparts[0].shape
    tr = _tile(rows, 256)
    n_r = rows // tr
    c1 = 1.0 - ADAM_B1 ** ADAM_STEP
    c2 = 1.0 - ADAM_B2 ** ADAM_STEP

    def body(*refs):
        g_refs = refs[:n_l]
        w_ref, m_ref, v_ref, go_ref, d_ref, mo_ref, vo_ref = refs[n_l:]
        layer = pl.program_id(0)
        g = jnp.zeros((tr, cols), F32)
        for li, g_ref in enumerate(g_refs):
            gl = g_ref[0].astype(F32)
            for k in range(1, p):
                gl = gl + g_ref[k].astype(F32)
            g = gl if n_l == 1 else jnp.where(layer == li, gl, g)
        m_new = ADAM_B1 * m_ref[...] + (1.0 - ADAM_B1) * g
        v_new = ADAM_B2 * v_ref[...] + (1.0 - ADAM_B2) * (g * g)
        m_hat = m_new / c1
        v_hat = v_new / c2
        go_ref[...] = g
        d_ref[...] = -ADAM_LR * (m_hat / (jnp.sqrt(v_hat) + ADAM_EPS) + ADAM_WD * w_ref[...])
        mo_ref[...] = m_new
        vo_ref[...] = v_new

    def g_spec(li):
        def index(l, r):
            return (0, jnp.where(l == li, r, jnp.where(l < li, 0, n_r - 1)), 0)
        return pl.BlockSpec((p, tr, cols), index)

    blk = pl.BlockSpec((None, tr, cols), lambda l, r: (l, r, 0))
    return pl.pallas_call(
        body, name=name, grid=(n_l, n_r),
        in_specs=[g_spec(li) for li in range(n_l)] + [blk, blk, blk],
        out_specs=[blk] * 4, out_shape=[_sds((n_l, rows, cols), F32)] * 4,
        compiler_params=_params("arbitrary", "arbitrary"))(*gparts, w, m, v)


def _adam_nd(grad, w, m, v, name):
    shape = w.shape
    cols = shape[-1]
    rows = w.size // cols
    as_rows = lambda a: a.reshape(1, rows, cols)
    out = _adam([as_rows(grad)], as_rows(w), as_rows(m), as_rows(v), name)
    return [o.reshape(shape) for o in out]


def _norm_mod(x, g, sc, sh, name):
    t = x.shape[0]
    tm = _tile(t, 512)

    def body(x_ref, g_ref, sc_ref, sh_ref, h_ref):
        h_ref[...] = _modulated_norm(x_ref[...], g_ref[...], sc_ref[...], sh_ref[...])

    row = pl.BlockSpec((1, D), lambda i: (0, 0))
    blk = pl.BlockSpec((tm, D), lambda i: (i, 0))
    return pl.pallas_call(body, name=name, grid=(t // tm,), in_specs=[blk, row, row, row], out_specs=blk,
                          out_shape=_sds((t, D), BF16), compiler_params=_params("arbitrary"))(x, g, sc, sh)


def _with_tasks(res_per, tasks):
    return res_per if tasks else res_per[0]


def _mm_cols(a, b8, bias8, name, tasks=()):
    t, k = a.shape
    j, _, n = b8.shape
    tm = _tile(t, 1024)

    def body(a_ref, b_ref, bias_ref, o_ref):
        o_ref[...] = (_dot(a_ref[...], b_ref[...]) + bias_ref[...]).astype(BF16)

    return _with_tasks(_hosted(
        tasks, body, name=name, grid=(j, t // tm),
        in_specs=[pl.BlockSpec((tm, k), lambda jj, i: (i, 0)),
                  pl.BlockSpec((None, k, n), lambda jj, i: (jj, 0, 0)),
                  pl.BlockSpec((None, 1, n), lambda jj, i: (jj, 0, 0))],
        out_specs=pl.BlockSpec((None, tm, n), lambda jj, i: (jj, i, 0)),
        out_shape=_sds((j, t, n), BF16))(a, b8, bias8), tasks)


def _halo_before(tm, col):
    return pl.BlockSpec((None, HALO, D), lambda i: (col, jnp.maximum(i * (tm // HALO) - 1, 0), 0))


def _pool_forward(p_ext, t0, rows):
    t = t0 + lax.broadcasted_iota(jnp.int32, (rows, 1), 0)
    out = []
    for gi, win in enumerate(POOL_WINDOWS):
        e = p_ext[:, gi * POOL_GC:(gi + 1) * POOL_GC]
        s, sh = e, 1
        while sh < win:
            s = s + pltpu.roll(s, sh, 0)
            sh *= 2
        cnt = jnp.minimum(t + 1, win).astype(F32)
        out.append(s[HALO:] / cnt - e[HALO:])
    return out


def _fill_shift_bank(bank_ref, ext, causal):
    n = ext.shape[0]
    bank_ref[0] = ext
    for b in range(1, SUBLANE):
        bank_ref[b] = pltpu.roll(ext, b if causal else n - b, 0)


def _branches_fwd(z8, ln_g, ln_b, w_s, bs_b, pool_w, pool_scale, conv_w, conv_b, cln_g, cln_b, name, tasks=()):
    t = z8.shape[1]
    tm = _tile(t, 256)
    n_ext = tm + HALO

    def body(zu_ref, zv_ref, p_ref, a_ref, ag_ref, ph_ref, ah_ref, agh_ref, lng_ref, lnb_ref, ws_ref, bsb_ref,
             wp_ref, ps_ref, cw_ref, cb_ref, clg_ref, clb_ref, sa_ref, sb_ref, sc_ref, cv_ref, bank_ref):
        i = pl.program_id(0)
        has_past = (i > 0).astype(F32)
        u = _gelu(zu_ref[...].astype(F32))
        vhat, _ = _ln_stats(_gelu(zv_ref[...].astype(F32)))
        vb = (vhat * lng_ref[...] + lnb_ref[...]).astype(BF16)
        mask = _tril_mask()
        for g in range(SGU_G):
            cols = slice(g * CHUNK, (g + 1) * CHUNK)
            wm = (ws_ref[g] * mask).astype(BF16)
            for n in range(tm // CHUNK):
                rows = slice(n * CHUNK, (n + 1) * CHUNK)
                mixed = _dot(wm, vb[rows, cols]) + bsb_ref[g]
                sa_ref[rows, cols] = (u[rows, cols] * mixed).astype(BF16)
        p_ext = jnp.concatenate([ph_ref[...].astype(F32) * has_past, p_ref[...].astype(F32)], axis=0)
        pooled = _pool_forward(p_ext, i * tm, tm)
        for gi in range(len(POOL_WINDOWS)):
            cols = slice(gi * POOL_GC, (gi + 1) * POOL_GC)
            y = _dot(pooled[gi].astype(BF16), wp_ref[gi].astype(BF16))
            sb_ref[:, cols] = (y * ps_ref[:, cols]).astype(BF16)
        for cb in range(D // LANE):
            cols = slice(cb * LANE, (cb + 1) * LANE)
            zc = jnp.concatenate(
                [ah_ref[:, cols].astype(F32) * has_past * _sigmoid(agh_ref[:, cols].astype(F32)),
                 a_ref[:, cols].astype(F32) * _sigmoid(ag_ref[:, cols].astype(F32))], axis=0)
            _fill_shift_bank(bank_ref, zc, causal=True)
            for r0 in range(0, tm, CONV_STRIP):
                acc = jnp.zeros((CONV_STRIP, LANE), F32) + cb_ref[:, cols]
                for k in range(CONV_K):
                    hi, lo = divmod(CONV_K - 1 - k, SUBLANE)
                    acc = acc + cw_ref[k:k + 1, cols] * bank_ref[lo, pl.ds(HALO - SUBLANE * hi + r0, CONV_STRIP), :]
                cv_ref[r0:r0 + CONV_STRIP, cols] = acc
        cv = cv_ref[...]
        chat, _ = _ln_stats(cv)
        cl = chat * clg_ref[...] + clb_ref[...]
        sc_ref[...] = (cl * _sigmoid(cl)).astype(BF16)

    def col(j):
        return pl.BlockSpec((None, tm, D), lambda i: (j, i, 0))

    row = pl.BlockSpec((1, D), lambda i: (0, 0))
    full3 = lambda s: pl.BlockSpec(s, lambda i: (0, 0, 0))
    blk = pl.BlockSpec((tm, D), lambda i: (i, 0))
    return _with_tasks(_hosted(
        tasks, body, name=name, grid=(t // tm,),
        in_specs=[col(0), col(1), col(2), col(3), col(4), _halo_before(tm, 2), _halo_before(tm, 3),
                  _halo_before(tm, 4), row, row, full3((SGU_G, CHUNK, CHUNK)), full3((SGU_G, CHUNK, CHUNK)),
                  full3((4, POOL_GC, POOL_GC)), row, pl.BlockSpec((HALO, D), lambda i: (0, 0)), row, row, row],
        out_specs=[blk, blk, blk, blk],
        out_shape=[_sds((t, D), BF16)] * 3 + [_sds((t, D), F32)],
        scratch_shapes=[pltpu.VMEM((SUBLANE, n_ext, LANE), F32)],
    )(z8, z8, z8, z8, z8, z8, z8, z8, ln_g, ln_b, w_s, bs_b, pool_w, pool_scale, conv_w, conv_b, cln_g, cln_b), tasks)


def _proj_merge(sa, sb, sc, w_pa, w_pb, w_pc, z8, name, tasks=()):
    t = sa.shape[0]
    tm = _tile(t, 512)

    def body(sa_ref, sb_ref, sc_ref, wa_ref, wb_ref, wc_ref, g0_ref, g1_ref, g2_ref, ya_ref, yb_ref, yc_ref, m_ref):
        merged = jnp.zeros((tm, D), F32)
        for s_ref, w_ref, g_ref, y_ref in ((sa_ref, wa_ref, g0_ref, ya_ref), (sb_ref, wb_ref, g1_ref, yb_ref),
                                           (sc_ref, wc_ref, g2_ref, yc_ref)):
            y = _dot(s_ref[...], w_ref[...])
            y_ref[...] = y.astype(BF16)
            merged = merged + _sigmoid(g_ref[...].astype(F32)) * y
        m_ref[...] = merged.astype(BF16)

    blk = pl.BlockSpec((tm, D), lambda i: (i, 0))
    wspec = pl.BlockSpec((D, D), lambda i: (0, 0))
    gate = lambda j: pl.BlockSpec((None, tm, D), lambda i: (j, i, 0))
    return _with_tasks(_hosted(
        tasks, body, name=name, grid=(t // tm,),
        in_specs=[blk, blk, blk, wspec, wspec, wspec, gate(5), gate(6), gate(7)],
        out_specs=[blk] * 4, out_shape=[_sds((t, D), BF16)] * 4)(sa, sb, sc, w_pa, w_pb, w_pc, z8, z8, z8), tasks)


def _modulated_norm(xv, g, sc, sh):
    r = lax.rsqrt(jnp.mean(xv * xv, axis=-1, keepdims=True) + EPS)
    return (xv * r * g * (1.0 + sc) + sh).astype(BF16)


def _out_proj(merged, w_out, x, gt, g, sc, sh, name):
    t = x.shape[0]
    tm = _tile(t, 512)

    def body(m_ref, w_ref, x_ref, gt_ref, g_ref, sc_ref, sh_ref, om_ref, x1_ref, h2_ref):
        om = _dot(m_ref[...], w_ref[...])
        om_ref[...] = om
        x1 = x_ref[...] + gt_ref[...] * om
        x1_ref[...] = x1
        h2_ref[...] = _modulated_norm(x1, g_ref[...], sc_ref[...], sh_ref[...])

    blk = pl.BlockSpec((tm, D), lambda i: (i, 0))
    row = pl.BlockSpec((1, D), lambda i: (0, 0))
    return pl.pallas_call(
        body, name=name, grid=(t // tm,),
        in_specs=[blk, pl.BlockSpec((D, D), lambda i: (0, 0)), blk, row, row, row, row],
        out_specs=[blk, blk, blk], out_shape=[_sds((t, D), F32)] * 2 + [_sds((t, D), BF16)],
        compiler_params=_params("arbitrary"))(merged, w_out, x, gt, g, sc, sh)


def _ffn_in(h2, wfi, name, tasks=()):
    t = h2.shape[0]
    tm = _tile(t, 512)

    def body(h_ref, w_ref, gu_ref, f_ref):
        hv = h_ref[...]
        gp = _dot_nt(hv, w_ref[0])
        up = _dot_nt(hv, w_ref[1])
        gu_ref[0] = gp.astype(BF16)
        gu_ref[1] = up.astype(BF16)
        f_ref[...] = (gp * _sigmoid(gp) * up).astype(BF16)

    return _with_tasks(_hosted(
        tasks, body, name=name, grid=(4, t // tm),
        in_specs=[pl.BlockSpec((tm, D), lambda j, i: (i, 0)),
                  pl.BlockSpec((2, None, FF_BLK, D), lambda j, i: (0, j, 0, 0))],
        out_specs=[pl.BlockSpec((2, None, tm, FF_BLK), lambda j, i: (0, j, i, 0)),
                   pl.BlockSpec((None, tm, FF_BLK), lambda j, i: (j, i, 0))],
        out_shape=[_sds((2, 4, t, FF_BLK), BF16), _sds((4, t, FF_BLK), BF16)])(h2, wfi), tasks)


def _ffn_out(f4, wfo4, x1, gt, g, sc, sh, name, tasks=()):
    t = x1.shape[0]
    tm = _tile(t, 512)
    with_norm = g is not None

    def body(f_ref, w_ref, x_ref, gt_ref, *rest):
        o_ref, x2_ref = rest[-3:-1] if with_norm else rest[-2:]
        j = pl.program_id(1)

        @pl.when(j == 0)
        def _():
            o_ref[...] = jnp.zeros_like(o_ref)

        o_ref[...] += _dot(f_ref[...], w_ref[...])

        @pl.when(j == 3)
        def _():
            x2 = x_ref[...] + gt_ref[...] * o_ref[...]
            x2_ref[...] = x2
            if with_norm:
                g_ref, sc_ref, sh_ref = rest[:3]
                rest[-1][...] = _modulated_norm(x2, g_ref[...], sc_ref[...], sh_ref[...])

    blk = pl.BlockSpec((tm, D), lambda i, j: (i, 0))
    row = pl.BlockSpec((1, D), lambda i, j: (0, 0))
    norm_args = [g, sc, sh] if with_norm else []
    return _with_tasks(_hosted(
        tasks, body, name=name, grid=(t // tm, 4),
        in_specs=[pl.BlockSpec((None, tm, FF_BLK), lambda i, j: (j, i, 0)),
                  pl.BlockSpec((None, FF_BLK, D), lambda i, j: (j, 0, 0)), blk, row] + [row] * len(norm_args),
        out_specs=[blk, blk] + [blk] * with_norm,
        out_shape=[_sds((t, D), F32)] * 2 + [_sds((t, D), BF16)] * with_norm)(f4, wfo4, x1, gt, *norm_args), tasks)


def _gate_grads(dx, o_ref, gt_ref, do_ref, dgt_ref):
    do_ref[...] = (dx * gt_ref[...]).astype(BF16)
    dgt_ref[...] += _colsum(dx * o_ref[...])


def _final_loss(x, g, target, o, gt, name):
    t = x.shape[0]
    tm = _tile(t, 512)

    def body(x_ref, g_ref, t_ref, o_ref, gt_ref, loss_ref, dx_ref, dg_ref, do_ref, dgt_ref):
        @pl.when(pl.program_id(0) == 0)
        def _():
            for ref in (loss_ref, dg_ref, dgt_ref):
                ref[...] = jnp.zeros_like(ref)

        xv = x_ref[...]
        r = lax.rsqrt(jnp.mean(xv * xv, axis=-1, keepdims=True) + EPS)
        xn = xv * r
        diff = xn * g_ref[...] - t_ref[...]
        loss_ref[...] += 0.5 * jnp.sum(jnp.mean(diff * diff, axis=-1, keepdims=True))
        dy = diff * (1.0 / D)
        dg_ref[...] += _colsum(dy * xn)
        dxn = dy * g_ref[...]
        dx = r * (dxn - xn * jnp.mean(dxn * xn, axis=-1, keepdims=True))
        dx_ref[...] = dx
        _gate_grads(dx, o_ref, gt_ref, do_ref, dgt_ref)

    blk = pl.BlockSpec((tm, D), lambda i: (i, 0))
    row = pl.BlockSpec((1, D), lambda i: (0, 0))
    return pl.pallas_call(
        body, name=name, grid=(t // tm,), in_specs=[blk, row, blk, blk, row],
        out_specs=[pl.BlockSpec((8, 128), lambda i: (0, 0)), blk, row, blk, row],
        out_shape=[_sds((8, 128), F32), _sds((t, D), F32), _sds((1, D), F32), _sds((t, D), BF16), _sds((1, D), F32)],
        compiler_params=_params("arbitrary"))(x, g, target, o, gt)


def _norm_mod_bwd(dh, x, dres, g, sc, o, gt, name, tasks=()):
    t = x.shape[0]
    tm = _tile(t, 512)
    with_gate = o is not None

    def body(dh_ref, x_ref, dr_ref, g_ref, sc_ref, *rest):
        dx_ref, st_ref = rest[2:4] if with_gate else rest

        @pl.when(pl.program_id(0) == 0)
        def _():
            st_ref[...] = jnp.zeros_like(st_ref)
            if with_gate:
                rest[5][...] = jnp.zeros_like(rest[5])

        xv, dhv = x_ref[...], dh_ref[...]
        r = lax.rsqrt(jnp.mean(xv * xv, axis=-1, keepdims=True) + EPS)
        xn = xv * r
        gv, mod = g_ref[...], 1.0 + sc_ref[...]
        st_ref[0:1, :] += _colsum(dhv)
        st_ref[1:2, :] += _colsum(dhv * xn * gv)
        st_ref[2:3, :] += _colsum(dhv * xn * mod)
        dxn = dhv * gv * mod
        dx = dr_ref[...] + r * (dxn - xn * jnp.mean(dxn * xn, axis=-1, keepdims=True))
        dx_ref[...] = dx
        if with_gate:
            _gate_grads(dx, rest[0], rest[1], rest[4], rest[5])

    blk = pl.BlockSpec((tm, D), lambda i: (i, 0))
    row = pl.BlockSpec((1, D), lambda i: (0, 0))
    gate_args = [o, gt] if with_gate else []
    return _with_tasks(_hosted(
        tasks, body, name=name, grid=(t // tm,), in_specs=[blk, blk, blk, row, row] + [blk, row] * with_gate,
        out_specs=[blk, pl.BlockSpec((3, D), lambda i: (0, 0))] + [blk, row] * with_gate,
        out_shape=[_sds((t, D), F32), _sds((3, D), F32)] + [_sds((t, D), BF16), _sds((1, D), F32)] * with_gate,
    )(dh, x, dres, g, sc, *gate_args), tasks)


def _ffn_bwd_act(do, wfo4, gu, name, tasks=()):
    t = do.shape[0]
    tm = _tile(t, 512)

    def body(do_ref, w_ref, gu_ref, dgu_ref):
        df = _dot_nt(do_ref[...], w_ref[...])
        gp, up = gu_ref[0].astype(F32), gu_ref[1].astype(F32)
        sg = _sigmoid(gp)
        dgu_ref[0] = (df * up * (sg * (1.0 + gp * (1.0 - sg)))).astype(BF16)
        dgu_ref[1] = (df * (gp * sg)).astype(BF16)

    gu_spec = pl.BlockSpec((2, None, tm, FF_BLK), lambda j, i: (0, j, i, 0))
    return _with_tasks(_hosted(
        tasks, body, name=name, grid=(4, t // tm),
        in_specs=[pl.BlockSpec((tm, D), lambda j, i: (i, 0)),
                  pl.BlockSpec((None, FF_BLK, D), lambda j, i: (j, 0, 0)), gu_spec],
        out_specs=gu_spec, out_shape=_sds((2, 4, t, FF_BLK), BF16))(do, wfo4, gu), tasks)


def _mm_nt_sum(a8, b8, name, b_is_kn=False, tasks=()):
    j, t, k = a8.shape
    n = b8.shape[2] if b_is_kn else b8.shape[1]
    tm = _tile(t, 1024 if j > 1 else 512)
    out_dtype = F32 if j > 1 else BF16
    dot = _dot if b_is_kn else _dot_nt

    def body(a_ref, b_ref, o_ref):
        if j == 1:
            o_ref[...] = dot(a_ref[...], b_ref[...]).astype(out_dtype)
            return

        @pl.when(pl.program_id(1) == 0)
        def _():
            o_ref[...] = jnp.zeros_like(o_ref)

        o_ref[...] += dot(a_ref[...], b_ref[...])

    return _with_tasks(_hosted(
        tasks, body, name=name, grid=(t // tm, j),
        in_specs=[pl.BlockSpec((None, tm, k), lambda i, jj: (jj, i, 0)),
                  pl.BlockSpec((None,) + b8.shape[1:], lambda i, jj: (jj, 0, 0))],
        out_specs=pl.BlockSpec((tm, n), lambda i, jj: (i, 0)), out_shape=_sds((t, n), out_dtype))(a8, b8), tasks)


def _mm_tn(a8, b8, name, tasks=()):
    ja, t, m = a8.shape
    jb, _, n = b8.shape
    j = max(ja, jb)
    tk = _tile(t, 1024)

    def body(a_ref, b_ref, o_ref):
        @pl.when(pl.program_id(1) == 0)
        def _():
            o_ref[...] = jnp.zeros_like(o_ref)

        o_ref[...] += _dot_tn(a_ref[...], b_ref[...])

    return _with_tasks(_hosted(
        tasks, body, name=name, grid=(j, t // tk),
        in_specs=[pl.BlockSpec((None, tk, m), (lambda jj, kk: (jj, kk, 0)) if ja > 1 else (lambda jj, kk: (0, kk, 0))),
                  pl.BlockSpec((None, tk, n), (lambda jj, kk: (jj, kk, 0)) if jb > 1 else (lambda jj, kk: (0, kk, 0)))],
        out_specs=pl.BlockSpec((None, m, n), lambda jj, kk: (jj, 0, 0)), out_shape=_sds((j, m, n), F32))(a8, b8),
        tasks)


def _merge_bwd(dom, w_out, z8, ya, yb, yc, name, tasks=()):
    t = dom.shape[0]
    tm = _tile(t, 512)

    def body(dom_ref, w_ref, g0_ref, g1_ref, g2_ref, ya_ref, yb_ref, yc_ref, dya_ref, dyb_ref, dyc_ref, dzg_ref,
             db_ref):
        @pl.when(pl.program_id(0) == 0)
        def _():
            db_ref[...] = jnp.zeros_like(db_ref)

        dm = _dot_nt(dom_ref[...], w_ref[...])
        for k, (g_ref, y_ref, dy_ref) in enumerate(((g0_ref, ya_ref, dya_ref), (g1_ref, yb_ref, dyb_ref),
                                                    (g2_ref, yc_ref, dyc_ref))):
            sg = _sigmoid(g_ref[...].astype(F32))
            dy_ref[...] = (dm * sg).astype(BF16)
            dzg = dm * y_ref[...].astype(F32) * (sg * (1.0 - sg))
            dzg_ref[k] = dzg.astype(BF16)
            db_ref[k:k + 1, :] += _colsum(dzg)

    blk = pl.BlockSpec((tm, D), lambda i: (i, 0))
    gate = lambda j: pl.BlockSpec((None, tm, D), lambda i: (j, i, 0))
    return _with_tasks(_hosted(
        tasks, body, name=name, grid=(t // tm,),
        in_specs=[blk, pl.BlockSpec((D, D), lambda i: (0, 0)), gate(5), gate(6), gate(7), blk, blk, blk],
        out_specs=[blk, blk, blk, pl.BlockSpec((3, tm, D), lambda i: (0, i, 0)), pl.BlockSpec((3, D), lambda i: (0, 0))],
        out_shape=[_sds((t, D), BF16)] * 3 + [_sds((3, t, D), BF16), _sds((3, D), F32)],
    )(dom, w_out, z8, z8, z8, ya, yb, yc), tasks)


def _branches_bwd(z8, cv, dsa, dsb, dsc, dzg, ln_g, ln_b, w_s, bs_b, pool_w, pool_scale, conv_w, cln_g, cln_b, name,
                  tasks=()):
    t = z8.shape[1]
    tm = _tile(t, 256)
    n_ext = tm + HALO
    n_tiles = t // tm

    def body(zu_ref, zv_ref, p_ref, a_ref, ag_ref, ph_ref, ah_ref, agh_ref, cv_ref, cvf_ref, dsa_ref, dsb_ref,
             dsbf_ref, dsc_ref, dscf_ref, dzg_ref, lng_ref, lnb_ref, ws_ref, bsb_ref, wp_ref, ps_ref, cw_ref,
             clg_ref, clb_ref, dz_ref, dbin_ref, rows_ref, dws_ref, dbs_ref, dwp_ref, dcw_ref, mixed_scr, dvln_scr,
             dcv_scr, zbank_ref, dbank_ref, dcw8_scr):
        i = pl.program_id(0)

        @pl.when(i == 0)
        def _():
            for ref in (dbin_ref, rows_ref, dws_ref, dbs_ref, dwp_ref, dcw8_scr):
                ref[...] = jnp.zeros_like(ref)

        has_past = (i > 0).astype(F32)
        has_next = (i < n_tiles - 1).astype(F32)

        def emit(j, val):
            dz_ref[j] = val.astype(BF16)
            dbin_ref[j:j + 1, :] += _colsum(val)

        zu, zv = zu_ref[...].astype(F32), zv_ref[...].astype(F32)
        u = _gelu(zu)
        vhat, v_rstd = _ln_stats(_gelu(zv))
        vb = (vhat * lng_ref[...] + lnb_ref[...]).astype(BF16)
        dsa = dsa_ref[...].astype(F32)
        dmixed = dsa * u
        dmb = dmixed.astype(BF16)
        mask = _tril_mask()
        lane = lax.broadcasted_iota(jnp.int32, (CHUNK, CHUNK), 1)
        for g in range(SGU_G):
            cols = slice(g * CHUNK, (g + 1) * CHUNK)
            wm = (ws_ref[g] * mask).astype(BF16)
            dws = jnp.zeros((CHUNK, CHUNK), F32)
            dbs = jnp.zeros((CHUNK, 1), F32)
            for n in range(tm // CHUNK):
                rows = slice(n * CHUNK, (n + 1) * CHUNK)
                mixed_scr[rows, cols] = _dot(wm, vb[rows, cols]) + bsb_ref[g]
                dvln_scr[rows, cols] = _dot_tn(wm, dmb[rows, cols])
                dws = dws + _dot_nt(dmb[rows, cols], vb[rows, cols])
                dbs = dbs + jnp.sum(dmixed[rows, cols], axis=1, keepdims=True)
            dws_ref[g] += dws
            dbs_ref[...] += jnp.where(lane == g, dbs, 0.0)
        emit(0, dsa * mixed_scr[...] * _gelu_grad(zu))
        dvln = dvln_scr[...]
        rows_ref[0:1, :] += _colsum(dvln * vhat)
        rows_ref[1:2, :] += _colsum(dvln)
        emit(1, _ln_bwd(dvln * lng_ref[...], vhat, v_rstd) * _gelu_grad(zv))

        p_ext = jnp.concatenate([ph_ref[...].astype(F32) * has_past, p_ref[...].astype(F32)], axis=0)
        pooled = _pool_forward(p_ext, i * tm, tm)
        dsb = dsb_ref[...].astype(F32)
        dpl_ext = jnp.concatenate([dsb, dsbf_ref[...].astype(F32) * has_next], axis=0) * ps_ref[...]
        t_ext = i * tm + lax.broadcasted_iota(jnp.int32, (n_ext, 1), 0)
        dp_parts = []
        for gi, win in enumerate(POOL_WINDOWS):
            cols = slice(gi * POOL_GC, (gi + 1) * POOL_GC)
            pooled_b = pooled[gi].astype(BF16)
            wpb = wp_ref[gi].astype(BF16)
            rows_ref[2:3, cols] += _colsum(dsb[:, cols] * _dot(pooled_b, wpb))
            dplb = dpl_ext[:, cols].astype(BF16)
            dwp_ref[gi] += _dot_tn(pooled_b, dplb[:tm])
            dpooled = _dot_nt(dplb, wpb)
            s, sh = dpooled / jnp.minimum(t_ext + 1, win).astype(F32), 1
            while sh < win:
                s = s + pltpu.roll(s, n_ext - sh, 0)
                sh *= 2
            dp_parts.append(s[:tm] - dpooled[:tm])
        emit(2, jnp.concatenate(dp_parts, axis=1))

        cv_ext = jnp.concatenate([cv_ref[...], cvf_ref[...]], axis=0)
        chat, c_rstd = _ln_stats(cv_ext)
        cl = chat * clg_ref[...] + clb_ref[...]
        sg = _sigmoid(cl)
        dsc_ext = jnp.concatenate([dsc_ref[...].astype(F32), dscf_ref[...].astype(F32)], axis=0)
        dcl = dsc_ext * (sg * (1.0 + cl * (1.0 - sg)))
        rows_ref[4:5, :] += _colsum((dcl * chat)[:tm])
        rows_ref[5:6, :] += _colsum(dcl[:tm])
        in_seq = jnp.concatenate([jnp.ones((tm, 1), F32), jnp.zeros((HALO, 1), F32) + has_next], axis=0)
        dcv = jnp.where(in_seq > 0.0, _ln_bwd(dcl * clg_ref[...], chat, c_rstd), 0.0)
        rows_ref[3:4, :] += _colsum(dcv[:tm])
        dcv_scr[...] = dcv
        for cb in range(D // LANE):
            cols = slice(cb * LANE, (cb + 1) * LANE)
            zc = jnp.concatenate(
                [ah_ref[:, cols].astype(F32) * has_past * _sigmoid(agh_ref[:, cols].astype(F32)),
                 a_ref[:, cols].astype(F32) * _sigmoid(ag_ref[:, cols].astype(F32))], axis=0)
            _fill_shift_bank(zbank_ref, zc, causal=True)
            _fill_shift_bank(dbank_ref, dcv_scr[:, cols], causal=False)
            for r0 in range(0, tm, CONV_STRIP):
                rows = slice(r0, r0 + CONV_STRIP)
                dcv_s = dcv_scr[rows, cols]
                dzc = jnp.zeros((CONV_STRIP, LANE), F32)
                for k in range(CONV_K):
                    hi, lo = divmod(CONV_K - 1 - k, SUBLANE)
                    z_win = zbank_ref[lo, pl.ds(HALO - SUBLANE * hi + r0, CONV_STRIP), :]
                    dcw8_scr[k, :, cols] += jnp.sum((dcv_s * z_win).reshape(CONV_STRIP // SUBLANE, SUBLANE, LANE), axis=0)
                    dzc = dzc + cw_ref[k:k + 1, cols] * dbank_ref[lo, pl.ds(SUBLANE * hi + r0, CONV_STRIP), :]
                a_s = a_ref[rows, cols].astype(F32)
                sga = _sigmoid(ag_ref[rows, cols].astype(F32))
                dza = dzc * sga
                dzag = dzc * a_s * (sga * (1.0 - sga))
                dz_ref[3, rows, cols] = dza.astype(BF16)
                dz_ref[4, rows, cols] = dzag.astype(BF16)
                dbin_ref[3:4, cols] += _colsum(dza)
                dbin_ref[4:5, cols] += _colsum(dzag)
        for k in range(3):
            dz_ref[5 + k] = dzg_ref[k]

        @pl.when(i == n_tiles - 1)
        def _():
            dcw_ref[...] = jnp.sum(dcw8_scr[...], axis=1)

    def col(j):
        return pl.BlockSpec((None, tm, D), lambda i: (j, i, 0))

    blk = pl.BlockSpec((tm, D), lambda i: (i, 0))
    after = pl.BlockSpec((HALO, D), lambda i: (jnp.minimum((i + 1) * (tm // HALO), t // HALO - 1), 0))
    row = pl.BlockSpec((1, D), lambda i: (0, 0))
    full2 = lambda s: pl.BlockSpec(s, lambda i: (0, 0))
    full3 = lambda s: pl.BlockSpec(s, lambda i: (0, 0, 0))
    return _with_tasks(_hosted(
        tasks, body, name=name, grid=(n_tiles,),
        in_specs=[col(0), col(1), col(2), col(3), col(4), _halo_before(tm, 2), _halo_before(tm, 3),
                  _halo_before(tm, 4), blk, after, blk, blk, after, blk, after,
                  pl.BlockSpec((3, tm, D), lambda i: (0, i, 0)), row, row, full3((SGU_G, CHUNK, CHUNK)),
                  full3((SGU_G, CHUNK, CHUNK)), full3((4, POOL_GC, POOL_GC)), row, full2((HALO, D)), row, row],
        out_specs=[pl.BlockSpec((8, tm, D), lambda i: (0, i, 0)), full2((8, D)), full2((8, D)),
                   full3((SGU_G, CHUNK, CHUNK)), full2((CHUNK, CHUNK)), full3((4, POOL_GC, POOL_GC)),
                   full2((HALO, D))],
        out_shape=[_sds((8, t, D), BF16), _sds((8, D), F32), _sds((8, D), F32), _sds((SGU_G, CHUNK, CHUNK), F32),
                   _sds((CHUNK, CHUNK), F32), _sds((4, POOL_GC, POOL_GC), F32), _sds((HALO, D), F32)],
        scratch_shapes=[pltpu.VMEM((tm, D), F32), pltpu.VMEM((tm, D), F32), pltpu.VMEM((n_ext, D), F32),
                        pltpu.VMEM((SUBLANE, n_ext, LANE), F32), pltpu.VMEM((SUBLANE, n_ext, LANE), F32),
                        pltpu.VMEM((HALO, SUBLANE, D), F32)],
    )(z8, z8, z8, z8, z8, z8, z8, z8, cv, cv, dsa, dsb, dsb, dsc, dsc, dzg, ln_g, ln_b, w_s, bs_b, pool_w,
      pool_scale, conv_w, cln_g, cln_b), tasks)


def _ada_fwd(c_all, w_ada, b_loc, name):
    def body(c_ref, w_ref, b_ref, o_ref):
        cv = c_ref[...]
        ca = (cv * _sigmoid(cv)).astype(BF16)
        for l in range(DEPTH):
            o_ref[l] = _dot(ca, w_ref[l].astype(BF16)) + b_ref[l]

    return pl.pallas_call(body, name=name, out_shape=_sds((DEPTH, N_DEV, ADA_BLK), F32),
                          compiler_params=_params())(c_all, w_ada, b_loc)


def _ada_bwd(c_all_t, d_loc, name):
    def body(c_ref, d_ref, o_ref):
        cv = c_ref[...]
        ca = cv * _sigmoid(cv)
        for l in range(DEPTH):
            acc = jnp.zeros((D, ADA_BLK), F32)
            for j in range(N_DEV):
                acc = acc + ca[:, j:j + 1] * d_ref[l, j:j + 1, :]
            o_ref[l] = acc

    return pl.pallas_call(body, name=name, out_shape=_sds((DEPTH, D, ADA_BLK), F32),
                          compiler_params=_params())(c_all_t, d_loc)


def _sum8(g8, name):
    _, rows, cols = g8.shape
    tr = _tile(rows, 256)

    def body(g_ref, o_ref):
        acc = g_ref[0]
        for k in range(1, N_DEV):
            acc = acc + g_ref[k]
        o_ref[...] = acc

    return pl.pallas_call(body, name=name, grid=(rows // tr,),
                          in_specs=[pl.BlockSpec((N_DEV, tr, cols), lambda r: (0, r, 0))],
                          out_specs=pl.BlockSpec((tr, cols), lambda r: (r, 0)), out_shape=_sds((rows, cols), F32),
                          compiler_params=_params("arbitrary"))(g8)


PROJ = ("w_pa", "w_pb", "w_pc", "w_out")
FWD_GATHERS = {
    (0, "in_proj"): tuple((n, 0) for n in PROJ) + (("w_ffn_out", 0),),
    (0, "branches"): (("w_ffn_in", 0),),
    (0, "ffn_in"): (("w_in", 1),),
    (1, "in_proj"): (("w_ffn_in", 1),),
    (1, "branches"): tuple((n, 1) for n in PROJ) + (("w_ffn_out", 1),),
}
FWD_GATHER_FILLS_KERNEL = ((0, "in_proj"), (0, "ffn_in"), (1, "branches"))
GRAD_GROUP = {"w_in": ("in", 0), "pool_w": ("in", 1), "w_pa": ("proj", 0), "w_pb": ("proj", 1), "w_pc": ("proj", 2),
              "w_out": ("proj", 3), "w_ffn_in": ("ffn", 0), "w_ffn_out": ("ffn", 1)}


EARLY_ROWS = dict(b_in=(0, 8), sgu_ln_g=(8, 9), sgu_ln_b=(9, 10), sgu_w_s=(10, 138), sgu_b_s=(138, 139),
                  pool_scale=(139, 140), conv_b=(140, 141), conv_ln_g=(141, 142), conv_ln_b=(142, 143), g_ffn=(143, 144))
CONV_ROW = 144
LATE_ROWS = dict(b_ada=(0, 6), g_mix=(6, 7))


def _run(factory, *args, tasks=(), **kw):
    out = factory(*args, tasks=tasks, **kw)
    return out if tasks else (out, [])


REPLICATED = ("b_ada", "g_mix", "b_in", "sgu_ln_g", "sgu_ln_b", "sgu_w_s", "sgu_b_s", "pool_scale", "conv_b",
              "conv_ln_g", "conv_ln_b", "g_ffn", "g_final")
WEIGHT_ORDER = ("w_ada", "b_ada", "g_mix", "w_in", "b_in", "sgu_ln_g", "sgu_ln_b", "sgu_w_s", "sgu_b_s", "w_pa",
                "pool_w", "pool_scale", "w_pb", "conv_w", "conv_b", "conv_ln_g", "conv_ln_b", "w_pc", "w_out",
                "g_ffn", "w_ffn_in", "w_ffn_out", "g_final")


def _rows(a):
    return a.reshape(-1, D)


def kernel(x, c, w_ada, b_ada, g_mix, w_in, b_in, sgu_ln_g, sgu_ln_b, sgu_w_s, sgu_b_s, w_pa, pool_w, pool_scale, w_pb, conv_w, conv_b, conv_ln_g, conv_ln_b, w_pc, w_out, g_ffn, w_ffn_in, w_ffn_out, g_final, loss_target, m_w_ada, m_b_ada, m_g_mix, m_w_in, m_b_in, m_sgu_ln_g, m_sgu_ln_b, m_sgu_w_s, m_sgu_b_s, m_w_pa, m_pool_w, m_pool_scale, m_w_pb, m_conv_w, m_conv_b, m_conv_ln_g, m_conv_ln_b, m_w_pc, m_w_out, m_g_ffn, m_w_ffn_in, m_w_ffn_out, m_g_final, v_w_ada, v_b_ada, v_g_mix, v_w_in, v_b_in, v_sgu_ln_g, v_sgu_ln_b, v_sgu_w_s, v_sgu_b_s, v_w_pa, v_pool_w, v_pool_scale, v_w_pb, v_conv_w, v_conv_b, v_conv_ln_g, v_conv_ln_b, v_w_pc, v_w_out, v_g_ffn, v_w_ffn_in, v_w_ffn_out, v_g_final):
    weights = dict(w_ada=w_ada, b_ada=b_ada, g_mix=g_mix, w_in=w_in, b_in=b_in, sgu_ln_g=sgu_ln_g, sgu_ln_b=sgu_ln_b,
                   sgu_w_s=sgu_w_s, sgu_b_s=sgu_b_s, w_pa=w_pa, pool_w=pool_w, pool_scale=pool_scale, w_pb=w_pb,
                   conv_w=conv_w, conv_b=conv_b, conv_ln_g=conv_ln_g, conv_ln_b=conv_ln_b, w_pc=w_pc, w_out=w_out,
                   g_ffn=g_ffn, w_ffn_in=w_ffn_in, w_ffn_out=w_ffn_out, g_final=g_final)
    mom1 = dict(w_ada=m_w_ada, b_ada=m_b_ada, g_mix=m_g_mix, w_in=m_w_in, b_in=m_b_in, sgu_ln_g=m_sgu_ln_g,
                sgu_ln_b=m_sgu_ln_b, sgu_w_s=m_sgu_w_s, sgu_b_s=m_sgu_b_s, w_pa=m_w_pa, pool_w=m_pool_w,
                pool_scale=m_pool_scale, w_pb=m_w_pb, conv_w=m_conv_w, conv_b=m_conv_b, conv_ln_g=m_conv_ln_g,
                conv_ln_b=m_conv_ln_b, w_pc=m_w_pc, w_out=m_w_out, g_ffn=m_g_ffn, w_ffn_in=m_w_ffn_in,
                w_ffn_out=m_w_ffn_out, g_final=m_g_final)
    mom2 = dict(w_ada=v_w_ada, b_ada=v_b_ada, g_mix=v_g_mix, w_in=v_w_in, b_in=v_b_in, sgu_ln_g=v_sgu_ln_g,
                sgu_ln_b=v_sgu_ln_b, sgu_w_s=v_sgu_w_s, sgu_b_s=v_sgu_b_s, w_pa=v_w_pa, pool_w=v_pool_w,
                pool_scale=v_pool_scale, w_pb=v_w_pb, conv_w=v_conv_w, conv_b=v_conv_b, conv_ln_g=v_conv_ln_g,
                conv_ln_b=v_conv_ln_b, w_pc=v_w_pc, w_out=v_w_out, g_ffn=v_g_ffn, w_ffn_in=v_w_ffn_in,
                w_ffn_out=v_w_ffn_out, g_final=v_g_final)

    for group in (weights, mom1, mom2):
        group["w_ffn_in"] = jnp.transpose(group["w_ffn_in"], (0, 2, 1))

    t = x.shape[1]
    xs = x.reshape(t, D)
    target = loss_target.reshape(t, D)
    me = 4 * lax.axis_index("x") + 2 * lax.axis_index("y") + lax.axis_index("c")
    core = lax.axis_index("c").astype(jnp.int32).reshape(1)

    bf = lambda n, l: weights[n][l].astype(BF16)
    (first,) = _transfer([_gather_task([bf("w_in", 0), c, pool_w, conv_w])], name="gather_first")
    w_in0, c_all, pool_all, conv_all = first
    full = [dict(w_in=w_in0)] + [dict() for _ in range(1, DEPTH)]

    def gather_at(l, stage):
        names = FWD_GATHERS.get((l, stage), ())
        relay_at = 1.0 if (l, stage) in FWD_GATHER_FILLS_KERNEL else 0.75
        return [_gather_task([bf(n, ll) for n, ll in names], relay_at)] if names else []

    def landed(l, stage, per):
        for (n, ll), arr in zip(FWD_GATHERS.get((l, stage), ()), per[0] if per else ()):
            full[ll][n] = arr

    c_all = c_all.reshape(N_DEV, D)
    pool_full = jnp.transpose(pool_all, (1, 2, 0, 3, 4)).reshape(DEPTH, 4, POOL_GC, POOL_GC)
    conv_full = jnp.transpose(conv_all, (1, 2, 0, 3)).reshape(DEPTH, CONV_K, D)
    conv_full = jnp.pad(conv_full, ((0, 0), (0, HALO - CONV_K), (0, 0)))

    b_loc = lax.dynamic_slice_in_dim(b_ada, me * ADA_BLK, ADA_BLK, axis=1).reshape(DEPTH, 1, ADA_BLK)
    ada_part = _ada_fwd(c_all, w_ada, b_loc, name="ada_fwd")
    ((ada_all,),) = _transfer([_gather_task([ada_part])], name="gather_ada")
    ada = lax.dynamic_index_in_dim(ada_all, me, axis=2, keepdims=False)
    ada = jnp.transpose(ada, (1, 0, 2)).reshape(DEPTH, 6, 1, D)

    bs_b = jnp.broadcast_to(sgu_b_s[..., None], (DEPTH, SGU_G, CHUNK, CHUNK))

    saved = []
    xl = xs
    h = _norm_mod(xl, g_mix[0].reshape(1, D), ada[0, 1], ada[0, 0], name="norm_mix_0")
    for l in range(DEPTH):
        w = full[l]
        sh_m, sc_m, gt_m, sh_f, sc_f, gt_f = (ada[l, k] for k in range(6))
        row = lambda a: a[l].reshape(1, D)
        z8, per = _run(_mm_cols, h, w["w_in"], b_in[l].reshape(8, 1, D), name=f"in_proj_{l}",
                       tasks=gather_at(l, "in_proj"))
        landed(l, "in_proj", per)
        (sa, sb, sc, cv), per = _run(
            _branches_fwd, z8, row(sgu_ln_g), row(sgu_ln_b), sgu_w_s[l], bs_b[l], pool_full[l], row(pool_scale),
            conv_full[l], row(conv_b), row(conv_ln_g), row(conv_ln_b), name=f"branches_{l}",
            tasks=gather_at(l, "branches"))
        landed(l, "branches", per)
        wpa, wpb, wpc, wout = (w[n].reshape(D, D) for n in PROJ)
        (ya, yb, yc, merged), per = _run(_proj_merge, sa, sb, sc, wpa, wpb, wpc, z8, name=f"proj_merge_{l}",
                                         tasks=gather_at(l, "proj_merge"))
        landed(l, "proj_merge", per)
        om, x1, h2 = _out_proj(merged, wout, xl, gt_m, row(g_ffn), sc_f, sh_f, name=f"out_proj_{l}")
        wfi = w["w_ffn_in"].reshape(2, 4, FF_BLK, D)
        (gu, f4), per = _run(_ffn_in, h2, wfi, name=f"ffn_in_{l}", tasks=gather_at(l, "ffn_in"))
        landed(l, "ffn_in", per)
        wfo4 = w["w_ffn_out"].reshape(4, FF_BLK, D)
        nxt = (g_mix[l + 1].reshape(1, D), ada[l + 1, 1], ada[l + 1, 0]) if l + 1 < DEPTH else (None, None, None)
        res, per = _run(_ffn_out, f4, wfo4, x1, gt_f, *nxt, name=f"ffn_out_{l}", tasks=gather_at(l, "ffn_out"))
        landed(l, "ffn_out", per)
        o, x2 = res[0], res[1]
        saved.append(dict(x=xl, h=h, z8=z8, sa=sa, sb=sb, sc=sc, cv=cv, ya=ya, yb=yb, yc=yc, merged=merged, om=om,
                          x1=x1, h2=h2, gu=gu, f4=f4, o=o, wpa=wpa, wpb=wpb, wpc=wpc, wout=wout, wfi=wfi, wfo4=wfo4))
        xl = x2
        h = res[2] if l + 1 < DEPTH else None

    loss_tile, dx, dg_final, do, dgt_f = _final_loss(xl, g_final.reshape(1, D), target, saved[-1]["o"],
                                                     ada[DEPTH - 1, 5], name="final_loss")
    loss_row = jnp.broadcast_to(loss_tile[0:1, 0:1], (1, D))

    chip_parts = [dict() for _ in range(DEPTH)]
    early_buf, late_buf = [None] * DEPTH, [None] * DEPTH
    early_all, late_all = [None] * DEPTH, [None] * DEPTH
    tril = jnp.tril(jnp.ones((CHUNK, CHUNK), F32))
    for l in reversed(range(DEPTH)):
        s, w = saved[l], full[l]
        above = l + 1 if l + 1 < DEPTH else None
        sh_m, sc_m, gt_m, sh_f, sc_f, gt_f = (ada[l, k] for k in range(6))
        row = lambda a: a[l].reshape(1, D)
        dgu, per = _run(_ffn_bwd_act, do, s["wfo4"], s["gu"], name=f"ffn_bwd_act_{l}",
                        tasks=[] if above is None else [_gather_task([early_buf[above], late_buf[above]])])
        if above is not None:
            early_all[above], late_all[above] = per[0]
        d_wfo = _mm_tn(s["f4"], do[None], name=f"dw_ffn_out_{l}")
        dgu8 = dgu.reshape(8, t, FF_BLK)
        dh2 = _mm_nt_sum(dgu8, w["w_ffn_in"], name=f"dh_ffn_{l}", b_is_kn=True)
        d_wfi = _mm_tn(dgu8, s["h2"][None], name=f"dw_ffn_in_{l}")
        dx1, st_f, dom, dgt_m = _norm_mod_bwd(dh2, s["x1"], dx, row(g_ffn), sc_f, s["om"], gt_m,
                                              name=f"norm_ffn_bwd_{l}")
        ffn_group = [d_wfi, d_wfo.reshape(8, D_FF // 8, D)]
        (dya, dyb, dyc, dzg, db_gate), per = _run(_merge_bwd, dom, s["wout"], s["z8"], s["ya"], s["yb"], s["yc"],
                                                  name=f"merge_bwd_{l}", tasks=[_sibling_task(ffn_group)])
        ffn_sums = _sibling_sums(ffn_group, per[0], core, tag=f"ffn_{l}")
        d_wout = _mm_tn(s["merged"][None], dom[None], name=f"dw_out_{l}")
        d_wpa = _mm_tn(s["sa"][None], dya[None], name=f"dw_pa_{l}")
        d_wpb = _mm_tn(s["sb"][None], dyb[None], name=f"dw_pb_{l}")
        d_wpc = _mm_tn(s["sc"][None], dyc[None], name=f"dw_pc_{l}")
        dsa = _mm_nt_sum(dya[None], s["wpa"][None], name=f"ds_a_{l}")
        dsb = _mm_nt_sum(dyb[None], s["wpb"][None], name=f"ds_b_{l}")
        dsc = _mm_nt_sum(dyc[None], s["wpc"][None], name=f"ds_c_{l}")
        proj_group = [g.reshape(8, D // 8, D) for g in (d_wpa, d_wpb, d_wpc, d_wout)]
        (dz8, db_in5, rows6, dws, dbs, dwp, dcw), per = _run(
            _branches_bwd, s["z8"], s["cv"], dsa, dsb, dsc, dzg, row(sgu_ln_g), row(sgu_ln_b), sgu_w_s[l], bs_b[l],
            pool_full[l], row(pool_scale), conv_full[l], row(conv_ln_g), row(conv_ln_b), name=f"branches_bwd_{l}",
            tasks=[_chips_task(ffn_sums), _sibling_task(proj_group)]
            + ([] if above is None else [_chips_task(in_sums)]))
        chip_parts[l]["ffn"] = per[0]
        proj_sums = _sibling_sums(proj_group, per[1], core, tag=f"proj_{l}")
        if above is not None:
            chip_parts[above]["in"] = per[2]
        early_buf[l] = jnp.concatenate([db_in5[0:5], db_gate, rows6[0:2], _rows(dws * tril),
                                        _rows(jnp.transpose(dbs[:, :SGU_G])), rows6[2:6], st_f[2:3], dcw], axis=0)
        d_win, per = _run(_mm_tn, s["h"][None], dz8, name=f"dw_in_{l}",
                          tasks=[_gather_task([early_buf[0]])] if l == 0 else [])
        if l == 0:
            (early_all[0],) = per[0]
        d_pool = jnp.transpose(dwp.reshape(4, N_DEV, POOL_GC // N_DEV, POOL_GC), (1, 0, 2, 3))
        in_group = [d_win, d_pool.reshape(N_DEV, 4 * POOL_GC // N_DEV, POOL_GC)]
        dh, per = _run(_mm_nt_sum, dz8, w["w_in"], name=f"dh_in_{l}",
                       tasks=[_chips_task(proj_sums), _sibling_task(in_group)])
        chip_parts[l]["proj"] = per[0]
        in_sums = _sibling_sums(in_group, per[1], core, tag=f"in_{l}")
        gate = (saved[l - 1]["o"], ada[l - 1, 5]) if l > 0 else (None, None)
        res, per = _run(_norm_mod_bwd, dh, s["x"], dx1, row(g_mix), sc_m, *gate, name=f"norm_mix_bwd_{l}",
                        tasks=[_chips_task(in_sums)] if l == 0 else [])
        dx, st_m = res[0], res[1]
        if l == 0:
            chip_parts[0]["in"] = per[0]
        late_buf[l] = jnp.concatenate([st_m[0:1], st_m[1:2], dgt_m, st_f[0:1], st_f[1:2], dgt_f, st_m[2:3],
                                       dg_final if l == 0 else loss_row], axis=0)
        if l > 0:
            do, dgt_f = res[2], res[3]
    late_all[0] = _transfer([_gather_task([late_buf[0]])], name="gather_last")[0][0]

    early = [_sum8(early_all[l], name=f"sum_early_grads_{l}") for l in range(DEPTH)]
    late = [_sum8(late_all[l], name=f"sum_late_grads_{l}") for l in range(DEPTH)]
    layers = lambda red, lo, hi: jnp.stack([red[l][lo:hi] for l in range(DEPTH)], axis=0)
    grads = {n: layers(early, lo, hi).reshape(weights[n].shape) for n, (lo, hi) in EARLY_ROWS.items()}
    grads.update({n: layers(late, lo, hi).reshape(weights[n].shape) for n, (lo, hi) in LATE_ROWS.items()})
    grads["g_final"] = late[0][7]
    loss = late[DEPTH - 1][7, 0]
    conv_g = layers(early, CONV_ROW, CONV_ROW + CONV_K)
    grads["conv_w"] = lax.dynamic_slice_in_dim(conv_g, me * (D // N_DEV), D // N_DEV, axis=2)
    d_ada_all = jnp.stack([late_all[l][:, 0:6] for l in range(DEPTH)], axis=1).reshape(N_DEV, DEPTH, 6 * D)
    d_loc = jnp.transpose(lax.dynamic_slice_in_dim(d_ada_all, me * ADA_BLK, ADA_BLK, axis=2), (1, 0, 2))
    grads["w_ada"] = _ada_bwd(jnp.transpose(c_all), d_loc, name="ada_bwd")

    out = {}
    for n in REPLICATED + ("conv_w", "w_ada"):
        out[n] = _adam_nd(grads[n], weights[n], mom1[n], mom2[n], name=f"adam_{n}")
    for n, (group, k) in GRAD_GROUP.items():
        parts = [chip_parts[l][group][k] for l in range(DEPTH)]
        shape = (DEPTH,) + parts[0].shape[1:]
        res = _adam(parts, weights[n].reshape(shape), mom1[n].reshape(shape), mom2[n].reshape(shape),
                    name=f"adam_{n}")
        out[n] = [r.reshape(weights[n].shape) for r in res]
    out["w_ffn_in"] = [jnp.transpose(r, (0, 2, 1)) for r in out["w_ffn_in"]]

    grad_x = dx.reshape(1, t, D)
    return (loss, grad_x, *[out[n][0] for n in WEIGHT_ORDER], *[out[n][1] for n in WEIGHT_ORDER],
            *[out[n][2] for n in WEIGHT_ORDER], *[out[n][3] for n in WEIGHT_ORDER])
```

```python
import math

import jax
import jax.numpy as jnp
from jax import lax
from jax.experimental import pallas as pl
from jax.experimental.pallas import tpu as pltpu

F32 = jnp.float32
BF16 = jnp.bfloat16
MESH = pl.DeviceIdType.MESH
AXES = ("x", "y", "c")
N_DEV = 8

D = 1024
DEPTH = 2
EPS = 1e-6
CHUNK = 128
SGU_G = 8
POOL_WINDOWS = (2, 4, 8, 16)
POOL_GC = 256
CONV_K = 31
HALO = 32
SUBLANE = 8
LANE = 128
CONV_STRIP = 128
D_FF = 2816
FF_BLK = D_FF // 4
ADA_BLK = 6 * D // N_DEV

ADAM_LR = 0.001
ADAM_B1 = 0.9
ADAM_B2 = 0.999
ADAM_EPS = 1e-08
ADAM_WD = 0.01
ADAM_STEP = 10

VMEM_LIMIT_V7X = 56 * 1024 * 1024
INV_SQRT2 = 1.0 / math.sqrt(2.0)
INV_SQRT_2PI = 1.0 / math.sqrt(2.0 * math.pi)


def _params(*sem):
    return pltpu.CompilerParams(dimension_semantics=sem if sem else None, vmem_limit_bytes=VMEM_LIMIT_V7X)


def _tile(n, pref):
    if n <= pref:
        return n
    for t in range(pref - pref % 8, 0, -8):
        if n % t == 0:
            return t
    raise ValueError((n, pref))


def _sds(shape, dtype):
    return jax.ShapeDtypeStruct(shape, dtype)


def _sigmoid(x):
    return 1.0 / (1.0 + jnp.exp(-x))


def _gelu(x):
    return 0.5 * x * (1.0 + lax.erf(x * INV_SQRT2))


def _gelu_grad(x):
    return 0.5 * (1.0 + lax.erf(x * INV_SQRT2)) + x * (INV_SQRT_2PI * jnp.exp(-0.5 * x * x))


def _ln_stats(v):
    mu = jnp.mean(v, axis=-1, keepdims=True)
    vc = v - mu
    rstd = lax.rsqrt(jnp.mean(vc * vc, axis=-1, keepdims=True) + EPS)
    return vc * rstd, rstd


def _ln_bwd(dvhat, vhat, rstd):
    return rstd * (dvhat - jnp.mean(dvhat, axis=-1, keepdims=True)
                   - vhat * jnp.mean(dvhat * vhat, axis=-1, keepdims=True))


def _colsum(v):
    return jnp.sum(v, axis=0, keepdims=True)


def _dot(a, b):
    return jnp.dot(a, b, preferred_element_type=F32)


def _dot_nt(a, b):
    return lax.dot_general(a, b, (((1,), (1,)), ((), ())), preferred_element_type=F32)


def _dot_tn(a, b):
    return lax.dot_general(a, b, (((0,), (0,)), ((), ())), preferred_element_type=F32)


def _tril_mask():
    r = lax.broadcasted_iota(jnp.int32, (CHUNK, CHUNK), 0)
    c = lax.broadcasted_iota(jnp.int32, (CHUNK, CHUNK), 1)
    return (r >= c).astype(F32)


def _mesh_pos():
    return tuple(lax.axis_index(a) for a in AXES)


class _Task:
    def __init__(self, arrays, out_shapes, scratch, start, finish, relay=None, relay_at=1.0):
        self.arrays, self.out_shapes, self.scratch = arrays, out_shapes, scratch
        self.start, self.finish, self.relay = start, finish, relay or (lambda ins, outs, sems: None)
        self.relay_at = relay_at


def _hosted(tasks, body, *, name, grid, in_specs, out_specs, out_shape, scratch_shapes=()):
    single = not isinstance(out_shape, (list, tuple))
    out_shape, out_specs = ([out_shape], [out_specs]) if single else (list(out_shape), list(out_specs))
    n_in, n_out, n_scr = len(in_specs), len(out_shape), len(scratch_shapes)
    sizes = [(len(t.arrays), len(t.out_shapes), len(t.scratch)) for t in tasks]
    t_in, t_out, t_scr = (sum(s[k] for s in sizes) for k in range(3))
    any_spec = pl.BlockSpec(memory_space=pl.ANY)

    def wrapped(*refs):
        refs = list(refs)
        ins, refs = refs[:n_in + t_in], refs[n_in + t_in:]
        outs, scr = refs[:n_out + t_out], refs[n_out + t_out:]

        def per_task(fn_name, only=None):
            i0, o0, s0 = n_in, n_out, n_scr
            for t, (ni, no, ns) in zip(tasks, sizes):
                if only is None or t is only:
                    getattr(t, fn_name)(ins[i0:i0 + ni], outs[o0:o0 + no], scr[s0:s0 + ns])
                i0, o0, s0 = i0 + ni, o0 + no, s0 + ns

        if tasks and grid:
            first, last, step, total = None, None, 0, 1
            for d, g in enumerate(grid):
                f, e = pl.program_id(d) == 0, pl.program_id(d) == g - 1
                first, last = (f, e) if first is None else (first & f, last & e)
                step, total = step * g + pl.program_id(d), total * g
            pl.when(first)(lambda: per_task("start"))
            for t in tasks:
                pl.when(step == min(int(t.relay_at * total), total - 1))(lambda t=t: per_task("relay", only=t))
        elif tasks:
            per_task("start")
            per_task("relay")
        body(*ins[:n_in], *outs[:n_out], *scr[:n_scr])
        if tasks and grid:
            pl.when(last)(lambda: per_task("finish"))
        elif tasks:
            per_task("finish")

    call = pl.pallas_call(
        wrapped, name=name, grid=grid,
        in_specs=list(in_specs) + [any_spec] * t_in, out_specs=out_specs + [any_spec] * t_out,
        out_shape=out_shape + [s for t in tasks for s in t.out_shapes],
        scratch_shapes=list(scratch_shapes) + [s for t in tasks for s in t.scratch],
        compiler_params=_params(*(("arbitrary",) * len(grid))))

    def run(*operands):
        res = list(call(*operands, *[a for t in tasks for a in t.arrays]))
        host, rest, per = res[:n_out], res[n_out:], []
        for _, no, _ in sizes:
            per.append(rest[:no])
            rest = rest[no:]
        return (host[0] if single else host), per

    return run


def _transfer(tasks, name):
    return _hosted(tasks, lambda: None, name=name, grid=(), in_specs=[], out_specs=[], out_shape=[])()[1]


def _gather_task(arrs, relay_at=0.75):
    n = len(arrs)

    def plan(ins, outs, sems):
        send_sems, recv_sems, local_sems = sems
        x, y, c = _mesh_pos()
        me, sibling = (x, y, c), (x, y, 1 - c)
        chips = [(1 - x, y), (x, 1 - y), (1 - x, 1 - y)]

        def slot(a, p):
            return outs[a].at[4 * p[0] + 2 * p[1] + p[2]]

        def copy(a, k, block, to, src=None):
            dst = slot(a, block)
            return pltpu.make_async_remote_copy(
                src_ref=dst if src is None else src, dst_ref=dst, send_sem=send_sems.at[a, k],
                recv_sem=recv_sems.at[a, k], device_id=to, device_id_type=MESH)

        def own_block_copies():
            mine = [pltpu.make_async_copy(ins[a], slot(a, me), local_sems.at[a]) for a in range(n)]
            first = []
            for a in range(n):
                first.append(copy(a, 0, me, sibling, src=ins[a]))
                first += [copy(a, 1 + j, me, (*chip, c), src=ins[a]) for j, chip in enumerate(chips)]
            return mine, first

        return c, me, sibling, chips, copy, own_block_copies

    def start(ins, outs, sems):
        mine, first = plan(ins, outs, sems)[-1]()
        for cp in mine + first:
            cp.start()

    def relay(ins, outs, sems):
        c, me, sibling, chips, copy, _ = plan(ins, outs, sems)
        for j, chip in enumerate(chips):
            for a in range(n):
                copy(a, 1 + j, (*chip, c), me).wait_recv()
                copy(a, 4 + j, (*chip, c), sibling).start()

    def finish(ins, outs, sems):
        c, me, sibling, chips, copy, own_block_copies = plan(ins, outs, sems)
        mine, first = own_block_copies()
        passed = [copy(a, 4 + j, (*chip, c), sibling) for j, chip in enumerate(chips) for a in range(n)]
        for a in range(n):
            copy(a, 0, sibling, me).wait_recv()
            for j, chip in enumerate(chips):
                copy(a, 4 + j, (*chip, 1 - c), me).wait_recv()
        for cp in first + passed:
            cp.wait_send()
        for m in mine:
            m.wait()

    return _Task(list(arrs), [_sds((N_DEV,) + a.shape, a.dtype) for a in arrs],
                 [pltpu.SemaphoreType.DMA((n, 7)), pltpu.SemaphoreType.DMA((n, 7)), pltpu.SemaphoreType.DMA((n,))],
                 start, finish, relay, relay_at)


def _sibling_task(arrs):
    n = len(arrs)

    def copies(ins, outs, sems):
        send_sems, recv_sems = sems
        x, y, c = _mesh_pos()
        return [pltpu.make_async_remote_copy(
            src_ref=ins[a].at[2 * q + (1 - c)], dst_ref=outs[a].at[q], send_sem=send_sems.at[a, q],
            recv_sem=recv_sems.at[a, q], device_id=(x, y, 1 - c), device_id_type=MESH)
            for a in range(n) for q in range(4)]

    def start(ins, outs, sems):
        for cp in copies(ins, outs, sems):
            cp.start()

    def finish(ins, outs, sems):
        for cp in copies(ins, outs, sems):
            cp.wait()

    return _Task(list(arrs), [_sds((4,) + a.shape[1:], a.dtype) for a in arrs],
                 [pltpu.SemaphoreType.DMA((n, 4)), pltpu.SemaphoreType.DMA((n, 4))], start, finish)


def _chips_task(arrs):
    n = len(arrs)

    def copies(ins, outs, sems):
        send_sems, recv_sems, local_sems = sems
        x, y, c = _mesh_pos()
        q_me = 2 * x + y
        chips = [(1 - x, y), (x, 1 - y), (1 - x, 1 - y)]
        own = [pltpu.make_async_copy(ins[a].at[q_me], outs[a].at[q_me], local_sems.at[a]) for a in range(n)]
        remote = [pltpu.make_async_remote_copy(
            src_ref=ins[a].at[2 * chip[0] + chip[1]], dst_ref=outs[a].at[q_me], send_sem=send_sems.at[a, j],
            recv_sem=recv_sems.at[a, j], device_id=(*chip, c), device_id_type=MESH)
            for a in range(n) for j, chip in enumerate(chips)]
        return own + remote

    def start(ins, outs, sems):
        for cp in copies(ins, outs, sems):
            cp.start()

    def finish(ins, outs, sems):
        for cp in copies(ins, outs, sems):
            cp.wait()

    return _Task(list(arrs), [_sds(a.shape, a.dtype) for a in arrs],
                 [pltpu.SemaphoreType.DMA((n, 3)), pltpu.SemaphoreType.DMA((n, 3)), pltpu.SemaphoreType.DMA((n,))],
                 start, finish)


def _sibling_sum(arr, land, core, name):
    _, rows, cols = arr.shape
    tr = _tile(rows, 512)
    arr4 = arr.reshape(4, 2, rows, cols)

    def body(c_ref, a_ref, l_ref, o_ref):
        o_ref[...] = (a_ref[...] + l_ref[...]).astype(BF16)

    grid_spec = pltpu.PrefetchScalarGridSpec(
        num_scalar_prefetch=1, grid=(4, rows // tr),
        in_specs=[pl.BlockSpec((None, None, tr, cols), lambda q, r, c_ref: (q, c_ref[0], r, 0)),
                  pl.BlockSpec((None, tr, cols), lambda q, r, c_ref: (q, r, 0))],
        out_specs=pl.BlockSpec((None, tr, cols), lambda q, r, c_ref: (q, r, 0)))
    return pl.pallas_call(body, name=name, grid_spec=grid_spec, out_shape=_sds((4, rows, cols), BF16),
                          compiler_params=_params("arbitrary", "arbitrary"))(core, arr4, land)


def _sibling_sums(arrs, land, core, tag):
    return [_sibling_sum(a, l, core, name=f"rs_sum_{tag}_{k}") for k, (a, l) in enumerate(zip(arrs, land))]


def _adam(gparts, w, m, v, name):
    n_l = len(gparts)
    p, rows, cols = gparts[0].shape
    tr = _tile(rows, 256)
    n_r = rows // tr
    c1 = 1.0 - ADAM_B1 ** ADAM_STEP
    c2 = 1.0 - ADAM_B2 ** ADAM_STEP

    def body(*refs):
        g_refs = refs[:n_l]
        w_ref, m_ref, v_ref, go_ref, d_ref, mo_ref, vo_ref = refs[n_l:]
        layer = pl.program_id(0)
        g = jnp.zeros((tr, cols), F32)
        for li, g_ref in enumerate(g_refs):
            gl = g_ref[0].astype(F32)
            for k in range(1, p):
                gl = gl + g_ref[k].astype(F32)
            g = gl if n_l == 1 else jnp.where(layer == li, gl, g)
        m_new = ADAM_B1 * m_ref[...] + (1.0 - ADAM_B1) * g
        v_new = ADAM_B2 * v_ref[...] + (1.0 - ADAM_B2) * (g * g)
        m_hat = m_new / c1
        v_hat = v_new / c2
        go_ref[...] = g
        d_ref[...] = -ADAM_LR * (m_hat / (jnp.sqrt(v_hat) + ADAM_EPS) + ADAM_WD * w_ref[...])
        mo_ref[...] = m_new
        vo_ref[...] = v_new

    def g_spec(li):
        def index(l, r):
            return (0, jnp.where(l == li, r, jnp.where(l < li, 0, n_r - 1)), 0)
        return pl.BlockSpec((p, tr, cols), index)

    blk = pl.BlockSpec((None, tr, cols), lambda l, r: (l, r, 0))
    return pl.pallas_call(
        body, name=name, grid=(n_l, n_r),
        in_specs=[g_spec(li) for li in range(n_l)] + [blk, blk, blk],
        out_specs=[blk] * 4, out_shape=[_sds((n_l, rows, cols), F32)] * 4,
        compiler_params=_params("arbitrary", "arbitrary"))(*gparts, w, m, v)


def _adam_nd(grad, w, m, v, name):
    shape = w.shape
    cols = shape[-1]
    rows = w.size // cols
    as_rows = lambda a: a.reshape(1, rows, cols)
    out = _adam([as_rows(grad)], as_rows(w), as_rows(m), as_rows(v), name)
    return [o.reshape(shape) for o in out]


def _norm_mod(x, g, sc, sh, name):
    t = x.shape[0]
    tm = _tile(t, 512)

    def body(x_ref, g_ref, sc_ref, sh_ref, h_ref):
        h_ref[...] = _modulated_norm(x_ref[...], g_ref[...], sc_ref[...], sh_ref[...])

    row = pl.BlockSpec((1, D), lambda i: (0, 0))
    blk = pl.BlockSpec((tm, D), lambda i: (i, 0))
    return pl.pallas_call(body, name=name, grid=(t // tm,), in_specs=[blk, row, row, row], out_specs=blk,
                          out_shape=_sds((t, D), BF16), compiler_params=_params("arbitrary"))(x, g, sc, sh)


def _with_tasks(res_per, tasks):
    return res_per if tasks else res_per[0]


def _mm_cols(a, b8, bias8, name, tasks=()):
    t, k = a.shape
    j, _, n = b8.shape
    tm = _tile(t, 1024)

    def body(a_ref, b_ref, bias_ref, o_ref):
        o_ref[...] = (_dot(a_ref[...], b_ref[...]) + bias_ref[...]).astype(BF16)

    return _with_tasks(_hosted(
        tasks, body, name=name, grid=(j, t // tm),
        in_specs=[pl.BlockSpec((tm, k), lambda jj, i: (i, 0)),
                  pl.BlockSpec((None, k, n), lambda jj, i: (jj, 0, 0)),
                  pl.BlockSpec((None, 1, n), lambda jj, i: (jj, 0, 0))],
        out_specs=pl.BlockSpec((None, tm, n), lambda jj, i: (jj, i, 0)),
        out_shape=_sds((j, t, n), BF16))(a, b8, bias8), tasks)


def _halo_before(tm, col):
    return pl.BlockSpec((None, HALO, D), lambda i: (col, jnp.maximum(i * (tm // HALO) - 1, 0), 0))


def _pool_forward(p_ext, t0, rows):
    t = t0 + lax.broadcasted_iota(jnp.int32, (rows, 1), 0)
    out = []
    for gi, win in enumerate(POOL_WINDOWS):
        e = p_ext[:, gi * POOL_GC:(gi + 1) * POOL_GC]
        s, sh = e, 1
        while sh < win:
            s = s + pltpu.roll(s, sh, 0)
            sh *= 2
        cnt = jnp.minimum(t + 1, win).astype(F32)
        out.append(s[HALO:] / cnt - e[HALO:])
    return out


def _fill_shift_bank(bank_ref, ext, causal):
    n = ext.shape[0]
    bank_ref[0] = ext
    for b in range(1, SUBLANE):
        bank_ref[b] = pltpu.roll(ext, b if causal else n - b, 0)


def _branches_fwd(z8, ln_g, ln_b, w_s, bs_b, pool_w, pool_scale, conv_w, conv_b, cln_g, cln_b, name, tasks=()):
    t = z8.shape[1]
    tm = _tile(t, 256)
    n_ext = tm + HALO

    def body(zu_ref, zv_ref, p_ref, a_ref, ag_ref, ph_ref, ah_ref, agh_ref, lng_ref, lnb_ref, ws_ref, bsb_ref,
             wp_ref, ps_ref, cw_ref, cb_ref, clg_ref, clb_ref, sa_ref, sb_ref, sc_ref, cv_ref, bank_ref):
        i = pl.program_id(0)
        has_past = (i > 0).astype(F32)
        u = _gelu(zu_ref[...].astype(F32))
        vhat, _ = _ln_stats(_gelu(zv_ref[...].astype(F32)))
        vb = (vhat * lng_ref[...] + lnb_ref[...]).astype(BF16)
        mask = _tril_mask()
        for g in range(SGU_G):
            cols = slice(g * CHUNK, (g + 1) * CHUNK)
            wm = (ws_ref[g] * mask).astype(BF16)
            for n in range(tm // CHUNK):
                rows = slice(n * CHUNK, (n + 1) * CHUNK)
                mixed = _dot(wm, vb[rows, cols]) + bsb_ref[g]
                sa_ref[rows, cols] = (u[rows, cols] * mixed).astype(BF16)
        p_ext = jnp.concatenate([ph_ref[...].astype(F32) * has_past, p_ref[...].astype(F32)], axis=0)
        pooled = _pool_forward(p_ext, i * tm, tm)
        for gi in range(len(POOL_WINDOWS)):
            cols = slice(gi * POOL_GC, (gi + 1) * POOL_GC)
            y = _dot(pooled[gi].astype(BF16), wp_ref[gi].astype(BF16))
            sb_ref[:, cols] = (y * ps_ref[:, cols]).astype(BF16)
        for cb in range(D // LANE):
            cols = slice(cb * LANE, (cb + 1) * LANE)
            zc = jnp.concatenate(
                [ah_ref[:, cols].astype(F32) * has_past * _sigmoid(agh_ref[:, cols].astype(F32)),
                 a_ref[:, cols].astype(F32) * _sigmoid(ag_ref[:, cols].astype(F32))], axis=0)
            _fill_shift_bank(bank_ref, zc, causal=True)
            for r0 in range(0, tm, CONV_STRIP):
                acc = jnp.zeros((CONV_STRIP, LANE), F32) + cb_ref[:, cols]
                for k in range(CONV_K):
                    hi, lo = divmod(CONV_K - 1 - k, SUBLANE)
                    acc = acc + cw_ref[k:k + 1, cols] * bank_ref[lo, pl.ds(HALO - SUBLANE * hi + r0, CONV_STRIP), :]
                cv_ref[r0:r0 + CONV_STRIP, cols] = acc
        cv = cv_ref[...]
        chat, _ = _ln_stats(cv)
        cl = chat * clg_ref[...] + clb_ref[...]
        sc_ref[...] = (cl * _sigmoid(cl)).astype(BF16)

    def col(j):
        return pl.BlockSpec((None, tm, D), lambda i: (j, i, 0))

    row = pl.BlockSpec((1, D), lambda i: (0, 0))
    full3 = lambda s: pl.BlockSpec(s, lambda i: (0, 0, 0))
    blk = pl.BlockSpec((tm, D), lambda i: (i, 0))
    return _with_tasks(_hosted(
        tasks, body, name=name, grid=(t // tm,),
        in_specs=[col(0), col(1), col(2), col(3), col(4), _halo_before(tm, 2), _halo_before(tm, 3),
                  _halo_before(tm, 4), row, row, full3((SGU_G, CHUNK, CHUNK)), full3((SGU_G, CHUNK, CHUNK)),
                  full3((4, POOL_GC, POOL_GC)), row, pl.BlockSpec((HALO, D), lambda i: (0, 0)), row, row, row],
        out_specs=[blk, blk, blk, blk],
        out_shape=[_sds((t, D), BF16)] * 3 + [_sds((t, D), F32)],
        scratch_shapes=[pltpu.VMEM((SUBLANE, n_ext, LANE), F32)],
    )(z8, z8, z8, z8, z8, z8, z8, z8, ln_g, ln_b, w_s, bs_b, pool_w, pool_scale, conv_w, conv_b, cln_g, cln_b), tasks)


def _proj_merge(sa, sb, sc, w_pa, w_pb, w_pc, z8, name, tasks=()):
    t = sa.shape[0]
    tm = _tile(t, 512)

    def body(sa_ref, sb_ref, sc_ref, wa_ref, wb_ref, wc_ref, g0_ref, g1_ref, g2_ref, ya_ref, yb_ref, yc_ref, m_ref):
        merged = jnp.zeros((tm, D), F32)
        for s_ref, w_ref, g_ref, y_ref in ((sa_ref, wa_ref, g0_ref, ya_ref), (sb_ref, wb_ref, g1_ref, yb_ref),
                                           (sc_ref, wc_ref, g2_ref, yc_ref)):
            y = _dot(s_ref[...], w_ref[...])
            y_ref[...] = y.astype(BF16)
            merged = merged + _sigmoid(g_ref[...].astype(F32)) * y
        m_ref[...] = merged.astype(BF16)

    blk = pl.BlockSpec((tm, D), lambda i: (i, 0))
    wspec = pl.BlockSpec((D, D), lambda i: (0, 0))
    gate = lambda j: pl.BlockSpec((None, tm, D), lambda i: (j, i, 0))
    return _with_tasks(_hosted(
        tasks, body, name=name, grid=(t // tm,),
        in_specs=[blk, blk, blk, wspec, wspec, wspec, gate(5), gate(6), gate(7)],
        out_specs=[blk] * 4, out_shape=[_sds((t, D), BF16)] * 4)(sa, sb, sc, w_pa, w_pb, w_pc, z8, z8, z8), tasks)


def _modulated_norm(xv, g, sc, sh):
    r = lax.rsqrt(jnp.mean(xv * xv, axis=-1, keepdims=True) + EPS)
    return (xv * r * g * (1.0 + sc) + sh).astype(BF16)


def _out_proj(merged, w_out, x, gt, g, sc, sh, name):
    t = x.shape[0]
    tm = _tile(t, 512)

    def body(m_ref, w_ref, x_ref, gt_ref, g_ref, sc_ref, sh_ref, om_ref, x1_ref, h2_ref):
        om = _dot(m_ref[...], w_ref[...])
        om_ref[...] = om
        x1 = x_ref[...] + gt_ref[...] * om
        x1_ref[...] = x1
        h2_ref[...] = _modulated_norm(x1, g_ref[...], sc_ref[...], sh_ref[...])

    blk = pl.BlockSpec((tm, D), lambda i: (i, 0))
    row = pl.BlockSpec((1, D), lambda i: (0, 0))
    return pl.pallas_call(
        body, name=name, grid=(t // tm,),
        in_specs=[blk, pl.BlockSpec((D, D), lambda i: (0, 0)), blk, row, row, row, row],
        out_specs=[blk, blk, blk], out_shape=[_sds((t, D), F32)] * 2 + [_sds((t, D), BF16)],
        compiler_params=_params("arbitrary"))(merged, w_out, x, gt, g, sc, sh)


def _ffn_in(h2, wfi, name, tasks=()):
    t = h2.shape[0]
    tm = _tile(t, 512)

    def body(h_ref, w_ref, gu_ref, f_ref):
        hv = h_ref[...]
        gp = _dot_nt(hv, w_ref[0])
        up = _dot_nt(hv, w_ref[1])
        gu_ref[0] = gp.astype(BF16)
        gu_ref[1] = up.astype(BF16)
        f_ref[...] = (gp * _sigmoid(gp) * up).astype(BF16)

    return _with_tasks(_hosted(
        tasks, body, name=name, grid=(4, t // tm),
        in_specs=[pl.BlockSpec((tm, D), lambda j, i: (i, 0)),
                  pl.BlockSpec((2, None, FF_BLK, D), lambda j, i: (0, j, 0, 0))],
        out_specs=[pl.BlockSpec((2, None, tm, FF_BLK), lambda j, i: (0, j, i, 0)),
                   pl.BlockSpec((None, tm, FF_BLK), lambda j, i: (j, i, 0))],
        out_shape=[_sds((2, 4, t, FF_BLK), BF16), _sds((4, t, FF_BLK), BF16)])(h2, wfi), tasks)


def _ffn_out(f4, wfo4, x1, gt, g, sc, sh, name, tasks=()):
    t = x1.shape[0]
    tm = _tile(t, 512)
    with_norm = g is not None

    def body(f_ref, w_ref, x_ref, gt_ref, *rest):
        o_ref, x2_ref = rest[-3:-1] if with_norm else rest[-2:]
        j = pl.program_id(1)

        @pl.when(j == 0)
        def _():
            o_ref[...] = jnp.zeros_like(o_ref)

        o_ref[...] += _dot(f_ref[...], w_ref[...])

        @pl.when(j == 3)
        def _():
            x2 = x_ref[...] + gt_ref[...] * o_ref[...]
            x2_ref[...] = x2
            if with_norm:
                g_ref, sc_ref, sh_ref = rest[:3]
                rest[-1][...] = _modulated_norm(x2, g_ref[...], sc_ref[...], sh_ref[...])

    blk = pl.BlockSpec((tm, D), lambda i, j: (i, 0))
    row = pl.BlockSpec((1, D), lambda i, j: (0, 0))
    norm_args = [g, sc, sh] if with_norm else []
    return _with_tasks(_hosted(
        tasks, body, name=name, grid=(t // tm, 4),
        in_specs=[pl.BlockSpec((None, tm, FF_BLK), lambda i, j: (j, i, 0)),
                  pl.BlockSpec((None, FF_BLK, D), lambda i, j: (j, 0, 0)), blk, row] + [row] * len(norm_args),
        out_specs=[blk, blk] + [blk] * with_norm,
        out_shape=[_sds((t, D), F32)] * 2 + [_sds((t, D), BF16)] * with_norm)(f4, wfo4, x1, gt, *norm_args), tasks)


def _gate_grads(dx, o_ref, gt_ref, do_ref, dgt_ref):
    do_ref[...] = (dx * gt_ref[...]).astype(BF16)
    dgt_ref[...] += _colsum(dx * o_ref[...])


def _final_loss(x, g, target, o, gt, name):
    t = x.shape[0]
    tm = _tile(t, 512)

    def body(x_ref, g_ref, t_ref, o_ref, gt_ref, loss_ref, dx_ref, dg_ref, do_ref, dgt_ref):
        @pl.when(pl.program_id(0) == 0)
        def _():
            for ref in (loss_ref, dg_ref, dgt_ref):
                ref[...] = jnp.zeros_like(ref)

        xv = x_ref[...]
        r = lax.rsqrt(jnp.mean(xv * xv, axis=-1, keepdims=True) + EPS)
        xn = xv * r
        diff = xn * g_ref[...] - t_ref[...]
        loss_ref[...] += 0.5 * jnp.sum(jnp.mean(diff * diff, axis=-1, keepdims=True))
        dy = diff * (1.0 / D)
        dg_ref[...] += _colsum(dy * xn)
        dxn = dy * g_ref[...]
        dx = r * (dxn - xn * jnp.mean(dxn * xn, axis=-1, keepdims=True))
        dx_ref[...] = dx
        _gate_grads(dx, o_ref, gt_ref, do_ref, dgt_ref)

    blk = pl.BlockSpec((tm, D), lambda i: (i, 0))
    row = pl.BlockSpec((1, D), lambda i: (0, 0))
    return pl.pallas_call(
        body, name=name, grid=(t // tm,), in_specs=[blk, row, blk, blk, row],
        out_specs=[pl.BlockSpec((8, 128), lambda i: (0, 0)), blk, row, blk, row],
        out_shape=[_sds((8, 128), F32), _sds((t, D), F32), _sds((1, D), F32), _sds((t, D), BF16), _sds((1, D), F32)],
        compiler_params=_params("arbitrary"))(x, g, target, o, gt)


def _norm_mod_bwd(dh, x, dres, g, sc, o, gt, name, tasks=()):
    t = x.shape[0]
    tm = _tile(t, 512)
    with_gate = o is not None

    def body(dh_ref, x_ref, dr_ref, g_ref, sc_ref, *rest):
        dx_ref, st_ref = rest[2:4] if with_gate else rest

        @pl.when(pl.program_id(0) == 0)
        def _():
            st_ref[...] = jnp.zeros_like(st_ref)
            if with_gate:
                rest[5][...] = jnp.zeros_like(rest[5])

        xv, dhv = x_ref[...], dh_ref[...]
        r = lax.rsqrt(jnp.mean(xv * xv, axis=-1, keepdims=True) + EPS)
        xn = xv * r
        gv, mod = g_ref[...], 1.0 + sc_ref[...]
        st_ref[0:1, :] += _colsum(dhv)
        st_ref[1:2, :] += _colsum(dhv * xn * gv)
        st_ref[2:3, :] += _colsum(dhv * xn * mod)
        dxn = dhv * gv * mod
        dx = dr_ref[...] + r * (dxn - xn * jnp.mean(dxn * xn, axis=-1, keepdims=True))
        dx_ref[...] = dx
        if with_gate:
            _gate_grads(dx, rest[0], rest[1], rest[4], rest[5])

    blk = pl.BlockSpec((tm, D), lambda i: (i, 0))
    row = pl.BlockSpec((1, D), lambda i: (0, 0))
    gate_args = [o, gt] if with_gate else []
    return _with_tasks(_hosted(
        tasks, body, name=name, grid=(t // tm,), in_specs=[blk, blk, blk, row, row] + [blk, row] * with_gate,
        out_specs=[blk, pl.BlockSpec((3, D), lambda i: (0, 0))] + [blk, row] * with_gate,
        out_shape=[_sds((t, D), F32), _sds((3, D), F32)] + [_sds((t, D), BF16), _sds((1, D), F32)] * with_gate,
    )(dh, x, dres, g, sc, *gate_args), tasks)


def _ffn_bwd_act(do, wfo4, gu, name, tasks=()):
    t = do.shape[0]
    tm = _tile(t, 512)

    def body(do_ref, w_ref, gu_ref, dgu_ref):
        df = _dot_nt(do_ref[...], w_ref[...])
        gp, up = gu_ref[0].astype(F32), gu_ref[1].astype(F32)
        sg = _sigmoid(gp)
        dgu_ref[0] = (df * up * (sg * (1.0 + gp * (1.0 - sg)))).astype(BF16)
        dgu_ref[1] = (df * (gp * sg)).astype(BF16)

    gu_spec = pl.BlockSpec((2, None, tm, FF_BLK), lambda j, i: (0, j, i, 0))
    return _with_tasks(_hosted(
        tasks, body, name=name, grid=(4, t // tm),
        in_specs=[pl.BlockSpec((tm, D), lambda j, i: (i, 0)),
                  pl.BlockSpec((None, FF_BLK, D), lambda j, i: (j, 0, 0)), gu_spec],
        out_specs=gu_spec, out_shape=_sds((2, 4, t, FF_BLK), BF16))(do, wfo4, gu), tasks)


def _mm_nt_sum(a8, b8, name, b_is_kn=False, tasks=()):
    j, t, k = a8.shape
    n = b8.shape[2] if b_is_kn else b8.shape[1]
    tm = _tile(t, 1024 if j > 1 else 512)
    out_dtype = F32 if j > 1 else BF16
    dot = _dot if b_is_kn else _dot_nt

    def body(a_ref, b_ref, o_ref):
        if j == 1:
            o_ref[...] = dot(a_ref[...], b_ref[...]).astype(out_dtype)
            return

        @pl.when(pl.program_id(1) == 0)
        def _():
            o_ref[...] = jnp.zeros_like(o_ref)

        o_ref[...] += dot(a_ref[...], b_ref[...])

    return _with_tasks(_hosted(
        tasks, body, name=name, grid=(t // tm, j),
        in_specs=[pl.BlockSpec((None, tm, k), lambda i, jj: (jj, i, 0)),
                  pl.BlockSpec((None,) + b8.shape[1:], lambda i, jj: (jj, 0, 0))],
        out_specs=pl.BlockSpec((tm, n), lambda i, jj: (i, 0)), out_shape=_sds((t, n), out_dtype))(a8, b8), tasks)


def _mm_tn(a8, b8, name, tasks=()):
    ja, t, m = a8.shape
    jb, _, n = b8.shape
    j = max(ja, jb)
    tk = _tile(t, 1024)

    def body(a_ref, b_ref, o_ref):
        @pl.when(pl.program_id(1) == 0)
        def _():
            o_ref[...] = jnp.zeros_like(o_ref)

        o_ref[...] += _dot_tn(a_ref[...], b_ref[...])

    return _with_tasks(_hosted(
        tasks, body, name=name, grid=(j, t // tk),
        in_specs=[pl.BlockSpec((None, tk, m), (lambda jj, kk: (jj, kk, 0)) if ja > 1 else (lambda jj, kk: (0, kk, 0))),
                  pl.BlockSpec((None, tk, n), (lambda jj, kk: (jj, kk, 0)) if jb > 1 else (lambda jj, kk: (0, kk, 0)))],
        out_specs=pl.BlockSpec((None, m, n), lambda jj, kk: (jj, 0, 0)), out_shape=_sds((j, m, n), F32))(a8, b8),
        tasks)


def _merge_bwd(dom, w_out, z8, ya, yb, yc, name, tasks=()):
    t = dom.shape[0]
    tm = _tile(t, 512)

    def body(dom_ref, w_ref, g0_ref, g1_ref, g2_ref, ya_ref, yb_ref, yc_ref, dya_ref, dyb_ref, dyc_ref, dzg_ref,
             db_ref):
        @pl.when(pl.program_id(0) == 0)
        def _():
            db_ref[...] = jnp.zeros_like(db_ref)

        dm = _dot_nt(dom_ref[...], w_ref[...])
        for k, (g_ref, y_ref, dy_ref) in enumerate(((g0_ref, ya_ref, dya_ref), (g1_ref, yb_ref, dyb_ref),
                                                    (g2_ref, yc_ref, dyc_ref))):
            sg = _sigmoid(g_ref[...].astype(F32))
            dy_ref[...] = (dm * sg).astype(BF16)
            dzg = dm * y_ref[...].astype(F32) * (sg * (1.0 - sg))
            dzg_ref[k] = dzg.astype(BF16)
            db_ref[k:k + 1, :] += _colsum(dzg)

    blk = pl.BlockSpec((tm, D), lambda i: (i, 0))
    gate = lambda j: pl.BlockSpec((None, tm, D), lambda i: (j, i, 0))
    return _with_tasks(_hosted(
        tasks, body, name=name, grid=(t // tm,),
        in_specs=[blk, pl.BlockSpec((D, D), lambda i: (0, 0)), gate(5), gate(6), gate(7), blk, blk, blk],
        out_specs=[blk, blk, blk, pl.BlockSpec((3, tm, D), lambda i: (0, i, 0)), pl.BlockSpec((3, D), lambda i: (0, 0))],
        out_shape=[_sds((t, D), BF16)] * 3 + [_sds((3, t, D), BF16), _sds((3, D), F32)],
    )(dom, w_out, z8, z8, z8, ya, yb, yc), tasks)


def _branches_bwd(z8, cv, dsa, dsb, dsc, dzg, ln_g, ln_b, w_s, bs_b, pool_w, pool_scale, conv_w, cln_g, cln_b, name,
                  tasks=()):
    t = z8.shape[1]
    tm = _tile(t, 256)
    n_ext = tm + HALO
    n_tiles = t // tm

    def body(zu_ref, zv_ref, p_ref, a_ref, ag_ref, ph_ref, ah_ref, agh_ref, cv_ref, cvf_ref, dsa_ref, dsb_ref,
             dsbf_ref, dsc_ref, dscf_ref, dzg_ref, lng_ref, lnb_ref, ws_ref, bsb_ref, wp_ref, ps_ref, cw_ref,
             clg_ref, clb_ref, dz_ref, dbin_ref, rows_ref, dws_ref, dbs_ref, dwp_ref, dcw_ref, mixed_scr, dvln_scr,
             dcv_scr, zbank_ref, dbank_ref, dcw8_scr):
        i = pl.program_id(0)

        @pl.when(i == 0)
        def _():
            for ref in (dbin_ref, rows_ref, dws_ref, dbs_ref, dwp_ref, dcw8_scr):
                ref[...] = jnp.zeros_like(ref)

        has_past = (i > 0).astype(F32)
        has_next = (i < n_tiles - 1).astype(F32)

        def emit(j, val):
            dz_ref[j] = val.astype(BF16)
            dbin_ref[j:j + 1, :] += _colsum(val)

        zu, zv = zu_ref[...].astype(F32), zv_ref[...].astype(F32)
        u = _gelu(zu)
        vhat, v_rstd = _ln_stats(_gelu(zv))
        vb = (vhat * lng_ref[...] + lnb_ref[...]).astype(BF16)
        dsa = dsa_ref[...].astype(F32)
        dmixed = dsa * u
        dmb = dmixed.astype(BF16)
        mask = _tril_mask()
        lane = lax.broadcasted_iota(jnp.int32, (CHUNK, CHUNK), 1)
        for g in range(SGU_G):
            cols = slice(g * CHUNK, (g + 1) * CHUNK)
            wm = (ws_ref[g] * mask).astype(BF16)
            dws = jnp.zeros((CHUNK, CHUNK), F32)
            dbs = jnp.zeros((CHUNK, 1), F32)
            for n in range(tm // CHUNK):
                rows = slice(n * CHUNK, (n + 1) * CHUNK)
                mixed_scr[rows, cols] = _dot(wm, vb[rows, cols]) + bsb_ref[g]
                dvln_scr[rows, cols] = _dot_tn(wm, dmb[rows, cols])
                dws = dws + _dot_nt(dmb[rows, cols], vb[rows, cols])
                dbs = dbs + jnp.sum(dmixed[rows, cols], axis=1, keepdims=True)
            dws_ref[g] += dws
            dbs_ref[...] += jnp.where(lane == g, dbs, 0.0)
        emit(0, dsa * mixed_scr[...] * _gelu_grad(zu))
        dvln = dvln_scr[...]
        rows_ref[0:1, :] += _colsum(dvln * vhat)
        rows_ref[1:2, :] += _colsum(dvln)
        emit(1, _ln_bwd(dvln * lng_ref[...], vhat, v_rstd) * _gelu_grad(zv))

        p_ext = jnp.concatenate([ph_ref[...].astype(F32) * has_past, p_ref[...].astype(F32)], axis=0)
        pooled = _pool_forward(p_ext, i * tm, tm)
        dsb = dsb_ref[...].astype(F32)
        dpl_ext = jnp.concatenate([dsb, dsbf_ref[...].astype(F32) * has_next], axis=0) * ps_ref[...]
        t_ext = i * tm + lax.broadcasted_iota(jnp.int32, (n_ext, 1), 0)
        dp_parts = []
        for gi, win in enumerate(POOL_WINDOWS):
            cols = slice(gi * POOL_GC, (gi + 1) * POOL_GC)
            pooled_b = pooled[gi].astype(BF16)
            wpb = wp_ref[gi].astype(BF16)
            rows_ref[2:3, cols] += _colsum(dsb[:, cols] * _dot(pooled_b, wpb))
            dplb = dpl_ext[:, cols].astype(BF16)
            dwp_ref[gi] += _dot_tn(pooled_b, dplb[:tm])
            dpooled = _dot_nt(dplb, wpb)
            s, sh = dpooled / jnp.minimum(t_ext + 1, win).astype(F32), 1
            while sh < win:
                s = s + pltpu.roll(s, n_ext - sh, 0)
                sh *= 2
            dp_parts.append(s[:tm] - dpooled[:tm])
        emit(2, jnp.concatenate(dp_parts, axis=1))

        cv_ext = jnp.concatenate([cv_ref[...], cvf_ref[...]], axis=0)
        chat, c_rstd = _ln_stats(cv_ext)
        cl = chat * clg_ref[...] + clb_ref[...]
        sg = _sigmoid(cl)
        dsc_ext = jnp.concatenate([dsc_ref[...].astype(F32), dscf_ref[...].astype(F32)], axis=0)
        dcl = dsc_ext * (sg * (1.0 + cl * (1.0 - sg)))
        rows_ref[4:5, :] += _colsum((dcl * chat)[:tm])
        rows_ref[5:6, :] += _colsum(dcl[:tm])
        in_seq = jnp.concatenate([jnp.ones((tm, 1), F32), jnp.zeros((HALO, 1), F32) + has_next], axis=0)
        dcv = jnp.where(in_seq > 0.0, _ln_bwd(dcl * clg_ref[...], chat, c_rstd), 0.0)
        rows_ref[3:4, :] += _colsum(dcv[:tm])
        dcv_scr[...] = dcv
        for cb in range(D // LANE):
            cols = slice(cb * LANE, (cb + 1) * LANE)
            zc = jnp.concatenate(
                [ah_ref[:, cols].astype(F32) * has_past * _sigmoid(agh_ref[:, cols].astype(F32)),
                 a_ref[:, cols].astype(F32) * _sigmoid(ag_ref[:, cols].astype(F32))], axis=0)
            _fill_shift_bank(zbank_ref, zc, causal=True)
            _fill_shift_bank(dbank_ref, dcv_scr[:, cols], causal=False)
            for r0 in range(0, tm, CONV_STRIP):
                rows = slice(r0, r0 + CONV_STRIP)
                dcv_s = dcv_scr[rows, cols]
                dzc = jnp.zeros((CONV_STRIP, LANE), F32)
                for k in range(CONV_K):
                    hi, lo = divmod(CONV_K - 1 - k, SUBLANE)
                    z_win = zbank_ref[lo, pl.ds(HALO - SUBLANE * hi + r0, CONV_STRIP), :]
                    dcw8_scr[k, :, cols] += jnp.sum((dcv_s * z_win).reshape(CONV_STRIP // SUBLANE, SUBLANE, LANE), axis=0)
                    dzc = dzc + cw_ref[k:k + 1, cols] * dbank_ref[lo, pl.ds(SUBLANE * hi + r0, CONV_STRIP), :]
                a_s = a_ref[rows, cols].astype(F32)
                sga = _sigmoid(ag_ref[rows, cols].astype(F32))
                dza = dzc * sga
                dzag = dzc * a_s * (sga * (1.0 - sga))
                dz_ref[3, rows, cols] = dza.astype(BF16)
                dz_ref[4, rows, cols] = dzag.astype(BF16)
                dbin_ref[3:4, cols] += _colsum(dza)
                dbin_ref[4:5, cols] += _colsum(dzag)
        for k in range(3):
            dz_ref[5 + k] = dzg_ref[k]

        @pl.when(i == n_tiles - 1)
        def _():
            dcw_ref[...] = jnp.sum(dcw8_scr[...], axis=1)

    def col(j):
        return pl.BlockSpec((None, tm, D), lambda i: (j, i, 0))

    blk = pl.BlockSpec((tm, D), lambda i: (i, 0))
    after = pl.BlockSpec((HALO, D), lambda i: (jnp.minimum((i + 1) * (tm // HALO), t // HALO - 1), 0))
    row = pl.BlockSpec((1, D), lambda i: (0, 0))
    full2 = lambda s: pl.BlockSpec(s, lambda i: (0, 0))
    full3 = lambda s: pl.BlockSpec(s, lambda i: (0, 0, 0))
    return _with_tasks(_hosted(
        tasks, body, name=name, grid=(n_tiles,),
        in_specs=[col(0), col(1), col(2), col(3), col(4), _halo_before(tm, 2), _halo_before(tm, 3),
                  _halo_before(tm, 4), blk, after, blk, blk, after, blk, after,
                  pl.BlockSpec((3, tm, D), lambda i: (0, i, 0)), row, row, full3((SGU_G, CHUNK, CHUNK)),
                  full3((SGU_G, CHUNK, CHUNK)), full3((4, POOL_GC, POOL_GC)), row, full2((HALO, D)), row, row],
        out_specs=[pl.BlockSpec((8, tm, D), lambda i: (0, i, 0)), full2((8, D)), full2((8, D)),
                   full3((SGU_G, CHUNK, CHUNK)), full2((CHUNK, CHUNK)), full3((4, POOL_GC, POOL_GC)),
                   full2((HALO, D))],
        out_shape=[_sds((8, t, D), BF16), _sds((8, D), F32), _sds((8, D), F32), _sds((SGU_G, CHUNK, CHUNK), F32),
                   _sds((CHUNK, CHUNK), F32), _sds((4, POOL_GC, POOL_GC), F32), _sds((HALO, D), F32)],
        scratch_shapes=[pltpu.VMEM((tm, D), F32), pltpu.VMEM((tm, D), F32), pltpu.VMEM((n_ext, D), F32),
                        pltpu.VMEM((SUBLANE, n_ext, LANE), F32), pltpu.VMEM((SUBLANE, n_ext, LANE), F32),
                        pltpu.VMEM((HALO, SUBLANE, D), F32)],
    )(z8, z8, z8, z8, z8, z8, z8, z8, cv, cv, dsa, dsb, dsb, dsc, dsc, dzg, ln_g, ln_b, w_s, bs_b, pool_w,
      pool_scale, conv_w, cln_g, cln_b), tasks)


def _ada_fwd(c_all, w_ada, b_loc, name):
    def body(c_ref, w_ref, b_ref, o_ref):
        cv = c_ref[...]
        ca = (cv * _sigmoid(cv)).astype(BF16)
        for l in range(DEPTH):
            o_ref[l] = _dot(ca, w_ref[l].astype(BF16)) + b_ref[l]

    return pl.pallas_call(body, name=name, out_shape=_sds((DEPTH, N_DEV, ADA_BLK), F32),
                          compiler_params=_params())(c_all, w_ada, b_loc)


def _ada_bwd(c_all_t, d_loc, name):
    def body(c_ref, d_ref, o_ref):
        cv = c_ref[...]
        ca = cv * _sigmoid(cv)
        for l in range(DEPTH):
            acc = jnp.zeros((D, ADA_BLK), F32)
            for j in range(N_DEV):
                acc = acc + ca[:, j:j + 1] * d_ref[l, j:j + 1, :]
            o_ref[l] = acc

    return pl.pallas_call(body, name=name, out_shape=_sds((DEPTH, D, ADA_BLK), F32),
                          compiler_params=_params())(c_all_t, d_loc)


def _sum8(g8, name):
    _, rows, cols = g8.shape
    tr = _tile(rows, 256)

    def body(g_ref, o_ref):
        acc = g_ref[0]
        for k in range(1, N_DEV):
            acc = acc + g_ref[k]
        o_ref[...] = acc

    return pl.pallas_call(body, name=name, grid=(rows // tr,),
                          in_specs=[pl.BlockSpec((N_DEV, tr, cols), lambda r: (0, r, 0))],
                          out_specs=pl.BlockSpec((tr, cols), lambda r: (r, 0)), out_shape=_sds((rows, cols), F32),
                          compiler_params=_params("arbitrary"))(g8)


PROJ = ("w_pa", "w_pb", "w_pc", "w_out")
FWD_GATHERS = {
    (0, "in_proj"): tuple((n, 0) for n in PROJ) + (("w_ffn_out", 0),),
    (0, "branches"): (("w_ffn_in", 0),),
    (0, "ffn_in"): (("w_in", 1),),
    (1, "in_proj"): (("w_ffn_in", 1),),
    (1, "branches"): tuple((n, 1) for n in PROJ) + (("w_ffn_out", 1),),
}
FWD_GATHER_FILLS_KERNEL = ((0, "in_proj"), (0, "ffn_in"), (1, "branches"))
GRAD_GROUP = {"w_in": (("in_a", 0), ("in_b", 0)), "pool_w": (("in_a", 1),), "w_pa": (("proj", 0),),
              "w_pb": (("proj", 1),), "w_pc": (("proj", 2),), "w_out": (("proj", 3),), "w_ffn_in": (("ffn", 0),),
              "w_ffn_out": (("ffn", 1),)}


EARLY_ROWS = dict(b_in=(0, 8), sgu_ln_g=(8, 9), sgu_ln_b=(9, 10), sgu_w_s=(10, 138), sgu_b_s=(138, 139),
                  pool_scale=(139, 140), conv_b=(140, 141), conv_ln_g=(141, 142), conv_ln_b=(142, 143), g_ffn=(143, 144))
CONV_ROW = 144
LATE_ROWS = dict(b_ada=(0, 6), g_mix=(6, 7))


def _run(factory, *args, tasks=(), **kw):
    out = factory(*args, tasks=tasks, **kw)
    return out if tasks else (out, [])


REPLICATED = ("b_ada", "g_mix", "b_in", "sgu_ln_g", "sgu_ln_b", "sgu_w_s", "sgu_b_s", "pool_scale", "conv_b",
              "conv_ln_g", "conv_ln_b", "g_ffn", "g_final")
WEIGHT_ORDER = ("w_ada", "b_ada", "g_mix", "w_in", "b_in", "sgu_ln_g", "sgu_ln_b", "sgu_w_s", "sgu_b_s", "w_pa",
                "pool_w", "pool_scale", "w_pb", "conv_w", "conv_b", "conv_ln_g", "conv_ln_b", "w_pc", "w_out",
                "g_ffn", "w_ffn_in", "w_ffn_out", "g_final")


def _rows(a):
    return a.reshape(-1, D)


def kernel(x, c, w_ada, b_ada, g_mix, w_in, b_in, sgu_ln_g, sgu_ln_b, sgu_w_s, sgu_b_s, w_pa, pool_w, pool_scale, w_pb, conv_w, conv_b, conv_ln_g, conv_ln_b, w_pc, w_out, g_ffn, w_ffn_in, w_ffn_out, g_final, loss_target, m_w_ada, m_b_ada, m_g_mix, m_w_in, m_b_in, m_sgu_ln_g, m_sgu_ln_b, m_sgu_w_s, m_sgu_b_s, m_w_pa, m_pool_w, m_pool_scale, m_w_pb, m_conv_w, m_conv_b, m_conv_ln_g, m_conv_ln_b, m_w_pc, m_w_out, m_g_ffn, m_w_ffn_in, m_w_ffn_out, m_g_final, v_w_ada, v_b_ada, v_g_mix, v_w_in, v_b_in, v_sgu_ln_g, v_sgu_ln_b, v_sgu_w_s, v_sgu_b_s, v_w_pa, v_pool_w, v_pool_scale, v_w_pb, v_conv_w, v_conv_b, v_conv_ln_g, v_conv_ln_b, v_w_pc, v_w_out, v_g_ffn, v_w_ffn_in, v_w_ffn_out, v_g_final):
    weights = dict(w_ada=w_ada, b_ada=b_ada, g_mix=g_mix, w_in=w_in, b_in=b_in, sgu_ln_g=sgu_ln_g, sgu_ln_b=sgu_ln_b,
                   sgu_w_s=sgu_w_s, sgu_b_s=sgu_b_s, w_pa=w_pa, pool_w=pool_w, pool_scale=pool_scale, w_pb=w_pb,
                   conv_w=conv_w, conv_b=conv_b, conv_ln_g=conv_ln_g, conv_ln_b=conv_ln_b, w_pc=w_pc, w_out=w_out,
                   g_ffn=g_ffn, w_ffn_in=w_ffn_in, w_ffn_out=w_ffn_out, g_final=g_final)
    mom1 = dict(w_ada=m_w_ada, b_ada=m_b_ada, g_mix=m_g_mix, w_in=m_w_in, b_in=m_b_in, sgu_ln_g=m_sgu_ln_g,
                sgu_ln_b=m_sgu_ln_b, sgu_w_s=m_sgu_w_s, sgu_b_s=m_sgu_b_s, w_pa=m_w_pa, pool_w=m_pool_w,
                pool_scale=m_pool_scale, w_pb=m_w_pb, conv_w=m_conv_w, conv_b=m_conv_b, conv_ln_g=m_conv_ln_g,
                conv_ln_b=m_conv_ln_b, w_pc=m_w_pc, w_out=m_w_out, g_ffn=m_g_ffn, w_ffn_in=m_w_ffn_in,
                w_ffn_out=m_w_ffn_out, g_final=m_g_final)
    mom2 = dict(w_ada=v_w_ada, b_ada=v_b_ada, g_mix=v_g_mix, w_in=v_w_in, b_in=v_b_in, sgu_ln_g=v_sgu_ln_g,
                sgu_ln_b=v_sgu_ln_b, sgu_w_s=v_sgu_w_s, sgu_b_s=v_sgu_b_s, w_pa=v_w_pa, pool_w=v_pool_w,
                pool_scale=v_pool_scale, w_pb=v_w_pb, conv_w=v_conv_w, conv_b=v_conv_b, conv_ln_g=v_conv_ln_g,
                conv_ln_b=v_conv_ln_b, w_pc=v_w_pc, w_out=v_w_out, g_ffn=v_g_ffn, w_ffn_in=v_w_ffn_in,
                w_ffn_out=v_w_ffn_out, g_final=v_g_final)

    for group in (weights, mom1, mom2):
        group["w_ffn_in"] = jnp.transpose(group["w_ffn_in"], (0, 2, 1))

    t = x.shape[1]
    xs = x.reshape(t, D)
    target = loss_target.reshape(t, D)
    me = 4 * lax.axis_index("x") + 2 * lax.axis_index("y") + lax.axis_index("c")
    core = lax.axis_index("c").astype(jnp.int32).reshape(1)

    bf = lambda n, l: weights[n][l].astype(BF16)
    (first,) = _transfer([_gather_task([bf("w_in", 0), c, pool_w, conv_w])], name="gather_first")
    w_in0, c_all, pool_all, conv_all = first
    full = [dict(w_in=w_in0)] + [dict() for _ in range(1, DEPTH)]

    def gather_at(l, stage):
        names = FWD_GATHERS.get((l, stage), ())
        relay_at = 1.0 if (l, stage) in FWD_GATHER_FILLS_KERNEL else 0.75
        return [_gather_task([bf(n, ll) for n, ll in names], relay_at)] if names else []

    def landed(l, stage, per):
        for (n, ll), arr in zip(FWD_GATHERS.get((l, stage), ()), per[0] if per else ()):
            full[ll][n] = arr

    c_all = c_all.reshape(N_DEV, D)
    pool_full = jnp.transpose(pool_all, (1, 2, 0, 3, 4)).reshape(DEPTH, 4, POOL_GC, POOL_GC)
    conv_full = jnp.transpose(conv_all, (1, 2, 0, 3)).reshape(DEPTH, CONV_K, D)
    conv_full = jnp.pad(conv_full, ((0, 0), (0, HALO - CONV_K), (0, 0)))

    b_loc = lax.dynamic_slice_in_dim(b_ada, me * ADA_BLK, ADA_BLK, axis=1).reshape(DEPTH, 1, ADA_BLK)
    ada_part = _ada_fwd(c_all, w_ada, b_loc, name="ada_fwd")
    ((ada_all,),) = _transfer([_gather_task([ada_part])], name="gather_ada")
    ada = lax.dynamic_index_in_dim(ada_all, me, axis=2, keepdims=False)
    ada = jnp.transpose(ada, (1, 0, 2)).reshape(DEPTH, 6, 1, D)

    bs_b = jnp.broadcast_to(sgu_b_s[..., None], (DEPTH, SGU_G, CHUNK, CHUNK))

    saved = []
    xl = xs
    h = _norm_mod(xl, g_mix[0].reshape(1, D), ada[0, 1], ada[0, 0], name="norm_mix_0")
    for l in range(DEPTH):
        w = full[l]
        sh_m, sc_m, gt_m, sh_f, sc_f, gt_f = (ada[l, k] for k in range(6))
        row = lambda a: a[l].reshape(1, D)
        z8, per = _run(_mm_cols, h, w["w_in"], b_in[l].reshape(8, 1, D), name=f"in_proj_{l}",
                       tasks=gather_at(l, "in_proj"))
        landed(l, "in_proj", per)
        (sa, sb, sc, cv), per = _run(
            _branches_fwd, z8, row(sgu_ln_g), row(sgu_ln_b), sgu_w_s[l], bs_b[l], pool_full[l], row(pool_scale),
            conv_full[l], row(conv_b), row(conv_ln_g), row(conv_ln_b), name=f"branches_{l}",
            tasks=gather_at(l, "branches"))
        landed(l, "branches", per)
        wpa, wpb, wpc, wout = (w[n].reshape(D, D) for n in PROJ)
        (ya, yb, yc, merged), per = _run(_proj_merge, sa, sb, sc, wpa, wpb, wpc, z8, name=f"proj_merge_{l}",
                                         tasks=gather_at(l, "proj_merge"))
        landed(l, "proj_merge", per)
        om, x1, h2 = _out_proj(merged, wout, xl, gt_m, row(g_ffn), sc_f, sh_f, name=f"out_proj_{l}")
        wfi = w["w_ffn_in"].reshape(2, 4, FF_BLK, D)
        (gu, f4), per = _run(_ffn_in, h2, wfi, name=f"ffn_in_{l}", tasks=gather_at(l, "ffn_in"))
        landed(l, "ffn_in", per)
        wfo4 = w["w_ffn_out"].reshape(4, FF_BLK, D)
        nxt = (g_mix[l + 1].reshape(1, D), ada[l + 1, 1], ada[l + 1, 0]) if l + 1 < DEPTH else (None, None, None)
        res, per = _run(_ffn_out, f4, wfo4, x1, gt_f, *nxt, name=f"ffn_out_{l}", tasks=gather_at(l, "ffn_out"))
        landed(l, "ffn_out", per)
        o, x2 = res[0], res[1]
        saved.append(dict(x=xl, h=h, z8=z8, sa=sa, sb=sb, sc=sc, cv=cv, ya=ya, yb=yb, yc=yc, merged=merged, om=om,
                          x1=x1, h2=h2, gu=gu, f4=f4, o=o, wpa=wpa, wpb=wpb, wpc=wpc, wout=wout, wfi=wfi, wfo4=wfo4))
        xl = x2
        h = res[2] if l + 1 < DEPTH else None

    loss_tile, dx, dg_final, do, dgt_f = _final_loss(xl, g_final.reshape(1, D), target, saved[-1]["o"],
                                                     ada[DEPTH - 1, 5], name="final_loss")
    loss_row = jnp.broadcast_to(loss_tile[0:1, 0:1], (1, D))

    chip_parts = [dict() for _ in range(DEPTH)]
    early_buf, late_buf = [None] * DEPTH, [None] * DEPTH
    early_all, late_all = [None] * DEPTH, [None] * DEPTH
    tril = jnp.tril(jnp.ones((CHUNK, CHUNK), F32))
    for l in reversed(range(DEPTH)):
        s, w = saved[l], full[l]
        above = l + 1 if l + 1 < DEPTH else None
        sh_m, sc_m, gt_m, sh_f, sc_f, gt_f = (ada[l, k] for k in range(6))
        row = lambda a: a[l].reshape(1, D)
        dgu, per = _run(_ffn_bwd_act, do, s["wfo4"], s["gu"], name=f"ffn_bwd_act_{l}",
                        tasks=[] if above is None else [_gather_task([early_buf[above], late_buf[above]])])
        if above is not None:
            early_all[above], late_all[above] = per[0]
        d_wfo = _mm_tn(s["f4"], do[None], name=f"dw_ffn_out_{l}")
        dgu8 = dgu.reshape(8, t, FF_BLK)
        dh2 = _mm_nt_sum(dgu8, w["w_ffn_in"], name=f"dh_ffn_{l}", b_is_kn=True)
        d_wfi = _mm_tn(dgu8, s["h2"][None], name=f"dw_ffn_in_{l}")
        dx1, st_f, dom, dgt_m = _norm_mod_bwd(dh2, s["x1"], dx, row(g_ffn), sc_f, s["om"], gt_m,
                                              name=f"norm_ffn_bwd_{l}")
        ffn_group = [d_wfi, d_wfo.reshape(8, D_FF // 8, D)]
        (dya, dyb, dyc, dzg, db_gate), per = _run(_merge_bwd, dom, s["wout"], s["z8"], s["ya"], s["yb"], s["yc"],
                                                  name=f"merge_bwd_{l}", tasks=[_sibling_task(ffn_group)])
        ffn_sums = _sibling_sums(ffn_group, per[0], core, tag=f"ffn_{l}")
        d_wout = _mm_tn(s["merged"][None], dom[None], name=f"dw_out_{l}")
        d_wpa = _mm_tn(s["sa"][None], dya[None], name=f"dw_pa_{l}")
        d_wpb = _mm_tn(s["sb"][None], dyb[None], name=f"dw_pb_{l}")
        d_wpc = _mm_tn(s["sc"][None], dyc[None], name=f"dw_pc_{l}")
        dsa = _mm_nt_sum(dya[None], s["wpa"][None], name=f"ds_a_{l}")
        dsb = _mm_nt_sum(dyb[None], s["wpb"][None], name=f"ds_b_{l}")
        dsc = _mm_nt_sum(dyc[None], s["wpc"][None], name=f"ds_c_{l}")
        proj_group = [g.reshape(8, D // 8, D) for g in (d_wpa, d_wpb, d_wpc, d_wout)]
        (dz8, db_in5, rows6, dws, dbs, dwp, dcw), per = _run(
            _branches_bwd, s["z8"], s["cv"], dsa, dsb, dsc, dzg, row(sgu_ln_g), row(sgu_ln_b), sgu_w_s[l], bs_b[l],
            pool_full[l], row(pool_scale), conv_full[l], row(conv_ln_g), row(conv_ln_b), name=f"branches_bwd_{l}",
            tasks=[_chips_task(ffn_sums), _sibling_task(proj_group)]
            + ([] if above is None else [_chips_task(in_b_sums)]))
        chip_parts[l]["ffn"] = per[0]
        proj_sums = _sibling_sums(proj_group, per[1], core, tag=f"proj_{l}")
        if above is not None:
            chip_parts[above]["in_b"] = per[2]
        early_buf[l] = jnp.concatenate([db_in5[0:5], db_gate, rows6[0:2], _rows(dws * tril),
                                        _rows(jnp.transpose(dbs[:, :SGU_G])), rows6[2:6], st_f[2:3], dcw], axis=0)
        d_win_a, per = _run(_mm_tn, s["h"][None, :, :D // 2], dz8, name=f"dw_in_a_{l}",
                            tasks=[_gather_task([early_buf[0]])] if l == 0 else [])
        if l == 0:
            (early_all[0],) = per[0]
        d_pool = jnp.transpose(dwp.reshape(4, N_DEV, POOL_GC // N_DEV, POOL_GC), (1, 0, 2, 3))
        in_a_group = [d_win_a, d_pool.reshape(N_DEV, 4 * POOL_GC // N_DEV, POOL_GC)]
        d_win_b, per = _run(_mm_tn, s["h"][None, :, D // 2:], dz8, name=f"dw_in_b_{l}",
                            tasks=[_sibling_task(in_a_group)])
        in_a_sums = _sibling_sums(in_a_group, per[0], core, tag=f"in_a_{l}")
        dh, per = _run(_mm_nt_sum, dz8, w["w_in"], name=f"dh_in_{l}",
                       tasks=[_chips_task(proj_sums), _chips_task(in_a_sums), _sibling_task([d_win_b])])
        chip_parts[l]["proj"], chip_parts[l]["in_a"] = per[0], per[1]
        in_b_sums = _sibling_sums([d_win_b], per[2], core, tag=f"in_b_{l}")
        gate = (saved[l - 1]["o"], ada[l - 1, 5]) if l > 0 else (None, None)
        res, per = _run(_norm_mod_bwd, dh, s["x"], dx1, row(g_mix), sc_m, *gate, name=f"norm_mix_bwd_{l}",
                        tasks=[_chips_task(in_b_sums)] if l == 0 else [])
        dx, st_m = res[0], res[1]
        if l == 0:
            chip_parts[0]["in_b"] = per[0]
        late_buf[l] = jnp.concatenate([st_m[0:1], st_m[1:2], dgt_m, st_f[0:1], st_f[1:2], dgt_f, st_m[2:3],
                                       dg_final if l == 0 else loss_row], axis=0)
        if l > 0:
            do, dgt_f = res[2], res[3]
    late_all[0] = _transfer([_gather_task([late_buf[0]])], name="gather_last")[0][0]

    early = [_sum8(early_all[l], name=f"sum_early_grads_{l}") for l in range(DEPTH)]
    late = [_sum8(late_all[l], name=f"sum_late_grads_{l}") for l in range(DEPTH)]
    layers = lambda red, lo, hi: jnp.stack([red[l][lo:hi] for l in range(DEPTH)], axis=0)
    grads = {n: layers(early, lo, hi).reshape(weights[n].shape) for n, (lo, hi) in EARLY_ROWS.items()}
    grads.update({n: layers(late, lo, hi).reshape(weights[n].shape) for n, (lo, hi) in LATE_ROWS.items()})
    grads["g_final"] = late[0][7]
    loss = late[DEPTH - 1][7, 0]
    conv_g = layers(early, CONV_ROW, CONV_ROW + CONV_K)
    grads["conv_w"] = lax.dynamic_slice_in_dim(conv_g, me * (D // N_DEV), D // N_DEV, axis=2)
    d_ada_all = jnp.stack([late_all[l][:, 0:6] for l in range(DEPTH)], axis=1).reshape(N_DEV, DEPTH, 6 * D)
    d_loc = jnp.transpose(lax.dynamic_slice_in_dim(d_ada_all, me * ADA_BLK, ADA_BLK, axis=2), (1, 0, 2))
    grads["w_ada"] = _ada_bwd(jnp.transpose(c_all), d_loc, name="ada_bwd")

    out = {}
    for n in REPLICATED + ("conv_w", "w_ada"):
        out[n] = _adam_nd(grads[n], weights[n], mom1[n], mom2[n], name=f"adam_{n}")
    for n, groups in GRAD_GROUP.items():
        parts = [chip_parts[l][group][k] for l in range(DEPTH) for group, k in groups]
        shape = (len(parts),) + parts[0].shape[1:]
        res = _adam(parts, weights[n].reshape(shape), mom1[n].reshape(shape), mom2[n].reshape(shape),
                    name=f"adam_{n}")
        out[n] = [r.reshape(weights[n].shape) for r in res]
    out["w_ffn_in"] = [jnp.transpose(r, (0, 2, 1)) for r in out["w_ffn_in"]]

    grad_x = dx.reshape(1, t, D)
    return (loss, grad_x, *[out[n][0] for n in WEIGHT_ORDER], *[out[n][1] for n in WEIGHT_ORDER],
            *[out[n][2] for n in WEIGHT_ORDER], *[out[n][3] for n in WEIGHT_ORDER])
```

```python
import math

import jax
import jax.numpy as jnp
from jax import lax
from jax.experimental import pallas as pl
from jax.experimental.pallas import tpu as pltpu

F32 = jnp.float32
BF16 = jnp.bfloat16
MESH = pl.DeviceIdType.MESH
AXES = ("x", "y", "c")
N_DEV = 8

D = 1024
DEPTH = 2
EPS = 1e-6
CHUNK = 128
SGU_G = 8
POOL_WINDOWS = (2, 4, 8, 16)
POOL_GC = 256
CONV_K = 31
HALO = 32
SUBLANE = 8
LANE = 128
CONV_STRIP = 128
D_FF = 2816
FF_BLK = D_FF // 4
ADA_BLK = 6 * D // N_DEV

ADAM_LR = 0.001
ADAM_B1 = 0.9
ADAM_B2 = 0.999
ADAM_EPS = 1e-08
ADAM_WD = 0.01
ADAM_STEP = 10

VMEM_LIMIT_V7X = 56 * 1024 * 1024
INV_SQRT2 = 1.0 / math.sqrt(2.0)
INV_SQRT_2PI = 1.0 / math.sqrt(2.0 * math.pi)


def _params(*sem):
    return pltpu.CompilerParams(dimension_semantics=sem if sem else None, vmem_limit_bytes=VMEM_LIMIT_V7X)


def _tile(n, pref):
    if n <= pref:
        return n
    for t in range(pref - pref % 8, 0, -8):
        if n % t == 0:
            return t
    raise ValueError((n, pref))


def _sds(shape, dtype):
    return jax.ShapeDtypeStruct(shape, dtype)


def _sigmoid(x):
    return 1.0 / (1.0 + jnp.exp(-x))


def _gelu(x):
    return 0.5 * x * (1.0 + lax.erf(x * INV_SQRT2))


def _gelu_grad(x):
    return 0.5 * (1.0 + lax.erf(x * INV_SQRT2)) + x * (INV_SQRT_2PI * jnp.exp(-0.5 * x * x))


def _ln_stats(v):
    mu = jnp.mean(v, axis=-1, keepdims=True)
    vc = v - mu
    rstd = lax.rsqrt(jnp.mean(vc * vc, axis=-1, keepdims=True) + EPS)
    return vc * rstd, rstd


def _ln_bwd(dvhat, vhat, rstd):
    return rstd * (dvhat - jnp.mean(dvhat, axis=-1, keepdims=True)
                   - vhat * jnp.mean(dvhat * vhat, axis=-1, keepdims=True))


def _colsum(v):
    return jnp.sum(v, axis=0, keepdims=True)


def _dot(a, b):
    return jnp.dot(a, b, preferred_element_type=F32)


def _dot_nt(a, b):
    return lax.dot_general(a, b, (((1,), (1,)), ((), ())), preferred_element_type=F32)


def _dot_tn(a, b):
    return lax.dot_general(a, b, (((0,), (0,)), ((), ())), preferred_element_type=F32)


def _tril_mask():
    r = lax.broadcasted_iota(jnp.int32, (CHUNK, CHUNK), 0)
    c = lax.broadcasted_iota(jnp.int32, (CHUNK, CHUNK), 1)
    return (r >= c).astype(F32)


def _mesh_pos():
    return tuple(lax.axis_index(a) for a in AXES)


class _Task:
    def __init__(self, arrays, out_shapes, scratch, start, finish, relay=None, relay_at=1.0):
        self.arrays, self.out_shapes, self.scratch = arrays, out_shapes, scratch
        self.start, self.finish, self.relay = start, finish, relay or (lambda ins, outs, sems: None)
        self.relay_at = relay_at


def _hosted(tasks, body, *, name, grid, in_specs, out_specs, out_shape, scratch_shapes=()):
    single = not isinstance(out_shape, (list, tuple))
    out_shape, out_specs = ([out_shape], [out_specs]) if single else (list(out_shape), list(out_specs))
    n_in, n_out, n_scr = len(in_specs), len(out_shape), len(scratch_shapes)
    sizes = [(len(t.arrays), len(t.out_shapes), len(t.scratch)) for t in tasks]
    t_in, t_out, t_scr = (sum(s[k] for s in sizes) for k in range(3))
    any_spec = pl.BlockSpec(memory_space=pl.ANY)

    def wrapped(*refs):
        refs = list(refs)
        ins, refs = refs[:n_in + t_in], refs[n_in + t_in:]
        outs, scr = refs[:n_out + t_out], refs[n_out + t_out:]

        def per_task(fn_name, only=None):
            i0, o0, s0 = n_in, n_out, n_scr
            for t, (ni, no, ns) in zip(tasks, sizes):
                if only is None or t is only:
                    getattr(t, fn_name)(ins[i0:i0 + ni], outs[o0:o0 + no], scr[s0:s0 + ns])
                i0, o0, s0 = i0 + ni, o0 + no, s0 + ns

        if tasks and grid:
            first, last, step, total = None, None, 0, 1
            for d, g in enumerate(grid):
                f, e = pl.program_id(d) == 0, pl.program_id(d) == g - 1
                first, last = (f, e) if first is None else (first & f, last & e)
                step, total = step * g + pl.program_id(d), total * g
            pl.when(first)(lambda: per_task("start"))
            for t in tasks:
                pl.when(step == min(int(t.relay_at * total), total - 1))(lambda t=t: per_task("relay", only=t))
        elif tasks:
            per_task("start")
            per_task("relay")
        body(*ins[:n_in], *outs[:n_out], *scr[:n_scr])
        if tasks and grid:
            pl.when(last)(lambda: per_task("finish"))
        elif tasks:
            per_task("finish")

    call = pl.pallas_call(
        wrapped, name=name, grid=grid,
        in_specs=list(in_specs) + [any_spec] * t_in, out_specs=out_specs + [any_spec] * t_out,
        out_shape=out_shape + [s for t in tasks for s in t.out_shapes],
        scratch_shapes=list(scratch_shapes) + [s for t in tasks for s in t.scratch],
        compiler_params=_params(*(("arbitrary",) * len(grid))))

    def run(*operands):
        res = list(call(*operands, *[a for t in tasks for a in t.arrays]))
        host, rest, per = res[:n_out], res[n_out:], []
        for _, no, _ in sizes:
            per.append(rest[:no])
            rest = rest[no:]
        return (host[0] if single else host), per

    return run


def _transfer(tasks, name):
    return _hosted(tasks, lambda: None, name=name, grid=(), in_specs=[], out_specs=[], out_shape=[])()[1]


def _gather_task(arrs, relay_at=0.75):
    n = len(arrs)

    def plan(ins, outs, sems):
        send_sems, recv_sems, local_sems = sems
        x, y, c = _mesh_pos()
        me, sibling = (x, y, c), (x, y, 1 - c)
        chips = [(1 - x, y), (x, 1 - y), (1 - x, 1 - y)]

        def slot(a, p):
            return outs[a].at[4 * p[0] + 2 * p[1] + p[2]]

        def copy(a, k, block, to, src=None):
            dst = slot(a, block)
            return pltpu.make_async_remote_copy(
                src_ref=dst if src is None else src, dst_ref=dst, send_sem=send_sems.at[a, k],
                recv_sem=recv_sems.at[a, k], device_id=to, device_id_type=MESH)

        def own_block_copies():
            mine = [pltpu.make_async_copy(ins[a], slot(a, me), local_sems.at[a]) for a in range(n)]
            first = []
            for a in range(n):
                first.append(copy(a, 0, me, sibling, src=ins[a]))
                first += [copy(a, 1 + j, me, (*chip, c), src=ins[a]) for j, chip in enumerate(chips)]
            return mine, first

        return c, me, sibling, chips, copy, own_block_copies

    def start(ins, outs, sems):
        mine, first = plan(ins, outs, sems)[-1]()
        for cp in mine + first:
            cp.start()

    def relay(ins, outs, sems):
        c, me, sibling, chips, copy, _ = plan(ins, outs, sems)
        for j, chip in enumerate(chips):
            for a in range(n):
                copy(a, 1 + j, (*chip, c), me).wait_recv()
                copy(a, 4 + j, (*chip, c), sibling).start()

    def finish(ins, outs, sems):
        c, me, sibling, chips, copy, own_block_copies = plan(ins, outs, sems)
        mine, first = own_block_copies()
        passed = [copy(a, 4 + j, (*chip, c), sibling) for j, chip in enumerate(chips) for a in range(n)]
        for a in range(n):
            copy(a, 0, sibling, me).wait_recv()
            for j, chip in enumerate(chips):
                copy(a, 4 + j, (*chip, 1 - c), me).wait_recv()
        for cp in first + passed:
            cp.wait_send()
        for m in mine:
            m.wait()

    return _Task(list(arrs), [_sds((N_DEV,) + a.shape, a.dtype) for a in arrs],
                 [pltpu.SemaphoreType.DMA((n, 7)), pltpu.SemaphoreType.DMA((n, 7)), pltpu.SemaphoreType.DMA((n,))],
                 start, finish, relay, relay_at)


def _sibling_task(arrs):
    n = len(arrs)

    def copies(ins, outs, sems):
        send_sems, recv_sems = sems
        x, y, c = _mesh_pos()
        return [pltpu.make_async_remote_copy(
            src_ref=ins[a].at[2 * q + (1 - c)], dst_ref=outs[a].at[q], send_sem=send_sems.at[a, q],
            recv_sem=recv_sems.at[a, q], device_id=(x, y, 1 - c), device_id_type=MESH)
            for a in range(n) for q in range(4)]

    def start(ins, outs, sems):
        for cp in copies(ins, outs, sems):
            cp.start()

    def finish(ins, outs, sems):
        for cp in copies(ins, outs, sems):
            cp.wait()

    return _Task(list(arrs), [_sds((4,) + a.shape[1:], a.dtype) for a in arrs],
                 [pltpu.SemaphoreType.DMA((n, 4)), pltpu.SemaphoreType.DMA((n, 4))], start, finish)


def _chips_task(arrs):
    n = len(arrs)

    def copies(ins, outs, sems):
        send_sems, recv_sems, local_sems = sems
        x, y, c = _mesh_pos()
        q_me = 2 * x + y
        chips = [(1 - x, y), (x, 1 - y), (1 - x, 1 - y)]
        own = [pltpu.make_async_copy(ins[a].at[q_me], outs[a].at[q_me], local_sems.at[a]) for a in range(n)]
        remote = [pltpu.make_async_remote_copy(
            src_ref=ins[a].at[2 * chip[0] + chip[1]], dst_ref=outs[a].at[q_me], send_sem=send_sems.at[a, j],
            recv_sem=recv_sems.at[a, j], device_id=(*chip, c), device_id_type=MESH)
            for a in range(n) for j, chip in enumerate(chips)]
        return own + remote

    def start(ins, outs, sems):
        for cp in copies(ins, outs, sems):
            cp.start()

    def finish(ins, outs, sems):
        for cp in copies(ins, outs, sems):
            cp.wait()

    return _Task(list(arrs), [_sds(a.shape, a.dtype) for a in arrs],
                 [pltpu.SemaphoreType.DMA((n, 3)), pltpu.SemaphoreType.DMA((n, 3)), pltpu.SemaphoreType.DMA((n,))],
                 start, finish)


def _sibling_sum(arr, land, core, name):
    _, rows, cols = arr.shape
    tr = _tile(rows, 512)
    arr4 = arr.reshape(4, 2, rows, cols)

    def body(c_ref, a_ref, l_ref, o_ref):
        o_ref[...] = (a_ref[...] + l_ref[...]).astype(BF16)

    grid_spec = pltpu.PrefetchScalarGridSpec(
        num_scalar_prefetch=1, grid=(4, rows // tr),
        in_specs=[pl.BlockSpec((None, None, tr, cols), lambda q, r, c_ref: (q, c_ref[0], r, 0)),
                  pl.BlockSpec((None, tr, cols), lambda q, r, c_ref: (q, r, 0))],
        out_specs=pl.BlockSpec((None, tr, cols), lambda q, r, c_ref: (q, r, 0)))
    return pl.pallas_call(body, name=name, grid_spec=grid_spec, out_shape=_sds((4, rows, cols), BF16),
                          compiler_params=_params("arbitrary", "arbitrary"))(core, arr4, land)


def _sibling_sums(arrs, land, core, tag):
    return [_sibling_sum(a, l, core, name=f"rs_sum_{tag}_{k}") for k, (a, l) in enumerate(zip(arrs, land))]


def _adam(gparts, w, m, v, name):
    n_l = len(gparts)
    p, rows, cols = gparts[0].shape
    tr = _tile(rows, 256)
    n_r = rows // tr
    c1 = 1.0 - ADAM_B1 ** ADAM_STEP
    c2 = 1.0 - ADAM_B2 ** ADAM_STEP

    def body(*refs):
        g_refs = refs[:n_l]
        w_ref, m_ref, v_ref, go_ref, d_ref, mo_ref, vo_ref = refs[n_l:]
        layer = pl.program_id(0)
        g = jnp.zeros((tr, cols), F32)
        for li, g_ref in enumerate(g_refs):
            gl = g_ref[0].astype(F32)
            for k in range(1, p):
                gl = gl + g_ref[k].astype(F32)
            g = gl if n_l == 1 else jnp.where(layer == li, gl, g)
        m_new = ADAM_B1 * m_ref[...] + (1.0 - ADAM_B1) * g
        v_new = ADAM_B2 * v_ref[...] + (1.0 - ADAM_B2) * (g * g)
        m_hat = m_new / c1
        v_hat = v_new / c2
        go_ref[...] = g
        d_ref[...] = -ADAM_LR * (m_hat / (jnp.sqrt(v_hat) + ADAM_EPS) + ADAM_WD * w_ref[...])
        mo_ref[...] = m_new
        vo_ref[...] = v_new

    def g_spec(li):
        def index(l, r):
            return (0, jnp.where(l == li, r, jnp.where(l < li, 0, n_r - 1)), 0)
        return pl.BlockSpec((p, tr, cols), index)

    blk = pl.BlockSpec((None, tr, cols), lambda l, r: (l, r, 0))
    return pl.pallas_call(
        body, name=name, grid=(n_l, n_r),
        in_specs=[g_spec(li) for li in range(n_l)] + [blk, blk, blk],
        out_specs=[blk] * 4, out_shape=[_sds((n_l, rows, cols), F32)] * 4,
        compiler_params=_params("arbitrary", "arbitrary"))(*gparts, w, m, v)


def _adam_nd(grad, w, m, v, name):
    shape = w.shape
    cols = shape[-1]
    rows = w.size // cols
    as_rows = lambda a: a.reshape(1, rows, cols)
    out = _adam([as_rows(grad)], as_rows(w), as_rows(m), as_rows(v), name)
    return [o.reshape(shape) for o in out]


def _norm_mod(x, g, sc, sh, name):
    t = x.shape[0]
    tm = _tile(t, 512)

    def body(x_ref, g_ref, sc_ref, sh_ref, h_ref):
        h_ref[...] = _modulated_norm(x_ref[...], g_ref[...], sc_ref[...], sh_ref[...])

    row = pl.BlockSpec((1, D), lambda i: (0, 0))
    blk = pl.BlockSpec((tm, D), lambda i: (i, 0))
    return pl.pallas_call(body, name=name, grid=(t // tm,), in_specs=[blk, row, row, row], out_specs=blk,
                          out_shape=_sds((t, D), BF16), compiler_params=_params("arbitrary"))(x, g, sc, sh)


def _with_tasks(res_per, tasks):
    return res_per if tasks else res_per[0]


def _mm_cols(a, b8, bias8, name, tasks=()):
    t, k = a.shape
    j, _, n = b8.shape
    tm = _tile(t, 1024)

    def body(a_ref, b_ref, bias_ref, o_ref):
        o_ref[...] = (_dot(a_ref[...], b_ref[...]) + bias_ref[...]).astype(BF16)

    return _with_tasks(_hosted(
        tasks, body, name=name, grid=(j, t // tm),
        in_specs=[pl.BlockSpec((tm, k), lambda jj, i: (i, 0)),
                  pl.BlockSpec((None, k, n), lambda jj, i: (jj, 0, 0)),
                  pl.BlockSpec((None, 1, n), lambda jj, i: (jj, 0, 0))],
        out_specs=pl.BlockSpec((None, tm, n), lambda jj, i: (jj, i, 0)),
        out_shape=_sds((j, t, n), BF16))(a, b8, bias8), tasks)


def _halo_before(tm, col):
    return pl.BlockSpec((None, HALO, D), lambda i: (col, jnp.maximum(i * (tm // HALO) - 1, 0), 0))


def _pool_forward(p_ext, t0, rows):
    t = t0 + lax.broadcasted_iota(jnp.int32, (rows, 1), 0)
    out = []
    for gi, win in enumerate(POOL_WINDOWS):
        e = p_ext[:, gi * POOL_GC:(gi + 1) * POOL_GC]
        s, sh = e, 1
        while sh < win:
            s = s + pltpu.roll(s, sh, 0)
            sh *= 2
        cnt = jnp.minimum(t + 1, win).astype(F32)
        out.append(s[HALO:] / cnt - e[HALO:])
    return out


def _fill_shift_bank(bank_ref, ext, causal):
    n = ext.shape[0]
    bank_ref[0] = ext
    for b in range(1, SUBLANE):
        bank_ref[b] = pltpu.roll(ext, b if causal else n - b, 0)


def _branches_fwd(z8, ln_g, ln_b, w_s, bs_b, pool_w, pool_scale, conv_w, conv_b, cln_g, cln_b, name, tasks=()):
    t = z8.shape[1]
    tm = _tile(t, 256)
    n_ext = tm + HALO

    def body(zu_ref, zv_ref, p_ref, a_ref, ag_ref, ph_ref, ah_ref, agh_ref, lng_ref, lnb_ref, ws_ref, bsb_ref,
             wp_ref, ps_ref, cw_ref, cb_ref, clg_ref, clb_ref, sa_ref, sb_ref, sc_ref, cv_ref, bank_ref):
        i = pl.program_id(0)
        has_past = (i > 0).astype(F32)
        u = _gelu(zu_ref[...].astype(F32))
        vhat, _ = _ln_stats(_gelu(zv_ref[...].astype(F32)))
        vb = (vhat * lng_ref[...] + lnb_ref[...]).astype(BF16)
        mask = _tril_mask()
        for g in range(SGU_G):
            cols = slice(g * CHUNK, (g + 1) * CHUNK)
            wm = (ws_ref[g] * mask).astype(BF16)
            for n in range(tm // CHUNK):
                rows = slice(n * CHUNK, (n + 1) * CHUNK)
                mixed = _dot(wm, vb[rows, cols]) + bsb_ref[g]
                sa_ref[rows, cols] = (u[rows, cols] * mixed).astype(BF16)
        p_ext = jnp.concatenate([ph_ref[...].astype(F32) * has_past, p_ref[...].astype(F32)], axis=0)
        pooled = _pool_forward(p_ext, i * tm, tm)
        for gi in range(len(POOL_WINDOWS)):
            cols = slice(gi * POOL_GC, (gi + 1) * POOL_GC)
            y = _dot(pooled[gi].astype(BF16), wp_ref[gi].astype(BF16))
            sb_ref[:, cols] = (y * ps_ref[:, cols]).astype(BF16)
        for cb in range(D // LANE):
            cols = slice(cb * LANE, (cb + 1) * LANE)
            zc = jnp.concatenate(
                [ah_ref[:, cols].astype(F32) * has_past * _sigmoid(agh_ref[:, cols].astype(F32)),
                 a_ref[:, cols].astype(F32) * _sigmoid(ag_ref[:, cols].astype(F32))], axis=0)
            _fill_shift_bank(bank_ref, zc, causal=True)
            for r0 in range(0, tm, CONV_STRIP):
                acc = jnp.zeros((CONV_STRIP, LANE), F32) + cb_ref[:, cols]
                for k in range(CONV_K):
                    hi, lo = divmod(CONV_K - 1 - k, SUBLANE)
                    acc = acc + cw_ref[k:k + 1, cols] * bank_ref[lo, pl.ds(HALO - SUBLANE * hi + r0, CONV_STRIP), :]
                cv_ref[r0:r0 + CONV_STRIP, cols] = acc
        cv = cv_ref[...]
        chat, _ = _ln_stats(cv)
        cl = chat * clg_ref[...] + clb_ref[...]
        sc_ref[...] = (cl * _sigmoid(cl)).astype(BF16)

    def col(j):
        return pl.BlockSpec((None, tm, D), lambda i: (j, i, 0))

    row = pl.BlockSpec((1, D), lambda i: (0, 0))
    full3 = lambda s: pl.BlockSpec(s, lambda i: (0, 0, 0))
    blk = pl.BlockSpec((tm, D), lambda i: (i, 0))
    return _with_tasks(_hosted(
        tasks, body, name=name, grid=(t // tm,),
        in_specs=[col(0), col(1), col(2), col(3), col(4), _halo_before(tm, 2), _halo_before(tm, 3),
                  _halo_before(tm, 4), row, row, full3((SGU_G, CHUNK, CHUNK)), full3((SGU_G, CHUNK, CHUNK)),
                  full3((4, POOL_GC, POOL_GC)), row, pl.BlockSpec((HALO, D), lambda i: (0, 0)), row, row, row],
        out_specs=[blk, blk, blk, blk],
        out_shape=[_sds((t, D), BF16)] * 3 + [_sds((t, D), F32)],
        scratch_shapes=[pltpu.VMEM((SUBLANE, n_ext, LANE), F32)],
    )(z8, z8, z8, z8, z8, z8, z8, z8, ln_g, ln_b, w_s, bs_b, pool_w, pool_scale, conv_w, conv_b, cln_g, cln_b), tasks)


def _proj_merge(sa, sb, sc, w_pa, w_pb, w_pc, z8, name, tasks=()):
    t = sa.shape[0]
    tm = _tile(t, 512)

    def body(sa_ref, sb_ref, sc_ref, wa_ref, wb_ref, wc_ref, g0_ref, g1_ref, g2_ref, ya_ref, yb_ref, yc_ref, m_ref):
        merged = jnp.zeros((tm, D), F32)
        for s_ref, w_ref, g_ref, y_ref in ((sa_ref, wa_ref, g0_ref, ya_ref), (sb_ref, wb_ref, g1_ref, yb_ref),
                                           (sc_ref, wc_ref, g2_ref, yc_ref)):
            y = _dot(s_ref[...], w_ref[...])
            y_ref[...] = y.astype(BF16)
            merged = merged + _sigmoid(g_ref[...].astype(F32)) * y
        m_ref[...] = merged.astype(BF16)

    blk = pl.BlockSpec((tm, D), lambda i: (i, 0))
    wspec = pl.BlockSpec((D, D), lambda i: (0, 0))
    gate = lambda j: pl.BlockSpec((None, tm, D), lambda i: (j, i, 0))
    return _with_tasks(_hosted(
        tasks, body, name=name, grid=(t // tm,),
        in_specs=[blk, blk, blk, wspec, wspec, wspec, gate(5), gate(6), gate(7)],
        out_specs=[blk] * 4, out_shape=[_sds((t, D), BF16)] * 4)(sa, sb, sc, w_pa, w_pb, w_pc, z8, z8, z8), tasks)


def _modulated_norm(xv, g, sc, sh):
    r = lax.rsqrt(jnp.mean(xv * xv, axis=-1, keepdims=True) + EPS)
    return (xv * r * g * (1.0 + sc) + sh).astype(BF16)


def _out_proj(merged, w_out, x, gt, g, sc, sh, name):
    t = x.shape[0]
    tm = _tile(t, 512)

    def body(m_ref, w_ref, x_ref, gt_ref, g_ref, sc_ref, sh_ref, om_ref, x1_ref, h2_ref):
        om = _dot(m_ref[...], w_ref[...])
        om_ref[...] = om
        x1 = x_ref[...] + gt_ref[...] * om
        x1_ref[...] = x1
        h2_ref[...] = _modulated_norm(x1, g_ref[...], sc_ref[...], sh_ref[...])

    blk = pl.BlockSpec((tm, D), lambda i: (i, 0))
    row = pl.BlockSpec((1, D), lambda i: (0, 0))
    return pl.pallas_call(
        body, name=name, grid=(t // tm,),
        in_specs=[blk, pl.BlockSpec((D, D), lambda i: (0, 0)), blk, row, row, row, row],
        out_specs=[blk, blk, blk], out_shape=[_sds((t, D), F32)] * 2 + [_sds((t, D), BF16)],
        compiler_params=_params("arbitrary"))(merged, w_out, x, gt, g, sc, sh)


def _ffn_in(h2, wfi, name, tasks=()):
    t = h2.shape[0]
    tm = _tile(t, 512)

    def body(h_ref, w_ref, gu_ref, f_ref):
        hv = h_ref[...]
        gp = _dot_nt(hv, w_ref[0])
        up = _dot_nt(hv, w_ref[1])
        gu_ref[0] = gp.astype(BF16)
        gu_ref[1] = up.astype(BF16)
        f_ref[...] = (gp * _sigmoid(gp) * up).astype(BF16)

    return _with_tasks(_hosted(
        tasks, body, name=name, grid=(4, t // tm),
        in_specs=[pl.BlockSpec((tm, D), lambda j, i: (i, 0)),
                  pl.BlockSpec((2, None, FF_BLK, D), lambda j, i: (0, j, 0, 0))],
        out_specs=[pl.BlockSpec((2, None, tm, FF_BLK), lambda j, i: (0, j, i, 0)),
                   pl.BlockSpec((None, tm, FF_BLK), lambda j, i: (j, i, 0))],
        out_shape=[_sds((2, 4, t, FF_BLK), BF16), _sds((4, t, FF_BLK), BF16)])(h2, wfi), tasks)


def _ffn_out(f4, wfo4, x1, gt, g, sc, sh, name, tasks=()):
    t = x1.shape[0]
    tm = _tile(t, 512)
    with_norm = g is not None

    def body(f_ref, w_ref, x_ref, gt_ref, *rest):
        o_ref, x2_ref = rest[-3:-1] if with_norm else rest[-2:]
        j = pl.program_id(1)

        @pl.when(j == 0)
        def _():
            o_ref[...] = jnp.zeros_like(o_ref)

        o_ref[...] += _dot(f_ref[...], w_ref[...])

        @pl.when(j == 3)
        def _():
            x2 = x_ref[...] + gt_ref[...] * o_ref[...]
            x2_ref[...] = x2
            if with_norm:
                g_ref, sc_ref, sh_ref = rest[:3]
                rest[-1][...] = _modulated_norm(x2, g_ref[...], sc_ref[...], sh_ref[...])

    blk = pl.BlockSpec((tm, D), lambda i, j: (i, 0))
    row = pl.BlockSpec((1, D), lambda i, j: (0, 0))
    norm_args = [g, sc, sh] if with_norm else []
    return _with_tasks(_hosted(
        tasks, body, name=name, grid=(t // tm, 4),
        in_specs=[pl.BlockSpec((None, tm, FF_BLK), lambda i, j: (j, i, 0)),
                  pl.BlockSpec((None, FF_BLK, D), lambda i, j: (j, 0, 0)), blk, row] + [row] * len(norm_args),
        out_specs=[blk, blk] + [blk] * with_norm,
        out_shape=[_sds((t, D), F32)] * 2 + [_sds((t, D), BF16)] * with_norm)(f4, wfo4, x1, gt, *norm_args), tasks)


def _gate_grads(dx, o_ref, gt_ref, do_ref, dgt_ref):
    do_ref[...] = (dx * gt_ref[...]).astype(BF16)
    dgt_ref[...] += _colsum(dx * o_ref[...])


def _final_loss(x, g, target, o, gt, name):
    t = x.shape[0]
    tm = _tile(t, 512)

    def body(x_ref, g_ref, t_ref, o_ref, gt_ref, loss_ref, dx_ref, dg_ref, do_ref, dgt_ref):
        @pl.when(pl.program_id(0) == 0)
        def _():
            for ref in (loss_ref, dg_ref, dgt_ref):
                ref[...] = jnp.zeros_like(ref)

        xv = x_ref[...]
        r = lax.rsqrt(jnp.mean(xv * xv, axis=-1, keepdims=True) + EPS)
        xn = xv * r
        diff = xn * g_ref[...] - t_ref[...]
        loss_ref[...] += 0.5 * jnp.sum(jnp.mean(diff * diff, axis=-1, keepdims=True))
        dy = diff * (1.0 / D)
        dg_ref[...] += _colsum(dy * xn)
        dxn = dy * g_ref[...]
        dx = r * (dxn - xn * jnp.mean(dxn * xn, axis=-1, keepdims=True))
        dx_ref[...] = dx
        _gate_grads(dx, o_ref, gt_ref, do_ref, dgt_ref)

    blk = pl.BlockSpec((tm, D), lambda i: (i, 0))
    row = pl.BlockSpec((1, D), lambda i: (0, 0))
    return pl.pallas_call(
        body, name=name, grid=(t // tm,), in_specs=[blk, row, blk, blk, row],
        out_specs=[pl.BlockSpec((8, 128), lambda i: (0, 0)), blk, row, blk, row],
        out_shape=[_sds((8, 128), F32), _sds((t, D), F32), _sds((1, D), F32), _sds((t, D), BF16), _sds((1, D), F32)],
        compiler_params=_params("arbitrary"))(x, g, target, o, gt)


def _norm_mod_bwd(dh, x, dres, g, sc, o, gt, name, tasks=()):
    t = x.shape[0]
    tm = _tile(t, 512)
    with_gate = o is not None

    def body(dh_ref, x_ref, dr_ref, g_ref, sc_ref, *rest):
        dx_ref, st_ref = rest[2:4] if with_gate else rest

        @pl.when(pl.program_id(0) == 0)
        def _():
            st_ref[...] = jnp.zeros_like(st_ref)
            if with_gate:
                rest[5][...] = jnp.zeros_like(rest[5])

        xv, dhv = x_ref[...], dh_ref[...]
        r = lax.rsqrt(jnp.mean(xv * xv, axis=-1, keepdims=True) + EPS)
        xn = xv * r
        gv, mod = g_ref[...], 1.0 + sc_ref[...]
        st_ref[0:1, :] += _colsum(dhv)
        st_ref[1:2, :] += _colsum(dhv * xn * gv)
        st_ref[2:3, :] += _colsum(dhv * xn * mod)
        dxn = dhv * gv * mod
        dx = dr_ref[...] + r * (dxn - xn * jnp.mean(dxn * xn, axis=-1, keepdims=True))
        dx_ref[...] = dx
        if with_gate:
            _gate_grads(dx, rest[0], rest[1], rest[4], rest[5])

    blk = pl.BlockSpec((tm, D), lambda i: (i, 0))
    row = pl.BlockSpec((1, D), lambda i: (0, 0))
    gate_args = [o, gt] if with_gate else []
    return _with_tasks(_hosted(
        tasks, body, name=name, grid=(t // tm,), in_specs=[blk, blk, blk, row, row] + [blk, row] * with_gate,
        out_specs=[blk, pl.BlockSpec((3, D), lambda i: (0, 0))] + [blk, row] * with_gate,
        out_shape=[_sds((t, D), F32), _sds((3, D), F32)] + [_sds((t, D), BF16), _sds((1, D), F32)] * with_gate,
    )(dh, x, dres, g, sc, *gate_args), tasks)


def _ffn_bwd_act(do, wfo4, gu, name, tasks=()):
    t = do.shape[0]
    tm = _tile(t, 512)

    def body(do_ref, w_ref, gu_ref, dgu_ref):
        df = _dot_nt(do_ref[...], w_ref[...])
        gp, up = gu_ref[0].astype(F32), gu_ref[1].astype(F32)
        sg = _sigmoid(gp)
        dgu_ref[0] = (df * up * (sg * (1.0 + gp * (1.0 - sg)))).astype(BF16)
        dgu_ref[1] = (df * (gp * sg)).astype(BF16)

    gu_spec = pl.BlockSpec((2, None, tm, FF_BLK), lambda j, i: (0, j, i, 0))
    return _with_tasks(_hosted(
        tasks, body, name=name, grid=(4, t // tm),
        in_specs=[pl.BlockSpec((tm, D), lambda j, i: (i, 0)),
                  pl.BlockSpec((None, FF_BLK, D), lambda j, i: (j, 0, 0)), gu_spec],
        out_specs=gu_spec, out_shape=_sds((2, 4, t, FF_BLK), BF16))(do, wfo4, gu), tasks)


def _mm_nt_sum(a8, b8, name, b_is_kn=False, tasks=()):
    j, t, k = a8.shape
    n = b8.shape[2] if b_is_kn else b8.shape[1]
    tm = _tile(t, 1024 if j > 1 else 512)
    out_dtype = F32 if j > 1 else BF16
    dot = _dot if b_is_kn else _dot_nt

    def body(a_ref, b_ref, o_ref):
        if j == 1:
            o_ref[...] = dot(a_ref[...], b_ref[...]).astype(out_dtype)
            return

        @pl.when(pl.program_id(1) == 0)
        def _():
            o_ref[...] = jnp.zeros_like(o_ref)

        o_ref[...] += dot(a_ref[...], b_ref[...])

    return _with_tasks(_hosted(
        tasks, body, name=name, grid=(t // tm, j),
        in_specs=[pl.BlockSpec((None, tm, k), lambda i, jj: (jj, i, 0)),
                  pl.BlockSpec((None,) + b8.shape[1:], lambda i, jj: (jj, 0, 0))],
        out_specs=pl.BlockSpec((tm, n), lambda i, jj: (i, 0)), out_shape=_sds((t, n), out_dtype))(a8, b8), tasks)


def _mm_tn(a8, b8, name, tasks=()):
    ja, t, m = a8.shape
    jb, _, n = b8.shape
    j = max(ja, jb)
    tk = _tile(t, 2048)

    def body(a_ref, b_ref, o_ref):
        @pl.when(pl.program_id(1) == 0)
        def _():
            o_ref[...] = jnp.zeros_like(o_ref)

        o_ref[...] += _dot_tn(a_ref[...], b_ref[...])

    return _with_tasks(_hosted(
        tasks, body, name=name, grid=(j, t // tk),
        in_specs=[pl.BlockSpec((None, tk, m), (lambda jj, kk: (jj, kk, 0)) if ja > 1 else (lambda jj, kk: (0, kk, 0))),
                  pl.BlockSpec((None, tk, n), (lambda jj, kk: (jj, kk, 0)) if jb > 1 else (lambda jj, kk: (0, kk, 0)))],
        out_specs=pl.BlockSpec((None, m, n), lambda jj, kk: (jj, 0, 0)), out_shape=_sds((j, m, n), F32))(a8, b8),
        tasks)


def _merge_bwd(dom, w_out, z8, ya, yb, yc, name, tasks=()):
    t = dom.shape[0]
    tm = _tile(t, 512)

    def body(dom_ref, w_ref, g0_ref, g1_ref, g2_ref, ya_ref, yb_ref, yc_ref, dya_ref, dyb_ref, dyc_ref, dzg_ref,
             db_ref):
        @pl.when(pl.program_id(0) == 0)
        def _():
            db_ref[...] = jnp.zeros_like(db_ref)

        dm = _dot_nt(dom_ref[...], w_ref[...])
        for k, (g_ref, y_ref, dy_ref) in enumerate(((g0_ref, ya_ref, dya_ref), (g1_ref, yb_ref, dyb_ref),
                                                    (g2_ref, yc_ref, dyc_ref))):
            sg = _sigmoid(g_ref[...].astype(F32))
            dy_ref[...] = (dm * sg).astype(BF16)
            dzg = dm * y_ref[...].astype(F32) * (sg * (1.0 - sg))
            dzg_ref[k] = dzg.astype(BF16)
            db_ref[k:k + 1, :] += _colsum(dzg)

    blk = pl.BlockSpec((tm, D), lambda i: (i, 0))
    gate = lambda j: pl.BlockSpec((None, tm, D), lambda i: (j, i, 0))
    return _with_tasks(_hosted(
        tasks, body, name=name, grid=(t // tm,),
        in_specs=[blk, pl.BlockSpec((D, D), lambda i: (0, 0)), gate(5), gate(6), gate(7), blk, blk, blk],
        out_specs=[blk, blk, blk, pl.BlockSpec((3, tm, D), lambda i: (0, i, 0)), pl.BlockSpec((3, D), lambda i: (0, 0))],
        out_shape=[_sds((t, D), BF16)] * 3 + [_sds((3, t, D), BF16), _sds((3, D), F32)],
    )(dom, w_out, z8, z8, z8, ya, yb, yc), tasks)


def _branches_bwd(z8, cv, dsa, dsb, dsc, dzg, ln_g, ln_b, w_s, bs_b, pool_w, pool_scale, conv_w, cln_g, cln_b, name,
                  tasks=()):
    t = z8.shape[1]
    tm = _tile(t, 256)
    n_ext = tm + HALO
    n_tiles = t // tm

    def body(zu_ref, zv_ref, p_ref, a_ref, ag_ref, ph_ref, ah_ref, agh_ref, cv_ref, cvf_ref, dsa_ref, dsb_ref,
             dsbf_ref, dsc_ref, dscf_ref, dzg_ref, lng_ref, lnb_ref, ws_ref, bsb_ref, wp_ref, ps_ref, cw_ref,
             clg_ref, clb_ref, dz_ref, dbin_ref, rows_ref, dws_ref, dbs_ref, dwp_ref, dcw_ref, mixed_scr, dvln_scr,
             dcv_scr, zbank_ref, dbank_ref, dcw8_scr):
        i = pl.program_id(0)

        @pl.when(i == 0)
        def _():
            for ref in (dbin_ref, rows_ref, dws_ref, dbs_ref, dwp_ref, dcw8_scr):
                ref[...] = jnp.zeros_like(ref)

        has_past = (i > 0).astype(F32)
        has_next = (i < n_tiles - 1).astype(F32)

        def emit(j, val):
            dz_ref[j] = val.astype(BF16)
            dbin_ref[j:j + 1, :] += _colsum(val)

        zu, zv = zu_ref[...].astype(F32), zv_ref[...].astype(F32)
        u = _gelu(zu)
        vhat, v_rstd = _ln_stats(_gelu(zv))
        vb = (vhat * lng_ref[...] + lnb_ref[...]).astype(BF16)
        dsa = dsa_ref[...].astype(F32)
        dmixed = dsa * u
        dmb = dmixed.astype(BF16)
        mask = _tril_mask()
        lane = lax.broadcasted_iota(jnp.int32, (CHUNK, CHUNK), 1)
        for g in range(SGU_G):
            cols = slice(g * CHUNK, (g + 1) * CHUNK)
            wm = (ws_ref[g] * mask).astype(BF16)
            dws = jnp.zeros((CHUNK, CHUNK), F32)
            dbs = jnp.zeros((CHUNK, 1), F32)
            for n in range(tm // CHUNK):
                rows = slice(n * CHUNK, (n + 1) * CHUNK)
                mixed_scr[rows, cols] = _dot(wm, vb[rows, cols]) + bsb_ref[g]
                dvln_scr[rows, cols] = _dot_tn(wm, dmb[rows, cols])
                dws = dws + _dot_nt(dmb[rows, cols], vb[rows, cols])
                dbs = dbs + jnp.sum(dmixed[rows, cols], axis=1, keepdims=True)
            dws_ref[g] += dws
            dbs_ref[...] += jnp.where(lane == g, dbs, 0.0)
        emit(0, dsa * mixed_scr[...] * _gelu_grad(zu))
        dvln = dvln_scr[...]
        rows_ref[0:1, :] += _colsum(dvln * vhat)
        rows_ref[1:2, :] += _colsum(dvln)
        emit(1, _ln_bwd(dvln * lng_ref[...], vhat, v_rstd) * _gelu_grad(zv))

        p_ext = jnp.concatenate([ph_ref[...].astype(F32) * has_past, p_ref[...].astype(F32)], axis=0)
        pooled = _pool_forward(p_ext, i * tm, tm)
        dsb = dsb_ref[...].astype(F32)
        dpl_ext = jnp.concatenate([dsb, dsbf_ref[...].astype(F32) * has_next], axis=0) * ps_ref[...]
        t_ext = i * tm + lax.broadcasted_iota(jnp.int32, (n_ext, 1), 0)
        dp_parts = []
        for gi, win in enumerate(POOL_WINDOWS):
            cols = slice(gi * POOL_GC, (gi + 1) * POOL_GC)
            pooled_b = pooled[gi].astype(BF16)
            wpb = wp_ref[gi].astype(BF16)
            rows_ref[2:3, cols] += _colsum(dsb[:, cols] * _dot(pooled_b, wpb))
            dplb = dpl_ext[:, cols].astype(BF16)
            dwp_ref[gi] += _dot_tn(pooled_b, dplb[:tm])
            dpooled = _dot_nt(dplb, wpb)
            s, sh = dpooled / jnp.minimum(t_ext + 1, win).astype(F32), 1
            while sh < win:
                s = s + pltpu.roll(s, n_ext - sh, 0)
                sh *= 2
            dp_parts.append(s[:tm] - dpooled[:tm])
        emit(2, jnp.concatenate(dp_parts, axis=1))

        cv_ext = jnp.concatenate([cv_ref[...], cvf_ref[...]], axis=0)
        chat, c_rstd = _ln_stats(cv_ext)
        cl = chat * clg_ref[...] + clb_ref[...]
        sg = _sigmoid(cl)
        dsc_ext = jnp.concatenate([dsc_ref[...].astype(F32), dscf_ref[...].astype(F32)], axis=0)
        dcl = dsc_ext * (sg * (1.0 + cl * (1.0 - sg)))
        rows_ref[4:5, :] += _colsum((dcl * chat)[:tm])
        rows_ref[5:6, :] += _colsum(dcl[:tm])
        in_seq = jnp.concatenate([jnp.ones((tm, 1), F32), jnp.zeros((HALO, 1), F32) + has_next], axis=0)
        dcv = jnp.where(in_seq > 0.0, _ln_bwd(dcl * clg_ref[...], chat, c_rstd), 0.0)
        rows_ref[3:4, :] += _colsum(dcv[:tm])
        dcv_scr[...] = dcv
        for cb in range(D // LANE):
            cols = slice(cb * LANE, (cb + 1) * LANE)
            zc = jnp.concatenate(
                [ah_ref[:, cols].astype(F32) * has_past * _sigmoid(agh_ref[:, cols].astype(F32)),
                 a_ref[:, cols].astype(F32) * _sigmoid(ag_ref[:, cols].astype(F32))], axis=0)
            _fill_shift_bank(zbank_ref, zc, causal=True)
            _fill_shift_bank(dbank_ref, dcv_scr[:, cols], causal=False)
            for r0 in range(0, tm, CONV_STRIP):
                rows = slice(r0, r0 + CONV_STRIP)
                dcv_s = dcv_scr[rows, cols]
                dzc = jnp.zeros((CONV_STRIP, LANE), F32)
                for k in range(CONV_K):
                    hi, lo = divmod(CONV_K - 1 - k, SUBLANE)
                    z_win = zbank_ref[lo, pl.ds(HALO - SUBLANE * hi + r0, CONV_STRIP), :]
                    dcw8_scr[k, :, cols] += jnp.sum((dcv_s * z_win).reshape(CONV_STRIP // SUBLANE, SUBLANE, LANE), axis=0)
                    dzc = dzc + cw_ref[k:k + 1, cols] * dbank_ref[lo, pl.ds(SUBLANE * hi + r0, CONV_STRIP), :]
                a_s = a_ref[rows, cols].astype(F32)
                sga = _sigmoid(ag_ref[rows, cols].astype(F32))
                dza = dzc * sga
                dzag = dzc * a_s * (sga * (1.0 - sga))
                dz_ref[3, rows, cols] = dza.astype(BF16)
                dz_ref[4, rows, cols] = dzag.astype(BF16)
                dbin_ref[3:4, cols] += _colsum(dza)
                dbin_ref[4:5, cols] += _colsum(dzag)
        for k in range(3):
            dz_ref[5 + k] = dzg_ref[k]

        @pl.when(i == n_tiles - 1)
        def _():
            dcw_ref[...] = jnp.sum(dcw8_scr[...], axis=1)

    def col(j):
        return pl.BlockSpec((None, tm, D), lambda i: (j, i, 0))

    blk = pl.BlockSpec((tm, D), lambda i: (i, 0))
    after = pl.BlockSpec((HALO, D), lambda i: (jnp.minimum((i + 1) * (tm // HALO), t // HALO - 1), 0))
    row = pl.BlockSpec((1, D), lambda i: (0, 0))
    full2 = lambda s: pl.BlockSpec(s, lambda i: (0, 0))
    full3 = lambda s: pl.BlockSpec(s, lambda i: (0, 0, 0))
    return _with_tasks(_hosted(
        tasks, body, name=name, grid=(n_tiles,),
        in_specs=[col(0), col(1), col(2), col(3), col(4), _halo_before(tm, 2), _halo_before(tm, 3),
                  _halo_before(tm, 4), blk, after, blk, blk, after, blk, after,
                  pl.BlockSpec((3, tm, D), lambda i: (0, i, 0)), row, row, full3((SGU_G, CHUNK, CHUNK)),
                  full3((SGU_G, CHUNK, CHUNK)), full3((4, POOL_GC, POOL_GC)), row, full2((HALO, D)), row, row],
        out_specs=[pl.BlockSpec((8, tm, D), lambda i: (0, i, 0)), full2((8, D)), full2((8, D)),
                   full3((SGU_G, CHUNK, CHUNK)), full2((CHUNK, CHUNK)), full3((4, POOL_GC, POOL_GC)),
                   full2((HALO, D))],
        out_shape=[_sds((8, t, D), BF16), _sds((8, D), F32), _sds((8, D), F32), _sds((SGU_G, CHUNK, CHUNK), F32),
                   _sds((CHUNK, CHUNK), F32), _sds((4, POOL_GC, POOL_GC), F32), _sds((HALO, D), F32)],
        scratch_shapes=[pltpu.VMEM((tm, D), F32), pltpu.VMEM((tm, D), F32), pltpu.VMEM((n_ext, D), F32),
                        pltpu.VMEM((SUBLANE, n_ext, LANE), F32), pltpu.VMEM((SUBLANE, n_ext, LANE), F32),
                        pltpu.VMEM((HALO, SUBLANE, D), F32)],
    )(z8, z8, z8, z8, z8, z8, z8, z8, cv, cv, dsa, dsb, dsb, dsc, dsc, dzg, ln_g, ln_b, w_s, bs_b, pool_w,
      pool_scale, conv_w, cln_g, cln_b), tasks)


def _ada_fwd(c_all, w_ada, b_loc, name):
    def body(c_ref, w_ref, b_ref, o_ref):
        cv = c_ref[...]
        ca = (cv * _sigmoid(cv)).astype(BF16)
        for l in range(DEPTH):
            o_ref[l] = _dot(ca, w_ref[l].astype(BF16)) + b_ref[l]

    return pl.pallas_call(body, name=name, out_shape=_sds((DEPTH, N_DEV, ADA_BLK), F32),
                          compiler_params=_params())(c_all, w_ada, b_loc)


def _ada_bwd(c_all_t, d_loc, name):
    def body(c_ref, d_ref, o_ref):
        cv = c_ref[...]
        ca = cv * _sigmoid(cv)
        for l in range(DEPTH):
            acc = jnp.zeros((D, ADA_BLK), F32)
            for j in range(N_DEV):
                acc = acc + ca[:, j:j + 1] * d_ref[l, j:j + 1, :]
            o_ref[l] = acc

    return pl.pallas_call(body, name=name, out_shape=_sds((DEPTH, D, ADA_BLK), F32),
                          compiler_params=_params())(c_all_t, d_loc)


def _sum8(g8, name):
    _, rows, cols = g8.shape
    tr = _tile(rows, 256)

    def body(g_ref, o_ref):
        acc = g_ref[0]
        for k in range(1, N_DEV):
            acc = acc + g_ref[k]
        o_ref[...] = acc

    return pl.pallas_call(body, name=name, grid=(rows // tr,),
                          in_specs=[pl.BlockSpec((N_DEV, tr, cols), lambda r: (0, r, 0))],
                          out_specs=pl.BlockSpec((tr, cols), lambda r: (r, 0)), out_shape=_sds((rows, cols), F32),
                          compiler_params=_params("arbitrary"))(g8)


PROJ = ("w_pa", "w_pb", "w_pc", "w_out")
FWD_GATHERS = {
    (0, "in_proj"): tuple((n, 0) for n in PROJ) + (("w_ffn_out", 0),),
    (0, "branches"): (("w_ffn_in", 0),),
    (0, "ffn_in"): (("w_in", 1),),
    (1, "in_proj"): (("w_ffn_in", 1),),
    (1, "branches"): tuple((n, 1) for n in PROJ) + (("w_ffn_out", 1),),
}
FWD_GATHER_FILLS_KERNEL = ((0, "in_proj"), (0, "ffn_in"), (1, "branches"))
GRAD_GROUP = {"w_in": (("in_a", 0), ("in_b", 0)), "pool_w": (("in_a", 1),), "w_pa": (("proj", 0),),
              "w_pb": (("proj", 1),), "w_pc": (("proj", 2),), "w_out": (("proj", 3),), "w_ffn_in": (("ffn", 0),),
              "w_ffn_out": (("ffn", 1),)}


EARLY_ROWS = dict(b_in=(0, 8), sgu_ln_g=(8, 9), sgu_ln_b=(9, 10), sgu_w_s=(10, 138), sgu_b_s=(138, 139),
                  pool_scale=(139, 140), conv_b=(140, 141), conv_ln_g=(141, 142), conv_ln_b=(142, 143), g_ffn=(143, 144))
CONV_ROW = 144
LATE_ROWS = dict(b_ada=(0, 6), g_mix=(6, 7))


def _run(factory, *args, tasks=(), **kw):
    out = factory(*args, tasks=tasks, **kw)
    return out if tasks else (out, [])


REPLICATED = ("b_ada", "g_mix", "b_in", "sgu_ln_g", "sgu_ln_b", "sgu_w_s", "sgu_b_s", "pool_scale", "conv_b",
              "conv_ln_g", "conv_ln_b", "g_ffn", "g_final")
WEIGHT_ORDER = ("w_ada", "b_ada", "g_mix", "w_in", "b_in", "sgu_ln_g", "sgu_ln_b", "sgu_w_s", "sgu_b_s", "w_pa",
                "pool_w", "pool_scale", "w_pb", "conv_w", "conv_b", "conv_ln_g", "conv_ln_b", "w_pc", "w_out",
                "g_ffn", "w_ffn_in", "w_ffn_out", "g_final")


def _rows(a):
    return a.reshape(-1, D)


def kernel(x, c, w_ada, b_ada, g_mix, w_in, b_in, sgu_ln_g, sgu_ln_b, sgu_w_s, sgu_b_s, w_pa, pool_w, pool_scale, w_pb, conv_w, conv_b, conv_ln_g, conv_ln_b, w_pc, w_out, g_ffn, w_ffn_in, w_ffn_out, g_final, loss_target, m_w_ada, m_b_ada, m_g_mix, m_w_in, m_b_in, m_sgu_ln_g, m_sgu_ln_b, m_sgu_w_s, m_sgu_b_s, m_w_pa, m_pool_w, m_pool_scale, m_w_pb, m_conv_w, m_conv_b, m_conv_ln_g, m_conv_ln_b, m_w_pc, m_w_out, m_g_ffn, m_w_ffn_in, m_w_ffn_out, m_g_final, v_w_ada, v_b_ada, v_g_mix, v_w_in, v_b_in, v_sgu_ln_g, v_sgu_ln_b, v_sgu_w_s, v_sgu_b_s, v_w_pa, v_pool_w, v_pool_scale, v_w_pb, v_conv_w, v_conv_b, v_conv_ln_g, v_conv_ln_b, v_w_pc, v_w_out, v_g_ffn, v_w_ffn_in, v_w_ffn_out, v_g_final):
    weights = dict(w_ada=w_ada, b_ada=b_ada, g_mix=g_mix, w_in=w_in, b_in=b_in, sgu_ln_g=sgu_ln_g, sgu_ln_b=sgu_ln_b,
                   sgu_w_s=sgu_w_s, sgu_b_s=sgu_b_s, w_pa=w_pa, pool_w=pool_w, pool_scale=pool_scale, w_pb=w_pb,
                   conv_w=conv_w, conv_b=conv_b, conv_ln_g=conv_ln_g, conv_ln_b=conv_ln_b, w_pc=w_pc, w_out=w_out,
                   g_ffn=g_ffn, w_ffn_in=w_ffn_in, w_ffn_out=w_ffn_out, g_final=g_final)
    mom1 = dict(w_ada=m_w_ada, b_ada=m_b_ada, g_mix=m_g_mix, w_in=m_w_in, b_in=m_b_in, sgu_ln_g=m_sgu_ln_g,
                sgu_ln_b=m_sgu_ln_b, sgu_w_s=m_sgu_w_s, sgu_b_s=m_sgu_b_s, w_pa=m_w_pa, pool_w=m_pool_w,
                pool_scale=m_pool_scale, w_pb=m_w_pb, conv_w=m_conv_w, conv_b=m_conv_b, conv_ln_g=m_conv_ln_g,
                conv_ln_b=m_conv_ln_b, w_pc=m_w_pc, w_out=m_w_out, g_ffn=m_g_ffn, w_ffn_in=m_w_ffn_in,
                w_ffn_out=m_w_ffn_out, g_final=m_g_final)
    mom2 = dict(w_ada=v_w_ada, b_ada=v_b_ada, g_mix=v_g_mix, w_in=v_w_in, b_in=v_b_in, sgu_ln_g=v_sgu_ln_g,
                sgu_ln_b=v_sgu_ln_b, sgu_w_s=v_sgu_w_s, sgu_b_s=v_sgu_b_s, w_pa=v_w_pa, pool_w=v_pool_w,
                pool_scale=v_pool_scale, w_pb=v_w_pb, conv_w=v_conv_w, conv_b=v_conv_b, conv_ln_g=v_conv_ln_g,
                conv_ln_b=v_conv_ln_b, w_pc=v_w_pc, w_out=v_w_out, g_ffn=v_g_ffn, w_ffn_in=v_w_ffn_in,
                w_ffn_out=v_w_ffn_out, g_final=v_g_final)

    for group in (weights, mom1, mom2):
        group["w_ffn_in"] = jnp.transpose(group["w_ffn_in"], (0, 2, 1))

    t = x.shape[1]
    xs = x.reshape(t, D)
    target = loss_target.reshape(t, D)
    me = 4 * lax.axis_index("x") + 2 * lax.axis_index("y") + lax.axis_index("c")
    core = lax.axis_index("c").astype(jnp.int32).reshape(1)

    bf = lambda n, l: weights[n][l].astype(BF16)
    (first,) = _transfer([_gather_task([bf("w_in", 0), c, pool_w, conv_w])], name="gather_first")
    w_in0, c_all, pool_all, conv_all = first
    full = [dict(w_in=w_in0)] + [dict() for _ in range(1, DEPTH)]

    def gather_at(l, stage):
        names = FWD_GATHERS.get((l, stage), ())
        relay_at = 1.0 if (l, stage) in FWD_GATHER_FILLS_KERNEL else 0.75
        return [_gather_task([bf(n, ll) for n, ll in names], relay_at)] if names else []

    def landed(l, stage, per):
        for (n, ll), arr in zip(FWD_GATHERS.get((l, stage), ()), per[0] if per else ()):
            full[ll][n] = arr

    c_all = c_all.reshape(N_DEV, D)
    pool_full = jnp.transpose(pool_all, (1, 2, 0, 3, 4)).reshape(DEPTH, 4, POOL_GC, POOL_GC)
    conv_full = jnp.transpose(conv_all, (1, 2, 0, 3)).reshape(DEPTH, CONV_K, D)
    conv_full = jnp.pad(conv_full, ((0, 0), (0, HALO - CONV_K), (0, 0)))

    b_loc = lax.dynamic_slice_in_dim(b_ada, me * ADA_BLK, ADA_BLK, axis=1).reshape(DEPTH, 1, ADA_BLK)
    ada_part = _ada_fwd(c_all, w_ada, b_loc, name="ada_fwd")
    ((ada_all,),) = _transfer([_gather_task([ada_part])], name="gather_ada")
    ada = lax.dynamic_index_in_dim(ada_all, me, axis=2, keepdims=False)
    ada = jnp.transpose(ada, (1, 0, 2)).reshape(DEPTH, 6, 1, D)

    bs_b = jnp.broadcast_to(sgu_b_s[..., None], (DEPTH, SGU_G, CHUNK, CHUNK))

    saved = []
    xl = xs
    h = _norm_mod(xl, g_mix[0].reshape(1, D), ada[0, 1], ada[0, 0], name="norm_mix_0")
    for l in range(DEPTH):
        w = full[l]
        sh_m, sc_m, gt_m, sh_f, sc_f, gt_f = (ada[l, k] for k in range(6))
        row = lambda a: a[l].reshape(1, D)
        z8, per = _run(_mm_cols, h, w["w_in"], b_in[l].reshape(8, 1, D), name=f"in_proj_{l}",
                       tasks=gather_at(l, "in_proj"))
        landed(l, "in_proj", per)
        (sa, sb, sc, cv), per = _run(
            _branches_fwd, z8, row(sgu_ln_g), row(sgu_ln_b), sgu_w_s[l], bs_b[l], pool_full[l], row(pool_scale),
            conv_full[l], row(conv_b), row(conv_ln_g), row(conv_ln_b), name=f"branches_{l}",
            tasks=gather_at(l, "branches"))
        landed(l, "branches", per)
        wpa, wpb, wpc, wout = (w[n].reshape(D, D) for n in PROJ)
        (ya, yb, yc, merged), per = _run(_proj_merge, sa, sb, sc, wpa, wpb, wpc, z8, name=f"proj_merge_{l}",
                                         tasks=gather_at(l, "proj_merge"))
        landed(l, "proj_merge", per)
        om, x1, h2 = _out_proj(merged, wout, xl, gt_m, row(g_ffn), sc_f, sh_f, name=f"out_proj_{l}")
        wfi = w["w_ffn_in"].reshape(2, 4, FF_BLK, D)
        (gu, f4), per = _run(_ffn_in, h2, wfi, name=f"ffn_in_{l}", tasks=gather_at(l, "ffn_in"))
        landed(l, "ffn_in", per)
        wfo4 = w["w_ffn_out"].reshape(4, FF_BLK, D)
        nxt = (g_mix[l + 1].reshape(1, D), ada[l + 1, 1], ada[l + 1, 0]) if l + 1 < DEPTH else (None, None, None)
        res, per = _run(_ffn_out, f4, wfo4, x1, gt_f, *nxt, name=f"ffn_out_{l}", tasks=gather_at(l, "ffn_out"))
        landed(l, "ffn_out", per)
        o, x2 = res[0], res[1]
        saved.append(dict(x=xl, h=h, z8=z8, sa=sa, sb=sb, sc=sc, cv=cv, ya=ya, yb=yb, yc=yc, merged=merged, om=om,
                          x1=x1, h2=h2, gu=gu, f4=f4, o=o, wpa=wpa, wpb=wpb, wpc=wpc, wout=wout, wfi=wfi, wfo4=wfo4))
        xl = x2
        h = res[2] if l + 1 < DEPTH else None

    loss_tile, dx, dg_final, do, dgt_f = _final_loss(xl, g_final.reshape(1, D), target, saved[-1]["o"],
                                                     ada[DEPTH - 1, 5], name="final_loss")
    loss_row = jnp.broadcast_to(loss_tile[0:1, 0:1], (1, D))

    chip_parts = [dict() for _ in range(DEPTH)]
    early_buf, late_buf = [None] * DEPTH, [None] * DEPTH
    early_all, late_all = [None] * DEPTH, [None] * DEPTH
    tril = jnp.tril(jnp.ones((CHUNK, CHUNK), F32))
    for l in reversed(range(DEPTH)):
        s, w = saved[l], full[l]
        above = l + 1 if l + 1 < DEPTH else None
        sh_m, sc_m, gt_m, sh_f, sc_f, gt_f = (ada[l, k] for k in range(6))
        row = lambda a: a[l].reshape(1, D)
        dgu, per = _run(_ffn_bwd_act, do, s["wfo4"], s["gu"], name=f"ffn_bwd_act_{l}",
                        tasks=[] if above is None else [_gather_task([early_buf[above], late_buf[above]])])
        if above is not None:
            early_all[above], late_all[above] = per[0]
        d_wfo = _mm_tn(s["f4"], do[None], name=f"dw_ffn_out_{l}")
        dgu8 = dgu.reshape(8, t, FF_BLK)
        dh2 = _mm_nt_sum(dgu8, w["w_ffn_in"], name=f"dh_ffn_{l}", b_is_kn=True)
        d_wfi = _mm_tn(dgu8, s["h2"][None], name=f"dw_ffn_in_{l}")
        dx1, st_f, dom, dgt_m = _norm_mod_bwd(dh2, s["x1"], dx, row(g_ffn), sc_f, s["om"], gt_m,
                                              name=f"norm_ffn_bwd_{l}")
        ffn_group = [d_wfi, d_wfo.reshape(8, D_FF // 8, D)]
        (dya, dyb, dyc, dzg, db_gate), per = _run(_merge_bwd, dom, s["wout"], s["z8"], s["ya"], s["yb"], s["yc"],
                                                  name=f"merge_bwd_{l}", tasks=[_sibling_task(ffn_group)])
        ffn_sums = _sibling_sums(ffn_group, per[0], core, tag=f"ffn_{l}")
        d_wout = _mm_tn(s["merged"][None], dom[None], name=f"dw_out_{l}")
        d_wpa = _mm_tn(s["sa"][None], dya[None], name=f"dw_pa_{l}")
        d_wpb = _mm_tn(s["sb"][None], dyb[None], name=f"dw_pb_{l}")
        d_wpc = _mm_tn(s["sc"][None], dyc[None], name=f"dw_pc_{l}")
        dsa = _mm_nt_sum(dya[None], s["wpa"][None], name=f"ds_a_{l}")
        dsb = _mm_nt_sum(dyb[None], s["wpb"][None], name=f"ds_b_{l}")
        dsc = _mm_nt_sum(dyc[None], s["wpc"][None], name=f"ds_c_{l}")
        proj_group = [g.reshape(8, D // 8, D) for g in (d_wpa, d_wpb, d_wpc, d_wout)]
        (dz8, db_in5, rows6, dws, dbs, dwp, dcw), per = _run(
            _branches_bwd, s["z8"], s["cv"], dsa, dsb, dsc, dzg, row(sgu_ln_g), row(sgu_ln_b), sgu_w_s[l], bs_b[l],
            pool_full[l], row(pool_scale), conv_full[l], row(conv_ln_g), row(conv_ln_b), name=f"branches_bwd_{l}",
            tasks=[_chips_task(ffn_sums), _sibling_task(proj_group)]
            + ([] if above is None else [_chips_task(in_b_sums)]))
        chip_parts[l]["ffn"] = per[0]
        proj_sums = _sibling_sums(proj_group, per[1], core, tag=f"proj_{l}")
        if above is not None:
            chip_parts[above]["in_b"] = per[2]
        early_buf[l] = jnp.concatenate([db_in5[0:5], db_gate, rows6[0:2], _rows(dws * tril),
                                        _rows(jnp.transpose(dbs[:, :SGU_G])), rows6[2:6], st_f[2:3], dcw], axis=0)
        d_win_a, per = _run(_mm_tn, s["h"][None, :, :D // 2], dz8, name=f"dw_in_a_{l}",
                            tasks=[_gather_task([early_buf[0]])] if l == 0 else [])
        if l == 0:
            (early_all[0],) = per[0]
        d_pool = jnp.transpose(dwp.reshape(4, N_DEV, POOL_GC // N_DEV, POOL_GC), (1, 0, 2, 3))
        in_a_group = [d_win_a, d_pool.reshape(N_DEV, 4 * POOL_GC // N_DEV, POOL_GC)]
        d_win_b, per = _run(_mm_tn, s["h"][None, :, D // 2:], dz8, name=f"dw_in_b_{l}",
                            tasks=[_sibling_task(in_a_group)])
        in_a_sums = _sibling_sums(in_a_group, per[0], core, tag=f"in_a_{l}")
        dh, per = _run(_mm_nt_sum, dz8, w["w_in"], name=f"dh_in_{l}",
                       tasks=[_chips_task(proj_sums), _chips_task(in_a_sums), _sibling_task([d_win_b])])
        chip_parts[l]["proj"], chip_parts[l]["in_a"] = per[0], per[1]
        in_b_sums = _sibling_sums([d_win_b], per[2], core, tag=f"in_b_{l}")
        gate = (saved[l - 1]["o"], ada[l - 1, 5]) if l > 0 else (None, None)
        res, per = _run(_norm_mod_bwd, dh, s["x"], dx1, row(g_mix), sc_m, *gate, name=f"norm_mix_bwd_{l}",
                        tasks=[_chips_task(in_b_sums)] if l == 0 else [])
        dx, st_m = res[0], res[1]
        if l == 0:
            chip_parts[0]["in_b"] = per[0]
        late_buf[l] = jnp.concatenate([st_m[0:1], st_m[1:2], dgt_m, st_f[0:1], st_f[1:2], dgt_f, st_m[2:3],
                                       dg_final if l == 0 else loss_row], axis=0)
        if l > 0:
            do, dgt_f = res[2], res[3]
    late_all[0] = _transfer([_gather_task([late_buf[0]])], name="gather_last")[0][0]

    early = [_sum8(early_all[l], name=f"sum_early_grads_{l}") for l in range(DEPTH)]
    late = [_sum8(late_all[l], name=f"sum_late_grads_{l}") for l in range(DEPTH)]
    layers = lambda red, lo, hi: jnp.stack([red[l][lo:hi] for l in range(DEPTH)], axis=0)
    grads = {n: layers(early, lo, hi).reshape(weights[n].shape) for n, (lo, hi) in EARLY_ROWS.items()}
    grads.update({n: layers(late, lo, hi).reshape(weights[n].shape) for n, (lo, hi) in LATE_ROWS.items()})
    grads["g_final"] = late[0][7]
    loss = late[DEPTH - 1][7, 0]
    conv_g = layers(early, CONV_ROW, CONV_ROW + CONV_K)
    grads["conv_w"] = lax.dynamic_slice_in_dim(conv_g, me * (D // N_DEV), D // N_DEV, axis=2)
    d_ada_all = jnp.stack([late_all[l][:, 0:6] for l in range(DEPTH)], axis=1).reshape(N_DEV, DEPTH, 6 * D)
    d_loc = jnp.transpose(lax.dynamic_slice_in_dim(d_ada_all, me * ADA_BLK, ADA_BLK, axis=2), (1, 0, 2))
    grads["w_ada"] = _ada_bwd(jnp.transpose(c_all), d_loc, name="ada_bwd")

    out = {}
    for n in REPLICATED + ("conv_w", "w_ada"):
        out[n] = _adam_nd(grads[n], weights[n], mom1[n], mom2[n], name=f"adam_{n}")
    for n, groups in GRAD_GROUP.items():
        parts = [chip_parts[l][group][k] for l in range(DEPTH) for group, k in groups]
        shape = (len(parts),) + parts[0].shape[1:]
        res = _adam(parts, weights[n].reshape(shape), mom1[n].reshape(shape), mom2[n].reshape(shape),
                    name=f"adam_{n}")
        out[n] = [r.reshape(weights[n].shape) for r in res]
    out["w_ffn_in"] = [jnp.transpose(r, (0, 2, 1)) for r in out["w_ffn_in"]]

    grad_x = dx.reshape(1, t, D)
    return (loss, grad_x, *[out[n][0] for n in WEIGHT_ORDER], *[out[n][1] for n in WEIGHT_ORDER],
            *[out[n][2] for n in WEIGHT_ORDER], *[out[n][3] for n in WEIGHT_ORDER])
```

```python
import math

import jax
import jax.numpy as jnp
from jax import lax
from jax.experimental import pallas as pl
from jax.experimental.pallas import tpu as pltpu

F32 = jnp.float32
BF16 = jnp.bfloat16
MESH = pl.DeviceIdType.MESH
AXES = ("x", "y", "c")
N_DEV = 8

D = 1024
DEPTH = 2
EPS = 1e-6
CHUNK = 128
SGU_G = 8
POOL_WINDOWS = (2, 4, 8, 16)
POOL_GC = 256
CONV_K = 31
HALO = 32
SUBLANE = 8
LANE = 128
CONV_STRIP = 128
D_FF = 2816
FF_BLK = D_FF // 4
ADA_BLK = 6 * D // N_DEV

ADAM_LR = 0.001
ADAM_B1 = 0.9
ADAM_B2 = 0.999
ADAM_EPS = 1e-08
ADAM_WD = 0.01
ADAM_STEP = 10

VMEM_LIMIT_V7X = 56 * 1024 * 1024
INV_SQRT2 = 1.0 / math.sqrt(2.0)
INV_SQRT_2PI = 1.0 / math.sqrt(2.0 * math.pi)


def _params(*sem):
    return pltpu.CompilerParams(dimension_semantics=sem if sem else None, vmem_limit_bytes=VMEM_LIMIT_V7X)


def _tile(n, pref):
    if n <= pref:
        return n
    for t in range(pref - pref % 8, 0, -8):
        if n % t == 0:
            return t
    raise ValueError((n, pref))


def _sds(shape, dtype):
    return jax.ShapeDtypeStruct(shape, dtype)


def _sigmoid(x):
    return 1.0 / (1.0 + jnp.exp(-x))


def _gelu(x):
    return 0.5 * x * (1.0 + lax.erf(x * INV_SQRT2))


def _gelu_grad(x):
    return 0.5 * (1.0 + lax.erf(x * INV_SQRT2)) + x * (INV_SQRT_2PI * jnp.exp(-0.5 * x * x))


def _ln_stats(v):
    mu = jnp.mean(v, axis=-1, keepdims=True)
    vc = v - mu
    rstd = lax.rsqrt(jnp.mean(vc * vc, axis=-1, keepdims=True) + EPS)
    return vc * rstd, rstd


def _ln_bwd(dvhat, vhat, rstd):
    return rstd * (dvhat - jnp.mean(dvhat, axis=-1, keepdims=True)
                   - vhat * jnp.mean(dvhat * vhat, axis=-1, keepdims=True))


def _colsum(v):
    return jnp.sum(v, axis=0, keepdims=True)


def _dot(a, b):
    return jnp.dot(a, b, preferred_element_type=F32)


def _dot_nt(a, b):
    return lax.dot_general(a, b, (((1,), (1,)), ((), ())), preferred_element_type=F32)


def _dot_tn(a, b):
    return lax.dot_general(a, b, (((0,), (0,)), ((), ())), preferred_element_type=F32)


def _tril_mask():
    r = lax.broadcasted_iota(jnp.int32, (CHUNK, CHUNK), 0)
    c = lax.broadcasted_iota(jnp.int32, (CHUNK, CHUNK), 1)
    return (r >= c).astype(F32)


def _mesh_pos():
    return tuple(lax.axis_index(a) for a in AXES)


class _Task:
    def __init__(self, arrays, out_shapes, scratch, start, finish, relay=None, relay_at=1.0):
        self.arrays, self.out_shapes, self.scratch = arrays, out_shapes, scratch
        self.start, self.finish, self.relay = start, finish, relay or (lambda ins, outs, sems: None)
        self.relay_at = relay_at


def _hosted(tasks, body, *, name, grid, in_specs, out_specs, out_shape, scratch_shapes=()):
    single = not isinstance(out_shape, (list, tuple))
    out_shape, out_specs = ([out_shape], [out_specs]) if single else (list(out_shape), list(out_specs))
    n_in, n_out, n_scr = len(in_specs), len(out_shape), len(scratch_shapes)
    sizes = [(len(t.arrays), len(t.out_shapes), len(t.scratch)) for t in tasks]
    t_in, t_out, t_scr = (sum(s[k] for s in sizes) for k in range(3))
    any_spec = pl.BlockSpec(memory_space=pl.ANY)

    def wrapped(*refs):
        refs = list(refs)
        ins, refs = refs[:n_in + t_in], refs[n_in + t_in:]
        outs, scr = refs[:n_out + t_out], refs[n_out + t_out:]

        def per_task(fn_name, only=None):
            i0, o0, s0 = n_in, n_out, n_scr
            for t, (ni, no, ns) in zip(tasks, sizes):
                if only is None or t is only:
                    getattr(t, fn_name)(ins[i0:i0 + ni], outs[o0:o0 + no], scr[s0:s0 + ns])
                i0, o0, s0 = i0 + ni, o0 + no, s0 + ns

        if tasks and grid:
            first, last, step, total = None, None, 0, 1
            for d, g in enumerate(grid):
                f, e = pl.program_id(d) == 0, pl.program_id(d) == g - 1
                first, last = (f, e) if first is None else (first & f, last & e)
                step, total = step * g + pl.program_id(d), total * g
            pl.when(first)(lambda: per_task("start"))
            for t in tasks:
                pl.when(step == min(int(t.relay_at * total), total - 1))(lambda t=t: per_task("relay", only=t))
        elif tasks:
            per_task("start")
            per_task("relay")
        body(*ins[:n_in], *outs[:n_out], *scr[:n_scr])
        if tasks and grid:
            pl.when(last)(lambda: per_task("finish"))
        elif tasks:
            per_task("finish")

    call = pl.pallas_call(
        wrapped, name=name, grid=grid,
        in_specs=list(in_specs) + [any_spec] * t_in, out_specs=out_specs + [any_spec] * t_out,
        out_shape=out_shape + [s for t in tasks for s in t.out_shapes],
        scratch_shapes=list(scratch_shapes) + [s for t in tasks for s in t.scratch],
        compiler_params=_params(*(("arbitrary",) * len(grid))))

    def run(*operands):
        res = list(call(*operands, *[a for t in tasks for a in t.arrays]))
        host, rest, per = res[:n_out], res[n_out:], []
        for _, no, _ in sizes:
            per.append(rest[:no])
            rest = rest[no:]
        return (host[0] if single else host), per

    return run


def _transfer(tasks, name):
    return _hosted(tasks, lambda: None, name=name, grid=(), in_specs=[], out_specs=[], out_shape=[])()[1]


def _gather_task(arrs, relay_at=0.75):
    n = len(arrs)

    def plan(ins, outs, sems):
        send_sems, recv_sems, local_sems = sems
        x, y, c = _mesh_pos()
        me, sibling = (x, y, c), (x, y, 1 - c)
        chips = [(1 - x, y), (x, 1 - y), (1 - x, 1 - y)]

        def slot(a, p):
            return outs[a].at[4 * p[0] + 2 * p[1] + p[2]]

        def copy(a, k, block, to, src=None):
            dst = slot(a, block)
            return pltpu.make_async_remote_copy(
                src_ref=dst if src is None else src, dst_ref=dst, send_sem=send_sems.at[a, k],
                recv_sem=recv_sems.at[a, k], device_id=to, device_id_type=MESH)

        def own_block_copies():
            mine = [pltpu.make_async_copy(ins[a], slot(a, me), local_sems.at[a]) for a in range(n)]
            first = []
            for a in range(n):
                first.append(copy(a, 0, me, sibling, src=ins[a]))
                first += [copy(a, 1 + j, me, (*chip, c), src=ins[a]) for j, chip in enumerate(chips)]
            return mine, first

        return c, me, sibling, chips, copy, own_block_copies

    def start(ins, outs, sems):
        mine, first = plan(ins, outs, sems)[-1]()
        for cp in mine + first:
            cp.start()

    def relay(ins, outs, sems):
        c, me, sibling, chips, copy, _ = plan(ins, outs, sems)
        for j, chip in enumerate(chips):
            for a in range(n):
                copy(a, 1 + j, (*chip, c), me).wait_recv()
                copy(a, 4 + j, (*chip, c), sibling).start()

    def finish(ins, outs, sems):
        c, me, sibling, chips, copy, own_block_copies = plan(ins, outs, sems)
        mine, first = own_block_copies()
        passed = [copy(a, 4 + j, (*chip, c), sibling) for j, chip in enumerate(chips) for a in range(n)]
        for a in range(n):
            copy(a, 0, sibling, me).wait_recv()
            for j, chip in enumerate(chips):
                copy(a, 4 + j, (*chip, 1 - c), me).wait_recv()
        for cp in first + passed:
            cp.wait_send()
        for m in mine:
            m.wait()

    return _Task(list(arrs), [_sds((N_DEV,) + a.shape, a.dtype) for a in arrs],
                 [pltpu.SemaphoreType.DMA((n, 7)), pltpu.SemaphoreType.DMA((n, 7)), pltpu.SemaphoreType.DMA((n,))],
                 start, finish, relay, relay_at)


def _sibling_task(arrs):
    n = len(arrs)

    def copies(ins, outs, sems):
        send_sems, recv_sems = sems
        x, y, c = _mesh_pos()
        return [pltpu.make_async_remote_copy(
            src_ref=ins[a].at[2 * q + (1 - c)], dst_ref=outs[a].at[q], send_sem=send_sems.at[a, q],
            recv_sem=recv_sems.at[a, q], device_id=(x, y, 1 - c), device_id_type=MESH)
            for a in range(n) for q in range(4)]

    def start(ins, outs, sems):
        for cp in copies(ins, outs, sems):
            cp.start()

    def finish(ins, outs, sems):
        for cp in copies(ins, outs, sems):
            cp.wait()

    return _Task(list(arrs), [_sds((4,) + a.shape[1:], a.dtype) for a in arrs],
                 [pltpu.SemaphoreType.DMA((n, 4)), pltpu.SemaphoreType.DMA((n, 4))], start, finish)


def _chips_task(arrs):
    n = len(arrs)

    def copies(ins, outs, sems):
        send_sems, recv_sems, local_sems = sems
        x, y, c = _mesh_pos()
        q_me = 2 * x + y
        chips = [(1 - x, y), (x, 1 - y), (1 - x, 1 - y)]
        own = [pltpu.make_async_copy(ins[a].at[q_me], outs[a].at[q_me], local_sems.at[a]) for a in range(n)]
        remote = [pltpu.make_async_remote_copy(
            src_ref=ins[a].at[2 * chip[0] + chip[1]], dst_ref=outs[a].at[q_me], send_sem=send_sems.at[a, j],
            recv_sem=recv_sems.at[a, j], device_id=(*chip, c), device_id_type=MESH)
            for a in range(n) for j, chip in enumerate(chips)]
        return own + remote

    def start(ins, outs, sems):
        for cp in copies(ins, outs, sems):
            cp.start()

    def finish(ins, outs, sems):
        for cp in copies(ins, outs, sems):
            cp.wait()

    return _Task(list(arrs), [_sds(a.shape, a.dtype) for a in arrs],
                 [pltpu.SemaphoreType.DMA((n, 3)), pltpu.SemaphoreType.DMA((n, 3)), pltpu.SemaphoreType.DMA((n,))],
                 start, finish)


def _sibling_sum(arr, land, core, name):
    _, rows, cols = arr.shape
    tr = _tile(rows, 512)
    arr4 = arr.reshape(4, 2, rows, cols)

    def body(c_ref, a_ref, l_ref, o_ref):
        o_ref[...] = (a_ref[...] + l_ref[...]).astype(BF16)

    grid_spec = pltpu.PrefetchScalarGridSpec(
        num_scalar_prefetch=1, grid=(4, rows // tr),
        in_specs=[pl.BlockSpec((None, None, tr, cols), lambda q, r, c_ref: (q, c_ref[0], r, 0)),
                  pl.BlockSpec((None, tr, cols), lambda q, r, c_ref: (q, r, 0))],
        out_specs=pl.BlockSpec((None, tr, cols), lambda q, r, c_ref: (q, r, 0)))
    return pl.pallas_call(body, name=name, grid_spec=grid_spec, out_shape=_sds((4, rows, cols), BF16),
                          compiler_params=_params("arbitrary", "arbitrary"))(core, arr4, land)


def _sibling_sums(arrs, land, core, tag):
    return [_sibling_sum(a, l, core, name=f"rs_sum_{tag}_{k}") for k, (a, l) in enumerate(zip(arrs, land))]


def _adam(gparts, w, m, v, name):
    n_l = len(gparts)
    p, rows, cols = gparts[0].shape
    tr = _tile(rows, 256)
    n_r = rows // tr
    c1 = 1.0 - ADAM_B1 ** ADAM_STEP
    c2 = 1.0 - ADAM_B2 ** ADAM_STEP

    def body(*refs):
        g_refs = refs[:n_l]
        w_ref, m_ref, v_ref, go_ref, d_ref, mo_ref, vo_ref = refs[n_l:]
        layer = pl.program_id(0)
        g = jnp.zeros((tr, cols), F32)
        for li, g_ref in enumerate(g_refs):
            gl = g_ref[0].astype(F32)
            for k in range(1, p):
                gl = gl + g_ref[k].astype(F32)
            g = gl if n_l == 1 else jnp.where(layer == li, gl, g)
        m_new = ADAM_B1 * m_ref[...] + (1.0 - ADAM_B1) * g
        v_new = ADAM_B2 * v_ref[...] + (1.0 - ADAM_B2) * (g * g)
        m_hat = m_new / c1
        v_hat = v_new / c2
        go_ref[...] = g
        d_ref[...] = -ADAM_LR * (m_hat / (jnp.sqrt(v_hat) + ADAM_EPS) + ADAM_WD * w_ref[...])
        mo_ref[...] = m_new
        vo_ref[...] = v_new

    def g_spec(li):
        def index(l, r):
            return (0, jnp.where(l == li, r, jnp.where(l < li, 0, n_r - 1)), 0)
        return pl.BlockSpec((p, tr, cols), index)

    blk = pl.BlockSpec((None, tr, cols), lambda l, r: (l, r, 0))
    return pl.pallas_call(
        body, name=name, grid=(n_l, n_r),
        in_specs=[g_spec(li) for li in range(n_l)] + [blk, blk, blk],
        out_specs=[blk] * 4, out_shape=[_sds((n_l, rows, cols), F32)] * 4,
        compiler_params=_params("arbitrary", "arbitrary"))(*gparts, w, m, v)


def _adam_nd(grad, w, m, v, name):
    shape = w.shape
    cols = shape[-1]
    rows = w.size // cols
    as_rows = lambda a: a.reshape(1, rows, cols)
    out = _adam([as_rows(grad)], as_rows(w), as_rows(m), as_rows(v), name)
    return [o.reshape(shape) for o in out]


def _norm_mod(x, g, sc, sh, name):
    t = x.shape[0]
    tm = _tile(t, 512)

    def body(x_ref, g_ref, sc_ref, sh_ref, h_ref):
        h_ref[...] = _modulated_norm(x_ref[...], g_ref[...], sc_ref[...], sh_ref[...])

    row = pl.BlockSpec((1, D), lambda i: (0, 0))
    blk = pl.BlockSpec((tm, D), lambda i: (i, 0))
    return pl.pallas_call(body, name=name, grid=(t // tm,), in_specs=[blk, row, row, row], out_specs=blk,
                          out_shape=_sds((t, D), BF16), compiler_params=_params("arbitrary"))(x, g, sc, sh)


def _with_tasks(res_per, tasks):
    return res_per if tasks else res_per[0]


def _mm_cols(a, b8, bias8, name, tasks=()):
    t, k = a.shape
    j, _, n = b8.shape
    tm = _tile(t, 1024)

    def body(a_ref, b_ref, bias_ref, o_ref):
        o_ref[...] = (_dot(a_ref[...], b_ref[...]) + bias_ref[...]).astype(BF16)

    return _with_tasks(_hosted(
        tasks, body, name=name, grid=(j, t // tm),
        in_specs=[pl.BlockSpec((tm, k), lambda jj, i: (i, 0)),
                  pl.BlockSpec((None, k, n), lambda jj, i: (jj, 0, 0)),
                  pl.BlockSpec((None, 1, n), lambda jj, i: (jj, 0, 0))],
        out_specs=pl.BlockSpec((None, tm, n), lambda jj, i: (jj, i, 0)),
        out_shape=_sds((j, t, n), BF16))(a, b8, bias8), tasks)


def _halo_before(tm, col):
    return pl.BlockSpec((None, HALO, D), lambda i: (col, jnp.maximum(i * (tm // HALO) - 1, 0), 0))


def _pool_forward(p_ext, t0, rows):
    t = t0 + lax.broadcasted_iota(jnp.int32, (rows, 1), 0)
    out = []
    for gi, win in enumerate(POOL_WINDOWS):
        e = p_ext[:, gi * POOL_GC:(gi + 1) * POOL_GC]
        s, sh = e, 1
        while sh < win:
            s = s + pltpu.roll(s, sh, 0)
            sh *= 2
        cnt = jnp.minimum(t + 1, win).astype(F32)
        out.append(s[HALO:] / cnt - e[HALO:])
    return out


def _fill_shift_bank(bank_ref, ext, causal):
    n = ext.shape[0]
    bank_ref[0] = ext
    for b in range(1, SUBLANE):
        bank_ref[b] = pltpu.roll(ext, b if causal else n - b, 0)


def _branches_fwd(z8, ln_g, ln_b, w_s, bs_b, pool_w, pool_scale, conv_w, conv_b, cln_g, cln_b, name, tasks=()):
    t = z8.shape[1]
    tm = _tile(t, 256)
    n_ext = tm + HALO

    def body(zu_ref, zv_ref, p_ref, a_ref, ag_ref, ph_ref, ah_ref, agh_ref, lng_ref, lnb_ref, ws_ref, bsb_ref,
             wp_ref, ps_ref, cw_ref, cb_ref, clg_ref, clb_ref, sa_ref, sb_ref, sc_ref, cv_ref, bank_ref):
        i = pl.program_id(0)
        has_past = (i > 0).astype(F32)
        u = _gelu(zu_ref[...].astype(F32))
        vhat, _ = _ln_stats(_gelu(zv_ref[...].astype(F32)))
        vb = (vhat * lng_ref[...] + lnb_ref[...]).astype(BF16)
        mask = _tril_mask()
        for g in range(SGU_G):
            cols = slice(g * CHUNK, (g + 1) * CHUNK)
            wm = (ws_ref[g] * mask).astype(BF16)
            for n in range(tm // CHUNK):
                rows = slice(n * CHUNK, (n + 1) * CHUNK)
                mixed = _dot(wm, vb[rows, cols]) + bsb_ref[g]
                sa_ref[rows, cols] = (u[rows, cols] * mixed).astype(BF16)
        p_ext = jnp.concatenate([ph_ref[...].astype(F32) * has_past, p_ref[...].astype(F32)], axis=0)
        pooled = _pool_forward(p_ext, i * tm, tm)
        for gi in range(len(POOL_WINDOWS)):
            cols = slice(gi * POOL_GC, (gi + 1) * POOL_GC)
            y = _dot(pooled[gi].astype(BF16), wp_ref[gi].astype(BF16))
            sb_ref[:, cols] = (y * ps_ref[:, cols]).astype(BF16)
        for cb in range(D // LANE):
            cols = slice(cb * LANE, (cb + 1) * LANE)
            zc = jnp.concatenate(
                [ah_ref[:, cols].astype(F32) * has_past * _sigmoid(agh_ref[:, cols].astype(F32)),
                 a_ref[:, cols].astype(F32) * _sigmoid(ag_ref[:, cols].astype(F32))], axis=0)
            _fill_shift_bank(bank_ref, zc, causal=True)
            for r0 in range(0, tm, CONV_STRIP):
                acc = jnp.zeros((CONV_STRIP, LANE), F32) + cb_ref[:, cols]
                for k in range(CONV_K):
                    hi, lo = divmod(CONV_K - 1 - k, SUBLANE)
                    acc = acc + cw_ref[k:k + 1, cols] * bank_ref[lo, pl.ds(HALO - SUBLANE * hi + r0, CONV_STRIP), :]
                cv_ref[r0:r0 + CONV_STRIP, cols] = acc
        cv = cv_ref[...]
        chat, _ = _ln_stats(cv)
        cl = chat * clg_ref[...] + clb_ref[...]
        sc_ref[...] = (cl * _sigmoid(cl)).astype(BF16)

    def col(j):
        return pl.BlockSpec((None, tm, D), lambda i: (j, i, 0))

    row = pl.BlockSpec((1, D), lambda i: (0, 0))
    full3 = lambda s: pl.BlockSpec(s, lambda i: (0, 0, 0))
    blk = pl.BlockSpec((tm, D), lambda i: (i, 0))
    return _with_tasks(_hosted(
        tasks, body, name=name, grid=(t // tm,),
        in_specs=[col(0), col(1), col(2), col(3), col(4), _halo_before(tm, 2), _halo_before(tm, 3),
                  _halo_before(tm, 4), row, row, full3((SGU_G, CHUNK, CHUNK)), full3((SGU_G, CHUNK, CHUNK)),
                  full3((4, POOL_GC, POOL_GC)), row, pl.BlockSpec((HALO, D), lambda i: (0, 0)), row, row, row],
        out_specs=[blk, blk, blk, blk],
        out_shape=[_sds((t, D), BF16)] * 3 + [_sds((t, D), F32)],
        scratch_shapes=[pltpu.VMEM((SUBLANE, n_ext, LANE), F32)],
    )(z8, z8, z8, z8, z8, z8, z8, z8, ln_g, ln_b, w_s, bs_b, pool_w, pool_scale, conv_w, conv_b, cln_g, cln_b), tasks)


def _proj_merge(sa, sb, sc, w_pa, w_pb, w_pc, z8, name, tasks=()):
    t = sa.shape[0]
    tm = _tile(t, 512)

    def body(sa_ref, sb_ref, sc_ref, wa_ref, wb_ref, wc_ref, g0_ref, g1_ref, g2_ref, ya_ref, yb_ref, yc_ref, m_ref):
        merged = jnp.zeros((tm, D), F32)
        for s_ref, w_ref, g_ref, y_ref in ((sa_ref, wa_ref, g0_ref, ya_ref), (sb_ref, wb_ref, g1_ref, yb_ref),
                                           (sc_ref, wc_ref, g2_ref, yc_ref)):
            y = _dot(s_ref[...], w_ref[...])
            y_ref[...] = y.astype(BF16)
            merged = merged + _sigmoid(g_ref[...].astype(F32)) * y
        m_ref[...] = merged.astype(BF16)

    blk = pl.BlockSpec((tm, D), lambda i: (i, 0))
    wspec = pl.BlockSpec((D, D), lambda i: (0, 0))
    gate = lambda j: pl.BlockSpec((None, tm, D), lambda i: (j, i, 0))
    return _with_tasks(_hosted(
        tasks, body, name=name, grid=(t // tm,),
        in_specs=[blk, blk, blk, wspec, wspec, wspec, gate(5), gate(6), gate(7)],
        out_specs=[blk] * 4, out_shape=[_sds((t, D), BF16)] * 4)(sa, sb, sc, w_pa, w_pb, w_pc, z8, z8, z8), tasks)


def _modulated_norm(xv, g, sc, sh):
    r = lax.rsqrt(jnp.mean(xv * xv, axis=-1, keepdims=True) + EPS)
    return (xv * r * g * (1.0 + sc) + sh).astype(BF16)


def _out_proj(merged, w_out, x, gt, g, sc, sh, name):
    t = x.shape[0]
    tm = _tile(t, 512)

    def body(m_ref, w_ref, x_ref, gt_ref, g_ref, sc_ref, sh_ref, om_ref, x1_ref, h2_ref):
        om = _dot(m_ref[...], w_ref[...])
        om_ref[...] = om
        x1 = x_ref[...] + gt_ref[...] * om
        x1_ref[...] = x1
        h2_ref[...] = _modulated_norm(x1, g_ref[...], sc_ref[...], sh_ref[...])

    blk = pl.BlockSpec((tm, D), lambda i: (i, 0))
    row = pl.BlockSpec((1, D), lambda i: (0, 0))
    return pl.pallas_call(
        body, name=name, grid=(t // tm,),
        in_specs=[blk, pl.BlockSpec((D, D), lambda i: (0, 0)), blk, row, row, row, row],
        out_specs=[blk, blk, blk], out_shape=[_sds((t, D), F32)] * 2 + [_sds((t, D), BF16)],
        compiler_params=_params("arbitrary"))(merged, w_out, x, gt, g, sc, sh)


def _ffn_in(h2, wfi, name, tasks=()):
    t = h2.shape[0]
    tm = _tile(t, 512)

    def body(h_ref, w_ref, gu_ref, f_ref):
        hv = h_ref[...]
        gp = _dot_nt(hv, w_ref[0])
        up = _dot_nt(hv, w_ref[1])
        gu_ref[0] = gp.astype(BF16)
        gu_ref[1] = up.astype(BF16)
        f_ref[...] = (gp * _sigmoid(gp) * up).astype(BF16)

    return _with_tasks(_hosted(
        tasks, body, name=name, grid=(4, t // tm),
        in_specs=[pl.BlockSpec((tm, D), lambda j, i: (i, 0)),
                  pl.BlockSpec((2, None, FF_BLK, D), lambda j, i: (0, j, 0, 0))],
        out_specs=[pl.BlockSpec((2, None, tm, FF_BLK), lambda j, i: (0, j, i, 0)),
                   pl.BlockSpec((None, tm, FF_BLK), lambda j, i: (j, i, 0))],
        out_shape=[_sds((2, 4, t, FF_BLK), BF16), _sds((4, t, FF_BLK), BF16)])(h2, wfi), tasks)


def _ffn_out(f4, wfo4, x1, gt, g, sc, sh, name, tasks=()):
    t = x1.shape[0]
    tm = _tile(t, 512)
    with_norm = g is not None

    def body(f_ref, w_ref, x_ref, gt_ref, *rest):
        o_ref, x2_ref = rest[-3:-1] if with_norm else rest[-2:]
        j = pl.program_id(1)

        @pl.when(j == 0)
        def _():
            o_ref[...] = jnp.zeros_like(o_ref)

        o_ref[...] += _dot(f_ref[...], w_ref[...])

        @pl.when(j == 3)
        def _():
            x2 = x_ref[...] + gt_ref[...] * o_ref[...]
            x2_ref[...] = x2
            if with_norm:
                g_ref, sc_ref, sh_ref = rest[:3]
                rest[-1][...] = _modulated_norm(x2, g_ref[...], sc_ref[...], sh_ref[...])

    blk = pl.BlockSpec((tm, D), lambda i, j: (i, 0))
    row = pl.BlockSpec((1, D), lambda i, j: (0, 0))
    norm_args = [g, sc, sh] if with_norm else []
    return _with_tasks(_hosted(
        tasks, body, name=name, grid=(t // tm, 4),
        in_specs=[pl.BlockSpec((None, tm, FF_BLK), lambda i, j: (j, i, 0)),
                  pl.BlockSpec((None, FF_BLK, D), lambda i, j: (j, 0, 0)), blk, row] + [row] * len(norm_args),
        out_specs=[blk, blk] + [blk] * with_norm,
        out_shape=[_sds((t, D), F32)] * 2 + [_sds((t, D), BF16)] * with_norm)(f4, wfo4, x1, gt, *norm_args), tasks)


def _gate_grads(dx, o_ref, gt_ref, do_ref, dgt_ref):
    do_ref[...] = (dx * gt_ref[...]).astype(BF16)
    dgt_ref[...] += _colsum(dx * o_ref[...])


def _final_loss(x, g, target, o, gt, name):
    t = x.shape[0]
    tm = _tile(t, 512)

    def body(x_ref, g_ref, t_ref, o_ref, gt_ref, loss_ref, dx_ref, dg_ref, do_ref, dgt_ref):
        @pl.when(pl.program_id(0) == 0)
        def _():
            for ref in (loss_ref, dg_ref, dgt_ref):
                ref[...] = jnp.zeros_like(ref)

        xv = x_ref[...]
        r = lax.rsqrt(jnp.mean(xv * xv, axis=-1, keepdims=True) + EPS)
        xn = xv * r
        diff = xn * g_ref[...] - t_ref[...]
        loss_ref[...] += 0.5 * jnp.sum(jnp.mean(diff * diff, axis=-1, keepdims=True))
        dy = diff * (1.0 / D)
        dg_ref[...] += _colsum(dy * xn)
        dxn = dy * g_ref[...]
        dx = r * (dxn - xn * jnp.mean(dxn * xn, axis=-1, keepdims=True))
        dx_ref[...] = dx
        _gate_grads(dx, o_ref, gt_ref, do_ref, dgt_ref)

    blk = pl.BlockSpec((tm, D), lambda i: (i, 0))
    row = pl.BlockSpec((1, D), lambda i: (0, 0))
    return pl.pallas_call(
        body, name=name, grid=(t // tm,), in_specs=[blk, row, blk, blk, row],
        out_specs=[pl.BlockSpec((8, 128), lambda i: (0, 0)), blk, row, blk, row],
        out_shape=[_sds((8, 128), F32), _sds((t, D), F32), _sds((1, D), F32), _sds((t, D), BF16), _sds((1, D), F32)],
        compiler_params=_params("arbitrary"))(x, g, target, o, gt)


def _norm_mod_bwd(dh, x, dres, g, sc, o, gt, name, tasks=()):
    t = x.shape[0]
    tm = _tile(t, 512)
    with_gate = o is not None

    def body(dh_ref, x_ref, dr_ref, g_ref, sc_ref, *rest):
        dx_ref, st_ref = rest[2:4] if with_gate else rest

        @pl.when(pl.program_id(0) == 0)
        def _():
            st_ref[...] = jnp.zeros_like(st_ref)
            if with_gate:
                rest[5][...] = jnp.zeros_like(rest[5])

        xv, dhv = x_ref[...], dh_ref[...]
        r = lax.rsqrt(jnp.mean(xv * xv, axis=-1, keepdims=True) + EPS)
        xn = xv * r
        gv, mod = g_ref[...], 1.0 + sc_ref[...]
        st_ref[0:1, :] += _colsum(dhv)
        st_ref[1:2, :] += _colsum(dhv * xn * gv)
        st_ref[2:3, :] += _colsum(dhv * xn * mod)
        dxn = dhv * gv * mod
        dx = dr_ref[...] + r * (dxn - xn * jnp.mean(dxn * xn, axis=-1, keepdims=True))
        dx_ref[...] = dx
        if with_gate:
            _gate_grads(dx, rest[0], rest[1], rest[4], rest[5])

    blk = pl.BlockSpec((tm, D), lambda i: (i, 0))
    row = pl.BlockSpec((1, D), lambda i: (0, 0))
    gate_args = [o, gt] if with_gate else []
    return _with_tasks(_hosted(
        tasks, body, name=name, grid=(t // tm,), in_specs=[blk, blk, blk, row, row] + [blk, row] * with_gate,
        out_specs=[blk, pl.BlockSpec((3, D), lambda i: (0, 0))] + [blk, row] * with_gate,
        out_shape=[_sds((t, D), F32), _sds((3, D), F32)] + [_sds((t, D), BF16), _sds((1, D), F32)] * with_gate,
    )(dh, x, dres, g, sc, *gate_args), tasks)


def _ffn_bwd_act(do, wfo4, gu, name, tasks=()):
    t = do.shape[0]
    tm = _tile(t, 512)

    def body(do_ref, w_ref, gu_ref, dgu_ref):
        df = _dot_nt(do_ref[...], w_ref[...])
        gp, up = gu_ref[0].astype(F32), gu_ref[1].astype(F32)
        sg = _sigmoid(gp)
        dgu_ref[0] = (df * up * (sg * (1.0 + gp * (1.0 - sg)))).astype(BF16)
        dgu_ref[1] = (df * (gp * sg)).astype(BF16)

    gu_spec = pl.BlockSpec((2, None, tm, FF_BLK), lambda j, i: (0, j, i, 0))
    return _with_tasks(_hosted(
        tasks, body, name=name, grid=(4, t // tm),
        in_specs=[pl.BlockSpec((tm, D), lambda j, i: (i, 0)),
                  pl.BlockSpec((None, FF_BLK, D), lambda j, i: (j, 0, 0)), gu_spec],
        out_specs=gu_spec, out_shape=_sds((2, 4, t, FF_BLK), BF16))(do, wfo4, gu), tasks)


def _mm_nt_sum(a8, b8, name, b_is_kn=False, tasks=()):
    j, t, k = a8.shape
    n = b8.shape[2] if b_is_kn else b8.shape[1]
    tm = _tile(t, 1024 if j > 1 else 512)
    out_dtype = F32 if j > 1 else BF16
    dot = _dot if b_is_kn else _dot_nt

    def body(a_ref, b_ref, o_ref):
        if j == 1:
            o_ref[...] = dot(a_ref[...], b_ref[...]).astype(out_dtype)
            return

        @pl.when(pl.program_id(1) == 0)
        def _():
            o_ref[...] = jnp.zeros_like(o_ref)

        o_ref[...] += dot(a_ref[...], b_ref[...])

    return _with_tasks(_hosted(
        tasks, body, name=name, grid=(t // tm, j),
        in_specs=[pl.BlockSpec((None, tm, k), lambda i, jj: (jj, i, 0)),
                  pl.BlockSpec((None,) + b8.shape[1:], lambda i, jj: (jj, 0, 0))],
        out_specs=pl.BlockSpec((tm, n), lambda i, jj: (i, 0)), out_shape=_sds((t, n), out_dtype))(a8, b8), tasks)


def _mm_tn(a8, b8, name, tasks=()):
    ja, t, m = a8.shape
    jb, _, n = b8.shape
    j = max(ja, jb)
    tk = _tile(t, 4096 if j > 1 else 1024)

    def body(a_ref, b_ref, o_ref):
        @pl.when(pl.program_id(1) == 0)
        def _():
            o_ref[...] = jnp.zeros_like(o_ref)

        o_ref[...] += _dot_tn(a_ref[...], b_ref[...])

    return _with_tasks(_hosted(
        tasks, body, name=name, grid=(j, t // tk),
        in_specs=[pl.BlockSpec((None, tk, m), (lambda jj, kk: (jj, kk, 0)) if ja > 1 else (lambda jj, kk: (0, kk, 0))),
                  pl.BlockSpec((None, tk, n), (lambda jj, kk: (jj, kk, 0)) if jb > 1 else (lambda jj, kk: (0, kk, 0)))],
        out_specs=pl.BlockSpec((None, m, n), lambda jj, kk: (jj, 0, 0)), out_shape=_sds((j, m, n), F32))(a8, b8),
        tasks)


def _merge_bwd(dom, w_out, z8, ya, yb, yc, name, tasks=()):
    t = dom.shape[0]
    tm = _tile(t, 512)

    def body(dom_ref, w_ref, g0_ref, g1_ref, g2_ref, ya_ref, yb_ref, yc_ref, dya_ref, dyb_ref, dyc_ref, dzg_ref,
             db_ref):
        @pl.when(pl.program_id(0) == 0)
        def _():
            db_ref[...] = jnp.zeros_like(db_ref)

        dm = _dot_nt(dom_ref[...], w_ref[...])
        for k, (g_ref, y_ref, dy_ref) in enumerate(((g0_ref, ya_ref, dya_ref), (g1_ref, yb_ref, dyb_ref),
                                                    (g2_ref, yc_ref, dyc_ref))):
            sg = _sigmoid(g_ref[...].astype(F32))
            dy_ref[...] = (dm * sg).astype(BF16)
            dzg = dm * y_ref[...].astype(F32) * (sg * (1.0 - sg))
            dzg_ref[k] = dzg.astype(BF16)
            db_ref[k:k + 1, :] += _colsum(dzg)

    blk = pl.BlockSpec((tm, D), lambda i: (i, 0))
    gate = lambda j: pl.BlockSpec((None, tm, D), lambda i: (j, i, 0))
    return _with_tasks(_hosted(
        tasks, body, name=name, grid=(t // tm,),
        in_specs=[blk, pl.BlockSpec((D, D), lambda i: (0, 0)), gate(5), gate(6), gate(7), blk, blk, blk],
        out_specs=[blk, blk, blk, pl.BlockSpec((3, tm, D), lambda i: (0, i, 0)), pl.BlockSpec((3, D), lambda i: (0, 0))],
        out_shape=[_sds((t, D), BF16)] * 3 + [_sds((3, t, D), BF16), _sds((3, D), F32)],
    )(dom, w_out, z8, z8, z8, ya, yb, yc), tasks)


def _branches_bwd(z8, cv, dsa, dsb, dsc, dzg, ln_g, ln_b, w_s, bs_b, pool_w, pool_scale, conv_w, cln_g, cln_b, name,
                  tasks=()):
    t = z8.shape[1]
    tm = _tile(t, 256)
    n_ext = tm + HALO
    n_tiles = t // tm

    def body(zu_ref, zv_ref, p_ref, a_ref, ag_ref, ph_ref, ah_ref, agh_ref, cv_ref, cvf_ref, dsa_ref, dsb_ref,
             dsbf_ref, dsc_ref, dscf_ref, dzg_ref, lng_ref, lnb_ref, ws_ref, bsb_ref, wp_ref, ps_ref, cw_ref,
             clg_ref, clb_ref, dz_ref, dbin_ref, rows_ref, dws_ref, dbs_ref, dwp_ref, dcw_ref, mixed_scr, dvln_scr,
             dcv_scr, zbank_ref, dbank_ref, dcw8_scr):
        i = pl.program_id(0)

        @pl.when(i == 0)
        def _():
            for ref in (dbin_ref, rows_ref, dws_ref, dbs_ref, dwp_ref, dcw8_scr):
                ref[...] = jnp.zeros_like(ref)

        has_past = (i > 0).astype(F32)
        has_next = (i < n_tiles - 1).astype(F32)

        def emit(j, val):
            dz_ref[j] = val.astype(BF16)
            dbin_ref[j:j + 1, :] += _colsum(val)

        zu, zv = zu_ref[...].astype(F32), zv_ref[...].astype(F32)
        u = _gelu(zu)
        vhat, v_rstd = _ln_stats(_gelu(zv))
        vb = (vhat * lng_ref[...] + lnb_ref[...]).astype(BF16)
        dsa = dsa_ref[...].astype(F32)
        dmixed = dsa * u
        dmb = dmixed.astype(BF16)
        mask = _tril_mask()
        lane = lax.broadcasted_iota(jnp.int32, (CHUNK, CHUNK), 1)
        for g in range(SGU_G):
            cols = slice(g * CHUNK, (g + 1) * CHUNK)
            wm = (ws_ref[g] * mask).astype(BF16)
            dws = jnp.zeros((CHUNK, CHUNK), F32)
            dbs = jnp.zeros((CHUNK, 1), F32)
            for n in range(tm // CHUNK):
                rows = slice(n * CHUNK, (n + 1) * CHUNK)
                mixed_scr[rows, cols] = _dot(wm, vb[rows, cols]) + bsb_ref[g]
                dvln_scr[rows, cols] = _dot_tn(wm, dmb[rows, cols])
                dws = dws + _dot_nt(dmb[rows, cols], vb[rows, cols])
                dbs = dbs + jnp.sum(dmixed[rows, cols], axis=1, keepdims=True)
            dws_ref[g] += dws
            dbs_ref[...] += jnp.where(lane == g, dbs, 0.0)
        emit(0, dsa * mixed_scr[...] * _gelu_grad(zu))
        dvln = dvln_scr[...]
        rows_ref[0:1, :] += _colsum(dvln * vhat)
        rows_ref[1:2, :] += _colsum(dvln)
        emit(1, _ln_bwd(dvln * lng_ref[...], vhat, v_rstd) * _gelu_grad(zv))

        p_ext = jnp.concatenate([ph_ref[...].astype(F32) * has_past, p_ref[...].astype(F32)], axis=0)
        pooled = _pool_forward(p_ext, i * tm, tm)
        dsb = dsb_ref[...].astype(F32)
        dpl_ext = jnp.concatenate([dsb, dsbf_ref[...].astype(F32) * has_next], axis=0) * ps_ref[...]
        t_ext = i * tm + lax.broadcasted_iota(jnp.int32, (n_ext, 1), 0)
        dp_parts = []
        for gi, win in enumerate(POOL_WINDOWS):
            cols = slice(gi * POOL_GC, (gi + 1) * POOL_GC)
            pooled_b = pooled[gi].astype(BF16)
            wpb = wp_ref[gi].astype(BF16)
            rows_ref[2:3, cols] += _colsum(dsb[:, cols] * _dot(pooled_b, wpb))
            dplb = dpl_ext[:, cols].astype(BF16)
            dwp_ref[gi] += _dot_tn(pooled_b, dplb[:tm])
            dpooled = _dot_nt(dplb, wpb)
            s, sh = dpooled / jnp.minimum(t_ext + 1, win).astype(F32), 1
            while sh < win:
                s = s + pltpu.roll(s, n_ext - sh, 0)
                sh *= 2
            dp_parts.append(s[:tm] - dpooled[:tm])
        emit(2, jnp.concatenate(dp_parts, axis=1))

        cv_ext = jnp.concatenate([cv_ref[...], cvf_ref[...]], axis=0)
        chat, c_rstd = _ln_stats(cv_ext)
        cl = chat * clg_ref[...] + clb_ref[...]
        sg = _sigmoid(cl)
        dsc_ext = jnp.concatenate([dsc_ref[...].astype(F32), dscf_ref[...].astype(F32)], axis=0)
        dcl = dsc_ext * (sg * (1.0 + cl * (1.0 - sg)))
        rows_ref[4:5, :] += _colsum((dcl * chat)[:tm])
        rows_ref[5:6, :] += _colsum(dcl[:tm])
        in_seq = jnp.concatenate([jnp.ones((tm, 1), F32), jnp.zeros((HALO, 1), F32) + has_next], axis=0)
        dcv = jnp.where(in_seq > 0.0, _ln_bwd(dcl * clg_ref[...], chat, c_rstd), 0.0)
        rows_ref[3:4, :] += _colsum(dcv[:tm])
        dcv_scr[...] = dcv
        for cb in range(D // LANE):
            cols = slice(cb * LANE, (cb + 1) * LANE)
            zc = jnp.concatenate(
                [ah_ref[:, cols].astype(F32) * has_past * _sigmoid(agh_ref[:, cols].astype(F32)),
                 a_ref[:, cols].astype(F32) * _sigmoid(ag_ref[:, cols].astype(F32))], axis=0)
            _fill_shift_bank(zbank_ref, zc, causal=True)
            _fill_shift_bank(dbank_ref, dcv_scr[:, cols], causal=False)
            for r0 in range(0, tm, CONV_STRIP):
                rows = slice(r0, r0 + CONV_STRIP)
                dcv_s = dcv_scr[rows, cols]
                dzc = jnp.zeros((CONV_STRIP, LANE), F32)
                for k in range(CONV_K):
                    hi, lo = divmod(CONV_K - 1 - k, SUBLANE)
                    z_win = zbank_ref[lo, pl.ds(HALO - SUBLANE * hi + r0, CONV_STRIP), :]
                    dcw8_scr[k, :, cols] += jnp.sum((dcv_s * z_win).reshape(CONV_STRIP // SUBLANE, SUBLANE, LANE), axis=0)
                    dzc = dzc + cw_ref[k:k + 1, cols] * dbank_ref[lo, pl.ds(SUBLANE * hi + r0, CONV_STRIP), :]
                a_s = a_ref[rows, cols].astype(F32)
                sga = _sigmoid(ag_ref[rows, cols].astype(F32))
                dza = dzc * sga
                dzag = dzc * a_s * (sga * (1.0 - sga))
                dz_ref[3, rows, cols] = dza.astype(BF16)
                dz_ref[4, rows, cols] = dzag.astype(BF16)
                dbin_ref[3:4, cols] += _colsum(dza)
                dbin_ref[4:5, cols] += _colsum(dzag)
        for k in range(3):
            dz_ref[5 + k] = dzg_ref[k]

        @pl.when(i == n_tiles - 1)
        def _():
            dcw_ref[...] = jnp.sum(dcw8_scr[...], axis=1)

    def col(j):
        return pl.BlockSpec((None, tm, D), lambda i: (j, i, 0))

    blk = pl.BlockSpec((tm, D), lambda i: (i, 0))
    after = pl.BlockSpec((HALO, D), lambda i: (jnp.minimum((i + 1) * (tm // HALO), t // HALO - 1), 0))
    row = pl.BlockSpec((1, D), lambda i: (0, 0))
    full2 = lambda s: pl.BlockSpec(s, lambda i: (0, 0))
    full3 = lambda s: pl.BlockSpec(s, lambda i: (0, 0, 0))
    return _with_tasks(_hosted(
        tasks, body, name=name, grid=(n_tiles,),
        in_specs=[col(0), col(1), col(2), col(3), col(4), _halo_before(tm, 2), _halo_before(tm, 3),
                  _halo_before(tm, 4), blk, after, blk, blk, after, blk, after,
                  pl.BlockSpec((3, tm, D), lambda i: (0, i, 0)), row, row, full3((SGU_G, CHUNK, CHUNK)),
                  full3((SGU_G, CHUNK, CHUNK)), full3((4, POOL_GC, POOL_GC)), row, full2((HALO, D)), row, row],
        out_specs=[pl.BlockSpec((8, tm, D), lambda i: (0, i, 0)), full2((8, D)), full2((8, D)),
                   full3((SGU_G, CHUNK, CHUNK)), full2((CHUNK, CHUNK)), full3((4, POOL_GC, POOL_GC)),
                   full2((HALO, D))],
        out_shape=[_sds((8, t, D), BF16), _sds((8, D), F32), _sds((8, D), F32), _sds((SGU_G, CHUNK, CHUNK), F32),
                   _sds((CHUNK, CHUNK), F32), _sds((4, POOL_GC, POOL_GC), F32), _sds((HALO, D), F32)],
        scratch_shapes=[pltpu.VMEM((tm, D), F32), pltpu.VMEM((tm, D), F32), pltpu.VMEM((n_ext, D), F32),
                        pltpu.VMEM((SUBLANE, n_ext, LANE), F32), pltpu.VMEM((SUBLANE, n_ext, LANE), F32),
                        pltpu.VMEM((HALO, SUBLANE, D), F32)],
    )(z8, z8, z8, z8, z8, z8, z8, z8, cv, cv, dsa, dsb, dsb, dsc, dsc, dzg, ln_g, ln_b, w_s, bs_b, pool_w,
      pool_scale, conv_w, cln_g, cln_b), tasks)


def _ada_fwd(c_all, w_ada, b_loc, name):
    def body(c_ref, w_ref, b_ref, o_ref):
        cv = c_ref[...]
        ca = (cv * _sigmoid(cv)).astype(BF16)
        for l in range(DEPTH):
            o_ref[l] = _dot(ca, w_ref[l].astype(BF16)) + b_ref[l]

    return pl.pallas_call(body, name=name, out_shape=_sds((DEPTH, N_DEV, ADA_BLK), F32),
                          compiler_params=_params())(c_all, w_ada, b_loc)


def _ada_bwd(c_all_t, d_loc, name):
    def body(c_ref, d_ref, o_ref):
        cv = c_ref[...]
        ca = cv * _sigmoid(cv)
        for l in range(DEPTH):
            acc = jnp.zeros((D, ADA_BLK), F32)
            for j in range(N_DEV):
                acc = acc + ca[:, j:j + 1] * d_ref[l, j:j + 1, :]
            o_ref[l] = acc

    return pl.pallas_call(body, name=name, out_shape=_sds((DEPTH, D, ADA_BLK), F32),
                          compiler_params=_params())(c_all_t, d_loc)


def _sum8(g8, name):
    _, rows, cols = g8.shape
    tr = _tile(rows, 256)

    def body(g_ref, o_ref):
        acc = g_ref[0]
        for k in range(1, N_DEV):
            acc = acc + g_ref[k]
        o_ref[...] = acc

    return pl.pallas_call(body, name=name, grid=(rows // tr,),
                          in_specs=[pl.BlockSpec((N_DEV, tr, cols), lambda r: (0, r, 0))],
                          out_specs=pl.BlockSpec((tr, cols), lambda r: (r, 0)), out_shape=_sds((rows, cols), F32),
                          compiler_params=_params("arbitrary"))(g8)


PROJ = ("w_pa", "w_pb", "w_pc", "w_out")
FWD_GATHERS = {
    (0, "in_proj"): tuple((n, 0) for n in PROJ) + (("w_ffn_out", 0),),
    (0, "branches"): (("w_ffn_in", 0),),
    (0, "ffn_in"): (("w_in", 1),),
    (1, "in_proj"): (("w_ffn_in", 1),),
    (1, "branches"): tuple((n, 1) for n in PROJ) + (("w_ffn_out", 1),),
}
FWD_GATHER_FILLS_KERNEL = ((0, "in_proj"), (0, "ffn_in"), (1, "branches"))
GRAD_GROUP = {"w_in": (("in_a", 0), ("in_b", 0)), "pool_w": (("in_a", 1),), "w_pa": (("proj", 0),),
              "w_pb": (("proj", 1),), "w_pc": (("proj", 2),), "w_out": (("proj", 3),), "w_ffn_in": (("ffn", 0),),
              "w_ffn_out": (("ffn", 1),)}


EARLY_ROWS = dict(b_in=(0, 8), sgu_ln_g=(8, 9), sgu_ln_b=(9, 10), sgu_w_s=(10, 138), sgu_b_s=(138, 139),
                  pool_scale=(139, 140), conv_b=(140, 141), conv_ln_g=(141, 142), conv_ln_b=(142, 143), g_ffn=(143, 144))
CONV_ROW = 144
LATE_ROWS = dict(b_ada=(0, 6), g_mix=(6, 7))


def _run(factory, *args, tasks=(), **kw):
    out = factory(*args, tasks=tasks, **kw)
    return out if tasks else (out, [])


REPLICATED = ("b_ada", "g_mix", "b_in", "sgu_ln_g", "sgu_ln_b", "sgu_w_s", "sgu_b_s", "pool_scale", "conv_b",
              "conv_ln_g", "conv_ln_b", "g_ffn", "g_final")
WEIGHT_ORDER = ("w_ada", "b_ada", "g_mix", "w_in", "b_in", "sgu_ln_g", "sgu_ln_b", "sgu_w_s", "sgu_b_s", "w_pa",
                "pool_w", "pool_scale", "w_pb", "conv_w", "conv_b", "conv_ln_g", "conv_ln_b", "w_pc", "w_out",
                "g_ffn", "w_ffn_in", "w_ffn_out", "g_final")


def _rows(a):
    return a.reshape(-1, D)


def kernel(x, c, w_ada, b_ada, g_mix, w_in, b_in, sgu_ln_g, sgu_ln_b, sgu_w_s, sgu_b_s, w_pa, pool_w, pool_scale, w_pb, conv_w, conv_b, conv_ln_g, conv_ln_b, w_pc, w_out, g_ffn, w_ffn_in, w_ffn_out, g_final, loss_target, m_w_ada, m_b_ada, m_g_mix, m_w_in, m_b_in, m_sgu_ln_g, m_sgu_ln_b, m_sgu_w_s, m_sgu_b_s, m_w_pa, m_pool_w, m_pool_scale, m_w_pb, m_conv_w, m_conv_b, m_conv_ln_g, m_conv_ln_b, m_w_pc, m_w_out, m_g_ffn, m_w_ffn_in, m_w_ffn_out, m_g_final, v_w_ada, v_b_ada, v_g_mix, v_w_in, v_b_in, v_sgu_ln_g, v_sgu_ln_b, v_sgu_w_s, v_sgu_b_s, v_w_pa, v_pool_w, v_pool_scale, v_w_pb, v_conv_w, v_conv_b, v_conv_ln_g, v_conv_ln_b, v_w_pc, v_w_out, v_g_ffn, v_w_ffn_in, v_w_ffn_out, v_g_final):
    weights = dict(w_ada=w_ada, b_ada=b_ada, g_mix=g_mix, w_in=w_in, b_in=b_in, sgu_ln_g=sgu_ln_g, sgu_ln_b=sgu_ln_b,
                   sgu_w_s=sgu_w_s, sgu_b_s=sgu_b_s, w_pa=w_pa, pool_w=pool_w, pool_scale=pool_scale, w_pb=w_pb,
                   conv_w=conv_w, conv_b=conv_b, conv_ln_g=conv_ln_g, conv_ln_b=conv_ln_b, w_pc=w_pc, w_out=w_out,
                   g_ffn=g_ffn, w_ffn_in=w_ffn_in, w_ffn_out=w_ffn_out, g_final=g_final)
    mom1 = dict(w_ada=m_w_ada, b_ada=m_b_ada, g_mix=m_g_mix, w_in=m_w_in, b_in=m_b_in, sgu_ln_g=m_sgu_ln_g,
                sgu_ln_b=m_sgu_ln_b, sgu_w_s=m_sgu_w_s, sgu_b_s=m_sgu_b_s, w_pa=m_w_pa, pool_w=m_pool_w,
                pool_scale=m_pool_scale, w_pb=m_w_pb, conv_w=m_conv_w, conv_b=m_conv_b, conv_ln_g=m_conv_ln_g,
                conv_ln_b=m_conv_ln_b, w_pc=m_w_pc, w_out=m_w_out, g_ffn=m_g_ffn, w_ffn_in=m_w_ffn_in,
                w_ffn_out=m_w_ffn_out, g_final=m_g_final)
    mom2 = dict(w_ada=v_w_ada, b_ada=v_b_ada, g_mix=v_g_mix, w_in=v_w_in, b_in=v_b_in, sgu_ln_g=v_sgu_ln_g,
                sgu_ln_b=v_sgu_ln_b, sgu_w_s=v_sgu_w_s, sgu_b_s=v_sgu_b_s, w_pa=v_w_pa, pool_w=v_pool_w,
                pool_scale=v_pool_scale, w_pb=v_w_pb, conv_w=v_conv_w, conv_b=v_conv_b, conv_ln_g=v_conv_ln_g,
                conv_ln_b=v_conv_ln_b, w_pc=v_w_pc, w_out=v_w_out, g_ffn=v_g_ffn, w_ffn_in=v_w_ffn_in,
                w_ffn_out=v_w_ffn_out, g_final=v_g_final)

    for group in (weights, mom1, mom2):
        group["w_ffn_in"] = jnp.transpose(group["w_ffn_in"], (0, 2, 1))

    t = x.shape[1]
    xs = x.reshape(t, D)
    target = loss_target.reshape(t, D)
    me = 4 * lax.axis_index("x") + 2 * lax.axis_index("y") + lax.axis_index("c")
    core = lax.axis_index("c").astype(jnp.int32).reshape(1)

    bf = lambda n, l: weights[n][l].astype(BF16)
    (first,) = _transfer([_gather_task([bf("w_in", 0), c, pool_w, conv_w])], name="gather_first")
    w_in0, c_all, pool_all, conv_all = first
    full = [dict(w_in=w_in0)] + [dict() for _ in range(1, DEPTH)]

    def gather_at(l, stage):
        names = FWD_GATHERS.get((l, stage), ())
        relay_at = 1.0 if (l, stage) in FWD_GATHER_FILLS_KERNEL else 0.75
        return [_gather_task([bf(n, ll) for n, ll in names], relay_at)] if names else []

    def landed(l, stage, per):
        for (n, ll), arr in zip(FWD_GATHERS.get((l, stage), ()), per[0] if per else ()):
            full[ll][n] = arr

    c_all = c_all.reshape(N_DEV, D)
    pool_full = jnp.transpose(pool_all, (1, 2, 0, 3, 4)).reshape(DEPTH, 4, POOL_GC, POOL_GC)
    conv_full = jnp.transpose(conv_all, (1, 2, 0, 3)).reshape(DEPTH, CONV_K, D)
    conv_full = jnp.pad(conv_full, ((0, 0), (0, HALO - CONV_K), (0, 0)))

    b_loc = lax.dynamic_slice_in_dim(b_ada, me * ADA_BLK, ADA_BLK, axis=1).reshape(DEPTH, 1, ADA_BLK)
    ada_part = _ada_fwd(c_all, w_ada, b_loc, name="ada_fwd")
    ((ada_all,),) = _transfer([_gather_task([ada_part])], name="gather_ada")
    ada = lax.dynamic_index_in_dim(ada_all, me, axis=2, keepdims=False)
    ada = jnp.transpose(ada, (1, 0, 2)).reshape(DEPTH, 6, 1, D)

    bs_b = jnp.broadcast_to(sgu_b_s[..., None], (DEPTH, SGU_G, CHUNK, CHUNK))

    saved = []
    xl = xs
    h = _norm_mod(xl, g_mix[0].reshape(1, D), ada[0, 1], ada[0, 0], name="norm_mix_0")
    for l in range(DEPTH):
        w = full[l]
        sh_m, sc_m, gt_m, sh_f, sc_f, gt_f = (ada[l, k] for k in range(6))
        row = lambda a: a[l].reshape(1, D)
        z8, per = _run(_mm_cols, h, w["w_in"], b_in[l].reshape(8, 1, D), name=f"in_proj_{l}",
                       tasks=gather_at(l, "in_proj"))
        landed(l, "in_proj", per)
        (sa, sb, sc, cv), per = _run(
            _branches_fwd, z8, row(sgu_ln_g), row(sgu_ln_b), sgu_w_s[l], bs_b[l], pool_full[l], row(pool_scale),
            conv_full[l], row(conv_b), row(conv_ln_g), row(conv_ln_b), name=f"branches_{l}",
            tasks=gather_at(l, "branches"))
        landed(l, "branches", per)
        wpa, wpb, wpc, wout = (w[n].reshape(D, D) for n in PROJ)
        (ya, yb, yc, merged), per = _run(_proj_merge, sa, sb, sc, wpa, wpb, wpc, z8, name=f"proj_merge_{l}",
                                         tasks=gather_at(l, "proj_merge"))
        landed(l, "proj_merge", per)
        om, x1, h2 = _out_proj(merged, wout, xl, gt_m, row(g_ffn), sc_f, sh_f, name=f"out_proj_{l}")
        wfi = w["w_ffn_in"].reshape(2, 4, FF_BLK, D)
        (gu, f4), per = _run(_ffn_in, h2, wfi, name=f"ffn_in_{l}", tasks=gather_at(l, "ffn_in"))
        landed(l, "ffn_in", per)
        wfo4 = w["w_ffn_out"].reshape(4, FF_BLK, D)
        nxt = (g_mix[l + 1].reshape(1, D), ada[l + 1, 1], ada[l + 1, 0]) if l + 1 < DEPTH else (None, None, None)
        res, per = _run(_ffn_out, f4, wfo4, x1, gt_f, *nxt, name=f"ffn_out_{l}", tasks=gather_at(l, "ffn_out"))
        landed(l, "ffn_out", per)
        o, x2 = res[0], res[1]
        saved.append(dict(x=xl, h=h, z8=z8, sa=sa, sb=sb, sc=sc, cv=cv, ya=ya, yb=yb, yc=yc, merged=merged, om=om,
                          x1=x1, h2=h2, gu=gu, f4=f4, o=o, wpa=wpa, wpb=wpb, wpc=wpc, wout=wout, wfi=wfi, wfo4=wfo4))
        xl = x2
        h = res[2] if l + 1 < DEPTH else None

    loss_tile, dx, dg_final, do, dgt_f = _final_loss(xl, g_final.reshape(1, D), target, saved[-1]["o"],
                                                     ada[DEPTH - 1, 5], name="final_loss")
    loss_row = jnp.broadcast_to(loss_tile[0:1, 0:1], (1, D))

    chip_parts = [dict() for _ in range(DEPTH)]
    early_buf, late_buf = [None] * DEPTH, [None] * DEPTH
    early_all, late_all = [None] * DEPTH, [None] * DEPTH
    tril = jnp.tril(jnp.ones((CHUNK, CHUNK), F32))
    for l in reversed(range(DEPTH)):
        s, w = saved[l], full[l]
        above = l + 1 if l + 1 < DEPTH else None
        sh_m, sc_m, gt_m, sh_f, sc_f, gt_f = (ada[l, k] for k in range(6))
        row = lambda a: a[l].reshape(1, D)
        dgu, per = _run(_ffn_bwd_act, do, s["wfo4"], s["gu"], name=f"ffn_bwd_act_{l}",
                        tasks=[] if above is None else [_gather_task([early_buf[above], late_buf[above]])])
        if above is not None:
            early_all[above], late_all[above] = per[0]
        d_wfo = _mm_tn(s["f4"], do[None], name=f"dw_ffn_out_{l}")
        dgu8 = dgu.reshape(8, t, FF_BLK)
        dh2 = _mm_nt_sum(dgu8, w["w_ffn_in"], name=f"dh_ffn_{l}", b_is_kn=True)
        d_wfi = _mm_tn(dgu8, s["h2"][None], name=f"dw_ffn_in_{l}")
        dx1, st_f, dom, dgt_m = _norm_mod_bwd(dh2, s["x1"], dx, row(g_ffn), sc_f, s["om"], gt_m,
                                              name=f"norm_ffn_bwd_{l}")
        ffn_group = [d_wfi, d_wfo.reshape(8, D_FF // 8, D)]
        (dya, dyb, dyc, dzg, db_gate), per = _run(_merge_bwd, dom, s["wout"], s["z8"], s["ya"], s["yb"], s["yc"],
                                                  name=f"merge_bwd_{l}", tasks=[_sibling_task(ffn_group)])
        ffn_sums = _sibling_sums(ffn_group, per[0], core, tag=f"ffn_{l}")
        d_wout = _mm_tn(s["merged"][None], dom[None], name=f"dw_out_{l}")
        d_wpa = _mm_tn(s["sa"][None], dya[None], name=f"dw_pa_{l}")
        d_wpb = _mm_tn(s["sb"][None], dyb[None], name=f"dw_pb_{l}")
        d_wpc = _mm_tn(s["sc"][None], dyc[None], name=f"dw_pc_{l}")
        dsa = _mm_nt_sum(dya[None], s["wpa"][None], name=f"ds_a_{l}")
        dsb = _mm_nt_sum(dyb[None], s["wpb"][None], name=f"ds_b_{l}")
        dsc = _mm_nt_sum(dyc[None], s["wpc"][None], name=f"ds_c_{l}")
        proj_group = [g.reshape(8, D // 8, D) for g in (d_wpa, d_wpb, d_wpc, d_wout)]
        (dz8, db_in5, rows6, dws, dbs, dwp, dcw), per = _run(
            _branches_bwd, s["z8"], s["cv"], dsa, dsb, dsc, dzg, row(sgu_ln_g), row(sgu_ln_b), sgu_w_s[l], bs_b[l],
            pool_full[l], row(pool_scale), conv_full[l], row(conv_ln_g), row(conv_ln_b), name=f"branches_bwd_{l}",
            tasks=[_chips_task(ffn_sums), _sibling_task(proj_group)]
            + ([] if above is None else [_chips_task(in_b_sums)]))
        chip_parts[l]["ffn"] = per[0]
        proj_sums = _sibling_sums(proj_group, per[1], core, tag=f"proj_{l}")
        if above is not None:
            chip_parts[above]["in_b"] = per[2]
        early_buf[l] = jnp.concatenate([db_in5[0:5], db_gate, rows6[0:2], _rows(dws * tril),
                                        _rows(jnp.transpose(dbs[:, :SGU_G])), rows6[2:6], st_f[2:3], dcw], axis=0)
        d_win_a, per = _run(_mm_tn, s["h"][None, :, :D // 2], dz8, name=f"dw_in_a_{l}",
                            tasks=[_gather_task([early_buf[0]])] if l == 0 else [])
        if l == 0:
            (early_all[0],) = per[0]
        d_pool = jnp.transpose(dwp.reshape(4, N_DEV, POOL_GC // N_DEV, POOL_GC), (1, 0, 2, 3))
        in_a_group = [d_win_a, d_pool.reshape(N_DEV, 4 * POOL_GC // N_DEV, POOL_GC)]
        d_win_b, per = _run(_mm_tn, s["h"][None, :, D // 2:], dz8, name=f"dw_in_b_{l}",
                            tasks=[_sibling_task(in_a_group)])
        in_a_sums = _sibling_sums(in_a_group, per[0], core, tag=f"in_a_{l}")
        dh, per = _run(_mm_nt_sum, dz8, w["w_in"], name=f"dh_in_{l}",
                       tasks=[_chips_task(proj_sums), _chips_task(in_a_sums), _sibling_task([d_win_b])])
        chip_parts[l]["proj"], chip_parts[l]["in_a"] = per[0], per[1]
        in_b_sums = _sibling_sums([d_win_b], per[2], core, tag=f"in_b_{l}")
        gate = (saved[l - 1]["o"], ada[l - 1, 5]) if l > 0 else (None, None)
        res, per = _run(_norm_mod_bwd, dh, s["x"], dx1, row(g_mix), sc_m, *gate, name=f"norm_mix_bwd_{l}",
                        tasks=[_chips_task(in_b_sums)] if l == 0 else [])
        dx, st_m = res[0], res[1]
        if l == 0:
            chip_parts[0]["in_b"] = per[0]
        late_buf[l] = jnp.concatenate([st_m[0:1], st_m[1:2], dgt_m, st_f[0:1], st_f[1:2], dgt_f, st_m[2:3],
                                       dg_final if l == 0 else loss_row], axis=0)
        if l > 0:
            do, dgt_f = res[2], res[3]
    late_all[0] = _transfer([_gather_task([late_buf[0]])], name="gather_last")[0][0]

    early = [_sum8(early_all[l], name=f"sum_early_grads_{l}") for l in range(DEPTH)]
    late = [_sum8(late_all[l], name=f"sum_late_grads_{l}") for l in range(DEPTH)]
    layers = lambda red, lo, hi: jnp.stack([red[l][lo:hi] for l in range(DEPTH)], axis=0)
    grads = {n: layers(early, lo, hi).reshape(weights[n].shape) for n, (lo, hi) in EARLY_ROWS.items()}
    grads.update({n: layers(late, lo, hi).reshape(weights[n].shape) for n, (lo, hi) in LATE_ROWS.items()})
    grads["g_final"] = late[0][7]
    loss = late[DEPTH - 1][7, 0]
    conv_g = layers(early, CONV_ROW, CONV_ROW + CONV_K)
    grads["conv_w"] = lax.dynamic_slice_in_dim(conv_g, me * (D // N_DEV), D // N_DEV, axis=2)
    d_ada_all = jnp.stack([late_all[l][:, 0:6] for l in range(DEPTH)], axis=1).reshape(N_DEV, DEPTH, 6 * D)
    d_loc = jnp.transpose(lax.dynamic_slice_in_dim(d_ada_all, me * ADA_BLK, ADA_BLK, axis=2), (1, 0, 2))
    grads["w_ada"] = _ada_bwd(jnp.transpose(c_all), d_loc, name="ada_bwd")

    out = {}
    for n in REPLICATED + ("conv_w", "w_ada"):
        out[n] = _adam_nd(grads[n], weights[n], mom1[n], mom2[n], name=f"adam_{n}")
    for n, groups in GRAD_GROUP.items():
        parts = [chip_parts[l][group][k] for l in range(DEPTH) for group, k in groups]
        shape = (len(parts),) + parts[0].shape[1:]
        res = _adam(parts, weights[n].reshape(shape), mom1[n].reshape(shape), mom2[n].reshape(shape),
                    name=f"adam_{n}")
        out[n] = [r.reshape(weights[n].shape) for r in res]
    out["w_ffn_in"] = [jnp.transpose(r, (0, 2, 1)) for r in out["w_ffn_in"]]

    grad_x = dx.reshape(1, t, D)
    return (loss, grad_x, *[out[n][0] for n in WEIGHT_ORDER], *[out[n][1] for n in WEIGHT_ORDER],
            *[out[n][2] for n in WEIGHT_ORDER], *[out[n][3] for n in WEIGHT_ORDER])
```
